```python
import jax, jax.numpy as jnp
from jax import lax
import numpy as np

D_MODEL = 2048
BATCH = 8
SEQ = 2048
DEPTH = 2

CHUNK = 64
N_MIXERS = 2
HEAD_DIM = 64
N_HEADS = D_MODEL // HEAD_DIM
D_DECAY_LORA = max(32, int(round(1.8 * D_MODEL ** 0.5 / 32)) * 32)
D_AAA_LORA = max(32, int(round(1.8 * D_MODEL ** 0.5 / 32)) * 32)
D_GATE_LORA = max(32, int(round(0.6 * D_MODEL ** 0.8 / 32)) * 32)
D_FF = -(-8 * D_MODEL // (3 * 256)) * 256
Q_BLOCK = 128
RMS_EPS = 1e-6
GN_EPS = 64e-5

kernel_name = "rwkv7_fox_interleaved_hybrid"


def rmsnorm(x, g):
    xf = x.astype(jnp.float32)
    y = xf * lax.rsqrt(jnp.mean(xf * xf, axis=-1, keepdims=True) + RMS_EPS)
    return (y * g.astype(jnp.float32)).astype(x.dtype)


def shift_prev(x):
    pad = [(0, 0)] * x.ndim
    pad[1] = (1, 0)
    return jnp.pad(x[:, :-1], pad)


def wkv7_scan(r, w, k, v, a, b):
    B, T, H, N = r.shape

    def to_chunks(t):
        return jnp.moveaxis(t, 1, 0).reshape(T // CHUNK, CHUNK, B, H, N)

    def step(S, inp):
        r_t, w_t, k_t, v_t, a_t, b_t = inp
        sa = jnp.einsum('bhij,bhj->bhi', S, a_t)
        S = (S * w_t[:, :, None, :] + sa[..., None] * b_t[:, :, None, :]
             + v_t[..., None] * k_t[:, :, None, :])
        return S, jnp.einsum('bhij,bhj->bhi', S, r_t)

    def chunk_step(S, inp):
        return lax.scan(step, S, inp)

    S0 = jnp.zeros((B, H, N, N), jnp.float32)
    _, y = lax.scan(chunk_step, S0, tuple(to_chunks(t) for t in (r, w, k, v, a, b)))
    return jnp.moveaxis(y.reshape(T, B, H, N), 0, 1)


def rwkv7_time_mix(h, mix, w_rkv, w0, w1, w2, a0, a1, a2, g1, g2, k_k, k_a, r_k,
                   lnx_g, lnx_b, w_o):
    B, T, D = h.shape
    H, N = N_HEADS, HEAD_DIM
    f32 = jnp.float32
    xx = shift_prev(h) - h
    xm = h[None] + xx[None] * mix[:, None, None, :]
    rkv = jnp.einsum('pbtc,pcd->pbtd', xm[:3], w_rkv)
    r, k, v = rkv[0], rkv[1], rkv[2]
    xw, xa, xg = xm[3], xm[4], xm[5]
    w_log = -jax.nn.softplus(-(w0 + jnp.tanh(xw @ w1) @ w2).astype(f32)) - 0.5
    decay = jnp.exp(-jnp.exp(w_log))
    a = jax.nn.sigmoid(a0 + (xa @ a1) @ a2)
    g = jax.nn.sigmoid(xg @ g1) @ g2
    kk = (k * k_k).reshape(B, T, H, N).astype(f32)
    kk = kk / jnp.maximum(jnp.sqrt(jnp.sum(kk * kk, axis=-1, keepdims=True)), 1e-12)
    k = k * (1.0 + (a - 1.0) * k_a)
    heads = lambda t: t.reshape(B, T, H, N).astype(f32)
    rh, wh, kh, vh, ah = heads(r), heads(decay), heads(k), heads(v), heads(a)
    y = wkv7_scan(rh, wh, kh, vh, -kk, kk * ah)
    mu = jnp.mean(y, axis=-1, keepdims=True)
    var = jnp.mean(jnp.square(y - mu), axis=-1, keepdims=True)
    y = ((y - mu) * lax.rsqrt(var + GN_EPS)).reshape(B, T, D) * lnx_g.astype(f32) + lnx_b.astype(f32)
    bonus = (jnp.sum(rh * kh * r_k.astype(f32), axis=-1, keepdims=True) * vh).reshape(B, T, D)
    y = (y + bonus).astype(h.dtype)
    return (y * g) @ w_o


def fox_attention(h, w_in, b_f, qn_g, kn_g, on_g, w_o):
    B, T, D = h.shape
    H, N = N_HEADS, HEAD_DIM
    f32 = jnp.float32
    proj = h @ w_in
    q = proj[..., 0 * D:1 * D].reshape(B, T, H, N)
    k = proj[..., 1 * D:2 * D].reshape(B, T, H, N)
    v = proj[..., 2 * D:3 * D].reshape(B, T, H, N)
    gate = proj[..., 3 * D:4 * D]
    f_logit = proj[..., 4 * D:4 * D + H]
    alpha_k = jax.nn.sigmoid(proj[..., 4 * D + H:4 * D + 2 * H])[..., None]
    alpha_v = jax.nn.sigmoid(proj[..., 4 * D + 2 * H:4 * D + 3 * H])[..., None]
    k = alpha_k * shift_prev(k) + (1.0 - alpha_k) * k
    v = alpha_v * shift_prev(v) + (1.0 - alpha_v) * v
    q = rmsnorm(q, qn_g)
    k = rmsnorm(k, kn_g)
    c = jnp.cumsum(jax.nn.log_sigmoid((f_logit + b_f).astype(f32)), axis=1)
    qh, kh, vh = (t.transpose(0, 2, 1, 3) for t in (q, k, v))
    c = c.transpose(0, 2, 1)
    scale = N ** -0.5
    outs = []
    for start in range(0, T, Q_BLOCK):
        end = start + Q_BLOCK
        s = jnp.einsum('bhqd,bhkd->bhqk', qh[:, :, start:end], kh[:, :, :end]).astype(f32) * scale
        s = s + c[:, :, start:end, None] - c[:, :, None, :end]
        mask = jnp.arange(start, end)[:, None] >= jnp.arange(end)[None, :]
        p = jax.nn.softmax(jnp.where(mask, s, -jnp.inf), axis=-1)
        outs.append(jnp.einsum('bhqk,bhkd->bhqd', p.astype(vh.dtype), vh[:, :, :end]))
    o = jnp.concatenate(outs, axis=2).transpose(0, 2, 1, 3)
    o = rmsnorm(o, on_g.reshape(H, N)).reshape(B, T, D)
    return (o * jax.nn.sigmoid(gate)) @ w_o


def swiglu(h, w_gu, w_d):
    gu = h @ w_gu
    return (jax.nn.silu(gu[..., :D_FF]) * gu[..., D_FF:]) @ w_d


def setup_inputs(seed: int = 0) -> dict:
    key = jax.random.key(seed)
    ks = iter(jax.random.split(key, 40))
    D, H, N = D_MODEL, N_HEADS, HEAD_DIM
    nA = (DEPTH + N_MIXERS - 1) // N_MIXERS
    nB = DEPTH // N_MIXERS
    f32 = jnp.float32

    def nrm(shape, fan_in, scale=1.0):
        return scale * fan_in ** -0.5 * jax.random.normal(next(ks), shape, f32)

    def gain(shape):
        return 1.0 + 0.02 * jax.random.normal(next(ks), shape, f32)

    def unif(shape, lo, hi):
        return jax.random.uniform(next(ks), shape, f32, lo, hi)

    def small(shape, scale):
        return scale * jax.random.normal(next(ks), shape, f32)

    return {
        "x": jax.random.normal(next(ks), (BATCH, SEQ, D), f32),
        "a_norm_g": gain((nA, D)),
        "a_mix": unif((nA, 6, D), 0.0, 1.0),
        "a_w_rkv": nrm((nA, 3, D, D), D),
        "a_w0": unif((nA, D), -6.5, -1.5),
        "a_w1": nrm((nA, D, D_DECAY_LORA), D),
        "a_w2": nrm((nA, D_DECAY_LORA, D), D_DECAY_LORA, 0.5),
        "a_a0": small((nA, D), 0.1),
        "a_a1": nrm((nA, D, D_AAA_LORA), D),
        "a_a2": nrm((nA, D_AAA_LORA, D), D_AAA_LORA, 0.5),
        "a_g1": nrm((nA, D, D_GATE_LORA), D),
        "a_g2": nrm((nA, D_GATE_LORA, D), D_GATE_LORA),
        "a_k_k": 0.85 + small((nA, D), 0.05),
        "a_k_a": 1.0 + small((nA, D), 0.05),
        "a_r_k": small((nA, H, N), 0.1),
        "a_lnx_g": gain((nA, D)),
        "a_lnx_b": small((nA, D), 0.02),
        "a_w_o": nrm((nA, D, D), D),
        "b_norm_g": gain((nB, D)),
        "b_w_in": nrm((nB, D, 4 * D + 3 * H), D),
        "b_b_f": unif((nB, H), 1.0, 5.0),
        "b_qn_g": gain((nB, N)),
        "b_kn_g": gain((nB, N)),
        "b_on_g": gain((nB, D)),
        "b_w_o": nrm((nB, D, D), D),
        "f_norm_g": gain((DEPTH, D)),
        "f_w_gu": nrm((DEPTH, D, 2 * D_FF), D),
        "f_w_d": nrm((DEPTH, D_FF, D), D_FF),
        "final_g": gain((D,)),
    }


def reference(x, a_norm_g, a_mix, a_w_rkv, a_w0, a_w1, a_w2, a_a0, a_a1, a_a2, a_g1, a_g2,
              a_k_k, a_k_a, a_r_k, a_lnx_g, a_lnx_b, a_w_o,
              b_norm_g, b_w_in, b_b_f, b_qn_g, b_kn_g, b_on_g, b_w_o,
              f_norm_g, f_w_gu, f_w_d, final_g):
    for i in range(DEPTH):
        j = i // N_MIXERS
        if i % N_MIXERS == 0:
            h = rmsnorm(x, a_norm_g[j])
            x = x + rwkv7_time_mix(h, a_mix[j], a_w_rkv[j], a_w0[j], a_w1[j], a_w2[j],
                                   a_a0[j], a_a1[j], a_a2[j], a_g1[j], a_g2[j],
                                   a_k_k[j], a_k_a[j], a_r_k[j], a_lnx_g[j], a_lnx_b[j], a_w_o[j])
        else:
            h = rmsnorm(x, b_norm_g[j])
            x = x + fox_attention(h, b_w_in[j], b_b_f[j], b_qn_g[j], b_kn_g[j], b_on_g[j], b_w_o[j])
        x = x + swiglu(rmsnorm(x, f_norm_g[i]), f_w_gu[i], f_w_d[i])
    return rmsnorm(x, final_g)
```

```python
import functools

import jax
import jax.numpy as jnp
from jax import lax
from jax.experimental import pallas as pl
from jax.experimental.pallas import tpu as pltpu

D_MODEL = 2048
BATCH = 8
SEQ = 2048
N_TOK = BATCH * SEQ
HEAD_DIM = 64
HEAD_SHIFT = HEAD_DIM.bit_length() - 1
N_HEADS = D_MODEL // HEAD_DIM
D_FF = 5632
RMS_EPS = 1e-6
GN_EPS = 64e-5

LANES = 128
SUBLANES = 8
HEADS_PER_VREG = LANES // HEAD_DIM
N_PAIRS = D_MODEL // LANES
VMEM_LIMIT = 56 * 1024 * 1024

WKV_CHUNK = 64
WKV_TBLK = 512
WKV_LANES = 256
ATT_TQ = 256
ATT_TK = 256
NEG_BIG = -1e30

f32 = jnp.float32
bf16 = jnp.bfloat16


def _dot(a, b):
    return jnp.dot(a.astype(bf16), b.astype(bf16), preferred_element_type=f32)


def _dot_nt(a, b):
    return lax.dot_general(a.astype(bf16), b.astype(bf16), (((1,), (1,)), ((), ())),
                           preferred_element_type=f32)


def _split3(x):
    hi = x.astype(bf16)
    r1 = x - hi.astype(f32)
    mid = r1.astype(bf16)
    lo = (r1 - mid.astype(f32)).astype(bf16)
    return hi, mid, lo


def _dot_sel_rhs(x, sel):
    hi, mid, lo = _split3(x)
    d = lambda p: jnp.dot(p, sel, preferred_element_type=f32)
    return d(hi) + d(mid) + d(lo)


def _dot_sel_lhs(sel, x):
    hi, mid, lo = _split3(x)
    d = lambda p: jnp.dot(sel, p, preferred_element_type=f32)
    return d(hi) + d(mid) + d(lo)


def _head_ones():
    r = lax.broadcasted_iota(jnp.int32, (LANES, LANES), 0) >> HEAD_SHIFT
    c = lax.broadcasted_iota(jnp.int32, (LANES, LANES), 1) >> HEAD_SHIFT
    return (r == c).astype(bf16)


def _first_head_mask():
    return lax.broadcasted_iota(jnp.int32, (1, LANES), 1) < HEAD_DIM


def _stack_heads(x, m0):
    z = jnp.zeros_like(x)
    return jnp.concatenate([jnp.where(m0, x, z), jnp.where(m0, z, x)], axis=0)


def _softplus(z):
    return jnp.maximum(z, 0.0) + jnp.log(1.0 + jnp.exp(-jnp.abs(z)))


def _rms(x, g):
    return x * lax.rsqrt(jnp.mean(x * x, axis=-1, keepdims=True) + RMS_EPS) * g


def _rmsnorm_kernel(x_ref, g_ref, o_ref):
    o_ref[...] = _rms(x_ref[...], g_ref[...]).astype(o_ref.dtype)


def _rmsnorm(x, g, out_dtype, tm=512):
    m, d = x.shape
    return pl.pallas_call(
        _rmsnorm_kernel,
        grid=(m // tm,),
        in_specs=[pl.BlockSpec((tm, d), lambda i: (i, 0)),
                  pl.BlockSpec((1, d), lambda i: (0, 0))],
        out_specs=pl.BlockSpec((tm, d), lambda i: (i, 0)),
        out_shape=jax.ShapeDtypeStruct((m, d), out_dtype),
        compiler_params=pltpu.CompilerParams(dimension_semantics=("parallel",),
                                             vmem_limit_bytes=VMEM_LIMIT),
        name="rmsnorm",
    )(x, g.reshape(1, d))


def _norm_mix_kernel(x_ref, xp_ref, g_ref, mix_ref, *o_refs, tm):
    i = pl.program_id(0)
    g = g_ref[...]
    h = _rms(x_ref[...], g)
    hp_row = _rms(xp_ref[...], g)[SUBLANES - 1:SUBLANES, :]
    seq_start = (i * tm) % SEQ == 0
    hp_row = jnp.where(seq_start, jnp.zeros_like(hp_row), hp_row)
    rid = lax.broadcasted_iota(jnp.int32, (tm, 1), 0)
    hprev = jnp.where(rid == 0, hp_row, pltpu.roll(h, 1, 0))
    xx = hprev - h
    for p, o_ref in enumerate(o_refs):
        o_ref[...] = (h + xx * mix_ref[p:p + 1, :]).astype(o_ref.dtype)


def _norm_mix(x, g, mix, tm=256):
    m, d = x.shape
    rb = tm // SUBLANES
    out = jax.ShapeDtypeStruct((m, d), bf16)
    return pl.pallas_call(
        functools.partial(_norm_mix_kernel, tm=tm),
        grid=(m // tm,),
        in_specs=[pl.BlockSpec((tm, d), lambda i: (i, 0)),
                  pl.BlockSpec((SUBLANES, d), lambda i: (jnp.maximum(i * rb - 1, 0), 0)),
                  pl.BlockSpec((1, d), lambda i: (0, 0)),
                  pl.BlockSpec((6, d), lambda i: (0, 0))],
        out_specs=[pl.BlockSpec((tm, d), lambda i: (i, 0))] * 6,
        out_shape=[out] * 6,
        compiler_params=pltpu.CompilerParams(dimension_semantics=("parallel",),
                                             vmem_limit_bytes=VMEM_LIMIT),
        name="norm_mix",
    )(x, x, g.reshape(1, d), mix)


def _mm_kernel(*refs, n_w, n_e, nk, epilogue):
    x_ref = refs[0]
    w_refs = refs[1:1 + n_w]
    e_refs = refs[1 + n_w:1 + n_w + n_e]
    o_ref = refs[1 + n_w + n_e]
    acc_refs = refs[2 + n_w + n_e:]
    x = x_ref[...]
    if nk == 1:
        accs = [jnp.dot(x, w[...], preferred_element_type=f32) for w in w_refs]
        o_ref[...] = epilogue(accs, [e[...] for e in e_refs]).astype(o_ref.dtype)
        return
    k = pl.program_id(2)

    @pl.when(k == 0)
    def _():
        for a in acc_refs:
            a[...] = jnp.zeros_like(a)

    for a, w in zip(acc_refs, w_refs):
        a[...] += jnp.dot(x, w[...], preferred_element_type=f32)

    @pl.when(k == nk - 1)
    def _():
        o_ref[...] = epilogue([a[...] for a in acc_refs],
                              [e[...] for e in e_refs]).astype(o_ref.dtype)


def _matmul(x, w, w_col_blocks, n_out, *, tm, tn, tk, epilogue, extras=(), out_dtype, name):
    m, kdim = x.shape
    nk = kdim // tk
    n_w = len(w_col_blocks)
    nb = n_out // tn
    in_specs = [pl.BlockSpec((tm, tk), lambda i, j, k: (i, k))]
    args = [x]
    for cb in w_col_blocks:
        in_specs.append(pl.BlockSpec((tk, tn), lambda i, j, k, cb=cb: (k, cb * nb + j)))
        args.append(w)
    for arr, kind in extras:
        if kind == "row":
            in_specs.append(pl.BlockSpec((1, tn), lambda i, j, k: (0, j)))
        else:
            in_specs.append(pl.BlockSpec((tm, tn), lambda i, j, k: (i, j)))
        args.append(arr)
    scratch = [pltpu.VMEM((tm, tn), f32) for _ in range(n_w)] if nk > 1 else []
    return pl.pallas_call(
        functools.partial(_mm_kernel, n_w=n_w, n_e=len(extras), nk=nk, epilogue=epilogue),
        grid=(m // tm, nb, nk),
        in_specs=in_specs,
        out_specs=pl.BlockSpec((tm, tn), lambda i, j, k: (i, j)),
        out_shape=jax.ShapeDtypeStruct((m, n_out), out_dtype),
        scratch_shapes=scratch,
        compiler_params=pltpu.CompilerParams(
            dimension_semantics=("parallel", "parallel", "arbitrary"),
            vmem_limit_bytes=VMEM_LIMIT),
        name=name,
    )(*args)


def _epi_plain(accs, extras):
    return accs[0]


def _epi_residual(accs, extras):
    return extras[0] + accs[0]


def _epi_swiglu(accs, extras):
    gate, up = accs
    return gate * jax.nn.sigmoid(gate) * up


def _lora_kernel(x_ref, w1_ref, w2_ref, b_ref, o_ref, *, act, epi):
    z = jnp.dot(x_ref[...], w1_ref[...], preferred_element_type=f32)
    y = jnp.dot(act(z).astype(bf16), w2_ref[...], preferred_element_type=f32)
    o_ref[...] = epi(y, b_ref[...]).astype(o_ref.dtype)


def _lora(x, w1, w2, bias, *, act, epi, name, tm=512):
    m, d = x.shape
    r = w1.shape[1]
    n = w2.shape[1]
    return pl.pallas_call(
        functools.partial(_lora_kernel, act=act, epi=epi),
        grid=(m // tm,),
        in_specs=[pl.BlockSpec((tm, d), lambda i: (i, 0)),
                  pl.BlockSpec((d, r), lambda i: (0, 0)),
                  pl.BlockSpec((r, n), lambda i: (0, 0)),
                  pl.BlockSpec((1, n), lambda i: (0, 0))],
        out_specs=pl.BlockSpec((tm, n), lambda i: (i, 0)),
        out_shape=jax.ShapeDtypeStruct((m, n), f32),
        compiler_params=pltpu.CompilerParams(dimension_semantics=("parallel",),
                                             vmem_limit_bytes=VMEM_LIMIT),
        name=name,
    )(x, w1, w2, bias.reshape(1, n))


def _epi_log_decay(y, w0):
    return -jnp.exp(-_softplus(-(w0 + y)) - 0.5)


def _epi_sigmoid_bias(y, a0):
    return jax.nn.sigmoid(a0 + y)


def _epi_ignore_bias(y, b):
    return y


def _pad_cols(w, n):
    return jnp.pad(w, ((0, 0), (0, n - w.shape[1])))


def _pad_rows(w, n):
    return jnp.pad(w, ((0, n - w.shape[0]), (0, 0)))


def _wkv_chunk(r, lw, k, v, a, g, kkw, kaw, rkw, lng, lnb, s_prev, consts):
    m0, head_ones, tri, strict, incl, eye = consts
    c = WKV_CHUNK
    kk = k * kkw
    ss = _dot_sel_rhs(kk * kk, head_ones)
    kk = kk / jnp.maximum(jnp.sqrt(ss), 1e-12)
    k2 = k * (1.0 + (a - 1.0) * kaw)
    av = -kk
    bv = kk * a
    lg = _dot_sel_lhs(tri, lw)
    lg_end = lg[c - 1:c, :]
    g_in = jnp.exp(lg)
    g_inv = jnp.exp(-lg)
    g_rem = jnp.exp(lg_end - lg)
    rs = _stack_heads(r * g_in, m0).astype(bf16)
    as_ = _stack_heads(av * jnp.exp(lg - lw), m0).astype(bf16)
    bs = _stack_heads(bv * g_inv, m0).astype(bf16)
    ks = _stack_heads(k2 * g_inv, m0).astype(bf16)
    bhs = _stack_heads(bv * g_rem, m0).astype(bf16)
    khs = _stack_heads(k2 * g_rem, m0).astype(bf16)
    vs = _stack_heads(v, m0).astype(bf16)
    lhs = jnp.concatenate([as_, rs], axis=0)
    sc = _dot_nt(lhs, jnp.concatenate([bs, ks], axis=0))
    n2 = 2 * c
    zero = jnp.zeros((n2, n2), f32)
    a_ab = jnp.where(strict, sc[:n2, :n2], zero)
    a_ak = jnp.where(strict, sc[:n2, n2:], zero)
    a_rb = jnp.where(incl, sc[n2:, :n2], zero)
    a_rk = jnp.where(incl, sc[n2:, n2:], zero)
    t = eye + a_ab
    p = a_ab
    for _ in range(5):
        p = _dot(p, p)
        t = t + _dot(p, t)
    su = _dot_nt(lhs, s_prev)
    us = _dot(t, su[:n2] + _dot(a_ak, vs))
    uv = jnp.concatenate([us.astype(bf16), vs], axis=0)
    ys = su[n2:] + _dot(jnp.concatenate([a_rb, a_rk], axis=1), uv)
    y = ys[:c] + ys[c:]
    s_new = s_prev * jnp.exp(lg_end) + _dot(uv.astype(f32).T, jnp.concatenate([bhs, khs], axis=0))
    inv_n = 1.0 / HEAD_DIM
    mu = _dot_sel_rhs(y, head_ones) * inv_n
    yc = y - mu
    var = _dot_sel_rhs(yc * yc, head_ones) * inv_n
    yn = yc * lax.rsqrt(var + GN_EPS) * lng + lnb
    bonus = _dot_sel_rhs(r * k2 * rkw, head_ones) * v
    return (yn + bonus) * g, s_new


def _wkv_kernel(r_ref, lw_ref, k_ref, v_ref, a_ref, g_ref, kk_ref, ka_ref, rk_ref, lng_ref, lnb_ref,
                o_ref, s_ref):
    c = WKV_CHUNK
    n2 = 2 * c
    npairs = WKV_LANES // LANES

    @pl.when(pl.program_id(2) == 0)
    def _():
        s_ref[...] = jnp.zeros_like(s_ref)

    ri = lax.broadcasted_iota(jnp.int32, (n2, n2), 0)
    ci = lax.broadcasted_iota(jnp.int32, (n2, n2), 1)
    tr = lax.broadcasted_iota(jnp.int32, (c, c), 0)
    tc = lax.broadcasted_iota(jnp.int32, (c, c), 1)
    consts = (_first_head_mask(), _head_ones(), (tr >= tc).astype(bf16),
              (ri & (c - 1)) > (ci & (c - 1)), (ri & (c - 1)) >= (ci & (c - 1)),
              (ri == ci).astype(f32))

    def body(ic, carry):
        rows = pl.ds(pl.multiple_of(ic * c, c), c)
        for p in range(npairs):
            ln = slice(p * LANES, (p + 1) * LANES)
            out, s_new = _wkv_chunk(
                r_ref[rows, ln], lw_ref[rows, ln], k_ref[rows, ln], v_ref[rows, ln], a_ref[rows, ln],
                g_ref[rows, ln], kk_ref[:, ln], ka_ref[:, ln], rk_ref[:, ln], lng_ref[:, ln],
                lnb_ref[:, ln], s_ref[p], consts)
            s_ref[p] = s_new
            o_ref[rows, ln] = out.astype(o_ref.dtype)
        return carry

    lax.fori_loop(0, WKV_TBLK // c, body, 0)


def _wkv(r, lw, k, v, a, g, k_k, k_a, r_k, lnx_g, lnx_b):
    m, d = r.shape
    tb = SEQ // WKV_TBLK
    tile = pl.BlockSpec((WKV_TBLK, WKV_LANES), lambda b, j, t: (b * tb + t, j))
    row = pl.BlockSpec((1, WKV_LANES), lambda b, j, t: (0, j))
    rows = [x.reshape(1, d) for x in (k_k, k_a, r_k, lnx_g, lnx_b)]
    return pl.pallas_call(
        _wkv_kernel,
        grid=(BATCH, d // WKV_LANES, tb),
        in_specs=[tile] * 6 + [row] * 5,
        out_specs=tile,
        out_shape=jax.ShapeDtypeStruct((m, d), bf16),
        scratch_shapes=[pltpu.VMEM((WKV_LANES // LANES, LANES, LANES), f32)],
        compiler_params=pltpu.CompilerParams(
            dimension_semantics=("parallel", "parallel", "arbitrary"),
            vmem_limit_bytes=VMEM_LIMIT),
        name="wkv7",
    )(r, lw, k, v, a, g, *rows)


def _fox_gate_kernel(s_ref, bf_ref, ct_ref, run_ref, *, tc):
    @pl.when(pl.program_id(1) == 0)
    def _():
        run_ref[...] = jnp.zeros_like(run_ref)

    ls = -_softplus(-(s_ref[...] + bf_ref[...]))
    tr = lax.broadcasted_iota(jnp.int32, (tc, tc), 0)
    tcc = lax.broadcasted_iota(jnp.int32, (tc, tc), 1)
    cs = _dot_sel_lhs((tr >= tcc).astype(bf16), ls) + run_ref[...]
    run_ref[...] = cs[tc - 1:tc, :]
    ct_ref[0] = cs.T


def _fox_gate(small, b_f, tc=512):
    m, w = small.shape
    nt = SEQ // tc
    bias = jnp.pad(b_f, (0, w - b_f.shape[0])).reshape(1, w)
    return pl.pallas_call(
        functools.partial(_fox_gate_kernel, tc=tc),
        grid=(BATCH, nt),
        in_specs=[pl.BlockSpec((tc, w), lambda b, t: (b * nt + t, 0)),
                  pl.BlockSpec((1, w), lambda b, t: (0, 0))],
        out_specs=pl.BlockSpec((1, w, tc), lambda b, t: (b, 0, t)),
        out_shape=jax.ShapeDtypeStruct((BATCH, w, SEQ), f32),
        scratch_shapes=[pltpu.VMEM((1, w), f32)],
        compiler_params=pltpu.CompilerParams(dimension_semantics=("parallel", "arbitrary"),
                                             vmem_limit_bytes=VMEM_LIMIT),
        name="fox_gate_cumsum",
    )(small, bias)


def _fox_prep_kernel(q_ref, k_ref, v_ref, kp_ref, vp_ref, s_ref, qg_ref, kg_ref,
                     qo_ref, ko_ref, vo_ref, *, tm):
    i = pl.program_id(0)
    j = pl.program_id(1)
    head_ones = _head_ones()
    small = s_ref[...]
    r = lax.broadcasted_iota(jnp.int32, (LANES, LANES), 0)
    hd = (lax.broadcasted_iota(jnp.int32, (LANES, LANES), 1) >> HEAD_SHIFT) + HEADS_PER_VREG * j
    ak = jax.nn.sigmoid(_dot_sel_rhs(small, (r == hd + N_HEADS).astype(bf16)))
    av = jax.nn.sigmoid(_dot_sel_rhs(small, (r == hd + 2 * N_HEADS).astype(bf16)))
    seq_start = (i * tm) % SEQ == 0
    rid = lax.broadcasted_iota(jnp.int32, (tm, 1), 0)

    def shifted(x, xp_ref):
        prow = xp_ref[SUBLANES - 1:SUBLANES, :]
        prow = jnp.where(seq_start, jnp.zeros_like(prow), prow)
        return jnp.where(rid == 0, prow, pltpu.roll(x, 1, 0))

    def head_rms(x, gain):
        ms = _dot_sel_rhs(x * x, head_ones) * (1.0 / HEAD_DIM)
        return x * lax.rsqrt(ms + RMS_EPS) * gain

    k = k_ref[...]
    v = v_ref[...]
    k = ak * shifted(k, kp_ref) + (1.0 - ak) * k
    v = av * shifted(v, vp_ref) + (1.0 - av) * v
    qo_ref[...] = (head_rms(q_ref[...], qg_ref[...]) * (HEAD_DIM ** -0.5)).astype(qo_ref.dtype)
    ko_ref[...] = head_rms(k, kg_ref[...]).astype(ko_ref.dtype)
    vo_ref[...] = v.astype(vo_ref.dtype)


def _fox_prep(proj, small, qn_g, kn_g, tm=1024):
    m = proj.shape[0]
    rb = tm // SUBLANES
    w = small.shape[1]
    tile = lambda cb: pl.BlockSpec((tm, LANES), lambda i, j, cb=cb: (i, cb * N_PAIRS + j))
    prev = lambda cb: pl.BlockSpec(
        (SUBLANES, LANES), lambda i, j, cb=cb: (jnp.maximum(i * rb - 1, 0), cb * N_PAIRS + j))
    gain = pl.BlockSpec((1, LANES), lambda i, j: (0, 0))
    out = jax.ShapeDtypeStruct((m, D_MODEL), bf16)
    tile_gain = lambda x: jnp.tile(x, HEADS_PER_VREG).reshape(1, LANES)
    return pl.pallas_call(
        functools.partial(_fox_prep_kernel, tm=tm),
        grid=(m // tm, N_PAIRS),
        in_specs=[tile(0), tile(1), tile(2), prev(1), prev(2),
                  pl.BlockSpec((tm, w), lambda i, j: (i, 0)), gain, gain],
        out_specs=[pl.BlockSpec((tm, LANES), lambda i, j: (i, j))] * 3,
        out_shape=[out] * 3,
        compiler_params=pltpu.CompilerParams(dimension_semantics=("parallel", "parallel"),
                                             vmem_limit_bytes=VMEM_LIMIT),
        name="fox_prep",
    )(proj, proj, proj, proj, proj, small, tile_gain(qn_g), tile_gain(kn_g))


def _fox_attn_kernel(q_ref, k_ref, v_ref, c_ref, gate_ref, og_ref, o_ref):
    tq, tk = ATT_TQ, ATT_TK
    qi = pl.program_id(2)
    m0 = _first_head_mask()
    qs = _stack_heads(q_ref[...], m0)

    def step(j, carry, diag):
        m_run, l_run, acc = carry
        cols = pl.ds(pl.multiple_of(j * tk, tk), tk)
        s = _dot_nt(qs, k_ref[cols, :])
        crow = c_ref[0, 0, :, cols]
        s = s - jnp.concatenate([jnp.broadcast_to(crow[0:1], (tq, tk)),
                                 jnp.broadcast_to(crow[1:2], (tq, tk))], axis=0)
        if diag:
            rr = lax.broadcasted_iota(jnp.int32, (2 * tq, tk), 0) & (tq - 1)
            cc = lax.broadcasted_iota(jnp.int32, (2 * tq, tk), 1)
            s = jnp.where(rr >= cc, s, NEG_BIG)
        m_new = jnp.maximum(m_run, jnp.max(s, axis=-1, keepdims=True))
        alpha = jnp.exp(m_run - m_new)
        p = jnp.exp(s - m_new)
        l_new = alpha * l_run + jnp.sum(p, axis=-1, keepdims=True)
        pcat = jnp.concatenate([p[:tq], p[tq:]], axis=1).astype(bf16)
        alpha_l = jnp.where(m0, alpha[:tq], alpha[tq:])
        acc = acc * alpha_l + jnp.dot(pcat, _stack_heads(v_ref[cols, :], m0),
                                      preferred_element_type=f32)
        return m_new, l_new, acc

    init = (jnp.full((2 * tq, 1), NEG_BIG, f32), jnp.zeros((2 * tq, 1), f32),
            jnp.zeros((tq, LANES), f32))
    carry = lax.fori_loop(0, qi, lambda j, cy: step(j, cy, False), init)
    _, l_run, acc = step(qi, carry, True)
    o = acc / jnp.where(m0, l_run[:tq], l_run[tq:])
    ms = _dot_sel_rhs(o * o, _head_ones()) * (1.0 / HEAD_DIM)
    o = o * lax.rsqrt(ms + RMS_EPS) * og_ref[...]
    o_ref[...] = (o * jax.nn.sigmoid(gate_ref[...])).astype(o_ref.dtype)


def _fox_attn(q, k, v, c_t, proj, on_g):
    m, d = q.shape
    nq = SEQ // ATT_TQ
    gate_col0 = 3 * N_PAIRS
    return pl.pallas_call(
        _fox_attn_kernel,
        grid=(BATCH, N_PAIRS, nq),
        in_specs=[pl.BlockSpec((ATT_TQ, LANES), lambda b, p, i: (b * nq + i, p)),
                  pl.BlockSpec((SEQ, LANES), lambda b, p, i: (b, p)),
                  pl.BlockSpec((SEQ, LANES), lambda b, p, i: (b, p)),
                  pl.BlockSpec((1, 1, HEADS_PER_VREG, SEQ), lambda b, p, i: (b, p, 0, 0)),
                  pl.BlockSpec((ATT_TQ, LANES), lambda b, p, i: (b * nq + i, gate_col0 + p)),
                  pl.BlockSpec((1, LANES), lambda b, p, i: (0, p))],
        out_specs=pl.BlockSpec((ATT_TQ, LANES), lambda b, p, i: (b * nq + i, p)),
        out_shape=jax.ShapeDtypeStruct((m, d), bf16),
        compiler_params=pltpu.CompilerParams(
            dimension_semantics=("parallel", "parallel", "arbitrary"),
            vmem_limit_bytes=VMEM_LIMIT),
        name="fox_attention",
    )(q, k, v, c_t, proj, on_g.reshape(1, d))


def _swiglu_block(x, norm_g, w_gu, w_d):
    hn = _rmsnorm(x, norm_g, bf16)
    act = _matmul(hn, w_gu.astype(bf16), (0, 1), D_FF, tm=2048, tn=512, tk=D_MODEL,
                  epilogue=_epi_swiglu, out_dtype=bf16, name="swiglu_gate_up")
    return _matmul(act, w_d.astype(bf16), (0,), D_MODEL, tm=1024, tn=1024, tk=D_FF // 2,
                   epilogue=_epi_residual, extras=((x, "tile"),), out_dtype=f32, name="swiglu_down")


def _proj(x, w, *, out_dtype=f32, epilogue=_epi_plain, extras=(), name):
    n = w.shape[1]
    return _matmul(x, w, (0,), n, tm=1024, tn=min(n, 1024), tk=x.shape[1], epilogue=epilogue,
                   extras=extras, out_dtype=out_dtype, name=name)


def _rwkv7_block(x, norm_g, mix, w_rkv, w0, w1, w2, a0, a1, a2, g1, g2, k_k, k_a, r_k, lnx_g, lnx_b, w_o):
    xr, xk, xv, xw, xa, xg = _norm_mix(x, norm_g, mix)
    w_rkv = w_rkv.astype(bf16)
    r = _proj(xr, w_rkv[0], name="rwkv_r")
    k = _proj(xk, w_rkv[1], name="rwkv_k")
    v = _proj(xv, w_rkv[2], name="rwkv_v")
    rank = LANES * pl.cdiv(w1.shape[1], LANES)
    lw = _lora(xw, _pad_cols(w1, rank).astype(bf16), _pad_rows(w2, rank).astype(bf16), w0,
               act=jnp.tanh, epi=_epi_log_decay, name="rwkv_decay")
    a = _lora(xa, _pad_cols(a1, rank).astype(bf16), _pad_rows(a2, rank).astype(bf16), a0,
              act=lambda z: z, epi=_epi_sigmoid_bias, name="rwkv_iclr")
    g = _lora(xg, g1.astype(bf16), g2.astype(bf16), jnp.zeros((D_MODEL,), f32),
              act=jax.nn.sigmoid, epi=_epi_ignore_bias, name="rwkv_gate")
    yg = _wkv(r, lw, k, v, a, g, k_k, k_a, r_k.reshape(-1), lnx_g, lnx_b)
    return _proj(yg, w_o.astype(bf16), epilogue=_epi_residual, extras=((x, "tile"),), name="rwkv_out")


def _fox_block(x, norm_g, w_in, b_f, qn_g, kn_g, on_g, w_o):
    hn = _rmsnorm(x, norm_g, bf16)
    n_main = 4 * D_MODEL
    proj = _proj(hn, w_in[:, :n_main].astype(bf16), name="fox_in")
    w_small = _pad_cols(w_in[:, n_main:], LANES).astype(bf16)
    small = _proj(hn, w_small, name="fox_in_gates")
    c_t = _fox_gate(small, b_f)
    c_t = c_t[:, :N_HEADS].reshape(BATCH, N_PAIRS, HEADS_PER_VREG, SEQ)
    q, k, v = _fox_prep(proj, small, qn_g, kn_g)
    og = _fox_attn(q, k, v, c_t, proj, on_g)
    return _proj(og, w_o.astype(bf16), epilogue=_epi_residual, extras=((x, "tile"),), name="fox_out")


def kernel(x, a_norm_g, a_mix, a_w_rkv, a_w0, a_w1, a_w2, a_a0, a_a1, a_a2, a_g1, a_g2, a_k_k, a_k_a, a_r_k, a_lnx_g, a_lnx_b, a_w_o, b_norm_g, b_w_in, b_b_f, b_qn_g, b_kn_g, b_on_g, b_w_o, f_norm_g, f_w_gu, f_w_d, final_g):
    b, t, d = x.shape
    h = x.reshape(b * t, d)
    h = _rwkv7_block(h, a_norm_g[0], a_mix[0], a_w_rkv[0], a_w0[0], a_w1[0], a_w2[0], a_a0[0], a_a1[0],
                     a_a2[0], a_g1[0], a_g2[0], a_k_k[0], a_k_a[0], a_r_k[0], a_lnx_g[0], a_lnx_b[0],
                     a_w_o[0])
    h = _swiglu_block(h, f_norm_g[0], f_w_gu[0], f_w_d[0])
    h = _fox_block(h, b_norm_g[0], b_w_in[0], b_b_f[0], b_qn_g[0], b_kn_g[0], b_on_g[0], b_w_o[0])
    h = _swiglu_block(h, f_norm_g[1], f_w_gu[1], f_w_d[1])
    return _rmsnorm(h, final_g, f32).reshape(b, t, d)
```

```python
import functools

import jax
import jax.numpy as jnp
from jax import lax
from jax.experimental import pallas as pl
from jax.experimental.pallas import tpu as pltpu

D_MODEL = 2048
BATCH = 8
SEQ = 2048
N_TOK = BATCH * SEQ
HEAD_DIM = 64
HEAD_SHIFT = HEAD_DIM.bit_length() - 1
N_HEADS = D_MODEL // HEAD_DIM
D_FF = 5632
RMS_EPS = 1e-6
GN_EPS = 64e-5

LANES = 128
SUBLANES = 8
HEADS_PER_VREG = LANES // HEAD_DIM
N_PAIRS = D_MODEL // LANES
VMEM_LIMIT = 56 * 1024 * 1024

WKV_CHUNK = 64
WKV_TBLK = 512
WKV_LANES = 256
ATT_TQ = 256
ATT_TK = 512
ATT_LANES = 256
NEG_BIG = -1e30
LOG2_E = 1.4426950408889634

f32 = jnp.float32
bf16 = jnp.bfloat16


def _dot(a, b):
    return jnp.dot(a.astype(bf16), b.astype(bf16), preferred_element_type=f32)


def _dot_nt(a, b):
    return lax.dot_general(a.astype(bf16), b.astype(bf16), (((1,), (1,)), ((), ())),
                           preferred_element_type=f32)


def _split3(x):
    hi = x.astype(bf16)
    r1 = x - hi.astype(f32)
    mid = r1.astype(bf16)
    lo = (r1 - mid.astype(f32)).astype(bf16)
    return hi, mid, lo


def _dot_sel_rhs(x, sel):
    hi, mid, lo = _split3(x)
    d = lambda p: jnp.dot(p, sel, preferred_element_type=f32)
    return d(hi) + d(mid) + d(lo)


def _dot_sel_lhs(sel, x):
    hi, mid, lo = _split3(x)
    d = lambda p: jnp.dot(sel, p, preferred_element_type=f32)
    return d(hi) + d(mid) + d(lo)


def _head_ones():
    r = lax.broadcasted_iota(jnp.int32, (LANES, LANES), 0) >> HEAD_SHIFT
    c = lax.broadcasted_iota(jnp.int32, (LANES, LANES), 1) >> HEAD_SHIFT
    return (r == c).astype(bf16)


def _first_head_mask():
    return lax.broadcasted_iota(jnp.int32, (1, LANES), 1) < HEAD_DIM


def _stack_heads(x, m0):
    z = jnp.zeros_like(x)
    return jnp.concatenate([jnp.where(m0, x, z), jnp.where(m0, z, x)], axis=0)


def _softplus(z):
    return jnp.maximum(z, 0.0) + jnp.log(1.0 + jnp.exp(-jnp.abs(z)))


def _rms(x, g):
    return x * lax.rsqrt(jnp.mean(x * x, axis=-1, keepdims=True) + RMS_EPS) * g


def _rmsnorm_kernel(x_ref, g_ref, o_ref):
    o_ref[...] = _rms(x_ref[...], g_ref[...]).astype(o_ref.dtype)


def _rmsnorm(x, g, out_dtype, tm=512):
    m, d = x.shape
    return pl.pallas_call(
        _rmsnorm_kernel,
        grid=(m // tm,),
        in_specs=[pl.BlockSpec((tm, d), lambda i: (i, 0)),
                  pl.BlockSpec((1, d), lambda i: (0, 0))],
        out_specs=pl.BlockSpec((tm, d), lambda i: (i, 0)),
        out_shape=jax.ShapeDtypeStruct((m, d), out_dtype),
        compiler_params=pltpu.CompilerParams(dimension_semantics=("parallel",),
                                             vmem_limit_bytes=VMEM_LIMIT),
        name="rmsnorm",
    )(x, g.reshape(1, d))


def _norm_mix_kernel(x_ref, xp_ref, g_ref, mix_ref, *o_refs, tm):
    i = pl.program_id(0)
    g = g_ref[...]
    h = _rms(x_ref[...], g)
    hp_row = _rms(xp_ref[...], g)[SUBLANES - 1:SUBLANES, :]
    seq_start = (i * tm) % SEQ == 0
    hp_row = jnp.where(seq_start, jnp.zeros_like(hp_row), hp_row)
    rid = lax.broadcasted_iota(jnp.int32, (tm, 1), 0)
    hprev = jnp.where(rid == 0, hp_row, pltpu.roll(h, 1, 0))
    xx = hprev - h
    for p, o_ref in enumerate(o_refs):
        o_ref[...] = (h + xx * mix_ref[p:p + 1, :]).astype(o_ref.dtype)


def _norm_mix(x, g, mix, tm=256):
    m, d = x.shape
    rb = tm // SUBLANES
    out = jax.ShapeDtypeStruct((m, d), bf16)
    return pl.pallas_call(
        functools.partial(_norm_mix_kernel, tm=tm),
        grid=(m // tm,),
        in_specs=[pl.BlockSpec((tm, d), lambda i: (i, 0)),
                  pl.BlockSpec((SUBLANES, d), lambda i: (jnp.maximum(i * rb - 1, 0), 0)),
                  pl.BlockSpec((1, d), lambda i: (0, 0)),
                  pl.BlockSpec((6, d), lambda i: (0, 0))],
        out_specs=[pl.BlockSpec((tm, d), lambda i: (i, 0))] * 6,
        out_shape=[out] * 6,
        compiler_params=pltpu.CompilerParams(dimension_semantics=("parallel",),
                                             vmem_limit_bytes=VMEM_LIMIT),
        name="norm_mix",
    )(x, x, g.reshape(1, d), mix)


def _mm_kernel(*refs, n_w, n_e, nk, epilogue):
    x_ref = refs[0]
    w_refs = refs[1:1 + n_w]
    e_refs = refs[1 + n_w:1 + n_w + n_e]
    o_ref = refs[1 + n_w + n_e]
    acc_refs = refs[2 + n_w + n_e:]
    x = x_ref[...]
    if nk == 1:
        accs = [jnp.dot(x, w[...], preferred_element_type=f32) for w in w_refs]
        o_ref[...] = epilogue(accs, [e[...] for e in e_refs]).astype(o_ref.dtype)
        return
    k = pl.program_id(2)

    @pl.when(k == 0)
    def _():
        for a in acc_refs:
            a[...] = jnp.zeros_like(a)

    for a, w in zip(acc_refs, w_refs):
        a[...] += jnp.dot(x, w[...], preferred_element_type=f32)

    @pl.when(k == nk - 1)
    def _():
        o_ref[...] = epilogue([a[...] for a in acc_refs],
                              [e[...] for e in e_refs]).astype(o_ref.dtype)


def _matmul(x, w, w_col_blocks, n_out, *, tm, tn, tk, epilogue, extras=(), out_dtype, name):
    m, kdim = x.shape
    nk = kdim // tk
    n_w = len(w_col_blocks)
    nb = n_out // tn
    in_specs = [pl.BlockSpec((tm, tk), lambda i, j, k: (i, k))]
    args = [x]
    for cb in w_col_blocks:
        in_specs.append(pl.BlockSpec((tk, tn), lambda i, j, k, cb=cb: (k, cb * nb + j)))
        args.append(w)
    for arr, kind in extras:
        if kind == "row":
            in_specs.append(pl.BlockSpec((1, tn), lambda i, j, k: (0, j)))
        else:
            in_specs.append(pl.BlockSpec((tm, tn), lambda i, j, k: (i, j)))
        args.append(arr)
    scratch = [pltpu.VMEM((tm, tn), f32) for _ in range(n_w)] if nk > 1 else []
    return pl.pallas_call(
        functools.partial(_mm_kernel, n_w=n_w, n_e=len(extras), nk=nk, epilogue=epilogue),
        grid=(m // tm, nb, nk),
        in_specs=in_specs,
        out_specs=pl.BlockSpec((tm, tn), lambda i, j, k: (i, j)),
        out_shape=jax.ShapeDtypeStruct((m, n_out), out_dtype),
        scratch_shapes=scratch,
        compiler_params=pltpu.CompilerParams(
            dimension_semantics=("parallel", "parallel", "arbitrary"),
            vmem_limit_bytes=VMEM_LIMIT),
        name=name,
    )(*args)


def _epi_plain(accs, extras):
    return accs[0]


def _epi_residual(accs, extras):
    return extras[0] + accs[0]


def _epi_swiglu(accs, extras):
    gate, up = accs
    return gate * jax.nn.sigmoid(gate) * up


def _lora_kernel(x_ref, w1_ref, w2_ref, b_ref, o_ref, *, act, epi):
    z = jnp.dot(x_ref[...], w1_ref[...], preferred_element_type=f32)
    y = jnp.dot(act(z).astype(bf16), w2_ref[...], preferred_element_type=f32)
    o_ref[...] = epi(y, b_ref[...]).astype(o_ref.dtype)


def _lora(x, w1, w2, bias, *, act, epi, name, tm=512):
    m, d = x.shape
    r = w1.shape[1]
    n = w2.shape[1]
    return pl.pallas_call(
        functools.partial(_lora_kernel, act=act, epi=epi),
        grid=(m // tm,),
        in_specs=[pl.BlockSpec((tm, d), lambda i: (i, 0)),
                  pl.BlockSpec((d, r), lambda i: (0, 0)),
                  pl.BlockSpec((r, n), lambda i: (0, 0)),
                  pl.BlockSpec((1, n), lambda i: (0, 0))],
        out_specs=pl.BlockSpec((tm, n), lambda i: (i, 0)),
        out_shape=jax.ShapeDtypeStruct((m, n), f32),
        compiler_params=pltpu.CompilerParams(dimension_semantics=("parallel",),
                                             vmem_limit_bytes=VMEM_LIMIT),
        name=name,
    )(x, w1, w2, bias.reshape(1, n))


def _epi_log_decay(y, w0):
    return -jnp.exp(-_softplus(-(w0 + y)) - 0.5)


def _epi_sigmoid_bias(y, a0):
    return jax.nn.sigmoid(a0 + y)


def _epi_ignore_bias(y, b):
    return y


def _pad_cols(w, n):
    return jnp.pad(w, ((0, 0), (0, n - w.shape[1])))


def _pad_rows(w, n):
    return jnp.pad(w, ((0, n - w.shape[0]), (0, 0)))


def _each(fn, *lists):
    return [fn(*xs) for xs in zip(*lists)]


def _wkv_chunk_maps(tiles, consts):
    m0, tri, strict, incl, eye = consts
    c = WKV_CHUNK
    n2 = 2 * c
    r, lw, k2, v, av, bv = (list(x) for x in zip(*tiles))
    stack = lambda x: _stack_heads(x, m0)
    lg = _each(lambda x: _dot_sel_lhs(tri, x), lw)
    lg_end = _each(lambda x: x[c - 1:c, :], lg)
    g_inv = _each(lambda x: jnp.exp(-x), lg)
    g_rem = _each(lambda e, x: jnp.exp(e - x), lg_end, lg)
    rs = _each(lambda x, l: stack(x * jnp.exp(l)), r, lg)
    as_ = _each(lambda x, l, w: stack(x * jnp.exp(l - w)).astype(bf16), av, lg, lw)
    bs = _each(lambda x, g: stack(x * g).astype(bf16), bv, g_inv)
    ks = _each(lambda x, g: stack(x * g).astype(bf16), k2, g_inv)
    bhs = _each(lambda x, g: stack(x * g).astype(bf16), bv, g_rem)
    khs = _each(lambda x, g: stack(x * g).astype(bf16), k2, g_rem)
    vs = _each(stack, v)
    sc = _each(lambda a, rr, b, k: _dot_nt(jnp.concatenate([a, rr.astype(bf16)], axis=0),
                                           jnp.concatenate([b, k], axis=0)), as_, rs, bs, ks)
    zero = jnp.zeros((n2, n2), f32)
    a_ab = _each(lambda s: jnp.where(strict, s[:n2, :n2], zero), sc)
    a_ak = _each(lambda s: jnp.where(strict, s[:n2, n2:], zero), sc)
    a_r = _each(lambda s: jnp.where(jnp.concatenate([incl, incl], axis=1), s[n2:, :],
                                    jnp.zeros((n2, 2 * n2), f32)).astype(bf16), sc)
    t = _each(lambda x: eye + x, a_ab)
    p = a_ab
    for _ in range(c.bit_length() - 2):
        p = _each(lambda x: _dot(x, x), p)
        t = _each(lambda tt, pp: tt + _dot(pp, tt), t, p)
    t = _each(lambda x: x.astype(bf16), t)
    akv = _each(_dot, a_ak, vs)
    ah = _each(_dot, t, as_)
    ws = _each(_dot, t, akv)
    wv = _each(lambda w, x: jnp.concatenate([w, x], axis=0), ws, vs)
    pc = _each(lambda x, ar, h: x + _dot(ar[:, :n2], h), rs, a_r, ah)
    qc = _each(_dot, a_r, wv)
    gm = _each(lambda h, b: _dot(h.T, b), ah, bhs)
    nc = _each(lambda x, b, k: _dot(x.T, jnp.concatenate([b, k], axis=0)), wv, bhs, khs)
    decay = _each(jnp.exp, lg_end)
    return pc, qc, gm, nc, decay


def _wkv_kernel(r_ref, lw_ref, k_ref, v_ref, a_ref, g_ref, kk_ref, ka_ref, rk_ref, lng_ref, lnb_ref,
                o_ref, s_ref):
    c = WKV_CHUNK
    n2 = 2 * c

    @pl.when(pl.program_id(2) == 0)
    def _():
        s_ref[...] = jnp.zeros_like(s_ref)

    ri = lax.broadcasted_iota(jnp.int32, (n2, n2), 0)
    ci = lax.broadcasted_iota(jnp.int32, (n2, n2), 1)
    tr = lax.broadcasted_iota(jnp.int32, (c, c), 0)
    tc = lax.broadcasted_iota(jnp.int32, (c, c), 1)
    consts = (_first_head_mask(), (tr >= tc).astype(bf16),
              (ri & (c - 1)) > (ci & (c - 1)), (ri & (c - 1)) >= (ci & (c - 1)),
              (ri == ci).astype(f32))
    head_ones = _head_ones()
    inv_n = 1.0 / HEAD_DIM

    npairs = WKV_LANES // LANES
    nchunk = WKV_TBLK // c
    lanes = [slice(p * LANES, (p + 1) * LANES) for p in range(npairs)]
    r = [r_ref[:, ln] for ln in lanes]
    v = [v_ref[:, ln] for ln in lanes]
    k2, av, bv = [], [], []
    for ln in lanes:
        k, a = k_ref[:, ln], a_ref[:, ln]
        kk = k * kk_ref[:, ln]
        ss = _dot_sel_rhs(kk * kk, head_ones)
        kk = kk / jnp.maximum(jnp.sqrt(ss), 1e-12)
        k2.append(k * (1.0 + (a - 1.0) * ka_ref[:, ln]))
        av.append(-kk)
        bv.append(kk * a)
    tiles = []
    for ic in range(nchunk):
        rows = slice(ic * c, (ic + 1) * c)
        for p, ln in enumerate(lanes):
            tiles.append((r[p][rows], lw_ref[rows, ln], k2[p][rows], v[p][rows], av[p][rows], bv[p][rows]))
    pc, qc, gm, nc, decay = _wkv_chunk_maps(tiles, consts)
    s = [s_ref[p] for p in range(npairs)]
    ys = [[] for _ in range(npairs)]
    for ic in range(nchunk):
        for p in range(npairs):
            i = ic * npairs + p
            y = _dot_nt(pc[i], s[p]) + qc[i]
            ys[p].append(y[:c] + y[c:])
            s[p] = s[p] * decay[i] + _dot(s[p], gm[i]) + nc[i]
    for p, ln in enumerate(lanes):
        s_ref[p] = s[p]
        y = jnp.concatenate(ys[p], axis=0)
        mu = _dot_sel_rhs(y, head_ones) * inv_n
        yc = y - mu
        var = _dot_sel_rhs(yc * yc, head_ones) * inv_n
        yn = yc * lax.rsqrt(var + GN_EPS) * lng_ref[:, ln] + lnb_ref[:, ln]
        bonus = _dot_sel_rhs(r[p] * k2[p] * rk_ref[:, ln], head_ones) * v[p]
        o_ref[:, ln] = ((yn + bonus) * g_ref[:, ln]).astype(o_ref.dtype)


def _wkv(r, lw, k, v, a, g, k_k, k_a, r_k, lnx_g, lnx_b):
    m, d = r.shape
    tb = SEQ // WKV_TBLK
    tile = pl.BlockSpec((WKV_TBLK, WKV_LANES), lambda b, j, t: (b * tb + t, j))
    row = pl.BlockSpec((1, WKV_LANES), lambda b, j, t: (0, j))
    rows = [x.reshape(1, d) for x in (k_k, k_a, r_k, lnx_g, lnx_b)]
    return pl.pallas_call(
        _wkv_kernel,
        grid=(BATCH, d // WKV_LANES, tb),
        in_specs=[tile] * 6 + [row] * 5,
        out_specs=tile,
        out_shape=jax.ShapeDtypeStruct((m, d), bf16),
        scratch_shapes=[pltpu.VMEM((WKV_LANES // LANES, LANES, LANES), f32)],
        compiler_params=pltpu.CompilerParams(
            dimension_semantics=("parallel", "parallel", "arbitrary"),
            vmem_limit_bytes=VMEM_LIMIT),
        name="wkv7",
    )(r, lw, k, v, a, g, *rows)


def _fox_gate_kernel(s_ref, bf_ref, ct_ref, run_ref, *, tc):
    @pl.when(pl.program_id(1) == 0)
    def _():
        run_ref[...] = jnp.zeros_like(run_ref)

    ls = -_softplus(-(s_ref[...] + bf_ref[...]))
    tr = lax.broadcasted_iota(jnp.int32, (tc, tc), 0)
    tcc = lax.broadcasted_iota(jnp.int32, (tc, tc), 1)
    cs = _dot_sel_lhs((tr >= tcc).astype(bf16), ls) + run_ref[...]
    run_ref[...] = cs[tc - 1:tc, :]
    ct_ref[0] = (cs * LOG2_E).T


def _fox_gate(small, b_f, tc=512):
    m, w = small.shape
    nt = SEQ // tc
    bias = jnp.pad(b_f, (0, w - b_f.shape[0])).reshape(1, w)
    return pl.pallas_call(
        functools.partial(_fox_gate_kernel, tc=tc),
        grid=(BATCH, nt),
        in_specs=[pl.BlockSpec((tc, w), lambda b, t: (b * nt + t, 0)),
                  pl.BlockSpec((1, w), lambda b, t: (0, 0))],
        out_specs=pl.BlockSpec((1, w, tc), lambda b, t: (b, 0, t)),
        out_shape=jax.ShapeDtypeStruct((BATCH, w, SEQ), f32),
        scratch_shapes=[pltpu.VMEM((1, w), f32)],
        compiler_params=pltpu.CompilerParams(dimension_semantics=("parallel", "arbitrary"),
                                             vmem_limit_bytes=VMEM_LIMIT),
        name="fox_gate_cumsum",
    )(small, bias)


def _fox_prep_kernel(q_ref, k_ref, v_ref, kp_ref, vp_ref, s_ref, qg_ref, kg_ref,
                     qo_ref, ko_ref, vo_ref, *, tm):
    i = pl.program_id(0)
    j = pl.program_id(1)
    head_ones = _head_ones()
    small = s_ref[...]
    r = lax.broadcasted_iota(jnp.int32, (LANES, LANES), 0)
    hd = (lax.broadcasted_iota(jnp.int32, (LANES, LANES), 1) >> HEAD_SHIFT) + HEADS_PER_VREG * j
    ak = jax.nn.sigmoid(_dot_sel_rhs(small, (r == hd + N_HEADS).astype(bf16)))
    av = jax.nn.sigmoid(_dot_sel_rhs(small, (r == hd + 2 * N_HEADS).astype(bf16)))
    seq_start = (i * tm) % SEQ == 0
    rid = lax.broadcasted_iota(jnp.int32, (tm, 1), 0)

    def shifted(x, xp_ref):
        prow = xp_ref[SUBLANES - 1:SUBLANES, :]
        prow = jnp.where(seq_start, jnp.zeros_like(prow), prow)
        return jnp.where(rid == 0, prow, pltpu.roll(x, 1, 0))

    def head_rms(x, gain):
        ms = _dot_sel_rhs(x * x, head_ones) * (1.0 / HEAD_DIM)
        return x * lax.rsqrt(ms + RMS_EPS) * gain

    k = k_ref[...]
    v = v_ref[...]
    k = ak * shifted(k, kp_ref) + (1.0 - ak) * k
    v = av * shifted(v, vp_ref) + (1.0 - av) * v
    qo_ref[...] = (head_rms(q_ref[...], qg_ref[...]) * (HEAD_DIM ** -0.5 * LOG2_E)).astype(qo_ref.dtype)
    ko_ref[...] = head_rms(k, kg_ref[...]).astype(ko_ref.dtype)
    vo_ref[...] = v.astype(vo_ref.dtype)


def _fox_prep(proj, small, qn_g, kn_g, tm=1024):
    m = proj.shape[0]
    rb = tm // SUBLANES
    w = small.shape[1]
    tile = lambda cb: pl.BlockSpec((tm, LANES), lambda i, j, cb=cb: (i, cb * N_PAIRS + j))
    prev = lambda cb: pl.BlockSpec(
        (SUBLANES, LANES), lambda i, j, cb=cb: (jnp.maximum(i * rb - 1, 0), cb * N_PAIRS + j))
    gain = pl.BlockSpec((1, LANES), lambda i, j: (0, 0))
    out = jax.ShapeDtypeStruct((m, D_MODEL), bf16)
    tile_gain = lambda x: jnp.tile(x, HEADS_PER_VREG).reshape(1, LANES)
    return pl.pallas_call(
        functools.partial(_fox_prep_kernel, tm=tm),
        grid=(m // tm, N_PAIRS),
        in_specs=[tile(0), tile(1), tile(2), prev(1), prev(2),
                  pl.BlockSpec((tm, w), lambda i, j: (i, 0)), gain, gain],
        out_specs=[pl.BlockSpec((tm, LANES), lambda i, j: (i, j))] * 3,
        out_shape=[out] * 3,
        compiler_params=pltpu.CompilerParams(dimension_semantics=("parallel", "parallel"),
                                             vmem_limit_bytes=VMEM_LIMIT),
        name="fox_prep",
    )(proj, proj, proj, proj, proj, small, tile_gain(qn_g), tile_gain(kn_g))


def _fox_attn_kernel(q_ref, k_ref, v_ref, c_ref, gate_ref, og_ref, o_ref):
    tq, tk = ATT_TQ, ATT_TK
    qi = pl.program_id(2)
    m0 = _first_head_mask()
    npairs = ATT_LANES // LANES
    lanes = [slice(p * LANES, (p + 1) * LANES) for p in range(npairs)]
    qs = [_stack_heads(q_ref[:, ln], m0) for ln in lanes]
    row0 = qi * tq

    def step(j, carry, diag):
        m_run, l_run, acc = carry
        cols = pl.ds(pl.multiple_of(j * tk, tk), tk)
        s = [_dot_nt(qs[p], k_ref[cols, lanes[p]]) for p in range(npairs)]
        crow = [c_ref[0, p, :, cols] for p in range(npairs)]
        s = [x - jnp.concatenate([jnp.broadcast_to(cr[0:1], (tq, tk)),
                                  jnp.broadcast_to(cr[1:2], (tq, tk))], axis=0) for x, cr in zip(s, crow)]
        if diag:
            rr = row0 + (lax.broadcasted_iota(jnp.int32, (2 * tq, tk), 0) & (tq - 1))
            cc = j * tk + lax.broadcasted_iota(jnp.int32, (2 * tq, tk), 1)
            s = [jnp.where(rr >= cc, x, NEG_BIG) for x in s]
        m_new = [jnp.maximum(mr, jnp.max(x, axis=-1, keepdims=True)) for mr, x in zip(m_run, s)]
        alpha = [jnp.exp2(mr - mn) for mr, mn in zip(m_run, m_new)]
        pr = [jnp.exp2(x - mn) for x, mn in zip(s, m_new)]
        l_new = [a * lr + jnp.sum(x, axis=-1, keepdims=True) for a, lr, x in zip(alpha, l_run, pr)]
        pcat = [jnp.concatenate([x[:tq], x[tq:]], axis=1).astype(bf16) for x in pr]
        pv = [jnp.dot(pcat[p], _stack_heads(v_ref[cols, lanes[p]], m0), preferred_element_type=f32)
              for p in range(npairs)]
        acc = [ac * jnp.where(m0, a[:tq], a[tq:]) + x for ac, a, x in zip(acc, alpha, pv)]
        return m_new, l_new, acc

    init = ([jnp.full((2 * tq, 1), NEG_BIG, f32)] * npairs, [jnp.zeros((2 * tq, 1), f32)] * npairs,
            [jnp.zeros((tq, LANES), f32)] * npairs)
    n_full = row0 // tk
    carry = lax.fori_loop(0, n_full, lambda j, cy: step(j, cy, False), init)
    _, l_run, acc = step(n_full, carry, True)
    head_ones = _head_ones()
    for p, ln in enumerate(lanes):
        o = acc[p] / jnp.where(m0, l_run[p][:tq], l_run[p][tq:])
        ms = _dot_sel_rhs(o * o, head_ones) * (1.0 / HEAD_DIM)
        o = o * lax.rsqrt(ms + RMS_EPS) * og_ref[:, ln]
        o_ref[:, ln] = (o * jax.nn.sigmoid(gate_ref[:, ln])).astype(o_ref.dtype)


def _fox_attn(q, k, v, c_t, proj, on_g):
    m, d = q.shape
    nq = SEQ // ATT_TQ
    npairs = ATT_LANES // LANES
    gate_col0 = 3 * D_MODEL // ATT_LANES
    return pl.pallas_call(
        _fox_attn_kernel,
        grid=(BATCH, d // ATT_LANES, nq),
        in_specs=[pl.BlockSpec((ATT_TQ, ATT_LANES), lambda b, p, i: (b * nq + i, p)),
                  pl.BlockSpec((SEQ, ATT_LANES), lambda b, p, i: (b, p)),
                  pl.BlockSpec((SEQ, ATT_LANES), lambda b, p, i: (b, p)),
                  pl.BlockSpec((1, npairs, HEADS_PER_VREG, SEQ), lambda b, p, i: (b, p, 0, 0)),
                  pl.BlockSpec((ATT_TQ, ATT_LANES), lambda b, p, i: (b * nq + i, gate_col0 + p)),
                  pl.BlockSpec((1, ATT_LANES), lambda b, p, i: (0, p))],
        out_specs=pl.BlockSpec((ATT_TQ, ATT_LANES), lambda b, p, i: (b * nq + i, p)),
        out_shape=jax.ShapeDtypeStruct((m, d), bf16),
        compiler_params=pltpu.CompilerParams(
            dimension_semantics=("parallel", "parallel", "arbitrary"),
            vmem_limit_bytes=VMEM_LIMIT),
        name="fox_attention",
    )(q, k, v, c_t, proj, on_g.reshape(1, d))


def _swiglu_block(x, norm_g, w_gu, w_d):
    hn = _rmsnorm(x, norm_g, bf16)
    act = _matmul(hn, w_gu.astype(bf16), (0, 1), D_FF, tm=2048, tn=512, tk=D_MODEL,
                  epilogue=_epi_swiglu, out_dtype=bf16, name="swiglu_gate_up")
    return _matmul(act, w_d.astype(bf16), (0,), D_MODEL, tm=1024, tn=1024, tk=D_FF // 2,
                   epilogue=_epi_residual, extras=((x, "tile"),), out_dtype=f32, name="swiglu_down")


def _proj(x, w, *, out_dtype=f32, epilogue=_epi_plain, extras=(), name):
    n = w.shape[1]
    return _matmul(x, w, (0,), n, tm=1024, tn=min(n, 1024), tk=x.shape[1], epilogue=epilogue,
                   extras=extras, out_dtype=out_dtype, name=name)


def _rwkv7_block(x, norm_g, mix, w_rkv, w0, w1, w2, a0, a1, a2, g1, g2, k_k, k_a, r_k, lnx_g, lnx_b, w_o):
    xr, xk, xv, xw, xa, xg = _norm_mix(x, norm_g, mix)
    w_rkv = w_rkv.astype(bf16)
    r = _proj(xr, w_rkv[0], name="rwkv_r")
    k = _proj(xk, w_rkv[1], name="rwkv_k")
    v = _proj(xv, w_rkv[2], name="rwkv_v")
    rank = LANES * pl.cdiv(w1.shape[1], LANES)
    lw = _lora(xw, _pad_cols(w1, rank).astype(bf16), _pad_rows(w2, rank).astype(bf16), w0,
               act=jnp.tanh, epi=_epi_log_decay, name="rwkv_decay")
    a = _lora(xa, _pad_cols(a1, rank).astype(bf16), _pad_rows(a2, rank).astype(bf16), a0,
              act=lambda z: z, epi=_epi_sigmoid_bias, name="rwkv_iclr")
    g = _lora(xg, g1.astype(bf16), g2.astype(bf16), jnp.zeros((D_MODEL,), f32),
              act=jax.nn.sigmoid, epi=_epi_ignore_bias, name="rwkv_gate")
    yg = _wkv(r, lw, k, v, a, g, k_k, k_a, r_k.reshape(-1), lnx_g, lnx_b)
    return _proj(yg, w_o.astype(bf16), epilogue=_epi_residual, extras=((x, "tile"),), name="rwkv_out")


def _fox_block(x, norm_g, w_in, b_f, qn_g, kn_g, on_g, w_o):
    hn = _rmsnorm(x, norm_g, bf16)
    n_main = 4 * D_MODEL
    proj = _proj(hn, w_in[:, :n_main].astype(bf16), name="fox_in")
    w_small = _pad_cols(w_in[:, n_main:], LANES).astype(bf16)
    small = _proj(hn, w_small, name="fox_in_gates")
    c_t = _fox_gate(small, b_f)
    c_t = c_t[:, :N_HEADS].reshape(BATCH, N_PAIRS, HEADS_PER_VREG, SEQ)
    q, k, v = _fox_prep(proj, small, qn_g, kn_g)
    og = _fox_attn(q, k, v, c_t, proj, on_g)
    return _proj(og, w_o.astype(bf16), epilogue=_epi_residual, extras=((x, "tile"),), name="fox_out")


def kernel(x, a_norm_g, a_mix, a_w_rkv, a_w0, a_w1, a_w2, a_a0, a_a1, a_a2, a_g1, a_g2, a_k_k, a_k_a, a_r_k, a_lnx_g, a_lnx_b, a_w_o, b_norm_g, b_w_in, b_b_f, b_qn_g, b_kn_g, b_on_g, b_w_o, f_norm_g, f_w_gu, f_w_d, final_g):
    b, t, d = x.shape
    h = x.reshape(b * t, d)
    h = _rwkv7_block(h, a_norm_g[0], a_mix[0], a_w_rkv[0], a_w0[0], a_w1[0], a_w2[0], a_a0[0], a_a1[0],
                     a_a2[0], a_g1[0], a_g2[0], a_k_k[0], a_k_a[0], a_r_k[0], a_lnx_g[0], a_lnx_b[0],
                     a_w_o[0])
    h = _swiglu_block(h, f_norm_g[0], f_w_gu[0], f_w_d[0])
    h = _fox_block(h, b_norm_g[0], b_w_in[0], b_b_f[0], b_qn_g[0], b_kn_g[0], b_on_g[0], b_w_o[0])
    h = _swiglu_block(h, f_norm_g[1], f_w_gu[1], f_w_d[1])
    return _rmsnorm(h, final_g, f32).reshape(b, t, d)
```

```python
import functools

import jax
import jax.numpy as jnp
from jax import lax
from jax.experimental import pallas as pl
from jax.experimental.pallas import tpu as pltpu

D_MODEL = 2048
BATCH = 8
SEQ = 2048
N_TOK = BATCH * SEQ
HEAD_DIM = 64
HEAD_SHIFT = HEAD_DIM.bit_length() - 1
N_HEADS = D_MODEL // HEAD_DIM
D_FF = 5632
RMS_EPS = 1e-6
GN_EPS = 64e-5

LANES = 128
SUBLANES = 8
HEADS_PER_VREG = LANES // HEAD_DIM
N_PAIRS = D_MODEL // LANES
VMEM_LIMIT = 56 * 1024 * 1024

WKV_CHUNK = 64
WKV_TBLK = 512
WKV_LANES = 256
ATT_TQ = 256
ATT_TK = 512
ATT_LANES = 512
N_SPLIT = 3
NEG_BIG = -1e30
LOG2_E = 1.4426950408889634

f32 = jnp.float32
bf16 = jnp.bfloat16


def _dot(a, b):
    return jnp.dot(a.astype(bf16), b.astype(bf16), preferred_element_type=f32)


def _dot_nt(a, b):
    return lax.dot_general(a.astype(bf16), b.astype(bf16), (((1,), (1,)), ((), ())),
                           preferred_element_type=f32)


def _split3(x):
    hi = x.astype(bf16)
    r1 = x - hi.astype(f32)
    mid = r1.astype(bf16)
    lo = (r1 - mid.astype(f32)).astype(bf16)
    return hi, mid, lo


def _dot_sel_rhs(x, sel):
    hi, mid, lo = _split3(x)
    d = lambda p: jnp.dot(p, sel, preferred_element_type=f32)
    return d(hi) + d(mid) + d(lo)


def _dot_sel_lhs(sel, x):
    hi, mid, lo = _split3(x)
    d = lambda p: jnp.dot(sel, p, preferred_element_type=f32)
    return d(hi) + d(mid) + d(lo)


def _head_ones():
    r = lax.broadcasted_iota(jnp.int32, (LANES, LANES), 0) >> HEAD_SHIFT
    c = lax.broadcasted_iota(jnp.int32, (LANES, LANES), 1) >> HEAD_SHIFT
    return (r == c).astype(bf16)


def _first_head_mask():
    return lax.broadcasted_iota(jnp.int32, (1, LANES), 1) < HEAD_DIM


def _stack_heads(x, m0):
    z = jnp.zeros_like(x)
    return jnp.concatenate([jnp.where(m0, x, z), jnp.where(m0, z, x)], axis=0)


def _softplus(z):
    return jnp.maximum(z, 0.0) + jnp.log(1.0 + jnp.exp(-jnp.abs(z)))


def _rms(x, g):
    return x * lax.rsqrt(jnp.mean(x * x, axis=-1, keepdims=True) + RMS_EPS) * g


def _rmsnorm_kernel(x_ref, g_ref, o_ref):
    o_ref[...] = _rms(x_ref[...], g_ref[...]).astype(o_ref.dtype)


def _rmsnorm(x, g, out_dtype, tm=512):
    m, d = x.shape
    return pl.pallas_call(
        _rmsnorm_kernel,
        grid=(m // tm,),
        in_specs=[pl.BlockSpec((tm, d), lambda i: (i, 0)),
                  pl.BlockSpec((1, d), lambda i: (0, 0))],
        out_specs=pl.BlockSpec((tm, d), lambda i: (i, 0)),
        out_shape=jax.ShapeDtypeStruct((m, d), out_dtype),
        compiler_params=pltpu.CompilerParams(dimension_semantics=("parallel",),
                                             vmem_limit_bytes=VMEM_LIMIT),
        name="rmsnorm",
    )(x, g.reshape(1, d))


def _norm_mix_kernel(x_ref, xp_ref, g_ref, mix_ref, *o_refs, tm):
    i = pl.program_id(0)
    g = g_ref[...]
    h = _rms(x_ref[...], g)
    hp_row = _rms(xp_ref[...], g)[SUBLANES - 1:SUBLANES, :]
    seq_start = (i * tm) % SEQ == 0
    hp_row = jnp.where(seq_start, jnp.zeros_like(hp_row), hp_row)
    rid = lax.broadcasted_iota(jnp.int32, (tm, 1), 0)
    hprev = jnp.where(rid == 0, hp_row, pltpu.roll(h, 1, 0))
    xx = hprev - h
    for p, o_ref in enumerate(o_refs):
        o_ref[...] = (h + xx * mix_ref[p:p + 1, :]).astype(o_ref.dtype)


def _norm_mix(x, g, mix, tm=256):
    m, d = x.shape
    rb = tm // SUBLANES
    out = jax.ShapeDtypeStruct((m, d), bf16)
    return pl.pallas_call(
        functools.partial(_norm_mix_kernel, tm=tm),
        grid=(m // tm,),
        in_specs=[pl.BlockSpec((tm, d), lambda i: (i, 0)),
                  pl.BlockSpec((SUBLANES, d), lambda i: (jnp.maximum(i * rb - 1, 0), 0)),
                  pl.BlockSpec((1, d), lambda i: (0, 0)),
                  pl.BlockSpec((6, d), lambda i: (0, 0))],
        out_specs=[pl.BlockSpec((tm, d), lambda i: (i, 0))] * 6,
        out_shape=[out] * 6,
        compiler_params=pltpu.CompilerParams(dimension_semantics=("parallel",),
                                             vmem_limit_bytes=VMEM_LIMIT),
        name="norm_mix",
    )(x, x, g.reshape(1, d), mix)


def _mm_kernel(*refs, n_w, n_e, nk, epilogue):
    x_ref = refs[0]
    w_refs = refs[1:1 + n_w]
    e_refs = refs[1 + n_w:1 + n_w + n_e]
    o_ref = refs[1 + n_w + n_e]
    acc_refs = refs[2 + n_w + n_e:]
    x = x_ref[...]
    if nk == 1:
        accs = [jnp.dot(x, w[...], preferred_element_type=f32) for w in w_refs]
        o_ref[...] = epilogue(accs, [e[...] for e in e_refs]).astype(o_ref.dtype)
        return
    k = pl.program_id(2)

    @pl.when(k == 0)
    def _():
        for a in acc_refs:
            a[...] = jnp.zeros_like(a)

    for a, w in zip(acc_refs, w_refs):
        a[...] += jnp.dot(x, w[...], preferred_element_type=f32)

    @pl.when(k == nk - 1)
    def _():
        o_ref[...] = epilogue([a[...] for a in acc_refs],
                              [e[...] for e in e_refs]).astype(o_ref.dtype)


def _matmul(x, w, layer, w_col_blocks, n_out, *, tm, tn, tk, epilogue, extras=(), out_dtype, name):
    m, kdim = x.shape
    nk = kdim // tk
    n_w = len(w_col_blocks)
    nb = n_out // tn
    in_specs = [pl.BlockSpec((tm, tk), lambda i, j, k: (i, k))]
    args = [x]
    for cb in w_col_blocks:
        in_specs.append(pl.BlockSpec((None, tk, tn), lambda i, j, k, cb=cb: (layer, k, cb * nb + j)))
        args.append(w)
    for arr, kind in extras:
        if kind == "row":
            in_specs.append(pl.BlockSpec((1, tn), lambda i, j, k: (0, j)))
        else:
            in_specs.append(pl.BlockSpec((tm, tn), lambda i, j, k: (i, j)))
        args.append(arr)
    scratch = [pltpu.VMEM((tm, tn), f32) for _ in range(n_w)] if nk > 1 else []
    return pl.pallas_call(
        functools.partial(_mm_kernel, n_w=n_w, n_e=len(extras), nk=nk, epilogue=epilogue),
        grid=(m // tm, nb, nk),
        in_specs=in_specs,
        out_specs=pl.BlockSpec((tm, tn), lambda i, j, k: (i, j)),
        out_shape=jax.ShapeDtypeStruct((m, n_out), out_dtype),
        scratch_shapes=scratch,
        compiler_params=pltpu.CompilerParams(
            dimension_semantics=("parallel", "parallel", "arbitrary"),
            vmem_limit_bytes=VMEM_LIMIT),
        name=name,
    )(*args)


def _epi_plain(accs, extras):
    return accs[0]


def _epi_residual(accs, extras):
    return extras[0] + accs[0]


def _epi_swiglu(accs, extras):
    gate, up = accs
    return gate * jax.nn.sigmoid(gate) * up


def _lora_kernel(x_ref, w1_ref, w2_ref, b_ref, o_ref, *, act, epi):
    z = jnp.dot(x_ref[...], w1_ref[...], preferred_element_type=f32)
    y = jnp.dot(act(z).astype(bf16), w2_ref[...], preferred_element_type=f32)
    o_ref[...] = epi(y, b_ref[...]).astype(o_ref.dtype)


def _lora(x, w1, w2, bias, *, act, epi, name, tm=512):
    m, d = x.shape
    r = w1.shape[1]
    n = w2.shape[1]
    return pl.pallas_call(
        functools.partial(_lora_kernel, act=act, epi=epi),
        grid=(m // tm,),
        in_specs=[pl.BlockSpec((tm, d), lambda i: (i, 0)),
                  pl.BlockSpec((d, r), lambda i: (0, 0)),
                  pl.BlockSpec((r, n), lambda i: (0, 0)),
                  pl.BlockSpec((1, n), lambda i: (0, 0))],
        out_specs=pl.BlockSpec((tm, n), lambda i: (i, 0)),
        out_shape=jax.ShapeDtypeStruct((m, n), f32),
        compiler_params=pltpu.CompilerParams(dimension_semantics=("parallel",),
                                             vmem_limit_bytes=VMEM_LIMIT),
        name=name,
    )(x, w1, w2, bias.reshape(1, n))


def _epi_log_decay(y, w0):
    return -jnp.exp(-_softplus(-(w0 + y)) - 0.5)


def _epi_sigmoid_bias(y, a0):
    return jax.nn.sigmoid(a0 + y)


def _epi_ignore_bias(y, b):
    return y


def _pad_cols(w, n):
    return jnp.pad(w, ((0, 0), (0, n - w.shape[1])))


def _pad_rows(w, n):
    return jnp.pad(w, ((0, n - w.shape[0]), (0, 0)))


def _each(fn, *lists):
    return [fn(*xs) for xs in zip(*lists)]


def _wkv_chunk_maps(tiles, consts):
    m0, tri, strict, incl, eye = consts
    c = WKV_CHUNK
    n2 = 2 * c
    r, lw, k2, v, av, bv = (list(x) for x in zip(*tiles))
    stack = lambda x: _stack_heads(x, m0)
    lg = _each(lambda x: _dot_sel_lhs(tri, x), lw)
    lg_end = _each(lambda x: x[c - 1:c, :], lg)
    g_inv = _each(lambda x: jnp.exp(-x), lg)
    g_rem = _each(lambda e, x: jnp.exp(e - x), lg_end, lg)
    rs = _each(lambda x, l: stack(x * jnp.exp(l)), r, lg)
    as_ = _each(lambda x, l, w: stack(x * jnp.exp(l - w)).astype(bf16), av, lg, lw)
    bs = _each(lambda x, g: stack(x * g).astype(bf16), bv, g_inv)
    ks = _each(lambda x, g: stack(x * g).astype(bf16), k2, g_inv)
    bhs = _each(lambda x, g: stack(x * g).astype(bf16), bv, g_rem)
    khs = _each(lambda x, g: stack(x * g).astype(bf16), k2, g_rem)
    vs = _each(stack, v)
    sc = _each(lambda a, rr, b, k: _dot_nt(jnp.concatenate([a, rr.astype(bf16)], axis=0),
                                           jnp.concatenate([b, k], axis=0)), as_, rs, bs, ks)
    zero = jnp.zeros((n2, n2), f32)
    a_ab = _each(lambda s: jnp.where(strict, s[:n2, :n2], zero), sc)
    a_ak = _each(lambda s: jnp.where(strict, s[:n2, n2:], zero), sc)
    a_r = _each(lambda s: jnp.where(jnp.concatenate([incl, incl], axis=1), s[n2:, :],
                                    jnp.zeros((n2, 2 * n2), f32)).astype(bf16), sc)
    t = _each(lambda x: eye + x, a_ab)
    p = a_ab
    for _ in range(c.bit_length() - 2):
        p = _each(lambda x: _dot(x, x), p)
        t = _each(lambda tt, pp: tt + _dot(pp, tt), t, p)
    t = _each(lambda x: x.astype(bf16), t)
    akv = _each(_dot, a_ak, vs)
    ah = _each(_dot, t, as_)
    ws = _each(_dot, t, akv)
    wv = _each(lambda w, x: jnp.concatenate([w, x], axis=0), ws, vs)
    pc = _each(lambda x, ar, h: x + _dot(ar[:, :n2], h), rs, a_r, ah)
    qc = _each(_dot, a_r, wv)
    gm = _each(lambda h, b: _dot(h.T, b), ah, bhs)
    nc = _each(lambda x, b, k: _dot(x.T, jnp.concatenate([b, k], axis=0)), wv, bhs, khs)
    decay = _each(jnp.exp, lg_end)
    return pc, qc, gm, nc, decay


def _wkv_kernel(r_ref, lw_ref, k_ref, v_ref, a_ref, g_ref, kk_ref, ka_ref, rk_ref, lng_ref, lnb_ref,
                o_ref, s_ref):
    c = WKV_CHUNK
    n2 = 2 * c

    @pl.when(pl.program_id(2) == 0)
    def _():
        s_ref[...] = jnp.zeros_like(s_ref)

    ri = lax.broadcasted_iota(jnp.int32, (n2, n2), 0)
    ci = lax.broadcasted_iota(jnp.int32, (n2, n2), 1)
    tr = lax.broadcasted_iota(jnp.int32, (c, c), 0)
    tc = lax.broadcasted_iota(jnp.int32, (c, c), 1)
    consts = (_first_head_mask(), (tr >= tc).astype(bf16),
              (ri & (c - 1)) > (ci & (c - 1)), (ri & (c - 1)) >= (ci & (c - 1)),
              (ri == ci).astype(f32))
    head_ones = _head_ones()
    inv_n = 1.0 / HEAD_DIM

    npairs = WKV_LANES // LANES
    nchunk = WKV_TBLK // c
    lanes = [slice(p * LANES, (p + 1) * LANES) for p in range(npairs)]
    r = [r_ref[:, ln] for ln in lanes]
    v = [v_ref[:, ln] for ln in lanes]
    k2, av, bv = [], [], []
    for ln in lanes:
        k, a = k_ref[:, ln], a_ref[:, ln]
        kk = k * kk_ref[:, ln]
        ss = _dot_sel_rhs(kk * kk, head_ones)
        kk = kk / jnp.maximum(jnp.sqrt(ss), 1e-12)
        k2.append(k * (1.0 + (a - 1.0) * ka_ref[:, ln]))
        av.append(-kk)
        bv.append(kk * a)
    tiles = []
    for ic in range(nchunk):
        rows = slice(ic * c, (ic + 1) * c)
        for p, ln in enumerate(lanes):
            tiles.append((r[p][rows], lw_ref[rows, ln], k2[p][rows], v[p][rows], av[p][rows], bv[p][rows]))
    pc, qc, gm, nc, decay = _wkv_chunk_maps(tiles, consts)
    s = [s_ref[p] for p in range(npairs)]
    ys = [[] for _ in range(npairs)]
    for ic in range(nchunk):
        for p in range(npairs):
            i = ic * npairs + p
            y = _dot_nt(pc[i], s[p]) + qc[i]
            ys[p].append(y[:c] + y[c:])
            s[p] = s[p] * decay[i] + _dot(s[p], gm[i]) + nc[i]
    for p, ln in enumerate(lanes):
        s_ref[p] = s[p]
        y = jnp.concatenate(ys[p], axis=0)
        mu = _dot_sel_rhs(y, head_ones) * inv_n
        yc = y - mu
        var = _dot_sel_rhs(yc * yc, head_ones) * inv_n
        yn = yc * lax.rsqrt(var + GN_EPS) * lng_ref[:, ln] + lnb_ref[:, ln]
        bonus = _dot_sel_rhs(r[p] * k2[p] * rk_ref[:, ln], head_ones) * v[p]
        o_ref[:, ln] = ((yn + bonus) * g_ref[:, ln]).astype(o_ref.dtype)


def _wkv(r, lw, k, v, a, g, k_k, k_a, r_k, lnx_g, lnx_b):
    m, d = r.shape
    tb = SEQ // WKV_TBLK
    tile = pl.BlockSpec((WKV_TBLK, WKV_LANES), lambda b, j, t: (b * tb + t, j))
    row = pl.BlockSpec((1, WKV_LANES), lambda b, j, t: (0, j))
    rows = [x.reshape(1, d) for x in (k_k, k_a, r_k, lnx_g, lnx_b)]
    return pl.pallas_call(
        _wkv_kernel,
        grid=(BATCH, d // WKV_LANES, tb),
        in_specs=[tile] * 6 + [row] * 5,
        out_specs=tile,
        out_shape=jax.ShapeDtypeStruct((m, d), bf16),
        scratch_shapes=[pltpu.VMEM((WKV_LANES // LANES, LANES, LANES), f32)],
        compiler_params=pltpu.CompilerParams(
            dimension_semantics=("parallel", "parallel", "arbitrary"),
            vmem_limit_bytes=VMEM_LIMIT),
        name="wkv7",
    )(r, lw, k, v, a, g, *rows)


def _fox_gate_kernel(s_ref, bf_ref, c_ref, run_ref, *, tc):
    @pl.when(pl.program_id(1) == 0)
    def _():
        run_ref[...] = jnp.zeros_like(run_ref)

    ls = -_softplus(-(s_ref[...] + bf_ref[...]))
    tr = lax.broadcasted_iota(jnp.int32, (tc, tc), 0)
    tcc = lax.broadcasted_iota(jnp.int32, (tc, tc), 1)
    cs = _dot_sel_lhs((tr >= tcc).astype(bf16), ls) + run_ref[...]
    run_ref[...] = cs[tc - 1:tc, :]
    c_ref[...] = cs * LOG2_E


def _fox_gate(small, b_f, tc=512):
    m, w = small.shape
    nt = SEQ // tc
    bias = jnp.pad(b_f, (0, w - b_f.shape[0])).reshape(1, w)
    return pl.pallas_call(
        functools.partial(_fox_gate_kernel, tc=tc),
        grid=(BATCH, nt),
        in_specs=[pl.BlockSpec((tc, w), lambda b, t: (b * nt + t, 0)),
                  pl.BlockSpec((1, w), lambda b, t: (0, 0))],
        out_specs=pl.BlockSpec((tc, w), lambda b, t: (b * nt + t, 0)),
        out_shape=jax.ShapeDtypeStruct((m, w), f32),
        scratch_shapes=[pltpu.VMEM((1, w), f32)],
        compiler_params=pltpu.CompilerParams(dimension_semantics=("parallel", "arbitrary"),
                                             vmem_limit_bytes=VMEM_LIMIT),
        name="fox_gate_cumsum",
    )(small, bias)


def _fox_prep_kernel(q_ref, k_ref, v_ref, kp_ref, vp_ref, s_ref, c_ref, qg_ref, kg_ref,
                     qo_ref, ko_ref, vo_ref, *, tm):
    i = pl.program_id(0)
    j = pl.program_id(1)
    head_ones = _head_ones()
    small = s_ref[...]
    r = lax.broadcasted_iota(jnp.int32, (LANES, LANES), 0)
    col = lax.broadcasted_iota(jnp.int32, (LANES, LANES), 1)
    hd = (col >> HEAD_SHIFT) + HEADS_PER_VREG * j
    ak = jax.nn.sigmoid(_dot_sel_rhs(small, (r == hd + N_HEADS).astype(bf16)))
    av = jax.nn.sigmoid(_dot_sel_rhs(small, (r == hd + 2 * N_HEADS).astype(bf16)))
    seq_start = (i * tm) % SEQ == 0
    rid = lax.broadcasted_iota(jnp.int32, (tm, 1), 0)

    def shifted(x, xp_ref):
        prow = xp_ref[SUBLANES - 1:SUBLANES, :]
        prow = jnp.where(seq_start, jnp.zeros_like(prow), prow)
        return jnp.where(rid == 0, prow, pltpu.roll(x, 1, 0))

    def head_rms(x, gain):
        ms = _dot_sel_rhs(x * x, head_ones) * (1.0 / HEAD_DIM)
        return x * lax.rsqrt(ms + RMS_EPS) * gain

    k = k_ref[...]
    v = v_ref[...]
    k = ak * shifted(k, kp_ref) + (1.0 - ak) * k
    v = av * shifted(v, vp_ref) + (1.0 - av) * v
    qo_ref[...] = (head_rms(q_ref[...], qg_ref[...]) * (HEAD_DIM ** -0.5 * LOG2_E)).astype(qo_ref.dtype)
    aug = jnp.zeros((tm, LANES), f32)
    for n, piece in enumerate(_split3(c_ref[...])):
        sel = jnp.zeros((LANES, LANES), jnp.bool_)
        for h in range(HEADS_PER_VREG):
            sel = sel | ((col == N_SPLIT * h + n) & (r == HEADS_PER_VREG * j + h))
        aug = aug + jnp.dot(piece, sel.astype(bf16), preferred_element_type=f32)
    ko_ref[...] = jnp.concatenate([head_rms(k, kg_ref[...]), aug], axis=1).astype(ko_ref.dtype)
    vo_ref[...] = v.T.astype(vo_ref.dtype)


def _fox_prep(proj, small, c, qn_g, kn_g, tm=1024):
    m = proj.shape[0]
    rb = tm // SUBLANES
    w = small.shape[1]
    tile = lambda cb: pl.BlockSpec((tm, LANES), lambda i, j, cb=cb: (i, cb * N_PAIRS + j))
    prev = lambda cb: pl.BlockSpec(
        (SUBLANES, LANES), lambda i, j, cb=cb: (jnp.maximum(i * rb - 1, 0), cb * N_PAIRS + j))
    gain = pl.BlockSpec((1, LANES), lambda i, j: (0, 0))
    small_tile = pl.BlockSpec((tm, w), lambda i, j: (i, 0))
    tile_gain = lambda x: jnp.tile(x, HEADS_PER_VREG).reshape(1, LANES)
    return pl.pallas_call(
        functools.partial(_fox_prep_kernel, tm=tm),
        grid=(m // tm, N_PAIRS),
        in_specs=[tile(0), tile(1), tile(2), prev(1), prev(2), small_tile, small_tile, gain, gain],
        out_specs=[pl.BlockSpec((tm, LANES), lambda i, j: (i, j)),
                   pl.BlockSpec((tm, 2 * LANES), lambda i, j: (i, j)),
                   pl.BlockSpec((LANES, tm), lambda i, j: (j, i))],
        out_shape=[jax.ShapeDtypeStruct((m, D_MODEL), bf16),
                   jax.ShapeDtypeStruct((m, 2 * D_MODEL), bf16),
                   jax.ShapeDtypeStruct((D_MODEL, m), bf16)],
        compiler_params=pltpu.CompilerParams(dimension_semantics=("parallel", "parallel"),
                                             vmem_limit_bytes=VMEM_LIMIT),
        name="fox_prep",
    )(proj, proj, proj, proj, proj, small, c, tile_gain(qn_g), tile_gain(kn_g))


def _fox_attn_kernel(q_ref, k_ref, vt_ref, gate_ref, og_ref, o_ref):
    tq, tk = ATT_TQ, ATT_TK
    qi = pl.program_id(2)
    npairs = ATT_LANES // LANES
    chains = [(p, h) for p in range(npairs) for h in range(HEADS_PER_VREG)]
    lane = lax.broadcasted_iota(jnp.int32, (tq, LANES), 1)
    q_aug = []
    for p, h in chains:
        q = q_ref[:, p * LANES:(p + 1) * LANES]
        own = (lane >> HEAD_SHIFT) == h
        minus_one = (lane >= N_SPLIT * h) & (lane < N_SPLIT * (h + 1))
        q_aug.append(jnp.concatenate([jnp.where(own, q, jnp.zeros_like(q)),
                                      jnp.where(minus_one, -1.0, 0.0).astype(bf16)], axis=1))

    def step(j, carry, diag):
        m_run, l_run, acc = carry
        keys = pl.ds(pl.multiple_of(j * tk, tk), tk)
        s = [_dot_nt(k_ref[keys, 2 * p * LANES:2 * (p + 1) * LANES], qa)
             for (p, h), qa in zip(chains, q_aug)]
        if diag:
            kidx = j * tk + lax.broadcasted_iota(jnp.int32, (tk, tq), 0)
            qidx = qi * tq + lax.broadcasted_iota(jnp.int32, (tk, tq), 1)
            s = [jnp.where(qidx >= kidx, x, NEG_BIG) for x in s]
        m_new = [jnp.maximum(mr, jnp.max(x, axis=0, keepdims=True)) for mr, x in zip(m_run, s)]
        alpha = [jnp.exp2(mr - mn) for mr, mn in zip(m_run, m_new)]
        pr = [jnp.exp2(x - mn) for x, mn in zip(s, m_new)]
        l_new = [a * lr + jnp.sum(x, axis=0, keepdims=True) for a, lr, x in zip(alpha, l_run, pr)]
        pv = [jnp.dot(vt_ref[pl.ds((p * HEADS_PER_VREG + h) * HEAD_DIM, HEAD_DIM), keys], x.astype(bf16),
                      preferred_element_type=f32) for (p, h), x in zip(chains, pr)]
        acc = [ac * a + x for ac, a, x in zip(acc, alpha, pv)]
        return m_new, l_new, acc

    n = len(chains)
    init = ([jnp.full((1, tq), NEG_BIG, f32)] * n, [jnp.zeros((1, tq), f32)] * n,
            [jnp.zeros((HEAD_DIM, tq), f32)] * n)
    n_full = (qi * tq) // tk
    carry = lax.fori_loop(0, n_full, lambda j, cy: step(j, cy, False), init)
    _, l_run, acc = step(n_full, carry, True)
    o_t = []
    for ac, lr in zip(acc, l_run):
        o = ac / lr
        o_t.append(o * lax.rsqrt(jnp.mean(o * o, axis=0, keepdims=True) + RMS_EPS))
    for p in range(npairs):
        ln = slice(p * LANES, (p + 1) * LANES)
        o = jnp.concatenate(o_t[HEADS_PER_VREG * p:HEADS_PER_VREG * (p + 1)], axis=0).T
        o_ref[:, ln] = (o * og_ref[:, ln] * jax.nn.sigmoid(gate_ref[:, ln])).astype(o_ref.dtype)


def _fox_attn(q, k_aug, v_t, proj, on_g):
    m, d = q.shape
    nq = SEQ // ATT_TQ
    gate_col0 = 3 * D_MODEL // ATT_LANES
    return pl.pallas_call(
        _fox_attn_kernel,
        grid=(BATCH, d // ATT_LANES, nq),
        in_specs=[pl.BlockSpec((ATT_TQ, ATT_LANES), lambda b, p, i: (b * nq + i, p)),
                  pl.BlockSpec((SEQ, 2 * ATT_LANES), lambda b, p, i: (b, p)),
                  pl.BlockSpec((ATT_LANES, SEQ), lambda b, p, i: (p, b)),
                  pl.BlockSpec((ATT_TQ, ATT_LANES), lambda b, p, i: (b * nq + i, gate_col0 + p)),
                  pl.BlockSpec((1, ATT_LANES), lambda b, p, i: (0, p))],
        out_specs=pl.BlockSpec((ATT_TQ, ATT_LANES), lambda b, p, i: (b * nq + i, p)),
        out_shape=jax.ShapeDtypeStruct((m, d), bf16),
        compiler_params=pltpu.CompilerParams(
            dimension_semantics=("parallel", "parallel", "arbitrary"),
            vmem_limit_bytes=VMEM_LIMIT),
        name="fox_attention",
    )(q, k_aug, v_t, proj, on_g.reshape(1, d))


def _swiglu_block(x, norm_g, w_gu, w_d, layer):
    hn = _rmsnorm(x, norm_g, bf16)
    act = _matmul(hn, w_gu, layer, (0, 1), D_FF, tm=2048, tn=512, tk=D_MODEL,
                  epilogue=_epi_swiglu, out_dtype=bf16, name="swiglu_gate_up")
    return _matmul(act, w_d, layer, (0,), D_MODEL, tm=1024, tn=1024, tk=D_FF // 2,
                   epilogue=_epi_residual, extras=((x, "tile"),), out_dtype=f32, name="swiglu_down")


def _proj(x, w, layer=0, *, n_out=None, out_dtype=f32, epilogue=_epi_plain, extras=(), name):
    if w.ndim == 2:
        w = w[None]
    n = w.shape[2] if n_out is None else n_out
    return _matmul(x, w, layer, (0,), n, tm=1024, tn=min(n, 1024), tk=x.shape[1], epilogue=epilogue,
                   extras=extras, out_dtype=out_dtype, name=name)


def _rwkv7_block(x, norm_g, mix, w_rkv, w0, w1, w2, a0, a1, a2, g1, g2, k_k, k_a, r_k, lnx_g, lnx_b, w_o):
    xr, xk, xv, xw, xa, xg = _norm_mix(x, norm_g, mix)
    w_rkv = w_rkv.astype(bf16)
    r = _proj(xr, w_rkv, 0, name="rwkv_r")
    k = _proj(xk, w_rkv, 1, name="rwkv_k")
    v = _proj(xv, w_rkv, 2, name="rwkv_v")
    rank = LANES * pl.cdiv(w1.shape[1], LANES)
    lw = _lora(xw, _pad_cols(w1, rank).astype(bf16), _pad_rows(w2, rank).astype(bf16), w0,
               act=jnp.tanh, epi=_epi_log_decay, name="rwkv_decay")
    a = _lora(xa, _pad_cols(a1, rank).astype(bf16), _pad_rows(a2, rank).astype(bf16), a0,
              act=lambda z: z, epi=_epi_sigmoid_bias, name="rwkv_iclr")
    g = _lora(xg, g1.astype(bf16), g2.astype(bf16), jnp.zeros((D_MODEL,), f32),
              act=jax.nn.sigmoid, epi=_epi_ignore_bias, name="rwkv_gate")
    yg = _wkv(r, lw, k, v, a, g, k_k, k_a, r_k.reshape(-1), lnx_g, lnx_b)
    return _proj(yg, w_o.astype(bf16), epilogue=_epi_residual, extras=((x, "tile"),), name="rwkv_out")


def _fox_block(x, norm_g, w_in, b_f, qn_g, kn_g, on_g, w_o):
    hn = _rmsnorm(x, norm_g, bf16)
    n_main = 4 * D_MODEL
    proj = _proj(hn, w_in.astype(bf16), n_out=n_main, name="fox_in")
    w_small = _pad_cols(w_in[:, n_main:], LANES).astype(bf16)
    small = _proj(hn, w_small, name="fox_in_gates")
    q, k_aug, v_t = _fox_prep(proj, small, _fox_gate(small, b_f), qn_g, kn_g)
    og = _fox_attn(q, k_aug, v_t, proj, on_g)
    return _proj(og, w_o.astype(bf16), epilogue=_epi_residual, extras=((x, "tile"),), name="fox_out")


def kernel(x, a_norm_g, a_mix, a_w_rkv, a_w0, a_w1, a_w2, a_a0, a_a1, a_a2, a_g1, a_g2, a_k_k, a_k_a, a_r_k, a_lnx_g, a_lnx_b, a_w_o, b_norm_g, b_w_in, b_b_f, b_qn_g, b_kn_g, b_on_g, b_w_o, f_norm_g, f_w_gu, f_w_d, final_g):
    b, t, d = x.shape
    h = x.reshape(b * t, d)
    w_gu = f_w_gu.astype(bf16)
    w_d = f_w_d.astype(bf16)
    h = _rwkv7_block(h, a_norm_g[0], a_mix[0], a_w_rkv[0], a_w0[0], a_w1[0], a_w2[0], a_a0[0], a_a1[0],
                     a_a2[0], a_g1[0], a_g2[0], a_k_k[0], a_k_a[0], a_r_k[0], a_lnx_g[0], a_lnx_b[0],
                     a_w_o[0])
    h = _swiglu_block(h, f_norm_g[0], w_gu, w_d, 0)
    h = _fox_block(h, b_norm_g[0], b_w_in[0], b_b_f[0], b_qn_g[0], b_kn_g[0], b_on_g[0], b_w_o[0])
    h = _swiglu_block(h, f_norm_g[1], w_gu, w_d, 1)
    return _rmsnorm(h, final_g, f32).reshape(b, t, d)
```

```python
import functools

import jax
import jax.numpy as jnp
from jax import lax
from jax.experimental import pallas as pl
from jax.experimental.pallas import tpu as pltpu

D_MODEL = 2048
BATCH = 8
SEQ = 2048
N_TOK = BATCH * SEQ
HEAD_DIM = 64
HEAD_SHIFT = HEAD_DIM.bit_length() - 1
N_HEADS = D_MODEL // HEAD_DIM
D_FF = 5632
RMS_EPS = 1e-6
GN_EPS = 64e-5

LANES = 128
LANE_SHIFT = LANES.bit_length() - 1
SUBLANES = 8
HEADS_PER_VREG = LANES // HEAD_DIM
N_PAIRS = D_MODEL // LANES
VMEM_LIMIT = 56 * 1024 * 1024

WKV_CHUNK = 64
WKV_TBLK = 512
WKV_LANES = 512
ATT_TQ = 256
ATT_TK = 512
ATT_LANES = 512
N_SPLIT = 3
NEG_BIG = -1e30
LOG2_E = 1.4426950408889634

f32 = jnp.float32
bf16 = jnp.bfloat16


def _dot(a, b):
    return jnp.dot(a.astype(bf16), b.astype(bf16), preferred_element_type=f32)


def _dot_nt(a, b):
    return lax.dot_general(a.astype(bf16), b.astype(bf16), (((1,), (1,)), ((), ())),
                           preferred_element_type=f32)


def _split3(x):
    hi = x.astype(bf16)
    r1 = x - hi.astype(f32)
    mid = r1.astype(bf16)
    lo = (r1 - mid.astype(f32)).astype(bf16)
    return hi, mid, lo


def _dot_sel_rhs(x, sel):
    hi, mid, lo = _split3(x)
    d = lambda p: jnp.dot(p, sel, preferred_element_type=f32)
    return d(hi) + d(mid) + d(lo)


def _dot_sel_lhs(sel, x):
    hi, mid, lo = _split3(x)
    d = lambda p: jnp.dot(sel, p, preferred_element_type=f32)
    return d(hi) + d(mid) + d(lo)


def _head_sums(x, head_ones):
    hi = x.astype(bf16)
    lo = (x - hi.astype(f32)).astype(bf16)
    return jnp.dot(jnp.concatenate([hi, lo], axis=1), jnp.concatenate([head_ones, head_ones], axis=0),
                   preferred_element_type=f32)


def _head_ones():
    r = lax.broadcasted_iota(jnp.int32, (LANES, LANES), 0) >> HEAD_SHIFT
    c = lax.broadcasted_iota(jnp.int32, (LANES, LANES), 1) >> HEAD_SHIFT
    return (r == c).astype(bf16)


def _first_head_mask():
    return lax.broadcasted_iota(jnp.int32, (1, LANES), 1) < HEAD_DIM


def _stack_heads(x, m0):
    z = jnp.zeros_like(x)
    return jnp.concatenate([jnp.where(m0, x, z), jnp.where(m0, z, x)], axis=0)


def _softplus(z):
    return jnp.maximum(z, 0.0) + jnp.log(1.0 + jnp.exp(-jnp.abs(z)))


def _rms(x, g):
    return x * lax.rsqrt(jnp.mean(x * x, axis=-1, keepdims=True) + RMS_EPS) * g


def _rmsnorm_kernel(x_ref, g_ref, o_ref):
    o_ref[...] = _rms(x_ref[...], g_ref[...]).astype(o_ref.dtype)


def _rmsnorm(x, g, out_dtype, tm=512):
    m, d = x.shape
    return pl.pallas_call(
        _rmsnorm_kernel,
        grid=(m // tm,),
        in_specs=[pl.BlockSpec((tm, d), lambda i: (i, 0)),
                  pl.BlockSpec((1, d), lambda i: (0, 0))],
        out_specs=pl.BlockSpec((tm, d), lambda i: (i, 0)),
        out_shape=jax.ShapeDtypeStruct((m, d), out_dtype),
        compiler_params=pltpu.CompilerParams(dimension_semantics=("parallel",),
                                             vmem_limit_bytes=VMEM_LIMIT),
        name="rmsnorm",
    )(x, g.reshape(1, d))


def _norm_mix_kernel(x_ref, xp_ref, g_ref, mix_ref, *o_refs, tm):
    i = pl.program_id(0)
    g = g_ref[...]
    h = _rms(x_ref[...], g)
    hp_row = _rms(xp_ref[...], g)[SUBLANES - 1:SUBLANES, :]
    seq_start = (i * tm) % SEQ == 0
    hp_row = jnp.where(seq_start, jnp.zeros_like(hp_row), hp_row)
    rid = lax.broadcasted_iota(jnp.int32, (tm, 1), 0)
    hprev = jnp.where(rid == 0, hp_row, pltpu.roll(h, 1, 0))
    xx = hprev - h
    for p, o_ref in enumerate(o_refs):
        o_ref[...] = (h + xx * mix_ref[p:p + 1, :]).astype(o_ref.dtype)


def _norm_mix(x, g, mix, tm=256):
    m, d = x.shape
    rb = tm // SUBLANES
    out = jax.ShapeDtypeStruct((m, d), bf16)
    return pl.pallas_call(
        functools.partial(_norm_mix_kernel, tm=tm),
        grid=(m // tm,),
        in_specs=[pl.BlockSpec((tm, d), lambda i: (i, 0)),
                  pl.BlockSpec((SUBLANES, d), lambda i: (jnp.maximum(i * rb - 1, 0), 0)),
                  pl.BlockSpec((1, d), lambda i: (0, 0)),
                  pl.BlockSpec((6, d), lambda i: (0, 0))],
        out_specs=[pl.BlockSpec((tm, d), lambda i: (i, 0))] * 6,
        out_shape=[out] * 6,
        compiler_params=pltpu.CompilerParams(dimension_semantics=("parallel",),
                                             vmem_limit_bytes=VMEM_LIMIT),
        name="norm_mix",
    )(x, x, g.reshape(1, d), mix)


def _mm_kernel(*refs, n_w, n_e, nk, epilogue):
    x_ref = refs[0]
    w_refs = refs[1:1 + n_w]
    e_refs = refs[1 + n_w:1 + n_w + n_e]
    o_ref = refs[1 + n_w + n_e]
    acc_refs = refs[2 + n_w + n_e:]
    x = x_ref[...]
    if nk == 1:
        accs = [jnp.dot(x, w[...], preferred_element_type=f32) for w in w_refs]
        o_ref[...] = epilogue(accs, [e[...] for e in e_refs]).astype(o_ref.dtype)
        return
    k = pl.program_id(2)

    @pl.when(k == 0)
    def _():
        for a in acc_refs:
            a[...] = jnp.zeros_like(a)

    for a, w in zip(acc_refs, w_refs):
        a[...] += jnp.dot(x, w[...], preferred_element_type=f32)

    @pl.when(k == nk - 1)
    def _():
        o_ref[...] = epilogue([a[...] for a in acc_refs],
                              [e[...] for e in e_refs]).astype(o_ref.dtype)


def _matmul(x, w, layer, w_col_blocks, n_out, *, tm, tn, tk, epilogue, extras=(), out_dtype, name):
    m, kdim = x.shape
    nk = kdim // tk
    n_w = len(w_col_blocks)
    nb = n_out // tn
    in_specs = [pl.BlockSpec((tm, tk), lambda i, j, k: (i, k))]
    args = [x]
    for cb in w_col_blocks:
        in_specs.append(pl.BlockSpec((None, tk, tn), lambda i, j, k, cb=cb: (layer, k, cb * nb + j)))
        args.append(w)
    for arr, kind in extras:
        if kind == "row":
            in_specs.append(pl.BlockSpec((1, tn), lambda i, j, k: (0, j)))
        else:
            in_specs.append(pl.BlockSpec((tm, tn), lambda i, j, k: (i, j)))
        args.append(arr)
    scratch = [pltpu.VMEM((tm, tn), f32) for _ in range(n_w)] if nk > 1 else []
    return pl.pallas_call(
        functools.partial(_mm_kernel, n_w=n_w, n_e=len(extras), nk=nk, epilogue=epilogue),
        grid=(m // tm, nb, nk),
        in_specs=in_specs,
        out_specs=pl.BlockSpec((tm, tn), lambda i, j, k: (i, j)),
        out_shape=jax.ShapeDtypeStruct((m, n_out), out_dtype),
        scratch_shapes=scratch,
        compiler_params=pltpu.CompilerParams(
            dimension_semantics=("parallel", "parallel", "arbitrary"),
            vmem_limit_bytes=VMEM_LIMIT),
        name=name,
    )(*args)


def _epi_plain(accs, extras):
    return accs[0]


def _epi_residual(accs, extras):
    return extras[0] + accs[0]


def _epi_swiglu(accs, extras):
    gate, up = accs
    return gate * jax.nn.sigmoid(gate) * up


def _lora_kernel(x_ref, w1_ref, w2_ref, b_ref, o_ref, *, act, epi):
    z = jnp.dot(x_ref[...], w1_ref[...], preferred_element_type=f32)
    y = jnp.dot(act(z).astype(bf16), w2_ref[...], preferred_element_type=f32)
    o_ref[...] = epi(y, b_ref[...]).astype(o_ref.dtype)


def _lora(x, w1, w2, bias, *, act, epi, name, tm=512):
    m, d = x.shape
    r = w1.shape[1]
    n = w2.shape[1]
    return pl.pallas_call(
        functools.partial(_lora_kernel, act=act, epi=epi),
        grid=(m // tm,),
        in_specs=[pl.BlockSpec((tm, d), lambda i: (i, 0)),
                  pl.BlockSpec((d, r), lambda i: (0, 0)),
                  pl.BlockSpec((r, n), lambda i: (0, 0)),
                  pl.BlockSpec((1, n), lambda i: (0, 0))],
        out_specs=pl.BlockSpec((tm, n), lambda i: (i, 0)),
        out_shape=jax.ShapeDtypeStruct((m, n), f32),
        compiler_params=pltpu.CompilerParams(dimension_semantics=("parallel",),
                                             vmem_limit_bytes=VMEM_LIMIT),
        name=name,
    )(x, w1, w2, bias.reshape(1, n))


def _epi_log_decay(y, w0):
    return -jnp.exp(-_softplus(-(w0 + y)) - 0.5)


def _epi_sigmoid_bias(y, a0):
    return jax.nn.sigmoid(a0 + y)


def _epi_ignore_bias(y, b):
    return y


def _pad_cols(w, n):
    return jnp.pad(w, ((0, 0), (0, n - w.shape[1])))


def _pad_rows(w, n):
    return jnp.pad(w, ((0, n - w.shape[0]), (0, 0)))


def _each(fn, *lists):
    return [fn(*xs) for xs in zip(*lists)]


def _wkv_chunk_maps(tiles, consts):
    m0, strict, incl, eye = consts
    c = WKV_CHUNK
    n2 = 2 * c
    r, lw, lg, k2, v, av, bv = (list(x) for x in zip(*tiles))
    stack = lambda x: _stack_heads(x, m0)
    lg_end = _each(lambda x: x[c - 1:c, :], lg)
    g_inv = _each(lambda x: jnp.exp(-x), lg)
    g_rem = _each(lambda e, x: jnp.exp(e - x), lg_end, lg)
    rs = _each(lambda x, l: stack(x * jnp.exp(l)), r, lg)
    as_ = _each(lambda x, l, w: stack(x * jnp.exp(l - w)).astype(bf16), av, lg, lw)
    bs = _each(lambda x, g: stack(x * g).astype(bf16), bv, g_inv)
    ks = _each(lambda x, g: stack(x * g).astype(bf16), k2, g_inv)
    bhs = _each(lambda x, g: stack(x * g).astype(bf16), bv, g_rem)
    khs = _each(lambda x, g: stack(x * g).astype(bf16), k2, g_rem)
    vs = _each(stack, v)
    sc = _each(lambda a, rr, b, k: _dot_nt(jnp.concatenate([a, rr.astype(bf16)], axis=0),
                                           jnp.concatenate([b, k], axis=0)), as_, rs, bs, ks)
    zero = jnp.zeros((n2, n2), f32)
    a_ab = _each(lambda s: jnp.where(strict, s[:n2, :n2], zero), sc)
    a_ak = _each(lambda s: jnp.where(strict, s[:n2, n2:], zero), sc)
    a_r = _each(lambda s: jnp.where(jnp.concatenate([incl, incl], axis=1), s[n2:, :],
                                    jnp.zeros((n2, 2 * n2), f32)).astype(bf16), sc)
    side = lambda a, b: jnp.concatenate([a, b], axis=1)
    t = _each(lambda x: eye + x, a_ab)
    p = _each(lambda x: _dot(x, x), a_ab)
    for _ in range(c.bit_length() - 3):
        pt = _each(lambda pp, tt: _dot(pp, side(pp, tt)), p, t)
        p = _each(lambda x: x[:, :n2], pt)
        t = _each(lambda tt, x: tt + x[:, n2:], t, pt)
    t = _each(lambda tt, pp: (tt + _dot(pp, tt)).astype(bf16), t, p)
    akv = _each(_dot, a_ak, vs)
    hw = _each(lambda tt, a, x: _dot(tt, side(a, x)), t, as_, akv)
    ah = _each(lambda x: x[:, :LANES], hw)
    ws = _each(lambda x: x[:, LANES:], hw)
    pq = _each(lambda ar, x, vv: _dot(ar, jnp.concatenate([x, side(jnp.zeros_like(vv), vv)], axis=0)),
               a_r, hw, vs)
    pc = _each(lambda x, y: x + y[:, :LANES], rs, pq)
    qc = _each(lambda y: y[:, LANES:], pq)
    gm = _each(lambda h, b: _dot(h.T, b), ah, bhs)
    nc = _each(lambda w, x, b, k: _dot(jnp.concatenate([w, x], axis=0).T, jnp.concatenate([b, k], axis=0)),
               ws, vs, bhs, khs)
    decay = _each(jnp.exp, lg_end)
    return pc, qc, gm, nc, decay


def _wkv_kernel(r_ref, lw_ref, k_ref, v_ref, a_ref, g_ref, kk_ref, ka_ref, rk_ref, lng_ref, lnb_ref,
                o_ref, s_ref):
    c = WKV_CHUNK
    n2 = 2 * c

    @pl.when(pl.program_id(2) == 0)
    def _():
        s_ref[...] = jnp.zeros_like(s_ref)

    ri = lax.broadcasted_iota(jnp.int32, (n2, n2), 0)
    ci = lax.broadcasted_iota(jnp.int32, (n2, n2), 1)
    tr = lax.broadcasted_iota(jnp.int32, (c, c), 0)
    tc = lax.broadcasted_iota(jnp.int32, (c, c), 1)
    consts = (_first_head_mask(), (ri & (c - 1)) > (ci & (c - 1)), (ri & (c - 1)) >= (ci & (c - 1)),
              (ri == ci).astype(f32))
    tri = (tr >= tc).astype(bf16)
    head_ones = _head_ones()
    inv_n = 1.0 / HEAD_DIM

    npairs = WKV_LANES // LANES
    nchunk = WKV_TBLK // c
    lanes = [slice(p * LANES, (p + 1) * LANES) for p in range(npairs)]
    r = [r_ref[:, ln] for ln in lanes]
    v = [v_ref[:, ln] for ln in lanes]
    k2, av, bv = [], [], []
    for ln in lanes:
        k, a = k_ref[:, ln], a_ref[:, ln]
        kk = k * kk_ref[:, ln]
        ss = _head_sums(kk * kk, head_ones)
        kk = kk / jnp.maximum(jnp.sqrt(ss), 1e-12)
        k2.append(k * (1.0 + (a - 1.0) * ka_ref[:, ln]))
        av.append(-kk)
        bv.append(kk * a)
    tiles = []
    for ic in range(nchunk):
        rows = slice(ic * c, (ic + 1) * c)
        lw = lw_ref[rows, :]
        lg = _dot_sel_lhs(tri, lw)
        for p, ln in enumerate(lanes):
            tiles.append((r[p][rows], lw[:, ln], lg[:, ln], k2[p][rows], v[p][rows], av[p][rows], bv[p][rows]))
    pc, qc, gm, nc, decay = _wkv_chunk_maps(tiles, consts)
    s = [s_ref[p] for p in range(npairs)]
    ys = [[] for _ in range(npairs)]
    for ic in range(nchunk):
        for p in range(npairs):
            i = ic * npairs + p
            y = _dot_nt(pc[i], s[p]) + qc[i]
            ys[p].append(y[:c] + y[c:])
            s[p] = s[p] * decay[i] + _dot(s[p], gm[i]) + nc[i]
    for p, ln in enumerate(lanes):
        s_ref[p] = s[p]
        y = jnp.concatenate(ys[p], axis=0)
        mu = _head_sums(y, head_ones) * inv_n
        yc = y - mu
        var = _head_sums(yc * yc, head_ones) * inv_n
        yn = yc * lax.rsqrt(var + GN_EPS) * lng_ref[:, ln] + lnb_ref[:, ln]
        bonus = _head_sums(r[p] * k2[p] * rk_ref[:, ln], head_ones) * v[p]
        o_ref[:, ln] = ((yn + bonus) * g_ref[:, ln]).astype(o_ref.dtype)


def _wkv(r, lw, k, v, a, g, k_k, k_a, r_k, lnx_g, lnx_b):
    m, d = r.shape
    tb = SEQ // WKV_TBLK
    tile = pl.BlockSpec((WKV_TBLK, WKV_LANES), lambda b, j, t: (b * tb + t, j))
    row = pl.BlockSpec((1, WKV_LANES), lambda b, j, t: (0, j))
    rows = [x.reshape(1, d) for x in (k_k, k_a, r_k, lnx_g, lnx_b)]
    return pl.pallas_call(
        _wkv_kernel,
        grid=(BATCH, d // WKV_LANES, tb),
        in_specs=[tile] * 6 + [row] * 5,
        out_specs=tile,
        out_shape=jax.ShapeDtypeStruct((m, d), bf16),
        scratch_shapes=[pltpu.VMEM((WKV_LANES // LANES, LANES, LANES), f32)],
        compiler_params=pltpu.CompilerParams(
            dimension_semantics=("parallel", "parallel", "arbitrary"),
            vmem_limit_bytes=VMEM_LIMIT),
        name="wkv7",
    )(r, lw, k, v, a, g, *rows)


def _fox_gate_kernel(s_ref, bf_ref, cp_ref, sp_ref, run_ref, *, tc):
    @pl.when(pl.program_id(1) == 0)
    def _():
        run_ref[...] = jnp.zeros_like(run_ref)

    small = s_ref[...]
    ls = -_softplus(-(small + bf_ref[...]))
    tr = lax.broadcasted_iota(jnp.int32, (tc, tc), 0)
    tcc = lax.broadcasted_iota(jnp.int32, (tc, tc), 1)
    cs = _dot_sel_lhs((tr >= tcc).astype(bf16), ls) + run_ref[...]
    run_ref[...] = cs[tc - 1:tc, :]
    cp_ref[...] = jnp.concatenate(_split3(cs * LOG2_E), axis=1)
    sp_ref[...] = jnp.concatenate(_split3(small), axis=1)


def _fox_gate(small, b_f, tc=512):
    m, w = small.shape
    nt = SEQ // tc
    bias = jnp.pad(b_f, (0, w - b_f.shape[0])).reshape(1, w)
    pieces = pl.BlockSpec((tc, N_SPLIT * w), lambda b, t: (b * nt + t, 0))
    return pl.pallas_call(
        functools.partial(_fox_gate_kernel, tc=tc),
        grid=(BATCH, nt),
        in_specs=[pl.BlockSpec((tc, w), lambda b, t: (b * nt + t, 0)),
                  pl.BlockSpec((1, w), lambda b, t: (0, 0))],
        out_specs=[pieces, pieces],
        out_shape=[jax.ShapeDtypeStruct((m, N_SPLIT * w), bf16)] * 2,
        scratch_shapes=[pltpu.VMEM((1, w), f32)],
        compiler_params=pltpu.CompilerParams(dimension_semantics=("parallel", "arbitrary"),
                                             vmem_limit_bytes=VMEM_LIMIT),
        name="fox_gate_cumsum",
    )(small, bias)


def _fox_prep_kernel(q_ref, k_ref, v_ref, kp_ref, vp_ref, s_ref, c_ref, qg_ref, kg_ref,
                     qo_ref, ko_ref, vo_ref, *, tm):
    i = pl.program_id(0)
    j = pl.program_id(1)
    head_ones = _head_ones()
    r = lax.broadcasted_iota(jnp.int32, (LANES, 2 * LANES), 0)
    col = lax.broadcasted_iota(jnp.int32, (LANES, 2 * LANES), 1)
    src = ((col & (LANES - 1)) >> HEAD_SHIFT) + HEADS_PER_VREG * j + N_HEADS * (1 + (col >> LANE_SHIFT))
    sel = (r == src).astype(bf16)
    logits = jnp.dot(s_ref[...], jnp.concatenate([sel] * N_SPLIT, axis=0), preferred_element_type=f32)
    ak = jax.nn.sigmoid(logits[:, :LANES])
    av = jax.nn.sigmoid(logits[:, LANES:])
    seq_start = (i * tm) % SEQ == 0
    rid = lax.broadcasted_iota(jnp.int32, (tm, 1), 0)

    def shifted(x, xp_ref):
        prow = xp_ref[SUBLANES - 1:SUBLANES, :]
        prow = jnp.where(seq_start, jnp.zeros_like(prow), prow)
        return jnp.where(rid == 0, prow, pltpu.roll(x, 1, 0))

    def head_rms(x, gain):
        ms = _head_sums(x * x, head_ones) * (1.0 / HEAD_DIM)
        return x * lax.rsqrt(ms + RMS_EPS) * gain

    k = k_ref[...]
    v = v_ref[...]
    k = ak * shifted(k, kp_ref) + (1.0 - ak) * k
    v = av * shifted(v, vp_ref) + (1.0 - av) * v
    qo_ref[...] = (head_rms(q_ref[...], qg_ref[...]) * (HEAD_DIM ** -0.5 * LOG2_E)).astype(qo_ref.dtype)
    ra = lax.broadcasted_iota(jnp.int32, (LANES, LANES), 0)
    ca = lax.broadcasted_iota(jnp.int32, (LANES, LANES), 1)
    sel_aug = []
    for n in range(N_SPLIT):
        hit = jnp.zeros((LANES, LANES), jnp.bool_)
        for h in range(HEADS_PER_VREG):
            hit = hit | ((ca == N_SPLIT * h + n) & (ra == HEADS_PER_VREG * j + h))
        sel_aug.append(hit.astype(bf16))
    aug = jnp.dot(c_ref[...], jnp.concatenate(sel_aug, axis=0), preferred_element_type=f32)
    ko_ref[...] = jnp.concatenate([head_rms(k, kg_ref[...]), aug], axis=1).astype(ko_ref.dtype)
    vo_ref[...] = v.T.astype(vo_ref.dtype)


def _fox_prep(proj, small_pieces, c_pieces, qn_g, kn_g, tm=1024):
    m = proj.shape[0]
    rb = tm // SUBLANES
    w = small_pieces.shape[1]
    tile = lambda cb: pl.BlockSpec((tm, LANES), lambda i, j, cb=cb: (i, cb * N_PAIRS + j))
    prev = lambda cb: pl.BlockSpec(
        (SUBLANES, LANES), lambda i, j, cb=cb: (jnp.maximum(i * rb - 1, 0), cb * N_PAIRS + j))
    gain = pl.BlockSpec((1, LANES), lambda i, j: (0, 0))
    small_tile = pl.BlockSpec((tm, w), lambda i, j: (i, 0))
    tile_gain = lambda x: jnp.tile(x, HEADS_PER_VREG).reshape(1, LANES)
    return pl.pallas_call(
        functools.partial(_fox_prep_kernel, tm=tm),
        grid=(m // tm, N_PAIRS),
        in_specs=[tile(0), tile(1), tile(2), prev(1), prev(2), small_tile, small_tile, gain, gain],
        out_specs=[pl.BlockSpec((tm, LANES), lambda i, j: (i, j)),
                   pl.BlockSpec((tm, 2 * LANES), lambda i, j: (i, j)),
                   pl.BlockSpec((LANES, tm), lambda i, j: (j, i))],
        out_shape=[jax.ShapeDtypeStruct((m, D_MODEL), bf16),
                   jax.ShapeDtypeStruct((m, 2 * D_MODEL), bf16),
                   jax.ShapeDtypeStruct((D_MODEL, m), bf16)],
        compiler_params=pltpu.CompilerParams(dimension_semantics=("parallel", "parallel"),
                                             vmem_limit_bytes=VMEM_LIMIT),
        name="fox_prep",
    )(proj, proj, proj, proj, proj, small_pieces, c_pieces, tile_gain(qn_g), tile_gain(kn_g))


def _fox_attn_kernel(q_ref, k_ref, vt_ref, gate_ref, og_ref, o_ref):
    tq, tk = ATT_TQ, ATT_TK
    qi = pl.program_id(2)
    npairs = ATT_LANES // LANES
    chains = [(p, h) for p in range(npairs) for h in range(HEADS_PER_VREG)]
    lane = lax.broadcasted_iota(jnp.int32, (tq, LANES), 1)
    q_aug = []
    for p, h in chains:
        q = q_ref[:, p * LANES:(p + 1) * LANES]
        own = (lane >> HEAD_SHIFT) == h
        minus_one = (lane >= N_SPLIT * h) & (lane < N_SPLIT * (h + 1))
        q_aug.append(jnp.concatenate([jnp.where(own, q, jnp.zeros_like(q)),
                                      jnp.where(minus_one, -1.0, 0.0).astype(bf16)], axis=1))

    def step(j, carry, diag):
        m_run, l_run, acc = carry
        keys = pl.ds(pl.multiple_of(j * tk, tk), tk)
        s = [_dot_nt(k_ref[keys, 2 * p * LANES:2 * (p + 1) * LANES], qa)
             for (p, h), qa in zip(chains, q_aug)]
        if diag:
            kidx = j * tk + lax.broadcasted_iota(jnp.int32, (tk, tq), 0)
            qidx = qi * tq + lax.broadcasted_iota(jnp.int32, (tk, tq), 1)
            s = [jnp.where(qidx >= kidx, x, NEG_BIG) for x in s]
        m_new = [jnp.maximum(mr, jnp.max(x, axis=0, keepdims=True)) for mr, x in zip(m_run, s)]
        alpha = [jnp.exp2(mr - mn) for mr, mn in zip(m_run, m_new)]
        pr = [jnp.exp2(x - mn) for x, mn in zip(s, m_new)]
        l_new = [a * lr + jnp.sum(x, axis=0, keepdims=True) for a, lr, x in zip(alpha, l_run, pr)]
        pv = [jnp.dot(vt_ref[pl.ds((p * HEADS_PER_VREG + h) * HEAD_DIM, HEAD_DIM), keys], x.astype(bf16),
                      preferred_element_type=f32) for (p, h), x in zip(chains, pr)]
        acc = [ac * a + x for ac, a, x in zip(acc, alpha, pv)]
        return m_new, l_new, acc

    n = len(chains)
    init = ([jnp.full((1, tq), NEG_BIG, f32)] * n, [jnp.zeros((1, tq), f32)] * n,
            [jnp.zeros((HEAD_DIM, tq), f32)] * n)
    n_full = (qi * tq) // tk
    carry = lax.fori_loop(0, n_full, lambda j, cy: step(j, cy, False), init)
    _, l_run, acc = step(n_full, carry, True)
    o_t = []
    for ac, lr in zip(acc, l_run):
        o = ac / lr
        o_t.append(o * lax.rsqrt(jnp.mean(o * o, axis=0, keepdims=True) + RMS_EPS))
    for p in range(npairs):
        ln = slice(p * LANES, (p + 1) * LANES)
        o = jnp.concatenate(o_t[HEADS_PER_VREG * p:HEADS_PER_VREG * (p + 1)], axis=0).T
        o_ref[:, ln] = (o * og_ref[:, ln] * jax.nn.sigmoid(gate_ref[:, ln])).astype(o_ref.dtype)


def _fox_attn(q, k_aug, v_t, proj, on_g):
    m, d = q.shape
    nq = SEQ // ATT_TQ
    gate_col0 = 3 * D_MODEL // ATT_LANES
    return pl.pallas_call(
        _fox_attn_kernel,
        grid=(BATCH, d // ATT_LANES, nq),
        in_specs=[pl.BlockSpec((ATT_TQ, ATT_LANES), lambda b, p, i: (b * nq + i, p)),
                  pl.BlockSpec((SEQ, 2 * ATT_LANES), lambda b, p, i: (b, p)),
                  pl.BlockSpec((ATT_LANES, SEQ), lambda b, p, i: (p, b)),
                  pl.BlockSpec((ATT_TQ, ATT_LANES), lambda b, p, i: (b * nq + i, gate_col0 + p)),
                  pl.BlockSpec((1, ATT_LANES), lambda b, p, i: (0, p))],
        out_specs=pl.BlockSpec((ATT_TQ, ATT_LANES), lambda b, p, i: (b * nq + i, p)),
        out_shape=jax.ShapeDtypeStruct((m, d), bf16),
        compiler_params=pltpu.CompilerParams(
            dimension_semantics=("parallel", "parallel", "arbitrary"),
            vmem_limit_bytes=VMEM_LIMIT),
        name="fox_attention",
    )(q, k_aug, v_t, proj, on_g.reshape(1, d))


def _swiglu_block(x, norm_g, w_gu, w_d, layer):
    hn = _rmsnorm(x, norm_g, bf16)
    act = _matmul(hn, w_gu, layer, (0, 1), D_FF, tm=2048, tn=512, tk=D_MODEL,
                  epilogue=_epi_swiglu, out_dtype=bf16, name="swiglu_gate_up")
    return _matmul(act, w_d, layer, (0,), D_MODEL, tm=1024, tn=1024, tk=D_FF // 2,
                   epilogue=_epi_residual, extras=((x, "tile"),), out_dtype=f32, name="swiglu_down")


def _proj(x, w, layer=0, *, n_out=None, out_dtype=f32, epilogue=_epi_plain, extras=(), name):
    if w.ndim == 2:
        w = w[None]
    n = w.shape[2] if n_out is None else n_out
    return _matmul(x, w, layer, (0,), n, tm=1024, tn=min(n, 1024), tk=x.shape[1], epilogue=epilogue,
                   extras=extras, out_dtype=out_dtype, name=name)


def _rwkv7_block(x, norm_g, mix, w_rkv, w0, w1, w2, a0, a1, a2, g1, g2, k_k, k_a, r_k, lnx_g, lnx_b, w_o):
    xr, xk, xv, xw, xa, xg = _norm_mix(x, norm_g, mix)
    w_rkv = w_rkv.astype(bf16)
    r = _proj(xr, w_rkv, 0, name="rwkv_r")
    k = _proj(xk, w_rkv, 1, name="rwkv_k")
    v = _proj(xv, w_rkv, 2, name="rwkv_v")
    rank = LANES * pl.cdiv(w1.shape[1], LANES)
    lw = _lora(xw, _pad_cols(w1, rank).astype(bf16), _pad_rows(w2, rank).astype(bf16), w0,
               act=jnp.tanh, epi=_epi_log_decay, name="rwkv_decay")
    a = _lora(xa, _pad_cols(a1, rank).astype(bf16), _pad_rows(a2, rank).astype(bf16), a0,
              act=lambda z: z, epi=_epi_sigmoid_bias, name="rwkv_iclr")
    g = _lora(xg, g1.astype(bf16), g2.astype(bf16), jnp.zeros((D_MODEL,), f32),
              act=jax.nn.sigmoid, epi=_epi_ignore_bias, name="rwkv_gate")
    yg = _wkv(r, lw, k, v, a, g, k_k, k_a, r_k.reshape(-1), lnx_g, lnx_b)
    return _proj(yg, w_o.astype(bf16), epilogue=_epi_residual, extras=((x, "tile"),), name="rwkv_out")


def _fox_block(x, norm_g, w_in, b_f, qn_g, kn_g, on_g, w_o):
    hn = _rmsnorm(x, norm_g, bf16)
    n_main = 4 * D_MODEL
    proj = _proj(hn, w_in.astype(bf16), n_out=n_main, name="fox_in")
    w_small = _pad_cols(w_in[:, n_main:], LANES).astype(bf16)
    small = _proj(hn, w_small, name="fox_in_gates")
    c_pieces, small_pieces = _fox_gate(small, b_f)
    q, k_aug, v_t = _fox_prep(proj, small_pieces, c_pieces, qn_g, kn_g)
    og = _fox_attn(q, k_aug, v_t, proj, on_g)
    return _proj(og, w_o.astype(bf16), epilogue=_epi_residual, extras=((x, "tile"),), name="fox_out")


def kernel(x, a_norm_g, a_mix, a_w_rkv, a_w0, a_w1, a_w2, a_a0, a_a1, a_a2, a_g1, a_g2, a_k_k, a_k_a, a_r_k, a_lnx_g, a_lnx_b, a_w_o, b_norm_g, b_w_in, b_b_f, b_qn_g, b_kn_g, b_on_g, b_w_o, f_norm_g, f_w_gu, f_w_d, final_g):
    b, t, d = x.shape
    h = x.reshape(b * t, d)
    w_gu = f_w_gu.astype(bf16)
    w_d = f_w_d.astype(bf16)
    h = _rwkv7_block(h, a_norm_g[0], a_mix[0], a_w_rkv[0], a_w0[0], a_w1[0], a_w2[0], a_a0[0], a_a1[0],
                     a_a2[0], a_g1[0], a_g2[0], a_k_k[0], a_k_a[0], a_r_k[0], a_lnx_g[0], a_lnx_b[0],
                     a_w_o[0])
    h = _swiglu_block(h, f_norm_g[0], w_gu, w_d, 0)
    h = _fox_block(h, b_norm_g[0], b_w_in[0], b_b_f[0], b_qn_g[0], b_kn_g[0], b_on_g[0], b_w_o[0])
    h = _swiglu_block(h, f_norm_g[1], w_gu, w_d, 1)
    return _rmsnorm(h, final_g, f32).reshape(b, t, d)
```

```python
import functools

import jax
import jax.numpy as jnp
from jax import lax
from jax.experimental import pallas as pl
from jax.experimental.pallas import tpu as pltpu

D_MODEL = 2048
BATCH = 8
SEQ = 2048
N_TOK = BATCH * SEQ
HEAD_DIM = 64
HEAD_SHIFT = HEAD_DIM.bit_length() - 1
N_HEADS = D_MODEL // HEAD_DIM
D_FF = 5632
RMS_EPS = 1e-6
GN_EPS = 64e-5

LANES = 128
LANE_SHIFT = LANES.bit_length() - 1
SUBLANES = 8
HEADS_PER_VREG = LANES // HEAD_DIM
N_PAIRS = D_MODEL // LANES
VMEM_LIMIT = 56 * 1024 * 1024

WKV_CHUNK = 64
WKV_TBLK = 512
WKV_LANES = 512
ATT_TQ = 256
ATT_TK = 512
ATT_LANES = 512
PREP_LANES = 512
N_SPLIT = 3
NEG_BIG = -1e30
LOG2_E = 1.4426950408889634

f32 = jnp.float32
bf16 = jnp.bfloat16


def _dot(a, b):
    return jnp.dot(a.astype(bf16), b.astype(bf16), preferred_element_type=f32)


def _dot_nt(a, b):
    return lax.dot_general(a.astype(bf16), b.astype(bf16), (((1,), (1,)), ((), ())),
                           preferred_element_type=f32)


def _split3(x):
    hi = x.astype(bf16)
    r1 = x - hi.astype(f32)
    mid = r1.astype(bf16)
    lo = (r1 - mid.astype(f32)).astype(bf16)
    return hi, mid, lo


def _dot_sel_rhs(x, sel):
    hi, mid, lo = _split3(x)
    d = lambda p: jnp.dot(p, sel, preferred_element_type=f32)
    return d(hi) + d(mid) + d(lo)


def _dot_sel_lhs(sel, x):
    hi, mid, lo = _split3(x)
    d = lambda p: jnp.dot(sel, p, preferred_element_type=f32)
    return d(hi) + d(mid) + d(lo)


def _head_sums(x, head_ones):
    hi = x.astype(bf16)
    lo = (x - hi.astype(f32)).astype(bf16)
    return jnp.dot(jnp.concatenate([hi, lo], axis=1), jnp.concatenate([head_ones, head_ones], axis=0),
                   preferred_element_type=f32)


def _head_ones():
    r = lax.broadcasted_iota(jnp.int32, (LANES, LANES), 0) >> HEAD_SHIFT
    c = lax.broadcasted_iota(jnp.int32, (LANES, LANES), 1) >> HEAD_SHIFT
    return (r == c).astype(bf16)


def _first_head_mask():
    return lax.broadcasted_iota(jnp.int32, (1, LANES), 1) < HEAD_DIM


def _stack_heads(x, m0):
    z = jnp.zeros_like(x)
    return jnp.concatenate([jnp.where(m0, x, z), jnp.where(m0, z, x)], axis=0)


def _softplus(z):
    return jnp.maximum(z, 0.0) + jnp.log(1.0 + jnp.exp(-jnp.abs(z)))


def _rms(x, g):
    return x * lax.rsqrt(jnp.mean(x * x, axis=-1, keepdims=True) + RMS_EPS) * g


def _rmsnorm_kernel(x_ref, g_ref, o_ref):
    o_ref[...] = _rms(x_ref[...], g_ref[...]).astype(o_ref.dtype)


def _rmsnorm(x, g, out_dtype, tm=512):
    m, d = x.shape
    return pl.pallas_call(
        _rmsnorm_kernel,
        grid=(m // tm,),
        in_specs=[pl.BlockSpec((tm, d), lambda i: (i, 0)),
                  pl.BlockSpec((1, d), lambda i: (0, 0))],
        out_specs=pl.BlockSpec((tm, d), lambda i: (i, 0)),
        out_shape=jax.ShapeDtypeStruct((m, d), out_dtype),
        compiler_params=pltpu.CompilerParams(dimension_semantics=("parallel",),
                                             vmem_limit_bytes=VMEM_LIMIT),
        name="rmsnorm",
    )(x, g.reshape(1, d))


def _norm_mix_kernel(x_ref, xp_ref, g_ref, mix_ref, *o_refs, tm):
    i = pl.program_id(0)
    g = g_ref[...]
    h = _rms(x_ref[...], g)
    hp_row = _rms(xp_ref[...], g)[SUBLANES - 1:SUBLANES, :]
    seq_start = (i * tm) % SEQ == 0
    hp_row = jnp.where(seq_start, jnp.zeros_like(hp_row), hp_row)
    rid = lax.broadcasted_iota(jnp.int32, (tm, 1), 0)
    hprev = jnp.where(rid == 0, hp_row, pltpu.roll(h, 1, 0))
    xx = hprev - h
    for p, o_ref in enumerate(o_refs):
        o_ref[...] = (h + xx * mix_ref[p:p + 1, :]).astype(o_ref.dtype)


def _norm_mix(x, g, mix, tm=512):
    m, d = x.shape
    rb = tm // SUBLANES
    out = jax.ShapeDtypeStruct((m, d), bf16)
    return pl.pallas_call(
        functools.partial(_norm_mix_kernel, tm=tm),
        grid=(m // tm,),
        in_specs=[pl.BlockSpec((tm, d), lambda i: (i, 0)),
                  pl.BlockSpec((SUBLANES, d), lambda i: (jnp.maximum(i * rb - 1, 0), 0)),
                  pl.BlockSpec((1, d), lambda i: (0, 0)),
                  pl.BlockSpec((6, d), lambda i: (0, 0))],
        out_specs=[pl.BlockSpec((tm, d), lambda i: (i, 0))] * 6,
        out_shape=[out] * 6,
        compiler_params=pltpu.CompilerParams(dimension_semantics=("parallel",),
                                             vmem_limit_bytes=VMEM_LIMIT),
        name="norm_mix",
    )(x, x, g.reshape(1, d), mix)


def _mm_kernel(*refs, n_w, n_e, n_o, nk, epilogue):
    x_ref = refs[0]
    w_refs = refs[1:1 + n_w]
    e_refs = refs[1 + n_w:1 + n_w + n_e]
    o_refs = refs[1 + n_w + n_e:1 + n_w + n_e + n_o]
    acc_refs = refs[1 + n_w + n_e + n_o:]
    x = x_ref[...]

    def finish(accs):
        outs = epilogue(accs, [e[...] for e in e_refs])
        for o_ref, out in zip(o_refs, outs if isinstance(outs, tuple) else (outs,)):
            o_ref[...] = out.astype(o_ref.dtype)

    if nk == 1:
        finish([jnp.dot(x, w[...], preferred_element_type=f32) for w in w_refs])
        return
    k = pl.program_id(2)

    @pl.when(k == 0)
    def _():
        for a in acc_refs:
            a[...] = jnp.zeros_like(a)

    for a, w in zip(acc_refs, w_refs):
        a[...] += jnp.dot(x, w[...], preferred_element_type=f32)

    @pl.when(k == nk - 1)
    def _():
        finish([a[...] for a in acc_refs])


def _matmul(x, w, layer, w_col_blocks, n_out, *, tm, tn, tk, epilogue, extras=(), out_dtype, name):
    m, kdim = x.shape
    nk = kdim // tk
    n_w = len(w_col_blocks)
    nb = n_out // tn
    in_specs = [pl.BlockSpec((tm, tk), lambda i, j, k: (i, k))]
    args = [x]
    for cb in w_col_blocks:
        in_specs.append(pl.BlockSpec((None, tk, tn), lambda i, j, k, cb=cb: (layer, k, cb * nb + j)))
        args.append(w)
    for arr, kind in extras:
        if kind == "row":
            in_specs.append(pl.BlockSpec((1, tn), lambda i, j, k: (0, j)))
        else:
            in_specs.append(pl.BlockSpec((tm, tn), lambda i, j, k: (i, j)))
        args.append(arr)
    scratch = [pltpu.VMEM((tm, tn), f32) for _ in range(n_w)] if nk > 1 else []
    multi = isinstance(out_dtype, tuple)
    dtypes = out_dtype if multi else (out_dtype,)
    out_spec = pl.BlockSpec((tm, tn), lambda i, j, k: (i, j))
    outs = pl.pallas_call(
        functools.partial(_mm_kernel, n_w=n_w, n_e=len(extras), n_o=len(dtypes), nk=nk, epilogue=epilogue),
        grid=(m // tm, nb, nk),
        in_specs=in_specs,
        out_specs=[out_spec] * len(dtypes),
        out_shape=[jax.ShapeDtypeStruct((m, n_out), dt) for dt in dtypes],
        scratch_shapes=scratch,
        compiler_params=pltpu.CompilerParams(
            dimension_semantics=("parallel", "parallel", "arbitrary"),
            vmem_limit_bytes=VMEM_LIMIT),
        name=name,
    )(*args)
    return tuple(outs) if multi else outs[0]


def _epi_plain(accs, extras):
    return accs[0]


def _epi_residual(accs, extras):
    return extras[0] + accs[0]


def _epi_residual_norm(accs, extras):
    y = extras[0] + accs[0]
    return y, _rms(y, extras[1])


def _epi_swiglu(accs, extras):
    gate, up = accs
    return gate * jax.nn.sigmoid(gate) * up


def _lora_kernel(x_ref, w1_ref, w2_ref, b_ref, o_ref, *, act, epi):
    z = jnp.dot(x_ref[...], w1_ref[...], preferred_element_type=f32)
    y = jnp.dot(act(z).astype(bf16), w2_ref[...], preferred_element_type=f32)
    o_ref[...] = epi(y, b_ref[...]).astype(o_ref.dtype)


def _lora(x, w1, w2, bias, *, act, epi, name, tm=512):
    m, d = x.shape
    r = w1.shape[1]
    n = w2.shape[1]
    return pl.pallas_call(
        functools.partial(_lora_kernel, act=act, epi=epi),
        grid=(m // tm,),
        in_specs=[pl.BlockSpec((tm, d), lambda i: (i, 0)),
                  pl.BlockSpec((d, r), lambda i: (0, 0)),
                  pl.BlockSpec((r, n), lambda i: (0, 0)),
                  pl.BlockSpec((1, n), lambda i: (0, 0))],
        out_specs=pl.BlockSpec((tm, n), lambda i: (i, 0)),
        out_shape=jax.ShapeDtypeStruct((m, n), f32),
        compiler_params=pltpu.CompilerParams(dimension_semantics=("parallel",),
                                             vmem_limit_bytes=VMEM_LIMIT),
        name=name,
    )(x, w1, w2, bias.reshape(1, n))


def _epi_log_decay(y, w0):
    return -jnp.exp(-_softplus(-(w0 + y)) - 0.5)


def _epi_sigmoid_bias(y, a0):
    return jax.nn.sigmoid(a0 + y)


def _epi_ignore_bias(y, b):
    return y


def _pad_cols(w, n):
    return jnp.pad(w, ((0, 0), (0, n - w.shape[1])))


def _pad_rows(w, n):
    return jnp.pad(w, ((0, n - w.shape[0]), (0, 0)))


def _each(fn, *lists):
    return [fn(*xs) for xs in zip(*lists)]


def _wkv_chunk_maps(tiles, consts):
    m0, strict, incl, eye = consts
    c = WKV_CHUNK
    n2 = 2 * c
    r, lw, lg, k2, v, av, bv = (list(x) for x in zip(*tiles))
    stack = lambda x: _stack_heads(x, m0)
    lg_end = _each(lambda x: x[c - 1:c, :], lg)
    g_inv = _each(lambda x: jnp.exp(-x), lg)
    g_rem = _each(lambda e, x: jnp.exp(e - x), lg_end, lg)
    rs = _each(lambda x, l: stack(x * jnp.exp(l)), r, lg)
    as_ = _each(lambda x, l, w: stack(x * jnp.exp(l - w)).astype(bf16), av, lg, lw)
    bs = _each(lambda x, g: stack(x * g).astype(bf16), bv, g_inv)
    ks = _each(lambda x, g: stack(x * g).astype(bf16), k2, g_inv)
    bhs = _each(lambda x, g: stack(x * g).astype(bf16), bv, g_rem)
    khs = _each(lambda x, g: stack(x * g).astype(bf16), k2, g_rem)
    vs = _each(stack, v)
    sc = _each(lambda a, rr, b, k: _dot_nt(jnp.concatenate([a, rr.astype(bf16)], axis=0),
                                           jnp.concatenate([b, k], axis=0)), as_, rs, bs, ks)
    zero = jnp.zeros((n2, n2), f32)
    a_ab = _each(lambda s: jnp.where(strict, s[:n2, :n2], zero), sc)
    a_ak = _each(lambda s: jnp.where(strict, s[:n2, n2:], zero), sc)
    a_r = _each(lambda s: jnp.where(jnp.concatenate([incl, incl], axis=1), s[n2:, :],
                                    jnp.zeros((n2, 2 * n2), f32)).astype(bf16), sc)
    side = lambda a, b: jnp.concatenate([a, b], axis=1)
    t = _each(lambda x: eye + x, a_ab)
    p = _each(lambda x: _dot(x, x), a_ab)
    for _ in range(c.bit_length() - 3):
        pt = _each(lambda pp, tt: _dot(pp, side(pp, tt)), p, t)
        p = _each(lambda x: x[:, :n2], pt)
        t = _each(lambda tt, x: tt + x[:, n2:], t, pt)
    t = _each(lambda tt, pp: (tt + _dot(pp, tt)).astype(bf16), t, p)
    akv = _each(_dot, a_ak, vs)
    hw = _each(lambda tt, a, x: _dot(tt, side(a, x)), t, as_, akv)
    ah = _each(lambda x: x[:, :LANES], hw)
    ws = _each(lambda x: x[:, LANES:], hw)
    pq = _each(lambda ar, x, vv: _dot(ar, jnp.concatenate([x, side(jnp.zeros_like(vv), vv)], axis=0)),
               a_r, hw, vs)
    pc = _each(lambda x, y: x + y[:, :LANES], rs, pq)
    qc = _each(lambda y: y[:, LANES:], pq)
    gm = _each(lambda h, b: _dot(h.T, b), ah, bhs)
    nc = _each(lambda w, x, b, k: _dot(jnp.concatenate([w, x], axis=0).T, jnp.concatenate([b, k], axis=0)),
               ws, vs, bhs, khs)
    decay = _each(jnp.exp, lg_end)
    return pc, qc, gm, nc, decay


def _wkv_kernel(r_ref, lw_ref, k_ref, v_ref, a_ref, g_ref, kk_ref, ka_ref, rk_ref, lng_ref, lnb_ref,
                o_ref, s_ref):
    c = WKV_CHUNK
    n2 = 2 * c

    @pl.when(pl.program_id(2) == 0)
    def _():
        s_ref[...] = jnp.zeros_like(s_ref)

    ri = lax.broadcasted_iota(jnp.int32, (n2, n2), 0)
    ci = lax.broadcasted_iota(jnp.int32, (n2, n2), 1)
    tr = lax.broadcasted_iota(jnp.int32, (c, c), 0)
    tc = lax.broadcasted_iota(jnp.int32, (c, c), 1)
    consts = (_first_head_mask(), (ri & (c - 1)) > (ci & (c - 1)), (ri & (c - 1)) >= (ci & (c - 1)),
              (ri == ci).astype(f32))
    tri = (tr >= tc).astype(bf16)
    head_ones = _head_ones()
    inv_n = 1.0 / HEAD_DIM

    npairs = WKV_LANES // LANES
    nchunk = WKV_TBLK // c
    lanes = [slice(p * LANES, (p + 1) * LANES) for p in range(npairs)]
    r = [r_ref[:, ln] for ln in lanes]
    v = [v_ref[:, ln] for ln in lanes]
    k2, av, bv = [], [], []
    for ln in lanes:
        k, a = k_ref[:, ln], a_ref[:, ln]
        kk = k * kk_ref[:, ln]
        ss = _head_sums(kk * kk, head_ones)
        kk = kk / jnp.maximum(jnp.sqrt(ss), 1e-12)
        k2.append(k * (1.0 + (a - 1.0) * ka_ref[:, ln]))
        av.append(-kk)
        bv.append(kk * a)
    tiles = []
    for ic in range(nchunk):
        rows = slice(ic * c, (ic + 1) * c)
        lw = lw_ref[rows, :]
        lg = _dot_sel_lhs(tri, lw)
        for p, ln in enumerate(lanes):
            tiles.append((r[p][rows], lw[:, ln], lg[:, ln], k2[p][rows], v[p][rows], av[p][rows], bv[p][rows]))
    pc, qc, gm, nc, decay = _wkv_chunk_maps(tiles, consts)
    s = [s_ref[p] for p in range(npairs)]
    ys = [[] for _ in range(npairs)]
    for ic in range(nchunk):
        for p in range(npairs):
            i = ic * npairs + p
            y = _dot_nt(pc[i], s[p]) + qc[i]
            ys[p].append(y[:c] + y[c:])
            s[p] = s[p] * decay[i] + _dot(s[p], gm[i]) + nc[i]
    for p, ln in enumerate(lanes):
        s_ref[p] = s[p]
        y = jnp.concatenate(ys[p], axis=0)
        mu = _head_sums(y, head_ones) * inv_n
        yc = y - mu
        var = _head_sums(yc * yc, head_ones) * inv_n
        yn = yc * lax.rsqrt(var + GN_EPS) * lng_ref[:, ln] + lnb_ref[:, ln]
        bonus = _head_sums(r[p] * k2[p] * rk_ref[:, ln], head_ones) * v[p]
        o_ref[:, ln] = ((yn + bonus) * g_ref[:, ln]).astype(o_ref.dtype)


def _wkv(r, lw, k, v, a, g, k_k, k_a, r_k, lnx_g, lnx_b):
    m, d = r.shape
    tb = SEQ // WKV_TBLK
    tile = pl.BlockSpec((WKV_TBLK, WKV_LANES), lambda b, j, t: (b * tb + t, j))
    row = pl.BlockSpec((1, WKV_LANES), lambda b, j, t: (0, j))
    rows = [x.reshape(1, d) for x in (k_k, k_a, r_k, lnx_g, lnx_b)]
    return pl.pallas_call(
        _wkv_kernel,
        grid=(BATCH, d // WKV_LANES, tb),
        in_specs=[tile] * 6 + [row] * 5,
        out_specs=tile,
        out_shape=jax.ShapeDtypeStruct((m, d), bf16),
        scratch_shapes=[pltpu.VMEM((WKV_LANES // LANES, LANES, LANES), f32)],
        compiler_params=pltpu.CompilerParams(
            dimension_semantics=("parallel", "parallel", "arbitrary"),
            vmem_limit_bytes=VMEM_LIMIT),
        name="wkv7",
    )(r, lw, k, v, a, g, *rows)


def _fox_gate_kernel(s_ref, bf_ref, cp_ref, sp_ref, run_ref, *, tc):
    @pl.when(pl.program_id(1) == 0)
    def _():
        run_ref[...] = jnp.zeros_like(run_ref)

    small = s_ref[...]
    ls = -_softplus(-(small + bf_ref[...]))
    tr = lax.broadcasted_iota(jnp.int32, (tc, tc), 0)
    tcc = lax.broadcasted_iota(jnp.int32, (tc, tc), 1)
    cs = _dot_sel_lhs((tr >= tcc).astype(bf16), ls) + run_ref[...]
    run_ref[...] = cs[tc - 1:tc, :]
    cp_ref[...] = jnp.concatenate(_split3(cs * LOG2_E), axis=1)
    sp_ref[...] = jnp.concatenate(_split3(small), axis=1)


def _fox_gate(small, b_f, tc=512):
    m, w = small.shape
    nt = SEQ // tc
    bias = jnp.pad(b_f, (0, w - b_f.shape[0])).reshape(1, w)
    pieces = pl.BlockSpec((tc, N_SPLIT * w), lambda b, t: (b * nt + t, 0))
    return pl.pallas_call(
        functools.partial(_fox_gate_kernel, tc=tc),
        grid=(BATCH, nt),
        in_specs=[pl.BlockSpec((tc, w), lambda b, t: (b * nt + t, 0)),
                  pl.BlockSpec((1, w), lambda b, t: (0, 0))],
        out_specs=[pieces, pieces],
        out_shape=[jax.ShapeDtypeStruct((m, N_SPLIT * w), bf16)] * 2,
        scratch_shapes=[pltpu.VMEM((1, w), f32)],
        compiler_params=pltpu.CompilerParams(dimension_semantics=("parallel", "arbitrary"),
                                             vmem_limit_bytes=VMEM_LIMIT),
        name="fox_gate_cumsum",
    )(small, bias)


def _fox_prep_kernel(q_ref, k_ref, v_ref, kp_ref, vp_ref, s_ref, c_ref, qg_ref, kg_ref,
                     qo_ref, ko_ref, vo_ref, *, tm):
    i = pl.program_id(0)
    npairs = PREP_LANES // LANES
    head_ones = _head_ones()
    seq_start = (i * tm) % SEQ == 0
    rid = lax.broadcasted_iota(jnp.int32, (tm, 1), 0)
    r = lax.broadcasted_iota(jnp.int32, (LANES, 2 * LANES), 0)
    col = lax.broadcasted_iota(jnp.int32, (LANES, 2 * LANES), 1)
    ra = lax.broadcasted_iota(jnp.int32, (LANES, LANES), 0)
    ca = lax.broadcasted_iota(jnp.int32, (LANES, LANES), 1)
    small_pieces = s_ref[...]
    c_pieces = c_ref[...]

    def shifted(x, prow):
        prow = jnp.where(seq_start, jnp.zeros_like(prow), prow)
        return jnp.where(rid == 0, prow, pltpu.roll(x, 1, 0))

    def head_rms(x, gain):
        ms = _head_sums(x * x, head_ones) * (1.0 / HEAD_DIM)
        return x * lax.rsqrt(ms + RMS_EPS) * gain

    for p in range(npairs):
        pair = pl.program_id(1) * npairs + p
        ln = slice(p * LANES, (p + 1) * LANES)
        src = (((col & (LANES - 1)) >> HEAD_SHIFT) + HEADS_PER_VREG * pair
               + N_HEADS * (1 + (col >> LANE_SHIFT)))
        sel = (r == src).astype(bf16)
        logits = jnp.dot(small_pieces, jnp.concatenate([sel] * N_SPLIT, axis=0), preferred_element_type=f32)
        ak = jax.nn.sigmoid(logits[:, :LANES])
        av = jax.nn.sigmoid(logits[:, LANES:])
        k = k_ref[:, ln]
        v = v_ref[:, ln]
        k = ak * shifted(k, kp_ref[SUBLANES - 1:SUBLANES, ln]) + (1.0 - ak) * k
        v = av * shifted(v, vp_ref[SUBLANES - 1:SUBLANES, ln]) + (1.0 - av) * v
        qo_ref[:, ln] = (head_rms(q_ref[:, ln], qg_ref[...])
                         * (HEAD_DIM ** -0.5 * LOG2_E)).astype(qo_ref.dtype)
        sel_aug = []
        for n in range(N_SPLIT):
            hit = jnp.zeros((LANES, LANES), jnp.bool_)
            for h in range(HEADS_PER_VREG):
                hit = hit | ((ca == N_SPLIT * h + n) & (ra == HEADS_PER_VREG * pair + h))
            sel_aug.append(hit.astype(bf16))
        aug = jnp.dot(c_pieces, jnp.concatenate(sel_aug, axis=0), preferred_element_type=f32)
        ko_ref[:, 2 * p * LANES:2 * (p + 1) * LANES] = jnp.concatenate(
            [head_rms(k, kg_ref[...]), aug], axis=1).astype(ko_ref.dtype)
        vo_ref[ln, :] = v.T.astype(vo_ref.dtype)


def _fox_prep(proj, small_pieces, c_pieces, qn_g, kn_g, tm=1024):
    m = proj.shape[0]
    rb = tm // SUBLANES
    w = small_pieces.shape[1]
    nb = D_MODEL // PREP_LANES
    tile = lambda cb: pl.BlockSpec((tm, PREP_LANES), lambda i, j, cb=cb: (i, cb * nb + j))
    prev = lambda cb: pl.BlockSpec(
        (SUBLANES, PREP_LANES), lambda i, j, cb=cb: (jnp.maximum(i * rb - 1, 0), cb * nb + j))
    gain = pl.BlockSpec((1, LANES), lambda i, j: (0, 0))
    small_tile = pl.BlockSpec((tm, w), lambda i, j: (i, 0))
    tile_gain = lambda x: jnp.tile(x, HEADS_PER_VREG).reshape(1, LANES)
    return pl.pallas_call(
        functools.partial(_fox_prep_kernel, tm=tm),
        grid=(m // tm, nb),
        in_specs=[tile(0), tile(1), tile(2), prev(1), prev(2), small_tile, small_tile, gain, gain],
        out_specs=[pl.BlockSpec((tm, PREP_LANES), lambda i, j: (i, j)),
                   pl.BlockSpec((tm, 2 * PREP_LANES), lambda i, j: (i, j)),
                   pl.BlockSpec((PREP_LANES, tm), lambda i, j: (j, i))],
        out_shape=[jax.ShapeDtypeStruct((m, D_MODEL), bf16),
                   jax.ShapeDtypeStruct((m, 2 * D_MODEL), bf16),
                   jax.ShapeDtypeStruct((D_MODEL, m), bf16)],
        compiler_params=pltpu.CompilerParams(dimension_semantics=("parallel", "parallel"),
                                             vmem_limit_bytes=VMEM_LIMIT),
        name="fox_prep",
    )(proj, proj, proj, proj, proj, small_pieces, c_pieces, tile_gain(qn_g), tile_gain(kn_g))


def _fox_attn_kernel(q_ref, k_ref, vt_ref, gate_ref, og_ref, o_ref):
    tq, tk = ATT_TQ, ATT_TK
    qi = pl.program_id(2)
    npairs = ATT_LANES // LANES
    chains = [(p, h) for p in range(npairs) for h in range(HEADS_PER_VREG)]
    lane = lax.broadcasted_iota(jnp.int32, (tq, LANES), 1)
    q_aug = []
    for p, h in chains:
        q = q_ref[:, p * LANES:(p + 1) * LANES]
        own = (lane >> HEAD_SHIFT) == h
        minus_one = (lane >= N_SPLIT * h) & (lane < N_SPLIT * (h + 1))
        q_aug.append(jnp.concatenate([jnp.where(own, q, jnp.zeros_like(q)),
                                      jnp.where(minus_one, -1.0, 0.0).astype(bf16)], axis=1))

    def step(key0, width, carry, diag):
        m_run, l_run, acc = carry
        keys = pl.ds(pl.multiple_of(key0, tq), width)
        s = [_dot_nt(k_ref[keys, 2 * p * LANES:2 * (p + 1) * LANES], qa)
             for (p, h), qa in zip(chains, q_aug)]
        if diag:
            kidx = key0 + lax.broadcasted_iota(jnp.int32, (width, tq), 0)
            qidx = qi * tq + lax.broadcasted_iota(jnp.int32, (width, tq), 1)
            s = [jnp.where(qidx >= kidx, x, NEG_BIG) for x in s]
        m_new = [jnp.maximum(mr, jnp.max(x, axis=0, keepdims=True)) for mr, x in zip(m_run, s)]
        alpha = [jnp.exp2(mr - mn) for mr, mn in zip(m_run, m_new)]
        pr = [jnp.exp2(x - mn) for x, mn in zip(s, m_new)]
        l_new = [a * lr + jnp.sum(x, axis=0, keepdims=True) for a, lr, x in zip(alpha, l_run, pr)]
        pv = [jnp.dot(vt_ref[pl.ds((p * HEADS_PER_VREG + h) * HEAD_DIM, HEAD_DIM), keys], x.astype(bf16),
                      preferred_element_type=f32) for (p, h), x in zip(chains, pr)]
        acc = [ac * a + x for ac, a, x in zip(acc, alpha, pv)]
        return m_new, l_new, acc

    n = len(chains)
    init = ([jnp.full((1, tq), NEG_BIG, f32)] * n, [jnp.zeros((1, tq), f32)] * n,
            [jnp.zeros((HEAD_DIM, tq), f32)] * n)
    n_full = (qi * tq) // tk
    carry = lax.fori_loop(0, n_full, lambda j, cy: step(j * tk, tk, cy, False), init)
    _, l_run, acc = lax.cond((qi * tq) % tk == 0,
                             lambda cy: step(n_full * tk, tq, cy, True),
                             lambda cy: step(n_full * tk, tk, cy, True), carry)
    o_t = []
    for ac, lr in zip(acc, l_run):
        o = ac / lr
        o_t.append(o * lax.rsqrt(jnp.mean(o * o, axis=0, keepdims=True) + RMS_EPS))
    for p in range(npairs):
        ln = slice(p * LANES, (p + 1) * LANES)
        o = jnp.concatenate(o_t[HEADS_PER_VREG * p:HEADS_PER_VREG * (p + 1)], axis=0).T
        o_ref[:, ln] = (o * og_ref[:, ln] * jax.nn.sigmoid(gate_ref[:, ln])).astype(o_ref.dtype)


def _fox_attn(q, k_aug, v_t, proj, on_g):
    m, d = q.shape
    nq = SEQ // ATT_TQ
    gate_col0 = 3 * D_MODEL // ATT_LANES
    return pl.pallas_call(
        _fox_attn_kernel,
        grid=(BATCH, d // ATT_LANES, nq),
        in_specs=[pl.BlockSpec((ATT_TQ, ATT_LANES), lambda b, p, i: (b * nq + i, p)),
                  pl.BlockSpec((SEQ, 2 * ATT_LANES), lambda b, p, i: (b, p)),
                  pl.BlockSpec((ATT_LANES, SEQ), lambda b, p, i: (p, b)),
                  pl.BlockSpec((ATT_TQ, ATT_LANES), lambda b, p, i: (b * nq + i, gate_col0 + p)),
                  pl.BlockSpec((1, ATT_LANES), lambda b, p, i: (0, p))],
        out_specs=pl.BlockSpec((ATT_TQ, ATT_LANES), lambda b, p, i: (b * nq + i, p)),
        out_shape=jax.ShapeDtypeStruct((m, d), bf16),
        compiler_params=pltpu.CompilerParams(
            dimension_semantics=("parallel", "parallel", "arbitrary"),
            vmem_limit_bytes=VMEM_LIMIT),
        name="fox_attention",
    )(q, k_aug, v_t, proj, on_g.reshape(1, d))


def _swiglu_block(x, hn, w_gu, w_d, layer):
    act = _matmul(hn, w_gu, layer, (0, 1), D_FF, tm=2048, tn=512, tk=D_MODEL,
                  epilogue=_epi_swiglu, out_dtype=bf16, name="swiglu_gate_up")
    return _matmul(act, w_d, layer, (0,), D_MODEL, tm=1024, tn=1024, tk=D_FF // 2,
                   epilogue=_epi_residual, extras=((x, "tile"),), out_dtype=f32, name="swiglu_down")


def _proj(x, w, layer=0, *, n_out=None, out_dtype=f32, epilogue=_epi_plain, extras=(), name):
    if w.ndim == 2:
        w = w[None]
    n = w.shape[2] if n_out is None else n_out
    return _matmul(x, w, layer, (0,), n, tm=1024, tn=min(n, 1024), tk=x.shape[1], epilogue=epilogue,
                   extras=extras, out_dtype=out_dtype, name=name)


def _out_proj_norm(y, w_o, x, next_norm_g, *, name):
    return _matmul(y, w_o.astype(bf16)[None], 0, (0,), D_MODEL, tm=512, tn=D_MODEL, tk=D_MODEL,
                   epilogue=_epi_residual_norm, extras=((x, "tile"), (next_norm_g.reshape(1, D_MODEL), "row")),
                   out_dtype=(f32, bf16), name=name)


def _rwkv7_block(x, norm_g, mix, w_rkv, w0, w1, w2, a0, a1, a2, g1, g2, k_k, k_a, r_k, lnx_g, lnx_b, w_o,
                 next_norm_g):
    xr, xk, xv, xw, xa, xg = _norm_mix(x, norm_g, mix)
    w_rkv = w_rkv.astype(bf16)
    r = _proj(xr, w_rkv, 0, name="rwkv_r")
    k = _proj(xk, w_rkv, 1, name="rwkv_k")
    v = _proj(xv, w_rkv, 2, name="rwkv_v")
    rank = LANES * pl.cdiv(w1.shape[1], LANES)
    lw = _lora(xw, _pad_cols(w1, rank).astype(bf16), _pad_rows(w2, rank).astype(bf16), w0,
               act=jnp.tanh, epi=_epi_log_decay, name="rwkv_decay")
    a = _lora(xa, _pad_cols(a1, rank).astype(bf16), _pad_rows(a2, rank).astype(bf16), a0,
              act=lambda z: z, epi=_epi_sigmoid_bias, name="rwkv_iclr")
    g = _lora(xg, g1.astype(bf16), g2.astype(bf16), jnp.zeros((D_MODEL,), f32),
              act=jax.nn.sigmoid, epi=_epi_ignore_bias, name="rwkv_gate")
    yg = _wkv(r, lw, k, v, a, g, k_k, k_a, r_k.reshape(-1), lnx_g, lnx_b)
    return _out_proj_norm(yg, w_o, x, next_norm_g, name="rwkv_out")


def _fox_block(x, norm_g, w_in, b_f, qn_g, kn_g, on_g, w_o, next_norm_g):
    hn = _rmsnorm(x, norm_g, bf16)
    n_main = 4 * D_MODEL
    proj = _proj(hn, w_in.astype(bf16), n_out=n_main, name="fox_in")
    w_small = _pad_cols(w_in[:, n_main:], LANES).astype(bf16)
    small = _proj(hn, w_small, name="fox_in_gates")
    c_pieces, small_pieces = _fox_gate(small, b_f)
    q, k_aug, v_t = _fox_prep(proj, small_pieces, c_pieces, qn_g, kn_g)
    og = _fox_attn(q, k_aug, v_t, proj, on_g)
    return _out_proj_norm(og, w_o, x, next_norm_g, name="fox_out")


def kernel(x, a_norm_g, a_mix, a_w_rkv, a_w0, a_w1, a_w2, a_a0, a_a1, a_a2, a_g1, a_g2, a_k_k, a_k_a, a_r_k, a_lnx_g, a_lnx_b, a_w_o, b_norm_g, b_w_in, b_b_f, b_qn_g, b_kn_g, b_on_g, b_w_o, f_norm_g, f_w_gu, f_w_d, final_g):
    b, t, d = x.shape
    h = x.reshape(b * t, d)
    w_gu = f_w_gu.astype(bf16)
    w_d = f_w_d.astype(bf16)
    h = _rwkv7_block(h, a_norm_g[0], a_mix[0], a_w_rkv[0], a_w0[0], a_w1[0], a_w2[0], a_a0[0], a_a1[0],
                     a_a2[0], a_g1[0], a_g2[0], a_k_k[0], a_k_a[0], a_r_k[0], a_lnx_g[0], a_lnx_b[0],
                     a_w_o[0], f_norm_g[0])
    h = _swiglu_block(*h, w_gu, w_d, 0)
    h = _fox_block(h, b_norm_g[0], b_w_in[0], b_b_f[0], b_qn_g[0], b_kn_g[0], b_on_g[0], b_w_o[0],
                   f_norm_g[1])
    h = _swiglu_block(*h, w_gu, w_d, 1)
    return _rmsnorm(h, final_g, f32).reshape(b, t, d)
```

```python
import functools

import jax
import jax.numpy as jnp
from jax import lax
from jax.experimental import pallas as pl
from jax.experimental.pallas import tpu as pltpu

D_MODEL = 2048
BATCH = 8
SEQ = 2048
N_TOK = BATCH * SEQ
HEAD_DIM = 64
HEAD_SHIFT = HEAD_DIM.bit_length() - 1
N_HEADS = D_MODEL // HEAD_DIM
D_FF = 5632
RMS_EPS = 1e-6
GN_EPS = 64e-5

LANES = 128
LANE_SHIFT = LANES.bit_length() - 1
SUBLANES = 8
HEADS_PER_VREG = LANES // HEAD_DIM
N_PAIRS = D_MODEL // LANES
VMEM_LIMIT = 56 * 1024 * 1024

WKV_CHUNK = 64
WKV_TBLK = 512
WKV_LANES = 512
ATT_TQ = 256
ATT_TK = 512
ATT_LANES = 512
PREP_LANES = 512
N_SPLIT = 3
NEG_BIG = -1e30
LOG2_E = 1.4426950408889634

f32 = jnp.float32
bf16 = jnp.bfloat16


def _dot(a, b):
    return jnp.dot(a.astype(bf16), b.astype(bf16), preferred_element_type=f32)


def _dot_nt(a, b):
    return lax.dot_general(a.astype(bf16), b.astype(bf16), (((1,), (1,)), ((), ())),
                           preferred_element_type=f32)


def _split3(x):
    hi = x.astype(bf16)
    r1 = x - hi.astype(f32)
    mid = r1.astype(bf16)
    lo = (r1 - mid.astype(f32)).astype(bf16)
    return hi, mid, lo


def _dot_sel_rhs(x, sel):
    hi, mid, lo = _split3(x)
    d = lambda p: jnp.dot(p, sel, preferred_element_type=f32)
    return d(hi) + d(mid) + d(lo)


def _dot_sel_lhs(sel, x):
    hi, mid, lo = _split3(x)
    d = lambda p: jnp.dot(sel, p, preferred_element_type=f32)
    return d(hi) + d(mid) + d(lo)


def _head_sums(x, head_ones):
    hi = x.astype(bf16)
    lo = (x - hi.astype(f32)).astype(bf16)
    return jnp.dot(jnp.concatenate([hi, lo], axis=1), jnp.concatenate([head_ones, head_ones], axis=0),
                   preferred_element_type=f32)


def _head_ones():
    r = lax.broadcasted_iota(jnp.int32, (LANES, LANES), 0) >> HEAD_SHIFT
    c = lax.broadcasted_iota(jnp.int32, (LANES, LANES), 1) >> HEAD_SHIFT
    return (r == c).astype(bf16)


def _first_head_mask():
    return lax.broadcasted_iota(jnp.int32, (1, LANES), 1) < HEAD_DIM


def _stack_heads(x, m0):
    z = jnp.zeros_like(x)
    return jnp.concatenate([jnp.where(m0, x, z), jnp.where(m0, z, x)], axis=0)


def _softplus(z):
    return jnp.maximum(z, 0.0) + jnp.log(1.0 + jnp.exp(-jnp.abs(z)))


def _rms(x, g):
    return x * lax.rsqrt(jnp.mean(x * x, axis=-1, keepdims=True) + RMS_EPS) * g


def _norm_mix_kernel(x_ref, xp_ref, g_ref, mix_ref, *o_refs, tm):
    i = pl.program_id(0)
    d = x_ref.shape[1]
    rows_per_chunk = 2 * SUBLANES
    lane_blk = 4 * LANES
    seq_start = (i * tm) % SEQ == 0
    hp_row = _rms(xp_ref[...], g_ref[...])[SUBLANES - 1:SUBLANES, :]
    hp_row = jnp.where(seq_start, jnp.zeros_like(hp_row), hp_row)
    rid = lax.broadcasted_iota(jnp.int32, (rows_per_chunk, 1), 0)

    def chunk(c, last_row):
        rows = pl.ds(pl.multiple_of(c * rows_per_chunk, rows_per_chunk), rows_per_chunk)
        x = x_ref[rows, :]
        inv = lax.rsqrt(jnp.mean(x * x, axis=-1, keepdims=True) + RMS_EPS)
        new_last = []
        for b in range(d // lane_blk):
            ln = slice(b * lane_blk, (b + 1) * lane_blk)
            h = x[:, ln] * inv * g_ref[:, ln]
            hprev = jnp.where(rid == 0, last_row[:, ln], pltpu.roll(h, 1, 0))
            xx = hprev - h
            for p, o_ref in enumerate(o_refs):
                o_ref[rows, ln] = (h + xx * mix_ref[p:p + 1, ln]).astype(o_ref.dtype)
            new_last.append(h[rows_per_chunk - 1:rows_per_chunk, :])
        return jnp.concatenate(new_last, axis=1)

    lax.fori_loop(0, tm // rows_per_chunk, chunk, hp_row, unroll=2)


def _norm_mix(x, g, mix, tm=512):
    m, d = x.shape
    rb = tm // SUBLANES
    out = jax.ShapeDtypeStruct((m, d), bf16)
    return pl.pallas_call(
        functools.partial(_norm_mix_kernel, tm=tm),
        grid=(m // tm,),
        in_specs=[pl.BlockSpec((tm, d), lambda i: (i, 0)),
                  pl.BlockSpec((SUBLANES, d), lambda i: (jnp.maximum(i * rb - 1, 0), 0)),
                  pl.BlockSpec((1, d), lambda i: (0, 0)),
                  pl.BlockSpec((6, d), lambda i: (0, 0))],
        out_specs=[pl.BlockSpec((tm, d), lambda i: (i, 0))] * 6,
        out_shape=[out] * 6,
        compiler_params=pltpu.CompilerParams(dimension_semantics=("parallel",),
                                             vmem_limit_bytes=VMEM_LIMIT),
        name="norm_mix",
    )(x, x, g.reshape(1, d), mix)


def _mm_kernel(*refs, n_w, n_e, n_o, nk, epilogue):
    x_ref = refs[0]
    w_refs = refs[1:1 + n_w]
    e_refs = refs[1 + n_w:1 + n_w + n_e]
    o_refs = refs[1 + n_w + n_e:1 + n_w + n_e + n_o]
    acc_refs = refs[1 + n_w + n_e + n_o:]
    x = x_ref[...]

    def finish(accs):
        outs = epilogue(accs, [e[...] for e in e_refs])
        for o_ref, out in zip(o_refs, outs if isinstance(outs, tuple) else (outs,)):
            o_ref[...] = out.astype(o_ref.dtype)

    if nk == 1:
        finish([jnp.dot(x, w[...], preferred_element_type=f32) for w in w_refs])
        return
    k = pl.program_id(2)

    @pl.when(k == 0)
    def _():
        for a in acc_refs:
            a[...] = jnp.zeros_like(a)

    for a, w in zip(acc_refs, w_refs):
        a[...] += jnp.dot(x, w[...], preferred_element_type=f32)

    @pl.when(k == nk - 1)
    def _():
        finish([a[...] for a in acc_refs])


def _matmul(x, w, layer, w_col_blocks, n_out, *, tm, tn, tk, epilogue, extras=(), out_dtype, name):
    m, kdim = x.shape
    nk = kdim // tk
    n_w = len(w_col_blocks)
    nb = n_out // tn
    in_specs = [pl.BlockSpec((tm, tk), lambda i, j, k: (i, k))]
    args = [x]
    for cb in w_col_blocks:
        in_specs.append(pl.BlockSpec((None, tk, tn), lambda i, j, k, cb=cb: (layer, k, cb * nb + j)))
        args.append(w)
    for arr, kind in extras:
        if kind == "row":
            in_specs.append(pl.BlockSpec((1, tn), lambda i, j, k: (0, j)))
        else:
            in_specs.append(pl.BlockSpec((tm, tn), lambda i, j, k: (i, j)))
        args.append(arr)
    scratch = [pltpu.VMEM((tm, tn), f32) for _ in range(n_w)] if nk > 1 else []
    multi = isinstance(out_dtype, tuple)
    dtypes = out_dtype if multi else (out_dtype,)
    out_spec = pl.BlockSpec((tm, tn), lambda i, j, k: (i, j))
    outs = pl.pallas_call(
        functools.partial(_mm_kernel, n_w=n_w, n_e=len(extras), n_o=len(dtypes), nk=nk, epilogue=epilogue),
        grid=(m // tm, nb, nk),
        in_specs=in_specs,
        out_specs=[out_spec] * len(dtypes),
        out_shape=[jax.ShapeDtypeStruct((m, n_out), dt) for dt in dtypes],
        scratch_shapes=scratch,
        compiler_params=pltpu.CompilerParams(
            dimension_semantics=("parallel", "parallel", "arbitrary"),
            vmem_limit_bytes=VMEM_LIMIT),
        name=name,
    )(*args)
    return tuple(outs) if multi else outs[0]


def _epi_plain(accs, extras):
    return accs[0]


def _epi_residual_norm(accs, extras):
    y = extras[0] + accs[0]
    return y, _rms(y, extras[1])


def _epi_residual_norm_only(accs, extras):
    return _rms(extras[0] + accs[0], extras[1])


def _epi_swiglu(accs, extras):
    gate, up = accs
    return gate * jax.nn.sigmoid(gate) * up


def _lora_kernel(x_ref, w1_ref, w2_ref, *rest, act, epi):
    *b_refs, o_ref = rest
    z = jnp.dot(x_ref[...], w1_ref[...], preferred_element_type=f32)
    y = jnp.dot(act(z).astype(bf16), w2_ref[...], preferred_element_type=f32)
    o_ref[...] = epi(y, *[b[...] for b in b_refs]).astype(o_ref.dtype)


def _lora(x, w1, w2, bias=None, *, act, epi, name, tm=512):
    m, d = x.shape
    r = w1.shape[1]
    n = w2.shape[1]
    biases = [] if bias is None else [bias.reshape(1, n)]
    return pl.pallas_call(
        functools.partial(_lora_kernel, act=act, epi=epi),
        grid=(m // tm,),
        in_specs=[pl.BlockSpec((tm, d), lambda i: (i, 0)),
                  pl.BlockSpec((d, r), lambda i: (0, 0)),
                  pl.BlockSpec((r, n), lambda i: (0, 0))] + [pl.BlockSpec((1, n), lambda i: (0, 0))] * len(biases),
        out_specs=pl.BlockSpec((tm, n), lambda i: (i, 0)),
        out_shape=jax.ShapeDtypeStruct((m, n), f32),
        compiler_params=pltpu.CompilerParams(dimension_semantics=("parallel",),
                                             vmem_limit_bytes=VMEM_LIMIT),
        name=name,
    )(x, w1, w2, *biases)


def _epi_log_decay(y, w0):
    return -jnp.exp(-_softplus(-(w0 + y)) - 0.5)


def _epi_sigmoid_bias(y, a0):
    return jax.nn.sigmoid(a0 + y)


def _epi_identity(y):
    return y


def _pad_cols(w, n):
    return jnp.pad(w, ((0, 0), (0, n - w.shape[1])))


def _pad_rows(w, n):
    return jnp.pad(w, ((0, n - w.shape[0]), (0, 0)))


def _each(fn, *lists):
    return [fn(*xs) for xs in zip(*lists)]


def _wkv_chunk_maps(tiles, consts):
    m0, strict, incl, eye = consts
    c = WKV_CHUNK
    n2 = 2 * c
    r, lw, lg, k2, v, av, bv = (list(x) for x in zip(*tiles))
    stack = lambda x: _stack_heads(x, m0)
    lg_end = _each(lambda x: x[c - 1:c, :], lg)
    g_inv = _each(lambda x: jnp.exp(-x), lg)
    g_rem = _each(lambda e, x: jnp.exp(e - x), lg_end, lg)
    rs = _each(lambda x, l: stack(x * jnp.exp(l)), r, lg)
    as_ = _each(lambda x, l, w: stack(x * jnp.exp(l - w)).astype(bf16), av, lg, lw)
    bs = _each(lambda x, g: stack(x * g).astype(bf16), bv, g_inv)
    ks = _each(lambda x, g: stack(x * g).astype(bf16), k2, g_inv)
    bhs = _each(lambda x, g: stack(x * g).astype(bf16), bv, g_rem)
    khs = _each(lambda x, g: stack(x * g).astype(bf16), k2, g_rem)
    vs = _each(stack, v)
    sc = _each(lambda a, rr, b, k: _dot_nt(jnp.concatenate([a, rr.astype(bf16)], axis=0),
                                           jnp.concatenate([b, k], axis=0)), as_, rs, bs, ks)
    zero = jnp.zeros((n2, n2), f32)
    a_ab = _each(lambda s: jnp.where(strict, s[:n2, :n2], zero), sc)
    a_ak = _each(lambda s: jnp.where(strict, s[:n2, n2:], zero), sc)
    a_r = _each(lambda s: jnp.where(jnp.concatenate([incl, incl], axis=1), s[n2:, :],
                                    jnp.zeros((n2, 2 * n2), f32)).astype(bf16), sc)
    side = lambda a, b: jnp.concatenate([a, b], axis=1)
    t = _each(lambda x: eye + x, a_ab)
    p = _each(lambda x: _dot(x, x), a_ab)
    for _ in range(c.bit_length() - 3):
        pt = _each(lambda pp, tt: _dot(pp, side(pp, tt)), p, t)
        p = _each(lambda x: x[:, :n2], pt)
        t = _each(lambda tt, x: tt + x[:, n2:], t, pt)
    t = _each(lambda tt, pp: (tt + _dot(pp, tt)).astype(bf16), t, p)
    akv = _each(_dot, a_ak, vs)
    hw = _each(lambda tt, a, x: _dot(tt, side(a, x)), t, as_, akv)
    ah = _each(lambda x: x[:, :LANES], hw)
    ws = _each(lambda x: x[:, LANES:], hw)
    pq = _each(lambda ar, x, vv: _dot(ar, jnp.concatenate([x, side(jnp.zeros_like(vv), vv)], axis=0)),
               a_r, hw, vs)
    pc = _each(lambda x, y: x + y[:, :LANES], rs, pq)
    qc = _each(lambda y: y[:, LANES:], pq)
    gm = _each(lambda h, b: _dot(h.T, b), ah, bhs)
    nc = _each(lambda w, x, b, k: _dot(jnp.concatenate([w, x], axis=0).T, jnp.concatenate([b, k], axis=0)),
               ws, vs, bhs, khs)
    decay = _each(jnp.exp, lg_end)
    return pc, qc, gm, nc, decay


def _wkv_kernel(r_ref, lw_ref, k_ref, v_ref, a_ref, g_ref, kk_ref, ka_ref, rk_ref, lng_ref, lnb_ref,
                o_ref, s_ref):
    c = WKV_CHUNK
    n2 = 2 * c

    @pl.when(pl.program_id(2) == 0)
    def _():
        s_ref[...] = jnp.zeros_like(s_ref)

    ri = lax.broadcasted_iota(jnp.int32, (n2, n2), 0)
    ci = lax.broadcasted_iota(jnp.int32, (n2, n2), 1)
    tr = lax.broadcasted_iota(jnp.int32, (c, c), 0)
    tc = lax.broadcasted_iota(jnp.int32, (c, c), 1)
    consts = (_first_head_mask(), (ri & (c - 1)) > (ci & (c - 1)), (ri & (c - 1)) >= (ci & (c - 1)),
              (ri == ci).astype(f32))
    tri = (tr >= tc).astype(bf16)
    head_ones = _head_ones()
    inv_n = 1.0 / HEAD_DIM

    npairs = WKV_LANES // LANES
    nchunk = WKV_TBLK // c
    lanes = [slice(p * LANES, (p + 1) * LANES) for p in range(npairs)]
    r = [r_ref[:, ln] for ln in lanes]
    v = [v_ref[:, ln] for ln in lanes]
    k2, av, bv = [], [], []
    for ln in lanes:
        k, a = k_ref[:, ln], a_ref[:, ln]
        kk = k * kk_ref[:, ln]
        ss = _head_sums(kk * kk, head_ones)
        kk = kk / jnp.maximum(jnp.sqrt(ss), 1e-12)
        k2.append(k * (1.0 + (a - 1.0) * ka_ref[:, ln]))
        av.append(-kk)
        bv.append(kk * a)
    tiles = []
    for ic in range(nchunk):
        rows = slice(ic * c, (ic + 1) * c)
        lw = lw_ref[rows, :]
        lg = _dot_sel_lhs(tri, lw)
        for p, ln in enumerate(lanes):
            tiles.append((r[p][rows], lw[:, ln], lg[:, ln], k2[p][rows], v[p][rows], av[p][rows], bv[p][rows]))
    pc, qc, gm, nc, decay = _wkv_chunk_maps(tiles, consts)
    s = [s_ref[p] for p in range(npairs)]
    ys = [[] for _ in range(npairs)]
    for ic in range(nchunk):
        for p in range(npairs):
            i = ic * npairs + p
            y = _dot_nt(pc[i], s[p]) + qc[i]
            ys[p].append(y[:c] + y[c:])
            s[p] = s[p] * decay[i] + _dot(s[p], gm[i]) + nc[i]
    for p, ln in enumerate(lanes):
        s_ref[p] = s[p]
        y = jnp.concatenate(ys[p], axis=0)
        mu = _head_sums(y, head_ones) * inv_n
        yc = y - mu
        var = _head_sums(yc * yc, head_ones) * inv_n
        yn = yc * lax.rsqrt(var + GN_EPS) * lng_ref[:, ln] + lnb_ref[:, ln]
        bonus = _head_sums(r[p] * k2[p] * rk_ref[:, ln], head_ones) * v[p]
        o_ref[:, ln] = ((yn + bonus) * g_ref[:, ln]).astype(o_ref.dtype)


def _wkv(r, lw, k, v, a, g, k_k, k_a, r_k, lnx_g, lnx_b):
    m, d = r.shape
    tb = SEQ // WKV_TBLK
    tile = pl.BlockSpec((WKV_TBLK, WKV_LANES), lambda b, j, t: (b * tb + t, j))
    row = pl.BlockSpec((1, WKV_LANES), lambda b, j, t: (0, j))
    rows = [x.reshape(1, d) for x in (k_k, k_a, r_k, lnx_g, lnx_b)]
    return pl.pallas_call(
        _wkv_kernel,
        grid=(BATCH, d // WKV_LANES, tb),
        in_specs=[tile] * 6 + [row] * 5,
        out_specs=tile,
        out_shape=jax.ShapeDtypeStruct((m, d), bf16),
        scratch_shapes=[pltpu.VMEM((WKV_LANES // LANES, LANES, LANES), f32)],
        compiler_params=pltpu.CompilerParams(
            dimension_semantics=("parallel", "parallel", "arbitrary"),
            vmem_limit_bytes=VMEM_LIMIT),
        name="wkv7",
    )(r, lw, k, v, a, g, *rows)


def _fox_gate_kernel(s_ref, bf_ref, cp_ref, sp_ref, run_ref, *, tc):
    @pl.when(pl.program_id(1) == 0)
    def _():
        run_ref[...] = jnp.zeros_like(run_ref)

    small = s_ref[...]
    ls = -_softplus(-(small + bf_ref[...]))
    tr = lax.broadcasted_iota(jnp.int32, (tc, tc), 0)
    tcc = lax.broadcasted_iota(jnp.int32, (tc, tc), 1)
    cs = _dot_sel_lhs((tr >= tcc).astype(bf16), ls) + run_ref[...]
    run_ref[...] = cs[tc - 1:tc, :]
    cp_ref[...] = jnp.concatenate(_split3(cs * LOG2_E), axis=1)
    sp_ref[...] = jnp.concatenate(_split3(small), axis=1)


def _fox_gate(small, b_f, tc=512):
    m, w = small.shape
    nt = SEQ // tc
    bias = jnp.pad(b_f, (0, w - b_f.shape[0])).reshape(1, w)
    pieces = pl.BlockSpec((tc, N_SPLIT * w), lambda b, t: (b * nt + t, 0))
    return pl.pallas_call(
        functools.partial(_fox_gate_kernel, tc=tc),
        grid=(BATCH, nt),
        in_specs=[pl.BlockSpec((tc, w), lambda b, t: (b * nt + t, 0)),
                  pl.BlockSpec((1, w), lambda b, t: (0, 0))],
        out_specs=[pieces, pieces],
        out_shape=[jax.ShapeDtypeStruct((m, N_SPLIT * w), bf16)] * 2,
        scratch_shapes=[pltpu.VMEM((1, w), f32)],
        compiler_params=pltpu.CompilerParams(dimension_semantics=("parallel", "arbitrary"),
                                             vmem_limit_bytes=VMEM_LIMIT),
        name="fox_gate_cumsum",
    )(small, bias)


def _fox_prep_kernel(q_ref, k_ref, v_ref, kp_ref, vp_ref, s_ref, c_ref, qg_ref, kg_ref,
                     qo_ref, ko_ref, vo_ref, *, tm):
    i = pl.program_id(0)
    npairs = PREP_LANES // LANES
    head_ones = _head_ones()
    seq_start = (i * tm) % SEQ == 0
    rid = lax.broadcasted_iota(jnp.int32, (tm, 1), 0)
    r = lax.broadcasted_iota(jnp.int32, (LANES, 2 * LANES), 0)
    col = lax.broadcasted_iota(jnp.int32, (LANES, 2 * LANES), 1)
    ra = lax.broadcasted_iota(jnp.int32, (LANES, LANES), 0)
    ca = lax.broadcasted_iota(jnp.int32, (LANES, LANES), 1)
    small_pieces = s_ref[...]
    c_pieces = c_ref[...]

    def shifted(x, prow):
        prow = jnp.where(seq_start, jnp.zeros_like(prow), prow)
        return jnp.where(rid == 0, prow, pltpu.roll(x, 1, 0))

    def head_rms(x, gain):
        ms = _head_sums(x * x, head_ones) * (1.0 / HEAD_DIM)
        return x * lax.rsqrt(ms + RMS_EPS) * gain

    for p in range(npairs):
        pair = pl.program_id(1) * npairs + p
        ln = slice(p * LANES, (p + 1) * LANES)
        src = (((col & (LANES - 1)) >> HEAD_SHIFT) + HEADS_PER_VREG * pair
               + N_HEADS * (1 + (col >> LANE_SHIFT)))
        sel = (r == src).astype(bf16)
        logits = jnp.dot(small_pieces, jnp.concatenate([sel] * N_SPLIT, axis=0), preferred_element_type=f32)
        ak = jax.nn.sigmoid(logits[:, :LANES])
        av = jax.nn.sigmoid(logits[:, LANES:])
        k = k_ref[:, ln]
        v = v_ref[:, ln]
        k = ak * shifted(k, kp_ref[SUBLANES - 1:SUBLANES, ln]) + (1.0 - ak) * k
        v = av * shifted(v, vp_ref[SUBLANES - 1:SUBLANES, ln]) + (1.0 - av) * v
        qo_ref[:, ln] = (head_rms(q_ref[:, ln], qg_ref[...])
                         * (HEAD_DIM ** -0.5 * LOG2_E)).astype(qo_ref.dtype)
        sel_aug = []
        for n in range(N_SPLIT):
            hit = jnp.zeros((LANES, LANES), jnp.bool_)
            for h in range(HEADS_PER_VREG):
                hit = hit | ((ca == N_SPLIT * h + n) & (ra == HEADS_PER_VREG * pair + h))
            sel_aug.append(hit.astype(bf16))
        aug = jnp.dot(c_pieces, jnp.concatenate(sel_aug, axis=0), preferred_element_type=f32)
        ko_ref[:, 2 * p * LANES:2 * (p + 1) * LANES] = jnp.concatenate(
            [head_rms(k, kg_ref[...]), aug], axis=1).astype(ko_ref.dtype)
        vo_ref[ln, :] = v.T.astype(vo_ref.dtype)


def _fox_prep(proj, small_pieces, c_pieces, qn_g, kn_g, tm=1024):
    m = proj.shape[0]
    rb = tm // SUBLANES
    w = small_pieces.shape[1]
    nb = D_MODEL // PREP_LANES
    tile = lambda cb: pl.BlockSpec((tm, PREP_LANES), lambda i, j, cb=cb: (i, cb * nb + j))
    prev = lambda cb: pl.BlockSpec(
        (SUBLANES, PREP_LANES), lambda i, j, cb=cb: (jnp.maximum(i * rb - 1, 0), cb * nb + j))
    gain = pl.BlockSpec((1, LANES), lambda i, j: (0, 0))
    small_tile = pl.BlockSpec((tm, w), lambda i, j: (i, 0))
    tile_gain = lambda x: jnp.tile(x, HEADS_PER_VREG).reshape(1, LANES)
    return pl.pallas_call(
        functools.partial(_fox_prep_kernel, tm=tm),
        grid=(m // tm, nb),
        in_specs=[tile(0), tile(1), tile(2), prev(1), prev(2), small_tile, small_tile, gain, gain],
        out_specs=[pl.BlockSpec((tm, PREP_LANES), lambda i, j: (i, j)),
                   pl.BlockSpec((tm, 2 * PREP_LANES), lambda i, j: (i, j)),
                   pl.BlockSpec((PREP_LANES, tm), lambda i, j: (j, i))],
        out_shape=[jax.ShapeDtypeStruct((m, D_MODEL), bf16),
                   jax.ShapeDtypeStruct((m, 2 * D_MODEL), bf16),
                   jax.ShapeDtypeStruct((D_MODEL, m), bf16)],
        compiler_params=pltpu.CompilerParams(dimension_semantics=("parallel", "parallel"),
                                             vmem_limit_bytes=VMEM_LIMIT),
        name="fox_prep",
    )(proj, proj, proj, proj, proj, small_pieces, c_pieces, tile_gain(qn_g), tile_gain(kn_g))


def _fox_attn_kernel(q_ref, k_ref, vt_ref, gate_ref, og_ref, o_ref):
    tq, tk = ATT_TQ, ATT_TK
    qi = pl.program_id(2)
    npairs = ATT_LANES // LANES
    chains = [(p, h) for p in range(npairs) for h in range(HEADS_PER_VREG)]
    lane = lax.broadcasted_iota(jnp.int32, (tq, LANES), 1)
    q_aug = []
    for p, h in chains:
        q = q_ref[:, p * LANES:(p + 1) * LANES]
        own = (lane >> HEAD_SHIFT) == h
        minus_one = (lane >= N_SPLIT * h) & (lane < N_SPLIT * (h + 1))
        q_aug.append(jnp.concatenate([jnp.where(own, q, jnp.zeros_like(q)),
                                      jnp.where(minus_one, -1.0, 0.0).astype(bf16)], axis=1))

    def step(key0, width, carry, diag):
        m_run, l_run, acc = carry
        keys = pl.ds(pl.multiple_of(key0, tq), width)
        s = [_dot_nt(k_ref[keys, 2 * p * LANES:2 * (p + 1) * LANES], qa)
             for (p, h), qa in zip(chains, q_aug)]
        if diag:
            kidx = key0 + lax.broadcasted_iota(jnp.int32, (width, tq), 0)
            qidx = qi * tq + lax.broadcasted_iota(jnp.int32, (width, tq), 1)
            s = [jnp.where(qidx >= kidx, x, NEG_BIG) for x in s]
        m_new = [jnp.maximum(mr, jnp.max(x, axis=0, keepdims=True)) for mr, x in zip(m_run, s)]
        alpha = [jnp.exp2(mr - mn) for mr, mn in zip(m_run, m_new)]
        pr = [jnp.exp2(x - mn) for x, mn in zip(s, m_new)]
        l_new = [a * lr + jnp.sum(x, axis=0, keepdims=True) for a, lr, x in zip(alpha, l_run, pr)]
        pv = [jnp.dot(vt_ref[pl.ds((p * HEADS_PER_VREG + h) * HEAD_DIM, HEAD_DIM), keys], x.astype(bf16),
                      preferred_element_type=f32) for (p, h), x in zip(chains, pr)]
        acc = [ac * a + x for ac, a, x in zip(acc, alpha, pv)]
        return m_new, l_new, acc

    n = len(chains)
    init = ([jnp.full((1, tq), NEG_BIG, f32)] * n, [jnp.zeros((1, tq), f32)] * n,
            [jnp.zeros((HEAD_DIM, tq), f32)] * n)
    n_full = (qi * tq) // tk
    carry = lax.fori_loop(0, n_full, lambda j, cy: step(j * tk, tk, cy, False), init)
    _, l_run, acc = lax.cond((qi * tq) % tk == 0,
                             lambda cy: step(n_full * tk, tq, cy, True),
                             lambda cy: step(n_full * tk, tk, cy, True), carry)
    o_t = []
    for ac, lr in zip(acc, l_run):
        o = ac / lr
        o_t.append(o * lax.rsqrt(jnp.mean(o * o, axis=0, keepdims=True) + RMS_EPS))
    for p in range(npairs):
        ln = slice(p * LANES, (p + 1) * LANES)
        o = jnp.concatenate(o_t[HEADS_PER_VREG * p:HEADS_PER_VREG * (p + 1)], axis=0).T
        o_ref[:, ln] = (o * og_ref[:, ln] * jax.nn.sigmoid(gate_ref[:, ln])).astype(o_ref.dtype)


def _fox_attn(q, k_aug, v_t, proj, on_g):
    m, d = q.shape
    nq = SEQ // ATT_TQ
    gate_col0 = 3 * D_MODEL // ATT_LANES
    return pl.pallas_call(
        _fox_attn_kernel,
        grid=(BATCH, d // ATT_LANES, nq),
        in_specs=[pl.BlockSpec((ATT_TQ, ATT_LANES), lambda b, p, i: (b * nq + i, p)),
                  pl.BlockSpec((SEQ, 2 * ATT_LANES), lambda b, p, i: (b, p)),
                  pl.BlockSpec((ATT_LANES, SEQ), lambda b, p, i: (p, b)),
                  pl.BlockSpec((ATT_TQ, ATT_LANES), lambda b, p, i: (b * nq + i, gate_col0 + p)),
                  pl.BlockSpec((1, ATT_LANES), lambda b, p, i: (0, p))],
        out_specs=pl.BlockSpec((ATT_TQ, ATT_LANES), lambda b, p, i: (b * nq + i, p)),
        out_shape=jax.ShapeDtypeStruct((m, d), bf16),
        compiler_params=pltpu.CompilerParams(
            dimension_semantics=("parallel", "parallel", "arbitrary"),
            vmem_limit_bytes=VMEM_LIMIT),
        name="fox_attention",
    )(q, k_aug, v_t, proj, on_g.reshape(1, d))


def _swiglu_block(x, hn, w_gu, w_d, layer, next_norm_g, last):
    act = _matmul(hn, w_gu, layer, (0, 1), D_FF, tm=2048, tn=512, tk=D_MODEL,
                  epilogue=_epi_swiglu, out_dtype=bf16, name="swiglu_gate_up")
    return _matmul(act, w_d, layer, (0,), D_MODEL, tm=512, tn=D_MODEL, tk=D_FF // 4,
                   epilogue=_epi_residual_norm_only if last else _epi_residual_norm,
                   extras=((x, "tile"), (next_norm_g.reshape(1, D_MODEL), "row")),
                   out_dtype=f32 if last else (f32, bf16), name="swiglu_down")


def _proj(x, w, layer=0, *, n_out=None, out_dtype=f32, epilogue=_epi_plain, extras=(), name):
    if w.ndim == 2:
        w = w[None]
    n = w.shape[2] if n_out is None else n_out
    return _matmul(x, w, layer, (0,), n, tm=1024, tn=min(n, 1024), tk=x.shape[1], epilogue=epilogue,
                   extras=extras, out_dtype=out_dtype, name=name)


def _out_proj_norm(y, w_o, x, next_norm_g, *, name):
    return _matmul(y, w_o.astype(bf16)[None], 0, (0,), D_MODEL, tm=512, tn=D_MODEL, tk=D_MODEL,
                   epilogue=_epi_residual_norm, extras=((x, "tile"), (next_norm_g.reshape(1, D_MODEL), "row")),
                   out_dtype=(f32, bf16), name=name)


def _rwkv7_block(x, norm_g, mix, w_rkv, w0, w1, w2, a0, a1, a2, g1, g2, k_k, k_a, r_k, lnx_g, lnx_b, w_o,
                 next_norm_g):
    xr, xk, xv, xw, xa, xg = _norm_mix(x, norm_g, mix)
    w_rkv = w_rkv.astype(bf16)
    r = _proj(xr, w_rkv, 0, name="rwkv_r")
    k = _proj(xk, w_rkv, 1, name="rwkv_k")
    v = _proj(xv, w_rkv, 2, name="rwkv_v")
    rank = LANES * pl.cdiv(w1.shape[1], LANES)
    lw = _lora(xw, _pad_cols(w1, rank).astype(bf16), _pad_rows(w2, rank).astype(bf16), w0,
               act=jnp.tanh, epi=_epi_log_decay, name="rwkv_decay")
    a = _lora(xa, _pad_cols(a1, rank).astype(bf16), _pad_rows(a2, rank).astype(bf16), a0,
              act=lambda z: z, epi=_epi_sigmoid_bias, name="rwkv_iclr")
    g = _lora(xg, g1.astype(bf16), g2.astype(bf16), act=jax.nn.sigmoid, epi=_epi_identity, name="rwkv_gate")
    yg = _wkv(r, lw, k, v, a, g, k_k, k_a, r_k.reshape(-1), lnx_g, lnx_b)
    return _out_proj_norm(yg, w_o, x, next_norm_g, name="rwkv_out")


def _fox_block(x, hn, w_in, b_f, qn_g, kn_g, on_g, w_o, next_norm_g):
    n_main = 4 * D_MODEL
    proj = _proj(hn, w_in.astype(bf16), n_out=n_main, name="fox_in")
    w_small = _pad_cols(w_in[:, n_main:], LANES).astype(bf16)
    small = _proj(hn, w_small, name="fox_in_gates")
    c_pieces, small_pieces = _fox_gate(small, b_f)
    q, k_aug, v_t = _fox_prep(proj, small_pieces, c_pieces, qn_g, kn_g)
    og = _fox_attn(q, k_aug, v_t, proj, on_g)
    return _out_proj_norm(og, w_o, x, next_norm_g, name="fox_out")


def kernel(x, a_norm_g, a_mix, a_w_rkv, a_w0, a_w1, a_w2, a_a0, a_a1, a_a2, a_g1, a_g2, a_k_k, a_k_a, a_r_k, a_lnx_g, a_lnx_b, a_w_o, b_norm_g, b_w_in, b_b_f, b_qn_g, b_kn_g, b_on_g, b_w_o, f_norm_g, f_w_gu, f_w_d, final_g):
    b, t, d = x.shape
    h = x.reshape(b * t, d)
    w_gu = f_w_gu.astype(bf16)
    w_d = f_w_d.astype(bf16)
    h = _rwkv7_block(h, a_norm_g[0], a_mix[0], a_w_rkv[0], a_w0[0], a_w1[0], a_w2[0], a_a0[0], a_a1[0],
                     a_a2[0], a_g1[0], a_g2[0], a_k_k[0], a_k_a[0], a_r_k[0], a_lnx_g[0], a_lnx_b[0],
                     a_w_o[0], f_norm_g[0])
    h = _swiglu_block(*h, w_gu, w_d, 0, b_norm_g[0], False)
    h = _fox_block(*h, b_w_in[0], b_b_f[0], b_qn_g[0], b_kn_g[0], b_on_g[0], b_w_o[0],
                   f_norm_g[1])
    return _swiglu_block(*h, w_gu, w_d, 1, final_g, True).reshape(b, t, d)
```

```python
import functools

import jax
import jax.numpy as jnp
from jax import lax
from jax.experimental import pallas as pl
from jax.experimental.pallas import tpu as pltpu

D_MODEL = 2048
BATCH = 8
SEQ = 2048
N_TOK = BATCH * SEQ
HEAD_DIM = 64
HEAD_SHIFT = HEAD_DIM.bit_length() - 1
N_HEADS = D_MODEL // HEAD_DIM
D_FF = 5632
RMS_EPS = 1e-6
GN_EPS = 64e-5

LANES = 128
LANE_SHIFT = LANES.bit_length() - 1
SUBLANES = 8
HEADS_PER_VREG = LANES // HEAD_DIM
N_PAIRS = D_MODEL // LANES
VMEM_LIMIT = 56 * 1024 * 1024

WKV_CHUNK = 64
WKV_TBLK = 512
WKV_LANES = 512
ATT_TQ = 256
ATT_TK = 512
ATT_LANES = 512
PREP_LANES = 512
N_SPLIT = 3
NEG_BIG = -1e30
LOG2_E = 1.4426950408889634

f32 = jnp.float32
bf16 = jnp.bfloat16


def _dot(a, b):
    return jnp.dot(a.astype(bf16), b.astype(bf16), preferred_element_type=f32)


def _dot_nt(a, b):
    return lax.dot_general(a.astype(bf16), b.astype(bf16), (((1,), (1,)), ((), ())),
                           preferred_element_type=f32)


def _split3(x):
    hi = x.astype(bf16)
    r1 = x - hi.astype(f32)
    mid = r1.astype(bf16)
    lo = (r1 - mid.astype(f32)).astype(bf16)
    return hi, mid, lo


def _dot_sel_rhs(x, sel):
    hi, mid, lo = _split3(x)
    d = lambda p: jnp.dot(p, sel, preferred_element_type=f32)
    return d(hi) + d(mid) + d(lo)


def _dot_sel_lhs(sel, x):
    hi, mid, lo = _split3(x)
    d = lambda p: jnp.dot(sel, p, preferred_element_type=f32)
    return d(hi) + d(mid) + d(lo)


def _head_sums(x, head_ones):
    hi = x.astype(bf16)
    lo = (x - hi.astype(f32)).astype(bf16)
    return jnp.dot(jnp.concatenate([hi, lo], axis=1), jnp.concatenate([head_ones, head_ones], axis=0),
                   preferred_element_type=f32)


def _head_ones():
    r = lax.broadcasted_iota(jnp.int32, (LANES, LANES), 0) >> HEAD_SHIFT
    c = lax.broadcasted_iota(jnp.int32, (LANES, LANES), 1) >> HEAD_SHIFT
    return (r == c).astype(bf16)


def _first_head_mask():
    return lax.broadcasted_iota(jnp.int32, (1, LANES), 1) < HEAD_DIM


def _stack_heads(x, m0):
    z = jnp.zeros_like(x)
    return jnp.concatenate([jnp.where(m0, x, z), jnp.where(m0, z, x)], axis=0)


def _softplus(z):
    return jnp.maximum(z, 0.0) + jnp.log(1.0 + jnp.exp(-jnp.abs(z)))


def _rms(x, g):
    return x * lax.rsqrt(jnp.mean(x * x, axis=-1, keepdims=True) + RMS_EPS) * g


def _norm_mix_kernel(x_ref, xp_ref, g_ref, mix_ref, *o_refs, tm):
    i = pl.program_id(0)
    d = x_ref.shape[1]
    rows_per_chunk = 2 * SUBLANES
    lane_blk = 4 * LANES
    seq_start = (i * tm) % SEQ == 0
    hp_row = _rms(xp_ref[...], g_ref[...])[SUBLANES - 1:SUBLANES, :]
    hp_row = jnp.where(seq_start, jnp.zeros_like(hp_row), hp_row)
    rid = lax.broadcasted_iota(jnp.int32, (rows_per_chunk, 1), 0)

    def chunk(c, last_row):
        rows = pl.ds(pl.multiple_of(c * rows_per_chunk, rows_per_chunk), rows_per_chunk)
        x = x_ref[rows, :]
        inv = lax.rsqrt(jnp.mean(x * x, axis=-1, keepdims=True) + RMS_EPS)
        new_last = []
        for b in range(d // lane_blk):
            ln = slice(b * lane_blk, (b + 1) * lane_blk)
            h = x[:, ln] * inv * g_ref[:, ln]
            hprev = jnp.where(rid == 0, last_row[:, ln], pltpu.roll(h, 1, 0))
            xx = hprev - h
            for p, o_ref in enumerate(o_refs):
                o_ref[rows, ln] = (h + xx * mix_ref[p:p + 1, ln]).astype(o_ref.dtype)
            new_last.append(h[rows_per_chunk - 1:rows_per_chunk, :])
        return jnp.concatenate(new_last, axis=1)

    lax.fori_loop(0, tm // rows_per_chunk, chunk, hp_row, unroll=2)


def _norm_mix(x, g, mix, tm=512):
    m, d = x.shape
    rb = tm // SUBLANES
    out = jax.ShapeDtypeStruct((m, d), bf16)
    return pl.pallas_call(
        functools.partial(_norm_mix_kernel, tm=tm),
        grid=(m // tm,),
        in_specs=[pl.BlockSpec((tm, d), lambda i: (i, 0)),
                  pl.BlockSpec((SUBLANES, d), lambda i: (jnp.maximum(i * rb - 1, 0), 0)),
                  pl.BlockSpec((1, d), lambda i: (0, 0)),
                  pl.BlockSpec((6, d), lambda i: (0, 0))],
        out_specs=[pl.BlockSpec((tm, d), lambda i: (i, 0))] * 6,
        out_shape=[out] * 6,
        compiler_params=pltpu.CompilerParams(dimension_semantics=("parallel",),
                                             vmem_limit_bytes=VMEM_LIMIT),
        name="norm_mix",
    )(x, x, g.reshape(1, d), mix)


def _mm_kernel(*refs, n_w, n_e, n_o, nk, epilogue):
    x_ref = refs[0]
    w_refs = refs[1:1 + n_w]
    e_refs = refs[1 + n_w:1 + n_w + n_e]
    o_refs = refs[1 + n_w + n_e:1 + n_w + n_e + n_o]
    acc_refs = refs[1 + n_w + n_e + n_o:]
    x = x_ref[...]

    def finish(accs):
        outs = epilogue(accs, [e[...] for e in e_refs])
        for o_ref, out in zip(o_refs, outs if isinstance(outs, tuple) else (outs,)):
            o_ref[...] = out.astype(o_ref.dtype)

    if nk == 1:
        finish([jnp.dot(x, w[...], preferred_element_type=f32) for w in w_refs])
        return
    k = pl.program_id(2)

    @pl.when(k == 0)
    def _():
        for a in acc_refs:
            a[...] = jnp.zeros_like(a)

    for a, w in zip(acc_refs, w_refs):
        a[...] += jnp.dot(x, w[...], preferred_element_type=f32)

    @pl.when(k == nk - 1)
    def _():
        finish([a[...] for a in acc_refs])


def _matmul(x, w, layer, w_col_blocks, n_out, *, tm, tn, tk, epilogue, extras=(), out_dtype, name,
            resident_w=False):
    m, kdim = x.shape
    nk = kdim // tk
    n_w = len(w_col_blocks)
    nb = n_out // tn
    in_specs = [pl.BlockSpec((tm, tk), lambda i, j, k: (i, k))]
    args = [x]
    w_mode = pl.Buffered(1) if resident_w else None
    for cb in w_col_blocks:
        in_specs.append(pl.BlockSpec((None, tk, tn), lambda i, j, k, cb=cb: (layer, k, cb * nb + j),
                                     pipeline_mode=w_mode))
        args.append(w)
    for arr, kind in extras:
        if kind == "row":
            in_specs.append(pl.BlockSpec((1, tn), lambda i, j, k: (0, j)))
        else:
            in_specs.append(pl.BlockSpec((tm, tn), lambda i, j, k: (i, j)))
        args.append(arr)
    scratch = [pltpu.VMEM((tm, tn), f32) for _ in range(n_w)] if nk > 1 else []
    multi = isinstance(out_dtype, tuple)
    dtypes = out_dtype if multi else (out_dtype,)
    out_spec = pl.BlockSpec((tm, tn), lambda i, j, k: (i, j))
    outs = pl.pallas_call(
        functools.partial(_mm_kernel, n_w=n_w, n_e=len(extras), n_o=len(dtypes), nk=nk, epilogue=epilogue),
        grid=(m // tm, nb, nk),
        in_specs=in_specs,
        out_specs=[out_spec] * len(dtypes),
        out_shape=[jax.ShapeDtypeStruct((m, n_out), dt) for dt in dtypes],
        scratch_shapes=scratch,
        compiler_params=pltpu.CompilerParams(
            dimension_semantics=("parallel", "parallel", "arbitrary"),
            vmem_limit_bytes=VMEM_LIMIT),
        name=name,
    )(*args)
    return tuple(outs) if multi else outs[0]


def _epi_plain(accs, extras):
    return accs[0]


def _epi_residual_norm(accs, extras):
    y = extras[0] + accs[0]
    return y, _rms(y, extras[1])


def _epi_residual_norm_only(accs, extras):
    return _rms(extras[0] + accs[0], extras[1])


def _epi_swiglu(accs, extras):
    gate, up = accs
    return gate * jax.nn.sigmoid(gate) * up


def _lora_kernel(x_ref, w1_ref, w2_ref, *rest, act, epi):
    *b_refs, o_ref = rest
    z = jnp.dot(x_ref[...], w1_ref[...], preferred_element_type=f32)
    y = jnp.dot(act(z).astype(bf16), w2_ref[...], preferred_element_type=f32)
    o_ref[...] = epi(y, *[b[...] for b in b_refs]).astype(o_ref.dtype)


def _lora(x, w1, w2, bias=None, *, act, epi, name, tm=512):
    m, d = x.shape
    r = w1.shape[1]
    n = w2.shape[1]
    biases = [] if bias is None else [bias.reshape(1, n)]
    return pl.pallas_call(
        functools.partial(_lora_kernel, act=act, epi=epi),
        grid=(m // tm,),
        in_specs=[pl.BlockSpec((tm, d), lambda i: (i, 0)),
                  pl.BlockSpec((d, r), lambda i: (0, 0)),
                  pl.BlockSpec((r, n), lambda i: (0, 0))] + [pl.BlockSpec((1, n), lambda i: (0, 0))] * len(biases),
        out_specs=pl.BlockSpec((tm, n), lambda i: (i, 0)),
        out_shape=jax.ShapeDtypeStruct((m, n), f32),
        compiler_params=pltpu.CompilerParams(dimension_semantics=("parallel",),
                                             vmem_limit_bytes=VMEM_LIMIT),
        name=name,
    )(x, w1, w2, *biases)


def _epi_log_decay(y, w0):
    return -jnp.exp(-_softplus(-(w0 + y)) - 0.5)


def _epi_sigmoid_bias(y, a0):
    return jax.nn.sigmoid(a0 + y)


def _epi_identity(y):
    return y


def _pad_cols(w, n):
    return jnp.pad(w, ((0, 0), (0, n - w.shape[1])))


def _pad_rows(w, n):
    return jnp.pad(w, ((0, n - w.shape[0]), (0, 0)))


def _each(fn, *lists):
    return [fn(*xs) for xs in zip(*lists)]


def _wkv_chunk_maps(tiles, consts):
    m0, strict, incl, eye = consts
    c = WKV_CHUNK
    n2 = 2 * c
    r, lw, lg, k2, v, av, bv = (list(x) for x in zip(*tiles))
    stack = lambda x: _stack_heads(x, m0)
    lg_end = _each(lambda x: x[c - 1:c, :], lg)
    g_inv = _each(lambda x: jnp.exp(-x), lg)
    g_rem = _each(lambda e, x: jnp.exp(e - x), lg_end, lg)
    rs = _each(lambda x, l: stack(x * jnp.exp(l)), r, lg)
    as_ = _each(lambda x, l, w: stack(x * jnp.exp(l - w)).astype(bf16), av, lg, lw)
    bs = _each(lambda x, g: stack(x * g).astype(bf16), bv, g_inv)
    ks = _each(lambda x, g: stack(x * g).astype(bf16), k2, g_inv)
    bhs = _each(lambda x, g: stack(x * g).astype(bf16), bv, g_rem)
    khs = _each(lambda x, g: stack(x * g).astype(bf16), k2, g_rem)
    vs = _each(stack, v)
    sc = _each(lambda a, rr, b, k: _dot_nt(jnp.concatenate([a, rr.astype(bf16)], axis=0),
                                           jnp.concatenate([b, k], axis=0)), as_, rs, bs, ks)
    zero = jnp.zeros((n2, n2), f32)
    a_ab = _each(lambda s: jnp.where(strict, s[:n2, :n2], zero), sc)
    a_ak = _each(lambda s: jnp.where(strict, s[:n2, n2:], zero), sc)
    a_r = _each(lambda s: jnp.where(jnp.concatenate([incl, incl], axis=1), s[n2:, :],
                                    jnp.zeros((n2, 2 * n2), f32)).astype(bf16), sc)
    side = lambda a, b: jnp.concatenate([a, b], axis=1)
    t = _each(lambda x: eye + x, a_ab)
    p = _each(lambda x: _dot(x, x), a_ab)
    for _ in range(c.bit_length() - 3):
        pt = _each(lambda pp, tt: _dot(pp, side(pp, tt)), p, t)
        p = _each(lambda x: x[:, :n2], pt)
        t = _each(lambda tt, x: tt + x[:, n2:], t, pt)
    t = _each(lambda tt, pp: (tt + _dot(pp, tt)).astype(bf16), t, p)
    akv = _each(_dot, a_ak, vs)
    hw = _each(lambda tt, a, x: _dot(tt, side(a, x)), t, as_, akv)
    ah = _each(lambda x: x[:, :LANES], hw)
    ws = _each(lambda x: x[:, LANES:], hw)
    pq = _each(lambda ar, x, vv: _dot(ar, jnp.concatenate([x, side(jnp.zeros_like(vv), vv)], axis=0)),
               a_r, hw, vs)
    pc = _each(lambda x, y: x + y[:, :LANES], rs, pq)
    qc = _each(lambda y: y[:, LANES:], pq)
    gm = _each(lambda h, b: _dot(h.T, b), ah, bhs)
    nc = _each(lambda w, x, b, k: _dot(jnp.concatenate([w, x], axis=0).T, jnp.concatenate([b, k], axis=0)),
               ws, vs, bhs, khs)
    decay = _each(jnp.exp, lg_end)
    return pc, qc, gm, nc, decay


def _wkv_kernel(r_ref, lw_ref, k_ref, v_ref, a_ref, g_ref, kk_ref, ka_ref, rk_ref, lng_ref, lnb_ref,
                o_ref, s_ref):
    c = WKV_CHUNK
    n2 = 2 * c

    @pl.when(pl.program_id(2) == 0)
    def _():
        s_ref[...] = jnp.zeros_like(s_ref)

    ri = lax.broadcasted_iota(jnp.int32, (n2, n2), 0)
    ci = lax.broadcasted_iota(jnp.int32, (n2, n2), 1)
    tr = lax.broadcasted_iota(jnp.int32, (c, c), 0)
    tc = lax.broadcasted_iota(jnp.int32, (c, c), 1)
    consts = (_first_head_mask(), (ri & (c - 1)) > (ci & (c - 1)), (ri & (c - 1)) >= (ci & (c - 1)),
              (ri == ci).astype(f32))
    tri = (tr >= tc).astype(bf16)
    head_ones = _head_ones()
    inv_n = 1.0 / HEAD_DIM

    npairs = WKV_LANES // LANES
    nchunk = WKV_TBLK // c
    lanes = [slice(p * LANES, (p + 1) * LANES) for p in range(npairs)]
    r = [r_ref[:, ln] for ln in lanes]
    v = [v_ref[:, ln] for ln in lanes]
    k2, av, bv = [], [], []
    for ln in lanes:
        k, a = k_ref[:, ln], a_ref[:, ln]
        kk = k * kk_ref[:, ln]
        ss = _head_sums(kk * kk, head_ones)
        kk = kk / jnp.maximum(jnp.sqrt(ss), 1e-12)
        k2.append(k * (1.0 + (a - 1.0) * ka_ref[:, ln]))
        av.append(-kk)
        bv.append(kk * a)
    tiles = []
    for ic in range(nchunk):
        rows = slice(ic * c, (ic + 1) * c)
        lw = lw_ref[rows, :]
        lg = _dot_sel_lhs(tri, lw)
        for p, ln in enumerate(lanes):
            tiles.append((r[p][rows], lw[:, ln], lg[:, ln], k2[p][rows], v[p][rows], av[p][rows], bv[p][rows]))
    pc, qc, gm, nc, decay = _wkv_chunk_maps(tiles, consts)
    s = [s_ref[p] for p in range(npairs)]
    ys = [[] for _ in range(npairs)]
    for ic in range(nchunk):
        for p in range(npairs):
            i = ic * npairs + p
            y = _dot_nt(pc[i], s[p]) + qc[i]
            ys[p].append(y[:c] + y[c:])
            s[p] = s[p] * decay[i] + _dot(s[p], gm[i]) + nc[i]
    for p, ln in enumerate(lanes):
        s_ref[p] = s[p]
        y = jnp.concatenate(ys[p], axis=0)
        mu = _head_sums(y, head_ones) * inv_n
        yc = y - mu
        var = _head_sums(yc * yc, head_ones) * inv_n
        yn = yc * lax.rsqrt(var + GN_EPS) * lng_ref[:, ln] + lnb_ref[:, ln]
        bonus = _head_sums(r[p] * k2[p] * rk_ref[:, ln], head_ones) * v[p]
        o_ref[:, ln] = ((yn + bonus) * g_ref[:, ln]).astype(o_ref.dtype)


def _wkv(r, lw, k, v, a, g, k_k, k_a, r_k, lnx_g, lnx_b):
    m, d = r.shape
    tb = SEQ // WKV_TBLK
    tile = pl.BlockSpec((WKV_TBLK, WKV_LANES), lambda b, j, t: (b * tb + t, j))
    row = pl.BlockSpec((1, WKV_LANES), lambda b, j, t: (0, j))
    rows = [x.reshape(1, d) for x in (k_k, k_a, r_k, lnx_g, lnx_b)]
    return pl.pallas_call(
        _wkv_kernel,
        grid=(BATCH, d // WKV_LANES, tb),
        in_specs=[tile] * 6 + [row] * 5,
        out_specs=tile,
        out_shape=jax.ShapeDtypeStruct((m, d), bf16),
        scratch_shapes=[pltpu.VMEM((WKV_LANES // LANES, LANES, LANES), f32)],
        compiler_params=pltpu.CompilerParams(
            dimension_semantics=("parallel", "parallel", "arbitrary"),
            vmem_limit_bytes=VMEM_LIMIT),
        name="wkv7",
    )(r, lw, k, v, a, g, *rows)


def _fox_gate_kernel(s_ref, bf_ref, cp_ref, sp_ref, run_ref, *, tc):
    @pl.when(pl.program_id(1) == 0)
    def _():
        run_ref[...] = jnp.zeros_like(run_ref)

    small = s_ref[...]
    ls = -_softplus(-(small + bf_ref[...]))
    tr = lax.broadcasted_iota(jnp.int32, (tc, tc), 0)
    tcc = lax.broadcasted_iota(jnp.int32, (tc, tc), 1)
    cs = _dot_sel_lhs((tr >= tcc).astype(bf16), ls) + run_ref[...]
    run_ref[...] = cs[tc - 1:tc, :]
    cp_ref[...] = jnp.concatenate(_split3(cs * LOG2_E), axis=1)
    sp_ref[...] = jnp.concatenate(_split3(small), axis=1)


def _fox_gate(small, b_f, tc=512):
    m, w = small.shape
    nt = SEQ // tc
    bias = jnp.pad(b_f, (0, w - b_f.shape[0])).reshape(1, w)
    pieces = pl.BlockSpec((tc, N_SPLIT * w), lambda b, t: (b * nt + t, 0))
    return pl.pallas_call(
        functools.partial(_fox_gate_kernel, tc=tc),
        grid=(BATCH, nt),
        in_specs=[pl.BlockSpec((tc, w), lambda b, t: (b * nt + t, 0)),
                  pl.BlockSpec((1, w), lambda b, t: (0, 0))],
        out_specs=[pieces, pieces],
        out_shape=[jax.ShapeDtypeStruct((m, N_SPLIT * w), bf16)] * 2,
        scratch_shapes=[pltpu.VMEM((1, w), f32)],
        compiler_params=pltpu.CompilerParams(dimension_semantics=("parallel", "arbitrary"),
                                             vmem_limit_bytes=VMEM_LIMIT),
        name="fox_gate_cumsum",
    )(small, bias)


def _fox_prep_kernel(q_ref, k_ref, v_ref, kp_ref, vp_ref, s_ref, c_ref, qg_ref, kg_ref,
                     qo_ref, ko_ref, vo_ref, *, tm):
    i = pl.program_id(0)
    npairs = PREP_LANES // LANES
    head_ones = _head_ones()
    seq_start = (i * tm) % SEQ == 0
    rid = lax.broadcasted_iota(jnp.int32, (tm, 1), 0)
    r = lax.broadcasted_iota(jnp.int32, (LANES, 2 * LANES), 0)
    col = lax.broadcasted_iota(jnp.int32, (LANES, 2 * LANES), 1)
    ra = lax.broadcasted_iota(jnp.int32, (LANES, LANES), 0)
    ca = lax.broadcasted_iota(jnp.int32, (LANES, LANES), 1)
    small_pieces = s_ref[...]
    c_pieces = c_ref[...]

    def shifted(x, prow):
        prow = jnp.where(seq_start, jnp.zeros_like(prow), prow)
        return jnp.where(rid == 0, prow, pltpu.roll(x, 1, 0))

    def head_rms(x, gain):
        ms = _head_sums(x * x, head_ones) * (1.0 / HEAD_DIM)
        return x * lax.rsqrt(ms + RMS_EPS) * gain

    for p in range(npairs):
        pair = pl.program_id(1) * npairs + p
        ln = slice(p * LANES, (p + 1) * LANES)
        src = (((col & (LANES - 1)) >> HEAD_SHIFT) + HEADS_PER_VREG * pair
               + N_HEADS * (1 + (col >> LANE_SHIFT)))
        sel = (r == src).astype(bf16)
        logits = jnp.dot(small_pieces, jnp.concatenate([sel] * N_SPLIT, axis=0), preferred_element_type=f32)
        ak = jax.nn.sigmoid(logits[:, :LANES])
        av = jax.nn.sigmoid(logits[:, LANES:])
        k = k_ref[:, ln]
        v = v_ref[:, ln]
        k = ak * shifted(k, kp_ref[SUBLANES - 1:SUBLANES, ln]) + (1.0 - ak) * k
        v = av * shifted(v, vp_ref[SUBLANES - 1:SUBLANES, ln]) + (1.0 - av) * v
        qo_ref[:, ln] = (head_rms(q_ref[:, ln], qg_ref[...])
                         * (HEAD_DIM ** -0.5 * LOG2_E)).astype(qo_ref.dtype)
        sel_aug = []
        for n in range(N_SPLIT):
            hit = jnp.zeros((LANES, LANES), jnp.bool_)
            for h in range(HEADS_PER_VREG):
                hit = hit | ((ca == N_SPLIT * h + n) & (ra == HEADS_PER_VREG * pair + h))
            sel_aug.append(hit.astype(bf16))
        aug = jnp.dot(c_pieces, jnp.concatenate(sel_aug, axis=0), preferred_element_type=f32)
        ko_ref[:, 2 * p * LANES:2 * (p + 1) * LANES] = jnp.concatenate(
            [head_rms(k, kg_ref[...]), aug], axis=1).astype(ko_ref.dtype)
        vo_ref[ln, :] = v.T.astype(vo_ref.dtype)


def _fox_prep(proj, small_pieces, c_pieces, qn_g, kn_g, tm=1024):
    m = proj.shape[0]
    rb = tm // SUBLANES
    w = small_pieces.shape[1]
    nb = D_MODEL // PREP_LANES
    tile = lambda cb: pl.BlockSpec((tm, PREP_LANES), lambda i, j, cb=cb: (i, cb * nb + j))
    prev = lambda cb: pl.BlockSpec(
        (SUBLANES, PREP_LANES), lambda i, j, cb=cb: (jnp.maximum(i * rb - 1, 0), cb * nb + j))
    gain = pl.BlockSpec((1, LANES), lambda i, j: (0, 0))
    small_tile = pl.BlockSpec((tm, w), lambda i, j: (i, 0))
    tile_gain = lambda x: jnp.tile(x, HEADS_PER_VREG).reshape(1, LANES)
    return pl.pallas_call(
        functools.partial(_fox_prep_kernel, tm=tm),
        grid=(m // tm, nb),
        in_specs=[tile(0), tile(1), tile(2), prev(1), prev(2), small_tile, small_tile, gain, gain],
        out_specs=[pl.BlockSpec((tm, PREP_LANES), lambda i, j: (i, j)),
                   pl.BlockSpec((tm, 2 * PREP_LANES), lambda i, j: (i, j)),
                   pl.BlockSpec((PREP_LANES, tm), lambda i, j: (j, i))],
        out_shape=[jax.ShapeDtypeStruct((m, D_MODEL), bf16),
                   jax.ShapeDtypeStruct((m, 2 * D_MODEL), bf16),
                   jax.ShapeDtypeStruct((D_MODEL, m), bf16)],
        compiler_params=pltpu.CompilerParams(dimension_semantics=("parallel", "parallel"),
                                             vmem_limit_bytes=VMEM_LIMIT),
        name="fox_prep",
    )(proj, proj, proj, proj, proj, small_pieces, c_pieces, tile_gain(qn_g), tile_gain(kn_g))


def _fox_attn_kernel(q_ref, k_ref, vt_ref, gate_ref, og_ref, o_ref):
    tq, tk = ATT_TQ, ATT_TK
    qi = pl.program_id(2)
    npairs = ATT_LANES // LANES
    chains = [(p, h) for p in range(npairs) for h in range(HEADS_PER_VREG)]
    lane = lax.broadcasted_iota(jnp.int32, (tq, LANES), 1)
    q_aug = []
    for p, h in chains:
        q = q_ref[:, p * LANES:(p + 1) * LANES]
        own = (lane >> HEAD_SHIFT) == h
        minus_one = (lane >= N_SPLIT * h) & (lane < N_SPLIT * (h + 1))
        q_aug.append(jnp.concatenate([jnp.where(own, q, jnp.zeros_like(q)),
                                      jnp.where(minus_one, -1.0, 0.0).astype(bf16)], axis=1))

    def step(key0, width, carry, diag):
        m_run, l_run, acc = carry
        keys = pl.ds(pl.multiple_of(key0, tq), width)
        s = [_dot_nt(k_ref[keys, 2 * p * LANES:2 * (p + 1) * LANES], qa)
             for (p, h), qa in zip(chains, q_aug)]
        if diag:
            kidx = key0 + lax.broadcasted_iota(jnp.int32, (width, tq), 0)
            qidx = qi * tq + lax.broadcasted_iota(jnp.int32, (width, tq), 1)
            s = [jnp.where(qidx >= kidx, x, NEG_BIG) for x in s]
        m_new = [jnp.maximum(mr, jnp.max(x, axis=0, keepdims=True)) for mr, x in zip(m_run, s)]
        alpha = [jnp.exp2(mr - mn) for mr, mn in zip(m_run, m_new)]
        pr = [jnp.exp2(x - mn) for x, mn in zip(s, m_new)]
        l_new = [a * lr + jnp.sum(x, axis=0, keepdims=True) for a, lr, x in zip(alpha, l_run, pr)]
        pv = [jnp.dot(vt_ref[pl.ds((p * HEADS_PER_VREG + h) * HEAD_DIM, HEAD_DIM), keys], x.astype(bf16),
                      preferred_element_type=f32) for (p, h), x in zip(chains, pr)]
        acc = [ac * a + x for ac, a, x in zip(acc, alpha, pv)]
        return m_new, l_new, acc

    n = len(chains)
    init = ([jnp.full((1, tq), NEG_BIG, f32)] * n, [jnp.zeros((1, tq), f32)] * n,
            [jnp.zeros((HEAD_DIM, tq), f32)] * n)
    n_full = (qi * tq) // tk
    carry = lax.fori_loop(0, n_full, lambda j, cy: step(j * tk, tk, cy, False), init)
    _, l_run, acc = lax.cond((qi * tq) % tk == 0,
                             lambda cy: step(n_full * tk, tq, cy, True),
                             lambda cy: step(n_full * tk, tk, cy, True), carry)
    o_t = []
    for ac, lr in zip(acc, l_run):
        o = ac / lr
        o_t.append(o * lax.rsqrt(jnp.mean(o * o, axis=0, keepdims=True) + RMS_EPS))
    for p in range(npairs):
        ln = slice(p * LANES, (p + 1) * LANES)
        o = jnp.concatenate(o_t[HEADS_PER_VREG * p:HEADS_PER_VREG * (p + 1)], axis=0).T
        o_ref[:, ln] = (o * og_ref[:, ln] * jax.nn.sigmoid(gate_ref[:, ln])).astype(o_ref.dtype)


def _fox_attn(q, k_aug, v_t, proj, on_g):
    m, d = q.shape
    nq = SEQ // ATT_TQ
    gate_col0 = 3 * D_MODEL // ATT_LANES
    return pl.pallas_call(
        _fox_attn_kernel,
        grid=(BATCH, d // ATT_LANES, nq),
        in_specs=[pl.BlockSpec((ATT_TQ, ATT_LANES), lambda b, p, i: (b * nq + i, p)),
                  pl.BlockSpec((SEQ, 2 * ATT_LANES), lambda b, p, i: (b, p)),
                  pl.BlockSpec((ATT_LANES, SEQ), lambda b, p, i: (p, b)),
                  pl.BlockSpec((ATT_TQ, ATT_LANES), lambda b, p, i: (b * nq + i, gate_col0 + p)),
                  pl.BlockSpec((1, ATT_LANES), lambda b, p, i: (0, p))],
        out_specs=pl.BlockSpec((ATT_TQ, ATT_LANES), lambda b, p, i: (b * nq + i, p)),
        out_shape=jax.ShapeDtypeStruct((m, d), bf16),
        compiler_params=pltpu.CompilerParams(
            dimension_semantics=("parallel", "parallel", "arbitrary"),
            vmem_limit_bytes=VMEM_LIMIT),
        name="fox_attention",
    )(q, k_aug, v_t, proj, on_g.reshape(1, d))


def _swiglu_block(x, hn, w_gu, w_d, layer, next_norm_g, last):
    act = _matmul(hn, w_gu, layer, (0, 1), D_FF, tm=2048, tn=512, tk=D_MODEL,
                  epilogue=_epi_swiglu, out_dtype=bf16, name="swiglu_gate_up")
    return _matmul(act, w_d, layer, (0,), D_MODEL, tm=256, tn=D_MODEL, tk=D_FF, resident_w=True,
                   epilogue=_epi_residual_norm_only if last else _epi_residual_norm,
                   extras=((x, "tile"), (next_norm_g.reshape(1, D_MODEL), "row")),
                   out_dtype=f32 if last else (f32, bf16), name="swiglu_down")


def _proj(x, w, layer=0, *, n_out=None, out_dtype=f32, epilogue=_epi_plain, extras=(), name):
    if w.ndim == 2:
        w = w[None]
    n = w.shape[2] if n_out is None else n_out
    return _matmul(x, w, layer, (0,), n, tm=1024, tn=min(n, 1024), tk=x.shape[1], epilogue=epilogue,
                   extras=extras, out_dtype=out_dtype, name=name)


def _out_proj_norm(y, w_o, x, next_norm_g, *, name):
    return _matmul(y, w_o.astype(bf16)[None], 0, (0,), D_MODEL, tm=512, tn=D_MODEL, tk=D_MODEL,
                   epilogue=_epi_residual_norm, extras=((x, "tile"), (next_norm_g.reshape(1, D_MODEL), "row")),
                   out_dtype=(f32, bf16), name=name)


def _rwkv7_block(x, norm_g, mix, w_rkv, w0, w1, w2, a0, a1, a2, g1, g2, k_k, k_a, r_k, lnx_g, lnx_b, w_o,
                 next_norm_g):
    xr, xk, xv, xw, xa, xg = _norm_mix(x, norm_g, mix)
    w_rkv = w_rkv.astype(bf16)
    r = _proj(xr, w_rkv, 0, name="rwkv_r")
    k = _proj(xk, w_rkv, 1, name="rwkv_k")
    v = _proj(xv, w_rkv, 2, name="rwkv_v")
    rank = LANES * pl.cdiv(w1.shape[1], LANES)
    lw = _lora(xw, _pad_cols(w1, rank).astype(bf16), _pad_rows(w2, rank).astype(bf16), w0,
               act=jnp.tanh, epi=_epi_log_decay, name="rwkv_decay")
    a = _lora(xa, _pad_cols(a1, rank).astype(bf16), _pad_rows(a2, rank).astype(bf16), a0,
              act=lambda z: z, epi=_epi_sigmoid_bias, name="rwkv_iclr")
    g = _lora(xg, g1.astype(bf16), g2.astype(bf16), act=jax.nn.sigmoid, epi=_epi_identity, name="rwkv_gate")
    yg = _wkv(r, lw, k, v, a, g, k_k, k_a, r_k.reshape(-1), lnx_g, lnx_b)
    return _out_proj_norm(yg, w_o, x, next_norm_g, name="rwkv_out")


def _fox_block(x, hn, w_in, b_f, qn_g, kn_g, on_g, w_o, next_norm_g):
    n_main = 4 * D_MODEL
    proj = _proj(hn, w_in.astype(bf16), n_out=n_main, name="fox_in")
    w_small = _pad_cols(w_in[:, n_main:], LANES).astype(bf16)
    small = _proj(hn, w_small, name="fox_in_gates")
    c_pieces, small_pieces = _fox_gate(small, b_f)
    q, k_aug, v_t = _fox_prep(proj, small_pieces, c_pieces, qn_g, kn_g)
    og = _fox_attn(q, k_aug, v_t, proj, on_g)
    return _out_proj_norm(og, w_o, x, next_norm_g, name="fox_out")


def kernel(x, a_norm_g, a_mix, a_w_rkv, a_w0, a_w1, a_w2, a_a0, a_a1, a_a2, a_g1, a_g2, a_k_k, a_k_a, a_r_k, a_lnx_g, a_lnx_b, a_w_o, b_norm_g, b_w_in, b_b_f, b_qn_g, b_kn_g, b_on_g, b_w_o, f_norm_g, f_w_gu, f_w_d, final_g):
    b, t, d = x.shape
    h = x.reshape(b * t, d)
    w_gu = f_w_gu.astype(bf16)
    w_d = f_w_d.astype(bf16)
    h = _rwkv7_block(h, a_norm_g[0], a_mix[0], a_w_rkv[0], a_w0[0], a_w1[0], a_w2[0], a_a0[0], a_a1[0],
                     a_a2[0], a_g1[0], a_g2[0], a_k_k[0], a_k_a[0], a_r_k[0], a_lnx_g[0], a_lnx_b[0],
                     a_w_o[0], f_norm_g[0])
    h = _swiglu_block(*h, w_gu, w_d, 0, b_norm_g[0], False)
    h = _fox_block(*h, b_w_in[0], b_b_f[0], b_qn_g[0], b_kn_g[0], b_on_g[0], b_w_o[0],
                   f_norm_g[1])
    return _swiglu_block(*h, w_gu, w_d, 1, final_g, True).reshape(b, t, d)
```

```python
import functools

import jax
import jax.numpy as jnp
from jax import lax
from jax.experimental import pallas as pl
from jax.experimental.pallas import tpu as pltpu

D_MODEL = 2048
BATCH = 8
SEQ = 2048
N_TOK = BATCH * SEQ
HEAD_DIM = 64
HEAD_SHIFT = HEAD_DIM.bit_length() - 1
N_HEADS = D_MODEL // HEAD_DIM
D_FF = 5632
RMS_EPS = 1e-6
GN_EPS = 64e-5

LANES = 128
LANE_SHIFT = LANES.bit_length() - 1
SUBLANES = 8
HEADS_PER_VREG = LANES // HEAD_DIM
N_PAIRS = D_MODEL // LANES
VMEM_LIMIT = 56 * 1024 * 1024

WKV_CHUNK = 64
WKV_TBLK = 512
WKV_LANES = 512
ATT_TQ = 256
ATT_TK = 512
ATT_LANES = 512
PREP_LANES = 512
N_SPLIT = 3
NEG_BIG = -1e30
LOG2_E = 1.4426950408889634

f32 = jnp.float32
bf16 = jnp.bfloat16


def _dot(a, b):
    return jnp.dot(a.astype(bf16), b.astype(bf16), preferred_element_type=f32)


def _dot_nt(a, b):
    return lax.dot_general(a.astype(bf16), b.astype(bf16), (((1,), (1,)), ((), ())),
                           preferred_element_type=f32)


def _split3(x):
    hi = x.astype(bf16)
    r1 = x - hi.astype(f32)
    mid = r1.astype(bf16)
    lo = (r1 - mid.astype(f32)).astype(bf16)
    return hi, mid, lo


def _dot_sel_rhs(x, sel):
    hi, mid, lo = _split3(x)
    d = lambda p: jnp.dot(p, sel, preferred_element_type=f32)
    return d(hi) + d(mid) + d(lo)


def _dot_sel_lhs(sel, x):
    hi, mid, lo = _split3(x)
    d = lambda p: jnp.dot(sel, p, preferred_element_type=f32)
    return d(hi) + d(mid) + d(lo)


def _head_sums(x, head_ones):
    hi = x.astype(bf16)
    lo = (x - hi.astype(f32)).astype(bf16)
    return jnp.dot(jnp.concatenate([hi, lo], axis=1), jnp.concatenate([head_ones, head_ones], axis=0),
                   preferred_element_type=f32)


def _head_ones():
    r = lax.broadcasted_iota(jnp.int32, (LANES, LANES), 0) >> HEAD_SHIFT
    c = lax.broadcasted_iota(jnp.int32, (LANES, LANES), 1) >> HEAD_SHIFT
    return (r == c).astype(bf16)


def _first_head_mask():
    return lax.broadcasted_iota(jnp.int32, (1, LANES), 1) < HEAD_DIM


def _stack_heads(x, m0):
    z = jnp.zeros_like(x)
    return jnp.concatenate([jnp.where(m0, x, z), jnp.where(m0, z, x)], axis=0)


def _softplus(z):
    return jnp.maximum(z, 0.0) + jnp.log(1.0 + jnp.exp(-jnp.abs(z)))


def _rms(x, g):
    return x * lax.rsqrt(jnp.mean(x * x, axis=-1, keepdims=True) + RMS_EPS) * g


def _norm_mix_kernel(x_ref, xp_ref, g_ref, mix_ref, *o_refs, tm):
    i = pl.program_id(0)
    d = x_ref.shape[1]
    rows_per_chunk = 2 * SUBLANES
    lane_blk = 4 * LANES
    seq_start = (i * tm) % SEQ == 0
    hp_row = _rms(xp_ref[...], g_ref[...])[SUBLANES - 1:SUBLANES, :]
    hp_row = jnp.where(seq_start, jnp.zeros_like(hp_row), hp_row)
    rid = lax.broadcasted_iota(jnp.int32, (rows_per_chunk, 1), 0)

    def chunk(c, last_row):
        rows = pl.ds(pl.multiple_of(c * rows_per_chunk, rows_per_chunk), rows_per_chunk)
        x = x_ref[rows, :]
        inv = lax.rsqrt(jnp.mean(x * x, axis=-1, keepdims=True) + RMS_EPS)
        new_last = []
        for b in range(d // lane_blk):
            ln = slice(b * lane_blk, (b + 1) * lane_blk)
            h = x[:, ln] * inv * g_ref[:, ln]
            hprev = jnp.where(rid == 0, last_row[:, ln], pltpu.roll(h, 1, 0))
            xx = hprev - h
            for p, o_ref in enumerate(o_refs):
                o_ref[rows, ln] = (h + xx * mix_ref[p:p + 1, ln]).astype(o_ref.dtype)
            new_last.append(h[rows_per_chunk - 1:rows_per_chunk, :])
        return jnp.concatenate(new_last, axis=1)

    lax.fori_loop(0, tm // rows_per_chunk, chunk, hp_row, unroll=2)


def _norm_mix(x, g, mix, tm=512):
    m, d = x.shape
    rb = tm // SUBLANES
    out = jax.ShapeDtypeStruct((m, d), bf16)
    return pl.pallas_call(
        functools.partial(_norm_mix_kernel, tm=tm),
        grid=(m // tm,),
        in_specs=[pl.BlockSpec((tm, d), lambda i: (i, 0)),
                  pl.BlockSpec((SUBLANES, d), lambda i: (jnp.maximum(i * rb - 1, 0), 0)),
                  pl.BlockSpec((1, d), lambda i: (0, 0)),
                  pl.BlockSpec((6, d), lambda i: (0, 0))],
        out_specs=[pl.BlockSpec((tm, d), lambda i: (i, 0))] * 6,
        out_shape=[out] * 6,
        compiler_params=pltpu.CompilerParams(dimension_semantics=("parallel",),
                                             vmem_limit_bytes=VMEM_LIMIT),
        name="norm_mix",
    )(x, x, g.reshape(1, d), mix)


def _mm_kernel(*refs, n_w, n_e, n_o, nk, epilogue):
    x_ref = refs[0]
    w_refs = refs[1:1 + n_w]
    e_refs = refs[1 + n_w:1 + n_w + n_e]
    o_refs = refs[1 + n_w + n_e:1 + n_w + n_e + n_o]
    acc_refs = refs[1 + n_w + n_e + n_o:]
    x = x_ref[...]

    def finish(accs):
        outs = epilogue(accs, [e[...] for e in e_refs])
        for o_ref, out in zip(o_refs, outs if isinstance(outs, tuple) else (outs,)):
            o_ref[...] = out.astype(o_ref.dtype)

    if nk == 1:
        finish([jnp.dot(x, w[...].astype(x.dtype), preferred_element_type=f32) for w in w_refs])
        return
    k = pl.program_id(2)

    @pl.when(k == 0)
    def _():
        for a in acc_refs:
            a[...] = jnp.zeros_like(a)

    for a, w in zip(acc_refs, w_refs):
        a[...] += jnp.dot(x, w[...], preferred_element_type=f32)

    @pl.when(k == nk - 1)
    def _():
        finish([a[...] for a in acc_refs])


def _matmul(x, w, layer, w_col_blocks, n_out, *, tm, tn, tk, epilogue, extras=(), out_dtype, name,
            resident_w=False):
    m, kdim = x.shape
    nk = kdim // tk
    n_w = len(w_col_blocks)
    nb = n_out // tn
    in_specs = [pl.BlockSpec((tm, tk), lambda i, j, k: (i, k))]
    args = [x]
    w_mode = pl.Buffered(1) if resident_w else None
    for cb in w_col_blocks:
        in_specs.append(pl.BlockSpec((None, tk, tn), lambda i, j, k, cb=cb: (layer, k, cb * nb + j),
                                     pipeline_mode=w_mode))
        args.append(w)
    for arr, kind in extras:
        if kind == "row":
            in_specs.append(pl.BlockSpec((1, tn), lambda i, j, k: (0, j)))
        else:
            in_specs.append(pl.BlockSpec((tm, tn), lambda i, j, k: (i, j)))
        args.append(arr)
    scratch = [pltpu.VMEM((tm, tn), f32) for _ in range(n_w)] if nk > 1 else []
    multi = isinstance(out_dtype, tuple)
    dtypes = out_dtype if multi else (out_dtype,)
    out_spec = pl.BlockSpec((tm, tn), lambda i, j, k: (i, j))
    outs = pl.pallas_call(
        functools.partial(_mm_kernel, n_w=n_w, n_e=len(extras), n_o=len(dtypes), nk=nk, epilogue=epilogue),
        grid=(m // tm, nb, nk),
        in_specs=in_specs,
        out_specs=[out_spec] * len(dtypes),
        out_shape=[jax.ShapeDtypeStruct((m, n_out), dt) for dt in dtypes],
        scratch_shapes=scratch,
        compiler_params=pltpu.CompilerParams(
            dimension_semantics=("parallel", "parallel", "arbitrary"),
            vmem_limit_bytes=VMEM_LIMIT),
        name=name,
    )(*args)
    return tuple(outs) if multi else outs[0]


def _epi_plain(accs, extras):
    return accs[0]


def _epi_residual_norm(accs, extras):
    y = extras[0] + accs[0]
    return y, _rms(y, extras[1])


def _epi_residual_norm_only(accs, extras):
    return _rms(extras[0] + accs[0], extras[1])


def _epi_swiglu(accs, extras):
    gate, up = accs
    return gate * jax.nn.sigmoid(gate) * up


def _lora_kernel(x_ref, w1_ref, w2_ref, *rest, act, epi):
    *b_refs, o_ref = rest
    z = jnp.dot(x_ref[...], w1_ref[...], preferred_element_type=f32)
    y = jnp.dot(act(z).astype(bf16), w2_ref[...], preferred_element_type=f32)
    o_ref[...] = epi(y, *[b[...] for b in b_refs]).astype(o_ref.dtype)


def _lora(x, w1, w2, bias=None, *, act, epi, name, tm=512):
    m, d = x.shape
    r = w1.shape[1]
    n = w2.shape[1]
    biases = [] if bias is None else [bias.reshape(1, n)]
    return pl.pallas_call(
        functools.partial(_lora_kernel, act=act, epi=epi),
        grid=(m // tm,),
        in_specs=[pl.BlockSpec((tm, d), lambda i: (i, 0)),
                  pl.BlockSpec((d, r), lambda i: (0, 0)),
                  pl.BlockSpec((r, n), lambda i: (0, 0))] + [pl.BlockSpec((1, n), lambda i: (0, 0))] * len(biases),
        out_specs=pl.BlockSpec((tm, n), lambda i: (i, 0)),
        out_shape=jax.ShapeDtypeStruct((m, n), f32),
        compiler_params=pltpu.CompilerParams(dimension_semantics=("parallel",),
                                             vmem_limit_bytes=VMEM_LIMIT),
        name=name,
    )(x, w1, w2, *biases)


def _epi_log_decay(y, w0):
    return -jnp.exp(-_softplus(-(w0 + y)) - 0.5)


def _epi_sigmoid_bias(y, a0):
    return jax.nn.sigmoid(a0 + y)


def _epi_identity(y):
    return y


def _pad_cols(w, n):
    return jnp.pad(w, ((0, 0), (0, n - w.shape[1])))


def _pad_rows(w, n):
    return jnp.pad(w, ((0, n - w.shape[0]), (0, 0)))


def _each(fn, *lists):
    return [fn(*xs) for xs in zip(*lists)]


def _wkv_chunk_maps(tiles, consts):
    m0, strict, incl, eye = consts
    c = WKV_CHUNK
    n2 = 2 * c
    r, lw, lg, k2, v, av, bv = (list(x) for x in zip(*tiles))
    stack = lambda x: _stack_heads(x, m0)
    lg_end = _each(lambda x: x[c - 1:c, :], lg)
    g_inv = _each(lambda x: jnp.exp(-x), lg)
    g_rem = _each(lambda e, x: jnp.exp(e - x), lg_end, lg)
    rs = _each(lambda x, l: stack(x * jnp.exp(l)), r, lg)
    as_ = _each(lambda x, l, w: stack(x * jnp.exp(l - w)).astype(bf16), av, lg, lw)
    bs = _each(lambda x, g: stack(x * g).astype(bf16), bv, g_inv)
    ks = _each(lambda x, g: stack(x * g).astype(bf16), k2, g_inv)
    bhs = _each(lambda x, g: stack(x * g).astype(bf16), bv, g_rem)
    khs = _each(lambda x, g: stack(x * g).astype(bf16), k2, g_rem)
    vs = _each(stack, v)
    sc = _each(lambda a, rr, b, k: _dot_nt(jnp.concatenate([a, rr.astype(bf16)], axis=0),
                                           jnp.concatenate([b, k], axis=0)), as_, rs, bs, ks)
    zero = jnp.zeros((n2, n2), f32)
    a_ab = _each(lambda s: jnp.where(strict, s[:n2, :n2], zero), sc)
    a_ak = _each(lambda s: jnp.where(strict, s[:n2, n2:], zero), sc)
    a_r = _each(lambda s: jnp.where(jnp.concatenate([incl, incl], axis=1), s[n2:, :],
                                    jnp.zeros((n2, 2 * n2), f32)).astype(bf16), sc)
    side = lambda a, b: jnp.concatenate([a, b], axis=1)
    t = _each(lambda x: eye + x, a_ab)
    p = _each(lambda x: _dot(x, x), a_ab)
    for _ in range(c.bit_length() - 3):
        pt = _each(lambda pp, tt: _dot(pp, side(pp, tt)), p, t)
        p = _each(lambda x: x[:, :n2], pt)
        t = _each(lambda tt, x: tt + x[:, n2:], t, pt)
    t = _each(lambda tt, pp: (tt + _dot(pp, tt)).astype(bf16), t, p)
    akv = _each(_dot, a_ak, vs)
    hw = _each(lambda tt, a, x: _dot(tt, side(a, x)), t, as_, akv)
    ah = _each(lambda x: x[:, :LANES], hw)
    ws = _each(lambda x: x[:, LANES:], hw)
    pq = _each(lambda ar, x, vv: _dot(ar, jnp.concatenate([x, side(jnp.zeros_like(vv), vv)], axis=0)),
               a_r, hw, vs)
    pc = _each(lambda x, y: x + y[:, :LANES], rs, pq)
    qc = _each(lambda y: y[:, LANES:], pq)
    gm = _each(lambda h, b: _dot(h.T, b), ah, bhs)
    nc = _each(lambda w, x, b, k: _dot(jnp.concatenate([w, x], axis=0).T, jnp.concatenate([b, k], axis=0)),
               ws, vs, bhs, khs)
    decay = _each(jnp.exp, lg_end)
    return pc, qc, gm, nc, decay


def _wkv_kernel(r_ref, lw_ref, k_ref, v_ref, a_ref, g_ref, kk_ref, ka_ref, rk_ref, lng_ref, lnb_ref,
                o_ref, s_ref):
    c = WKV_CHUNK
    n2 = 2 * c

    @pl.when(pl.program_id(2) == 0)
    def _():
        s_ref[...] = jnp.zeros_like(s_ref)

    ri = lax.broadcasted_iota(jnp.int32, (n2, n2), 0)
    ci = lax.broadcasted_iota(jnp.int32, (n2, n2), 1)
    tr = lax.broadcasted_iota(jnp.int32, (c, c), 0)
    tc = lax.broadcasted_iota(jnp.int32, (c, c), 1)
    consts = (_first_head_mask(), (ri & (c - 1)) > (ci & (c - 1)), (ri & (c - 1)) >= (ci & (c - 1)),
              (ri == ci).astype(f32))
    tri = (tr >= tc).astype(bf16)
    head_ones = _head_ones()
    inv_n = 1.0 / HEAD_DIM

    npairs = WKV_LANES // LANES
    nchunk = WKV_TBLK // c
    lanes = [slice(p * LANES, (p + 1) * LANES) for p in range(npairs)]
    r = [r_ref[:, ln] for ln in lanes]
    v = [v_ref[:, ln] for ln in lanes]
    k2, av, bv = [], [], []
    for ln in lanes:
        k, a = k_ref[:, ln], a_ref[:, ln]
        kk = k * kk_ref[:, ln]
        ss = _head_sums(kk * kk, head_ones)
        kk = kk / jnp.maximum(jnp.sqrt(ss), 1e-12)
        k2.append(k * (1.0 + (a - 1.0) * ka_ref[:, ln]))
        av.append(-kk)
        bv.append(kk * a)
    tiles = []
    for ic in range(nchunk):
        rows = slice(ic * c, (ic + 1) * c)
        lw = lw_ref[rows, :]
        lg = _dot_sel_lhs(tri, lw)
        for p, ln in enumerate(lanes):
            tiles.append((r[p][rows], lw[:, ln], lg[:, ln], k2[p][rows], v[p][rows], av[p][rows], bv[p][rows]))
    pc, qc, gm, nc, decay = _wkv_chunk_maps(tiles, consts)
    s = [s_ref[p] for p in range(npairs)]
    ys = [[] for _ in range(npairs)]
    for ic in range(nchunk):
        for p in range(npairs):
            i = ic * npairs + p
            y = _dot_nt(pc[i], s[p]) + qc[i]
            ys[p].append(y[:c] + y[c:])
            s[p] = s[p] * decay[i] + _dot(s[p], gm[i]) + nc[i]
    for p, ln in enumerate(lanes):
        s_ref[p] = s[p]
        y = jnp.concatenate(ys[p], axis=0)
        mu = _head_sums(y, head_ones) * inv_n
        yc = y - mu
        var = _head_sums(yc * yc, head_ones) * inv_n
        yn = yc * lax.rsqrt(var + GN_EPS) * lng_ref[:, ln] + lnb_ref[:, ln]
        bonus = _head_sums(r[p] * k2[p] * rk_ref[:, ln], head_ones) * v[p]
        o_ref[:, ln] = ((yn + bonus) * g_ref[:, ln]).astype(o_ref.dtype)


def _wkv(r, lw, k, v, a, g, k_k, k_a, r_k, lnx_g, lnx_b):
    m, d = r.shape
    tb = SEQ // WKV_TBLK
    tile = pl.BlockSpec((WKV_TBLK, WKV_LANES), lambda b, j, t: (b * tb + t, j))
    row = pl.BlockSpec((1, WKV_LANES), lambda b, j, t: (0, j))
    rows = [x.reshape(1, d) for x in (k_k, k_a, r_k, lnx_g, lnx_b)]
    return pl.pallas_call(
        _wkv_kernel,
        grid=(BATCH, d // WKV_LANES, tb),
        in_specs=[tile] * 6 + [row] * 5,
        out_specs=tile,
        out_shape=jax.ShapeDtypeStruct((m, d), bf16),
        scratch_shapes=[pltpu.VMEM((WKV_LANES // LANES, LANES, LANES), f32)],
        compiler_params=pltpu.CompilerParams(
            dimension_semantics=("parallel", "parallel", "arbitrary"),
            vmem_limit_bytes=VMEM_LIMIT),
        name="wkv7",
    )(r, lw, k, v, a, g, *rows)


def _fox_gate_kernel(s_ref, bf_ref, cp_ref, sp_ref, run_ref, *, tc):
    @pl.when(pl.program_id(1) == 0)
    def _():
        run_ref[...] = jnp.zeros_like(run_ref)

    small = s_ref[...]
    ls = -_softplus(-(small + bf_ref[...]))
    tr = lax.broadcasted_iota(jnp.int32, (tc, tc), 0)
    tcc = lax.broadcasted_iota(jnp.int32, (tc, tc), 1)
    cs = _dot_sel_lhs((tr >= tcc).astype(bf16), ls) + run_ref[...]
    run_ref[...] = cs[tc - 1:tc, :]
    cp_ref[...] = jnp.concatenate(_split3(cs * LOG2_E), axis=1)
    sp_ref[...] = jnp.concatenate(_split3(small), axis=1)


def _fox_gate(small, b_f, tc=512):
    m, w = small.shape
    nt = SEQ // tc
    bias = jnp.pad(b_f, (0, w - b_f.shape[0])).reshape(1, w)
    pieces = pl.BlockSpec((tc, N_SPLIT * w), lambda b, t: (b * nt + t, 0))
    return pl.pallas_call(
        functools.partial(_fox_gate_kernel, tc=tc),
        grid=(BATCH, nt),
        in_specs=[pl.BlockSpec((tc, w), lambda b, t: (b * nt + t, 0)),
                  pl.BlockSpec((1, w), lambda b, t: (0, 0))],
        out_specs=[pieces, pieces],
        out_shape=[jax.ShapeDtypeStruct((m, N_SPLIT * w), bf16)] * 2,
        scratch_shapes=[pltpu.VMEM((1, w), f32)],
        compiler_params=pltpu.CompilerParams(dimension_semantics=("parallel", "arbitrary"),
                                             vmem_limit_bytes=VMEM_LIMIT),
        name="fox_gate_cumsum",
    )(small, bias)


def _fox_prep_kernel(q_ref, k_ref, v_ref, kp_ref, vp_ref, s_ref, c_ref, qg_ref, kg_ref,
                     qo_ref, ko_ref, vo_ref, *, tm):
    i = pl.program_id(0)
    npairs = PREP_LANES // LANES
    head_ones = _head_ones()
    seq_start = (i * tm) % SEQ == 0
    rid = lax.broadcasted_iota(jnp.int32, (tm, 1), 0)
    r = lax.broadcasted_iota(jnp.int32, (LANES, 2 * LANES), 0)
    col = lax.broadcasted_iota(jnp.int32, (LANES, 2 * LANES), 1)
    ra = lax.broadcasted_iota(jnp.int32, (LANES, LANES), 0)
    ca = lax.broadcasted_iota(jnp.int32, (LANES, LANES), 1)
    small_pieces = s_ref[...]
    c_pieces = c_ref[...]

    def shifted(x, prow):
        prow = jnp.where(seq_start, jnp.zeros_like(prow), prow)
        return jnp.where(rid == 0, prow, pltpu.roll(x, 1, 0))

    def head_rms(x, gain):
        ms = _head_sums(x * x, head_ones) * (1.0 / HEAD_DIM)
        return x * lax.rsqrt(ms + RMS_EPS) * gain

    for p in range(npairs):
        pair = pl.program_id(1) * npairs + p
        ln = slice(p * LANES, (p + 1) * LANES)
        src = (((col & (LANES - 1)) >> HEAD_SHIFT) + HEADS_PER_VREG * pair
               + N_HEADS * (1 + (col >> LANE_SHIFT)))
        sel = (r == src).astype(bf16)
        logits = jnp.dot(small_pieces, jnp.concatenate([sel] * N_SPLIT, axis=0), preferred_element_type=f32)
        ak = jax.nn.sigmoid(logits[:, :LANES])
        av = jax.nn.sigmoid(logits[:, LANES:])
        k = k_ref[:, ln]
        v = v_ref[:, ln]
        k = ak * shifted(k, kp_ref[SUBLANES - 1:SUBLANES, ln]) + (1.0 - ak) * k
        v = av * shifted(v, vp_ref[SUBLANES - 1:SUBLANES, ln]) + (1.0 - av) * v
        qo_ref[:, ln] = (head_rms(q_ref[:, ln], qg_ref[...])
                         * (HEAD_DIM ** -0.5 * LOG2_E)).astype(qo_ref.dtype)
        sel_aug = []
        for n in range(N_SPLIT):
            hit = jnp.zeros((LANES, LANES), jnp.bool_)
            for h in range(HEADS_PER_VREG):
                hit = hit | ((ca == N_SPLIT * h + n) & (ra == HEADS_PER_VREG * pair + h))
            sel_aug.append(hit.astype(bf16))
        aug = jnp.dot(c_pieces, jnp.concatenate(sel_aug, axis=0), preferred_element_type=f32)
        ko_ref[:, 2 * p * LANES:2 * (p + 1) * LANES] = jnp.concatenate(
            [head_rms(k, kg_ref[...]), aug], axis=1).astype(ko_ref.dtype)
        vo_ref[ln, :] = v.T.astype(vo_ref.dtype)


def _fox_prep(proj, small_pieces, c_pieces, qn_g, kn_g, tm=1024):
    m = proj.shape[0]
    rb = tm // SUBLANES
    w = small_pieces.shape[1]
    nb = D_MODEL // PREP_LANES
    tile = lambda cb: pl.BlockSpec((tm, PREP_LANES), lambda i, j, cb=cb: (i, cb * nb + j))
    prev = lambda cb: pl.BlockSpec(
        (SUBLANES, PREP_LANES), lambda i, j, cb=cb: (jnp.maximum(i * rb - 1, 0), cb * nb + j))
    gain = pl.BlockSpec((1, LANES), lambda i, j: (0, 0))
    small_tile = pl.BlockSpec((tm, w), lambda i, j: (i, 0))
    tile_gain = lambda x: jnp.tile(x, HEADS_PER_VREG).reshape(1, LANES)
    return pl.pallas_call(
        functools.partial(_fox_prep_kernel, tm=tm),
        grid=(m // tm, nb),
        in_specs=[tile(0), tile(1), tile(2), prev(1), prev(2), small_tile, small_tile, gain, gain],
        out_specs=[pl.BlockSpec((tm, PREP_LANES), lambda i, j: (i, j)),
                   pl.BlockSpec((tm, 2 * PREP_LANES), lambda i, j: (i, j)),
                   pl.BlockSpec((PREP_LANES, tm), lambda i, j: (j, i))],
        out_shape=[jax.ShapeDtypeStruct((m, D_MODEL), bf16),
                   jax.ShapeDtypeStruct((m, 2 * D_MODEL), bf16),
                   jax.ShapeDtypeStruct((D_MODEL, m), bf16)],
        compiler_params=pltpu.CompilerParams(dimension_semantics=("parallel", "parallel"),
                                             vmem_limit_bytes=VMEM_LIMIT),
        name="fox_prep",
    )(proj, proj, proj, proj, proj, small_pieces, c_pieces, tile_gain(qn_g), tile_gain(kn_g))


def _fox_attn_kernel(q_ref, k_ref, vt_ref, gate_ref, og_ref, o_ref):
    tq, tk = ATT_TQ, ATT_TK
    qi = pl.program_id(2)
    npairs = ATT_LANES // LANES
    chains = [(p, h) for p in range(npairs) for h in range(HEADS_PER_VREG)]
    lane = lax.broadcasted_iota(jnp.int32, (tq, LANES), 1)
    q_aug = []
    for p, h in chains:
        q = q_ref[:, p * LANES:(p + 1) * LANES]
        own = (lane >> HEAD_SHIFT) == h
        minus_one = (lane >= N_SPLIT * h) & (lane < N_SPLIT * (h + 1))
        q_aug.append(jnp.concatenate([jnp.where(own, q, jnp.zeros_like(q)),
                                      jnp.where(minus_one, -1.0, 0.0).astype(bf16)], axis=1))

    def step(key0, width, carry, diag):
        m_run, l_run, acc = carry
        keys = pl.ds(pl.multiple_of(key0, tq), width)
        s = [_dot_nt(k_ref[keys, 2 * p * LANES:2 * (p + 1) * LANES], qa)
             for (p, h), qa in zip(chains, q_aug)]
        if diag:
            kidx = key0 + lax.broadcasted_iota(jnp.int32, (width, tq), 0)
            qidx = qi * tq + lax.broadcasted_iota(jnp.int32, (width, tq), 1)
            s = [jnp.where(qidx >= kidx, x, NEG_BIG) for x in s]
        m_new = [jnp.maximum(mr, jnp.max(x, axis=0, keepdims=True)) for mr, x in zip(m_run, s)]
        alpha = [jnp.exp2(mr - mn) for mr, mn in zip(m_run, m_new)]
        pr = [jnp.exp2(x - mn) for x, mn in zip(s, m_new)]
        l_new = [a * lr + jnp.sum(x, axis=0, keepdims=True) for a, lr, x in zip(alpha, l_run, pr)]
        pv = [jnp.dot(vt_ref[pl.ds((p * HEADS_PER_VREG + h) * HEAD_DIM, HEAD_DIM), keys], x.astype(bf16),
                      preferred_element_type=f32) for (p, h), x in zip(chains, pr)]
        acc = [ac * a + x for ac, a, x in zip(acc, alpha, pv)]
        return m_new, l_new, acc

    n = len(chains)
    init = ([jnp.full((1, tq), NEG_BIG, f32)] * n, [jnp.zeros((1, tq), f32)] * n,
            [jnp.zeros((HEAD_DIM, tq), f32)] * n)
    n_full = (qi * tq) // tk
    carry = lax.fori_loop(0, n_full, lambda j, cy: step(j * tk, tk, cy, False), init)
    _, l_run, acc = lax.cond((qi * tq) % tk == 0,
                             lambda cy: step(n_full * tk, tq, cy, True),
                             lambda cy: step(n_full * tk, tk, cy, True), carry)
    o_t = []
    for ac, lr in zip(acc, l_run):
        o = ac / lr
        o_t.append(o * lax.rsqrt(jnp.mean(o * o, axis=0, keepdims=True) + RMS_EPS))
    for p in range(npairs):
        ln = slice(p * LANES, (p + 1) * LANES)
        o = jnp.concatenate(o_t[HEADS_PER_VREG * p:HEADS_PER_VREG * (p + 1)], axis=0).T
        o_ref[:, ln] = (o * og_ref[:, ln] * jax.nn.sigmoid(gate_ref[:, ln])).astype(o_ref.dtype)


def _fox_attn(q, k_aug, v_t, proj, on_g):
    m, d = q.shape
    nq = SEQ // ATT_TQ
    gate_col0 = 3 * D_MODEL // ATT_LANES
    return pl.pallas_call(
        _fox_attn_kernel,
        grid=(BATCH, d // ATT_LANES, nq),
        in_specs=[pl.BlockSpec((ATT_TQ, ATT_LANES), lambda b, p, i: (b * nq + i, p)),
                  pl.BlockSpec((SEQ, 2 * ATT_LANES), lambda b, p, i: (b, p)),
                  pl.BlockSpec((ATT_LANES, SEQ), lambda b, p, i: (p, b)),
                  pl.BlockSpec((ATT_TQ, ATT_LANES), lambda b, p, i: (b * nq + i, gate_col0 + p)),
                  pl.BlockSpec((1, ATT_LANES), lambda b, p, i: (0, p))],
        out_specs=pl.BlockSpec((ATT_TQ, ATT_LANES), lambda b, p, i: (b * nq + i, p)),
        out_shape=jax.ShapeDtypeStruct((m, d), bf16),
        compiler_params=pltpu.CompilerParams(
            dimension_semantics=("parallel", "parallel", "arbitrary"),
            vmem_limit_bytes=VMEM_LIMIT),
        name="fox_attention",
    )(q, k_aug, v_t, proj, on_g.reshape(1, d))


def _swiglu_block(x, hn, w_gu, w_d, layer, next_norm_g, last):
    act = _matmul(hn, w_gu, layer, (0, 1), D_FF, tm=2048, tn=512, tk=D_MODEL,
                  epilogue=_epi_swiglu, out_dtype=bf16, name="swiglu_gate_up")
    return _matmul(act, w_d, layer, (0,), D_MODEL, tm=256, tn=D_MODEL, tk=D_FF, resident_w=True,
                   epilogue=_epi_residual_norm_only if last else _epi_residual_norm,
                   extras=((x, "tile"), (next_norm_g.reshape(1, D_MODEL), "row")),
                   out_dtype=f32 if last else (f32, bf16), name="swiglu_down")


def _proj(x, w, layer=0, *, n_out=None, out_dtype=f32, epilogue=_epi_plain, extras=(), name):
    if w.ndim == 2:
        w = w[None]
    n = w.shape[2] if n_out is None else n_out
    return _matmul(x, w, layer, (0,), n, tm=1024, tn=min(n, 1024), tk=x.shape[1], epilogue=epilogue,
                   extras=extras, out_dtype=out_dtype, name=name)


def _out_proj_norm(y, w_o, x, next_norm_g, *, name):
    return _matmul(y, w_o.astype(bf16)[None], 0, (0,), D_MODEL, tm=512, tn=D_MODEL, tk=D_MODEL,
                   epilogue=_epi_residual_norm, extras=((x, "tile"), (next_norm_g.reshape(1, D_MODEL), "row")),
                   out_dtype=(f32, bf16), name=name)


def _rwkv7_block(x, norm_g, mix, w_rkv, w0, w1, w2, a0, a1, a2, g1, g2, k_k, k_a, r_k, lnx_g, lnx_b, w_o,
                 next_norm_g):
    xr, xk, xv, xw, xa, xg = _norm_mix(x, norm_g, mix)
    w_rkv = w_rkv.astype(bf16)
    r = _proj(xr, w_rkv, 0, name="rwkv_r")
    k = _proj(xk, w_rkv, 1, name="rwkv_k")
    v = _proj(xv, w_rkv, 2, name="rwkv_v")
    rank = LANES * pl.cdiv(w1.shape[1], LANES)
    lw = _lora(xw, _pad_cols(w1, rank).astype(bf16), _pad_rows(w2, rank).astype(bf16), w0,
               act=jnp.tanh, epi=_epi_log_decay, name="rwkv_decay")
    a = _lora(xa, _pad_cols(a1, rank).astype(bf16), _pad_rows(a2, rank).astype(bf16), a0,
              act=lambda z: z, epi=_epi_sigmoid_bias, name="rwkv_iclr")
    g = _lora(xg, g1.astype(bf16), g2.astype(bf16), act=jax.nn.sigmoid, epi=_epi_identity, name="rwkv_gate")
    yg = _wkv(r, lw, k, v, a, g, k_k, k_a, r_k.reshape(-1), lnx_g, lnx_b)
    return _out_proj_norm(yg, w_o, x, next_norm_g, name="rwkv_out")


def _fox_block(x, hn, w_in, b_f, qn_g, kn_g, on_g, w_o, next_norm_g):
    n_main = 4 * D_MODEL
    proj = _proj(hn, w_in, n_out=n_main, name="fox_in")
    w_small = _pad_cols(w_in[:, n_main:], LANES).astype(bf16)
    small = _proj(hn, w_small, name="fox_in_gates")
    c_pieces, small_pieces = _fox_gate(small, b_f)
    q, k_aug, v_t = _fox_prep(proj, small_pieces, c_pieces, qn_g, kn_g)
    og = _fox_attn(q, k_aug, v_t, proj, on_g)
    return _out_proj_norm(og, w_o, x, next_norm_g, name="fox_out")


def kernel(x, a_norm_g, a_mix, a_w_rkv, a_w0, a_w1, a_w2, a_a0, a_a1, a_a2, a_g1, a_g2, a_k_k, a_k_a, a_r_k, a_lnx_g, a_lnx_b, a_w_o, b_norm_g, b_w_in, b_b_f, b_qn_g, b_kn_g, b_on_g, b_w_o, f_norm_g, f_w_gu, f_w_d, final_g):
    b, t, d = x.shape
    h = x.reshape(b * t, d)
    w_d = f_w_d.astype(bf16)
    h = _rwkv7_block(h, a_norm_g[0], a_mix[0], a_w_rkv[0], a_w0[0], a_w1[0], a_w2[0], a_a0[0], a_a1[0],
                     a_a2[0], a_g1[0], a_g2[0], a_k_k[0], a_k_a[0], a_r_k[0], a_lnx_g[0], a_lnx_b[0],
                     a_w_o[0], f_norm_g[0])
    h = _swiglu_block(*h, f_w_gu, w_d, 0, b_norm_g[0], False)
    h = _fox_block(*h, b_w_in[0], b_b_f[0], b_qn_g[0], b_kn_g[0], b_on_g[0], b_w_o[0],
                   f_norm_g[1])
    return _swiglu_block(*h, f_w_gu, w_d, 1, final_g, True).reshape(b, t, d)
```

```python
import functools

import jax
import jax.numpy as jnp
from jax import lax
from jax.experimental import pallas as pl
from jax.experimental.pallas import tpu as pltpu

D_MODEL = 2048
BATCH = 8
SEQ = 2048
N_TOK = BATCH * SEQ
HEAD_DIM = 64
HEAD_SHIFT = HEAD_DIM.bit_length() - 1
N_HEADS = D_MODEL // HEAD_DIM
D_FF = 5632
RMS_EPS = 1e-6
GN_EPS = 64e-5

LANES = 128
LANE_SHIFT = LANES.bit_length() - 1
SUBLANES = 8
HEADS_PER_VREG = LANES // HEAD_DIM
N_PAIRS = D_MODEL // LANES
VMEM_LIMIT = 56 * 1024 * 1024

WKV_CHUNK = 64
WKV_TBLK = 512
WKV_LANES = 512
ATT_TQ = 256
ATT_TK = 512
ATT_LANES = 512
PREP_LANES = 512
N_SPLIT = 3
NEG_BIG = -1e30
LOG2_E = 1.4426950408889634

f32 = jnp.float32
bf16 = jnp.bfloat16


def _dot(a, b):
    return jnp.dot(a.astype(bf16), b.astype(bf16), preferred_element_type=f32)


def _dot_nt(a, b):
    return lax.dot_general(a.astype(bf16), b.astype(bf16), (((1,), (1,)), ((), ())),
                           preferred_element_type=f32)


def _split3(x):
    hi = x.astype(bf16)
    r1 = x - hi.astype(f32)
    mid = r1.astype(bf16)
    lo = (r1 - mid.astype(f32)).astype(bf16)
    return hi, mid, lo


def _dot_sel_rhs(x, sel):
    hi, mid, lo = _split3(x)
    d = lambda p: jnp.dot(p, sel, preferred_element_type=f32)
    return d(hi) + d(mid) + d(lo)


def _dot_sel_lhs(sel, x):
    hi, mid, lo = _split3(x)
    d = lambda p: jnp.dot(sel, p, preferred_element_type=f32)
    return d(hi) + d(mid) + d(lo)


def _head_sums(x, head_ones):
    hi = x.astype(bf16)
    lo = (x - hi.astype(f32)).astype(bf16)
    return jnp.dot(jnp.concatenate([hi, lo], axis=1), jnp.concatenate([head_ones, head_ones], axis=0),
                   preferred_element_type=f32)


def _head_ones():
    r = lax.broadcasted_iota(jnp.int32, (LANES, LANES), 0) >> HEAD_SHIFT
    c = lax.broadcasted_iota(jnp.int32, (LANES, LANES), 1) >> HEAD_SHIFT
    return (r == c).astype(bf16)


def _first_head_mask():
    return lax.broadcasted_iota(jnp.int32, (1, LANES), 1) < HEAD_DIM


def _stack_heads(x, m0):
    z = jnp.zeros_like(x)
    return jnp.concatenate([jnp.where(m0, x, z), jnp.where(m0, z, x)], axis=0)


def _softplus(z):
    return jnp.maximum(z, 0.0) + jnp.log(1.0 + jnp.exp(-jnp.abs(z)))


def _rms(x, g):
    return x * lax.rsqrt(jnp.mean(x * x, axis=-1, keepdims=True) + RMS_EPS) * g


def _norm_mix_kernel(x_ref, xp_ref, g_ref, mix_ref, w1_ref, a1_ref, g1_ref,
                     xr_ref, xk_ref, xv_ref, zw_ref, za_ref, zg_ref, xw_s, xa_s, xg_s, *, tm):
    i = pl.program_id(0)
    d = x_ref.shape[1]
    rows_per_chunk = 2 * SUBLANES
    lane_blk = 4 * LANES
    seq_start = (i * tm) % SEQ == 0
    hp_row = _rms(xp_ref[...], g_ref[...])[SUBLANES - 1:SUBLANES, :]
    hp_row = jnp.where(seq_start, jnp.zeros_like(hp_row), hp_row)
    rid = lax.broadcasted_iota(jnp.int32, (rows_per_chunk, 1), 0)
    mix_dsts = (xr_ref, xk_ref, xv_ref, xw_s, xa_s, xg_s)

    def chunk(c, last_row):
        rows = pl.ds(pl.multiple_of(c * rows_per_chunk, rows_per_chunk), rows_per_chunk)
        x = x_ref[rows, :]
        inv = lax.rsqrt(jnp.mean(x * x, axis=-1, keepdims=True) + RMS_EPS)
        new_last = []
        for b in range(d // lane_blk):
            ln = slice(b * lane_blk, (b + 1) * lane_blk)
            h = x[:, ln] * inv * g_ref[:, ln]
            hprev = jnp.where(rid == 0, last_row[:, ln], pltpu.roll(h, 1, 0))
            xx = hprev - h
            for p, dst in enumerate(mix_dsts):
                dst[rows, ln] = (h + xx * mix_ref[p:p + 1, ln]).astype(dst.dtype)
            new_last.append(h[rows_per_chunk - 1:rows_per_chunk, :])
        return jnp.concatenate(new_last, axis=1)

    lax.fori_loop(0, tm // rows_per_chunk, chunk, hp_row, unroll=2)
    down = lambda xs, w: jnp.dot(xs[...], w[...], preferred_element_type=f32)
    zw_ref[...] = jnp.tanh(down(xw_s, w1_ref)).astype(zw_ref.dtype)
    za_ref[...] = down(xa_s, a1_ref).astype(za_ref.dtype)
    zg_ref[...] = jax.nn.sigmoid(down(xg_s, g1_ref)).astype(zg_ref.dtype)


def _norm_mix(x, g, mix, w1, a1, g1, tm=512):
    m, d = x.shape
    rb = tm // SUBLANES
    row_tile = lambda n: pl.BlockSpec((tm, n), lambda i: (i, 0))
    whole = lambda a: pl.BlockSpec(a.shape, lambda i: (0, 0))
    lows = (w1, a1, g1)
    return pl.pallas_call(
        functools.partial(_norm_mix_kernel, tm=tm),
        grid=(m // tm,),
        in_specs=[row_tile(d),
                  pl.BlockSpec((SUBLANES, d), lambda i: (jnp.maximum(i * rb - 1, 0), 0)),
                  pl.BlockSpec((1, d), lambda i: (0, 0)),
                  pl.BlockSpec((6, d), lambda i: (0, 0))] + [whole(w) for w in lows],
        out_specs=[row_tile(d)] * 3 + [row_tile(w.shape[1]) for w in lows],
        out_shape=[jax.ShapeDtypeStruct((m, d), bf16)] * 3
        + [jax.ShapeDtypeStruct((m, w.shape[1]), bf16) for w in lows],
        scratch_shapes=[pltpu.VMEM((tm, d), bf16)] * 3,
        compiler_params=pltpu.CompilerParams(dimension_semantics=("parallel",),
                                             vmem_limit_bytes=VMEM_LIMIT),
        name="norm_mix",
    )(x, x, g.reshape(1, d), mix, *lows)


def _mm_kernel(*refs, n_w, n_e, n_o, nk, epilogue):
    x_ref = refs[0]
    w_refs = refs[1:1 + n_w]
    e_refs = refs[1 + n_w:1 + n_w + n_e]
    o_refs = refs[1 + n_w + n_e:1 + n_w + n_e + n_o]
    acc_refs = refs[1 + n_w + n_e + n_o:]
    x = x_ref[...]

    def finish(accs):
        outs = epilogue(accs, [e[...] for e in e_refs])
        for o_ref, out in zip(o_refs, outs if isinstance(outs, tuple) else (outs,)):
            o_ref[...] = out.astype(o_ref.dtype)

    if nk == 1:
        finish([jnp.dot(x, w[...].astype(x.dtype), preferred_element_type=f32) for w in w_refs])
        return
    k = pl.program_id(2)

    @pl.when(k == 0)
    def _():
        for a in acc_refs:
            a[...] = jnp.zeros_like(a)

    for a, w in zip(acc_refs, w_refs):
        a[...] += jnp.dot(x, w[...], preferred_element_type=f32)

    @pl.when(k == nk - 1)
    def _():
        finish([a[...] for a in acc_refs])


def _matmul(x, w, layer, w_col_blocks, n_out, *, tm, tn, tk, epilogue, extras=(), out_dtype, name,
            resident_w=False):
    m, kdim = x.shape
    nk = kdim // tk
    n_w = len(w_col_blocks)
    nb = n_out // tn
    in_specs = [pl.BlockSpec((tm, tk), lambda i, j, k: (i, k))]
    args = [x]
    w_mode = pl.Buffered(1) if resident_w else None
    for cb in w_col_blocks:
        in_specs.append(pl.BlockSpec((None, tk, tn), lambda i, j, k, cb=cb: (layer, k, cb * nb + j),
                                     pipeline_mode=w_mode))
        args.append(w)
    for arr, kind in extras:
        if kind == "row":
            in_specs.append(pl.BlockSpec((1, tn), lambda i, j, k: (0, j)))
        else:
            in_specs.append(pl.BlockSpec((tm, tn), lambda i, j, k: (i, j)))
        args.append(arr)
    scratch = [pltpu.VMEM((tm, tn), f32) for _ in range(n_w)] if nk > 1 else []
    multi = isinstance(out_dtype, tuple)
    dtypes = out_dtype if multi else (out_dtype,)
    out_spec = pl.BlockSpec((tm, tn), lambda i, j, k: (i, j))
    outs = pl.pallas_call(
        functools.partial(_mm_kernel, n_w=n_w, n_e=len(extras), n_o=len(dtypes), nk=nk, epilogue=epilogue),
        grid=(m // tm, nb, nk),
        in_specs=in_specs,
        out_specs=[out_spec] * len(dtypes),
        out_shape=[jax.ShapeDtypeStruct((m, n_out), dt) for dt in dtypes],
        scratch_shapes=scratch,
        compiler_params=pltpu.CompilerParams(
            dimension_semantics=("parallel", "parallel", "arbitrary"),
            vmem_limit_bytes=VMEM_LIMIT),
        name=name,
    )(*args)
    return tuple(outs) if multi else outs[0]


def _epi_plain(accs, extras):
    return accs[0]


def _epi_residual_norm(accs, extras):
    y = extras[0] + accs[0]
    return y, _rms(y, extras[1])


def _epi_residual_norm_only(accs, extras):
    return _rms(extras[0] + accs[0], extras[1])


def _epi_swiglu(accs, extras):
    gate, up = accs
    return gate * jax.nn.sigmoid(gate) * up


def _log_decay(y, w0):
    return -jnp.exp(-_softplus(-(w0 + y)) - 0.5)


def _pad_cols(w, n):
    return jnp.pad(w, ((0, 0), (0, n - w.shape[1])))


def _pad_rows(w, n):
    return jnp.pad(w, ((0, n - w.shape[0]), (0, 0)))


def _each(fn, *lists):
    return [fn(*xs) for xs in zip(*lists)]


def _wkv_chunk_maps(tiles, consts):
    m0, strict, incl, eye = consts
    c = WKV_CHUNK
    n2 = 2 * c
    r, lw, lg, k2, v, av, bv = (list(x) for x in zip(*tiles))
    stack = lambda x: _stack_heads(x, m0)
    lg_end = _each(lambda x: x[c - 1:c, :], lg)
    g_inv = _each(lambda x: jnp.exp(-x), lg)
    g_rem = _each(lambda e, x: jnp.exp(e - x), lg_end, lg)
    rs = _each(lambda x, l: stack(x * jnp.exp(l)), r, lg)
    as_ = _each(lambda x, l, w: stack(x * jnp.exp(l - w)).astype(bf16), av, lg, lw)
    bs = _each(lambda x, g: stack(x * g).astype(bf16), bv, g_inv)
    ks = _each(lambda x, g: stack(x * g).astype(bf16), k2, g_inv)
    bhs = _each(lambda x, g: stack(x * g).astype(bf16), bv, g_rem)
    khs = _each(lambda x, g: stack(x * g).astype(bf16), k2, g_rem)
    vs = _each(stack, v)
    sc = _each(lambda a, rr, b, k: _dot_nt(jnp.concatenate([a, rr.astype(bf16)], axis=0),
                                           jnp.concatenate([b, k], axis=0)), as_, rs, bs, ks)
    zero = jnp.zeros((n2, n2), f32)
    a_ab = _each(lambda s: jnp.where(strict, s[:n2, :n2], zero), sc)
    a_ak = _each(lambda s: jnp.where(strict, s[:n2, n2:], zero), sc)
    a_r = _each(lambda s: jnp.where(jnp.concatenate([incl, incl], axis=1), s[n2:, :],
                                    jnp.zeros((n2, 2 * n2), f32)).astype(bf16), sc)
    side = lambda a, b: jnp.concatenate([a, b], axis=1)
    t = _each(lambda x: eye + x, a_ab)
    p = _each(lambda x: _dot(x, x), a_ab)
    for _ in range(c.bit_length() - 3):
        pt = _each(lambda pp, tt: _dot(pp, side(pp, tt)), p, t)
        p = _each(lambda x: x[:, :n2], pt)
        t = _each(lambda tt, x: tt + x[:, n2:], t, pt)
    t = _each(lambda tt, pp: (tt + _dot(pp, tt)).astype(bf16), t, p)
    akv = _each(_dot, a_ak, vs)
    hw = _each(lambda tt, a, x: _dot(tt, side(a, x)), t, as_, akv)
    ah = _each(lambda x: x[:, :LANES], hw)
    ws = _each(lambda x: x[:, LANES:], hw)
    pq = _each(lambda ar, x, vv: _dot(ar, jnp.concatenate([x, side(jnp.zeros_like(vv), vv)], axis=0)),
               a_r, hw, vs)
    pc = _each(lambda x, y: x + y[:, :LANES], rs, pq)
    qc = _each(lambda y: y[:, LANES:], pq)
    gm = _each(lambda h, b: _dot(h.T, b), ah, bhs)
    nc = _each(lambda w, x, b, k: _dot(jnp.concatenate([w, x], axis=0).T, jnp.concatenate([b, k], axis=0)),
               ws, vs, bhs, khs)
    decay = _each(jnp.exp, lg_end)
    return pc, qc, gm, nc, decay


def _wkv_kernel(r_ref, k_ref, v_ref, zw_ref, za_ref, zg_ref, w2_ref, a2_ref, g2_ref, w0_ref, a0_ref,
                kk_ref, ka_ref, rk_ref, lng_ref, lnb_ref, o_ref, s_ref):
    c = WKV_CHUNK
    n2 = 2 * c
    up = lambda z_ref, w_ref: jnp.dot(z_ref[...], w_ref[...], preferred_element_type=f32)
    lw_all = _log_decay(up(zw_ref, w2_ref), w0_ref[...])
    a_all = jax.nn.sigmoid(a0_ref[...] + up(za_ref, a2_ref))
    g_all = up(zg_ref, g2_ref)

    @pl.when(pl.program_id(2) == 0)
    def _():
        s_ref[...] = jnp.zeros_like(s_ref)

    ri = lax.broadcasted_iota(jnp.int32, (n2, n2), 0)
    ci = lax.broadcasted_iota(jnp.int32, (n2, n2), 1)
    tr = lax.broadcasted_iota(jnp.int32, (c, c), 0)
    tc = lax.broadcasted_iota(jnp.int32, (c, c), 1)
    consts = (_first_head_mask(), (ri & (c - 1)) > (ci & (c - 1)), (ri & (c - 1)) >= (ci & (c - 1)),
              (ri == ci).astype(f32))
    tri = (tr >= tc).astype(bf16)
    head_ones = _head_ones()
    inv_n = 1.0 / HEAD_DIM

    npairs = WKV_LANES // LANES
    nchunk = WKV_TBLK // c
    lanes = [slice(p * LANES, (p + 1) * LANES) for p in range(npairs)]
    r = [r_ref[:, ln] for ln in lanes]
    v = [v_ref[:, ln] for ln in lanes]
    k2, av, bv = [], [], []
    for ln in lanes:
        k, a = k_ref[:, ln], a_all[:, ln]
        kk = k * kk_ref[:, ln]
        ss = _head_sums(kk * kk, head_ones)
        kk = kk / jnp.maximum(jnp.sqrt(ss), 1e-12)
        k2.append(k * (1.0 + (a - 1.0) * ka_ref[:, ln]))
        av.append(-kk)
        bv.append(kk * a)
    tiles = []
    for ic in range(nchunk):
        rows = slice(ic * c, (ic + 1) * c)
        lw = lw_all[rows, :]
        lg = _dot_sel_lhs(tri, lw)
        for p, ln in enumerate(lanes):
            tiles.append((r[p][rows], lw[:, ln], lg[:, ln], k2[p][rows], v[p][rows], av[p][rows], bv[p][rows]))
    pc, qc, gm, nc, decay = _wkv_chunk_maps(tiles, consts)
    s = [s_ref[p] for p in range(npairs)]
    ys = [[] for _ in range(npairs)]
    for ic in range(nchunk):
        for p in range(npairs):
            i = ic * npairs + p
            y = _dot_nt(pc[i], s[p]) + qc[i]
            ys[p].append(y[:c] + y[c:])
            s[p] = s[p] * decay[i] + _dot(s[p], gm[i]) + nc[i]
    for p, ln in enumerate(lanes):
        s_ref[p] = s[p]
        y = jnp.concatenate(ys[p], axis=0)
        mu = _head_sums(y, head_ones) * inv_n
        yc = y - mu
        var = _head_sums(yc * yc, head_ones) * inv_n
        yn = yc * lax.rsqrt(var + GN_EPS) * lng_ref[:, ln] + lnb_ref[:, ln]
        bonus = _head_sums(r[p] * k2[p] * rk_ref[:, ln], head_ones) * v[p]
        o_ref[:, ln] = ((yn + bonus) * g_all[:, ln]).astype(o_ref.dtype)


def _wkv(r, k, v, zs, ups, w0, a0, k_k, k_a, r_k, lnx_g, lnx_b):
    m, d = r.shape
    tb = SEQ // WKV_TBLK
    tile = pl.BlockSpec((WKV_TBLK, WKV_LANES), lambda b, j, t: (b * tb + t, j))
    row = pl.BlockSpec((1, WKV_LANES), lambda b, j, t: (0, j))
    z_tile = lambda z: pl.BlockSpec((WKV_TBLK, z.shape[1]), lambda b, j, t: (b * tb + t, 0))
    up_tile = lambda w: pl.BlockSpec((w.shape[0], WKV_LANES), lambda b, j, t: (0, j))
    rows = [x.reshape(1, d) for x in (w0, a0, k_k, k_a, r_k, lnx_g, lnx_b)]
    return pl.pallas_call(
        _wkv_kernel,
        grid=(BATCH, d // WKV_LANES, tb),
        in_specs=[tile] * 3 + [z_tile(z) for z in zs] + [up_tile(w) for w in ups] + [row] * len(rows),
        out_specs=tile,
        out_shape=jax.ShapeDtypeStruct((m, d), bf16),
        scratch_shapes=[pltpu.VMEM((WKV_LANES // LANES, LANES, LANES), f32)],
        compiler_params=pltpu.CompilerParams(
            dimension_semantics=("parallel", "parallel", "arbitrary"),
            vmem_limit_bytes=VMEM_LIMIT),
        name="wkv7",
    )(r, k, v, *zs, *ups, *rows)


def _fox_gate_kernel(s_ref, bf_ref, cp_ref, sp_ref, run_ref, *, tc):
    @pl.when(pl.program_id(1) == 0)
    def _():
        run_ref[...] = jnp.zeros_like(run_ref)

    small = s_ref[...]
    ls = -_softplus(-(small + bf_ref[...]))
    tr = lax.broadcasted_iota(jnp.int32, (tc, tc), 0)
    tcc = lax.broadcasted_iota(jnp.int32, (tc, tc), 1)
    cs = _dot_sel_lhs((tr >= tcc).astype(bf16), ls) + run_ref[...]
    run_ref[...] = cs[tc - 1:tc, :]
    cp_ref[...] = jnp.concatenate(_split3(cs * LOG2_E), axis=1)
    sp_ref[...] = jnp.concatenate(_split3(small), axis=1)


def _fox_gate(small, b_f, tc=512):
    m, w = small.shape
    nt = SEQ // tc
    bias = jnp.pad(b_f, (0, w - b_f.shape[0])).reshape(1, w)
    pieces = pl.BlockSpec((tc, N_SPLIT * w), lambda b, t: (b * nt + t, 0))
    return pl.pallas_call(
        functools.partial(_fox_gate_kernel, tc=tc),
        grid=(BATCH, nt),
        in_specs=[pl.BlockSpec((tc, w), lambda b, t: (b * nt + t, 0)),
                  pl.BlockSpec((1, w), lambda b, t: (0, 0))],
        out_specs=[pieces, pieces],
        out_shape=[jax.ShapeDtypeStruct((m, N_SPLIT * w), bf16)] * 2,
        scratch_shapes=[pltpu.VMEM((1, w), f32)],
        compiler_params=pltpu.CompilerParams(dimension_semantics=("parallel", "arbitrary"),
                                             vmem_limit_bytes=VMEM_LIMIT),
        name="fox_gate_cumsum",
    )(small, bias)


def _fox_prep_kernel(q_ref, k_ref, v_ref, kp_ref, vp_ref, s_ref, c_ref, qg_ref, kg_ref,
                     qo_ref, ko_ref, vo_ref, *, tm):
    i = pl.program_id(0)
    npairs = PREP_LANES // LANES
    head_ones = _head_ones()
    seq_start = (i * tm) % SEQ == 0
    rid = lax.broadcasted_iota(jnp.int32, (tm, 1), 0)
    r = lax.broadcasted_iota(jnp.int32, (LANES, 2 * LANES), 0)
    col = lax.broadcasted_iota(jnp.int32, (LANES, 2 * LANES), 1)
    ra = lax.broadcasted_iota(jnp.int32, (LANES, LANES), 0)
    ca = lax.broadcasted_iota(jnp.int32, (LANES, LANES), 1)
    small_pieces = s_ref[...]
    c_pieces = c_ref[...]

    def shifted(x, prow):
        prow = jnp.where(seq_start, jnp.zeros_like(prow), prow)
        return jnp.where(rid == 0, prow, pltpu.roll(x, 1, 0))

    def head_rms(x, gain):
        ms = _head_sums(x * x, head_ones) * (1.0 / HEAD_DIM)
        return x * lax.rsqrt(ms + RMS_EPS) * gain

    for p in range(npairs):
        pair = pl.program_id(1) * npairs + p
        ln = slice(p * LANES, (p + 1) * LANES)
        src = (((col & (LANES - 1)) >> HEAD_SHIFT) + HEADS_PER_VREG * pair
               + N_HEADS * (1 + (col >> LANE_SHIFT)))
        sel = (r == src).astype(bf16)
        logits = jnp.dot(small_pieces, jnp.concatenate([sel] * N_SPLIT, axis=0), preferred_element_type=f32)
        ak = jax.nn.sigmoid(logits[:, :LANES])
        av = jax.nn.sigmoid(logits[:, LANES:])
        k = k_ref[:, ln]
        v = v_ref[:, ln]
        k = ak * shifted(k, kp_ref[SUBLANES - 1:SUBLANES, ln]) + (1.0 - ak) * k
        v = av * shifted(v, vp_ref[SUBLANES - 1:SUBLANES, ln]) + (1.0 - av) * v
        qo_ref[:, ln] = (head_rms(q_ref[:, ln], qg_ref[...])
                         * (HEAD_DIM ** -0.5 * LOG2_E)).astype(qo_ref.dtype)
        sel_aug = []
        for n in range(N_SPLIT):
            hit = jnp.zeros((LANES, LANES), jnp.bool_)
            for h in range(HEADS_PER_VREG):
                hit = hit | ((ca == N_SPLIT * h + n) & (ra == HEADS_PER_VREG * pair + h))
            sel_aug.append(hit.astype(bf16))
        aug = jnp.dot(c_pieces, jnp.concatenate(sel_aug, axis=0), preferred_element_type=f32)
        ko_ref[:, 2 * p * LANES:2 * (p + 1) * LANES] = jnp.concatenate(
            [head_rms(k, kg_ref[...]), aug], axis=1).astype(ko_ref.dtype)
        vo_ref[ln, :] = v.T.astype(vo_ref.dtype)


def _fox_prep(proj, small_pieces, c_pieces, qn_g, kn_g, tm=1024):
    m = proj.shape[0]
    rb = tm // SUBLANES
    w = small_pieces.shape[1]
    nb = D_MODEL // PREP_LANES
    tile = lambda cb: pl.BlockSpec((tm, PREP_LANES), lambda i, j, cb=cb: (i, cb * nb + j))
    prev = lambda cb: pl.BlockSpec(
        (SUBLANES, PREP_LANES), lambda i, j, cb=cb: (jnp.maximum(i * rb - 1, 0), cb * nb + j))
    gain = pl.BlockSpec((1, LANES), lambda i, j: (0, 0))
    small_tile = pl.BlockSpec((tm, w), lambda i, j: (i, 0))
    tile_gain = lambda x: jnp.tile(x, HEADS_PER_VREG).reshape(1, LANES)
    return pl.pallas_call(
        functools.partial(_fox_prep_kernel, tm=tm),
        grid=(m // tm, nb),
        in_specs=[tile(0), tile(1), tile(2), prev(1), prev(2), small_tile, small_tile, gain, gain],
        out_specs=[pl.BlockSpec((tm, PREP_LANES), lambda i, j: (i, j)),
                   pl.BlockSpec((tm, 2 * PREP_LANES), lambda i, j: (i, j)),
                   pl.BlockSpec((PREP_LANES, tm), lambda i, j: (j, i))],
        out_shape=[jax.ShapeDtypeStruct((m, D_MODEL), bf16),
                   jax.ShapeDtypeStruct((m, 2 * D_MODEL), bf16),
                   jax.ShapeDtypeStruct((D_MODEL, m), bf16)],
        compiler_params=pltpu.CompilerParams(dimension_semantics=("parallel", "parallel"),
                                             vmem_limit_bytes=VMEM_LIMIT),
        name="fox_prep",
    )(proj, proj, proj, proj, proj, small_pieces, c_pieces, tile_gain(qn_g), tile_gain(kn_g))


def _fox_attn_kernel(q_ref, k_ref, vt_ref, gate_ref, og_ref, o_ref):
    tq, tk = ATT_TQ, ATT_TK
    qi = pl.program_id(2)
    npairs = ATT_LANES // LANES
    chains = [(p, h) for p in range(npairs) for h in range(HEADS_PER_VREG)]
    lane = lax.broadcasted_iota(jnp.int32, (tq, LANES), 1)
    q_aug = []
    for p, h in chains:
        q = q_ref[:, p * LANES:(p + 1) * LANES]
        own = (lane >> HEAD_SHIFT) == h
        minus_one = (lane >= N_SPLIT * h) & (lane < N_SPLIT * (h + 1))
        q_aug.append(jnp.concatenate([jnp.where(own, q, jnp.zeros_like(q)),
                                      jnp.where(minus_one, -1.0, 0.0).astype(bf16)], axis=1))

    def step(key0, width, carry, diag):
        m_run, l_run, acc = carry
        keys = pl.ds(pl.multiple_of(key0, tq), width)
        s = [_dot_nt(k_ref[keys, 2 * p * LANES:2 * (p + 1) * LANES], qa)
             for (p, h), qa in zip(chains, q_aug)]
        if diag:
            kidx = key0 + lax.broadcasted_iota(jnp.int32, (width, tq), 0)
            qidx = qi * tq + lax.broadcasted_iota(jnp.int32, (width, tq), 1)
            s = [jnp.where(qidx >= kidx, x, NEG_BIG) for x in s]
        m_new = [jnp.maximum(mr, jnp.max(x, axis=0, keepdims=True)) for mr, x in zip(m_run, s)]
        alpha = [jnp.exp2(mr - mn) for mr, mn in zip(m_run, m_new)]
        pr = [jnp.exp2(x - mn) for x, mn in zip(s, m_new)]
        l_new = [a * lr + jnp.sum(x, axis=0, keepdims=True) for a, lr, x in zip(alpha, l_run, pr)]
        pv = [jnp.dot(vt_ref[pl.ds((p * HEADS_PER_VREG + h) * HEAD_DIM, HEAD_DIM), keys], x.astype(bf16),
                      preferred_element_type=f32) for (p, h), x in zip(chains, pr)]
        acc = [ac * a + x for ac, a, x in zip(acc, alpha, pv)]
        return m_new, l_new, acc

    n = len(chains)
    init = ([jnp.full((1, tq), NEG_BIG, f32)] * n, [jnp.zeros((1, tq), f32)] * n,
            [jnp.zeros((HEAD_DIM, tq), f32)] * n)
    n_full = (qi * tq) // tk
    carry = lax.fori_loop(0, n_full, lambda j, cy: step(j * tk, tk, cy, False), init)
    _, l_run, acc = lax.cond((qi * tq) % tk == 0,
                             lambda cy: step(n_full * tk, tq, cy, True),
                             lambda cy: step(n_full * tk, tk, cy, True), carry)
    o_t = []
    for ac, lr in zip(acc, l_run):
        o = ac / lr
        o_t.append(o * lax.rsqrt(jnp.mean(o * o, axis=0, keepdims=True) + RMS_EPS))
    for p in range(npairs):
        ln = slice(p * LANES, (p + 1) * LANES)
        o = jnp.concatenate(o_t[HEADS_PER_VREG * p:HEADS_PER_VREG * (p + 1)], axis=0).T
        o_ref[:, ln] = (o * og_ref[:, ln] * jax.nn.sigmoid(gate_ref[:, ln])).astype(o_ref.dtype)


def _fox_attn(q, k_aug, v_t, proj, on_g):
    m, d = q.shape
    nq = SEQ // ATT_TQ
    gate_col0 = 3 * D_MODEL // ATT_LANES
    return pl.pallas_call(
        _fox_attn_kernel,
        grid=(BATCH, d // ATT_LANES, nq),
        in_specs=[pl.BlockSpec((ATT_TQ, ATT_LANES), lambda b, p, i: (b * nq + i, p)),
                  pl.BlockSpec((SEQ, 2 * ATT_LANES), lambda b, p, i: (b, p)),
                  pl.BlockSpec((ATT_LANES, SEQ), lambda b, p, i: (p, b)),
                  pl.BlockSpec((ATT_TQ, ATT_LANES), lambda b, p, i: (b * nq + i, gate_col0 + p)),
                  pl.BlockSpec((1, ATT_LANES), lambda b, p, i: (0, p))],
        out_specs=pl.BlockSpec((ATT_TQ, ATT_LANES), lambda b, p, i: (b * nq + i, p)),
        out_shape=jax.ShapeDtypeStruct((m, d), bf16),
        compiler_params=pltpu.CompilerParams(
            dimension_semantics=("parallel", "parallel", "arbitrary"),
            vmem_limit_bytes=VMEM_LIMIT),
        name="fox_attention",
    )(q, k_aug, v_t, proj, on_g.reshape(1, d))


def _swiglu_block(x, hn, w_gu, w_d, layer, next_norm_g, last):
    act = _matmul(hn, w_gu, layer, (0, 1), D_FF, tm=2048, tn=512, tk=D_MODEL,
                  epilogue=_epi_swiglu, out_dtype=bf16, name="swiglu_gate_up")
    return _matmul(act, w_d, layer, (0,), D_MODEL, tm=256, tn=D_MODEL, tk=D_FF, resident_w=True,
                   epilogue=_epi_residual_norm_only if last else _epi_residual_norm,
                   extras=((x, "tile"), (next_norm_g.reshape(1, D_MODEL), "row")),
                   out_dtype=f32 if last else (f32, bf16), name="swiglu_down")


def _proj(x, w, layer=0, *, n_out=None, out_dtype=f32, epilogue=_epi_plain, extras=(), name):
    if w.ndim == 2:
        w = w[None]
    n = w.shape[2] if n_out is None else n_out
    return _matmul(x, w, layer, (0,), n, tm=1024, tn=min(n, 1024), tk=x.shape[1], epilogue=epilogue,
                   extras=extras, out_dtype=out_dtype, name=name)


def _out_proj_norm(y, w_o, x, next_norm_g, *, name):
    return _matmul(y, w_o.astype(bf16)[None], 0, (0,), D_MODEL, tm=512, tn=D_MODEL, tk=D_MODEL,
                   epilogue=_epi_residual_norm, extras=((x, "tile"), (next_norm_g.reshape(1, D_MODEL), "row")),
                   out_dtype=(f32, bf16), name=name)


def _rwkv7_block(x, norm_g, mix, w_rkv, w0, w1, w2, a0, a1, a2, g1, g2, k_k, k_a, r_k, lnx_g, lnx_b, w_o,
                 next_norm_g):
    rank = LANES * pl.cdiv(w1.shape[1], LANES)
    downs = (_pad_cols(w1, rank).astype(bf16), _pad_cols(a1, rank).astype(bf16), g1.astype(bf16))
    ups = (_pad_rows(w2, rank).astype(bf16), _pad_rows(a2, rank).astype(bf16), g2.astype(bf16))
    xr, xk, xv, *zs = _norm_mix(x, norm_g, mix, *downs)
    w_rkv = w_rkv.astype(bf16)
    r = _proj(xr, w_rkv, 0, name="rwkv_r")
    k = _proj(xk, w_rkv, 1, name="rwkv_k")
    v = _proj(xv, w_rkv, 2, name="rwkv_v")
    yg = _wkv(r, k, v, zs, ups, w0, a0, k_k, k_a, r_k.reshape(-1), lnx_g, lnx_b)
    return _out_proj_norm(yg, w_o, x, next_norm_g, name="rwkv_out")


def _fox_block(x, hn, w_in, b_f, qn_g, kn_g, on_g, w_o, next_norm_g):
    n_main = 4 * D_MODEL
    proj = _proj(hn, w_in.astype(bf16), n_out=n_main, name="fox_in")
    w_small = _pad_cols(w_in[:, n_main:], LANES).astype(bf16)
    small = _proj(hn, w_small, name="fox_in_gates")
    c_pieces, small_pieces = _fox_gate(small, b_f)
    q, k_aug, v_t = _fox_prep(proj, small_pieces, c_pieces, qn_g, kn_g)
    og = _fox_attn(q, k_aug, v_t, proj, on_g)
    return _out_proj_norm(og, w_o, x, next_norm_g, name="fox_out")


def kernel(x, a_norm_g, a_mix, a_w_rkv, a_w0, a_w1, a_w2, a_a0, a_a1, a_a2, a_g1, a_g2, a_k_k, a_k_a, a_r_k, a_lnx_g, a_lnx_b, a_w_o, b_norm_g, b_w_in, b_b_f, b_qn_g, b_kn_g, b_on_g, b_w_o, f_norm_g, f_w_gu, f_w_d, final_g):
    b, t, d = x.shape
    h = x.reshape(b * t, d)
    w_d = f_w_d.astype(bf16)
    h = _rwkv7_block(h, a_norm_g[0], a_mix[0], a_w_rkv[0], a_w0[0], a_w1[0], a_w2[0], a_a0[0], a_a1[0],
                     a_a2[0], a_g1[0], a_g2[0], a_k_k[0], a_k_a[0], a_r_k[0], a_lnx_g[0], a_lnx_b[0],
                     a_w_o[0], f_norm_g[0])
    h = _swiglu_block(*h, f_w_gu, w_d, 0, b_norm_g[0], False)
    h = _fox_block(*h, b_w_in[0], b_b_f[0], b_qn_g[0], b_kn_g[0], b_on_g[0], b_w_o[0],
                   f_norm_g[1])
    return _swiglu_block(*h, f_w_gu, w_d, 1, final_g, True).reshape(b, t, d)
```

```python
import functools

import jax
import jax.numpy as jnp
from jax import lax
from jax.experimental import pallas as pl
from jax.experimental.pallas import tpu as pltpu

D_MODEL = 2048
BATCH = 8
SEQ = 2048
N_TOK = BATCH * SEQ
HEAD_DIM = 64
HEAD_SHIFT = HEAD_DIM.bit_length() - 1
N_HEADS = D_MODEL // HEAD_DIM
D_FF = 5632
RMS_EPS = 1e-6
GN_EPS = 64e-5

LANES = 128
LANE_SHIFT = LANES.bit_length() - 1
SUBLANES = 8
HEADS_PER_VREG = LANES // HEAD_DIM
N_PAIRS = D_MODEL // LANES
VMEM_LIMIT = 56 * 1024 * 1024

WKV_CHUNK = 64
WKV_TBLK = 512
WKV_LANES = 512
ATT_TQ = 256
ATT_TK = 512
ATT_LANES = 512
PREP_LANES = 512
N_SPLIT = 3
NEG_BIG = -1e30
LOG2_E = 1.4426950408889634

f32 = jnp.float32
bf16 = jnp.bfloat16


def _dot(a, b):
    return jnp.dot(a.astype(bf16), b.astype(bf16), preferred_element_type=f32)


def _dot_nt(a, b):
    return lax.dot_general(a.astype(bf16), b.astype(bf16), (((1,), (1,)), ((), ())),
                           preferred_element_type=f32)


def _split3(x):
    hi = x.astype(bf16)
    r1 = x - hi.astype(f32)
    mid = r1.astype(bf16)
    lo = (r1 - mid.astype(f32)).astype(bf16)
    return hi, mid, lo


def _dot_sel_rhs(x, sel):
    hi, mid, lo = _split3(x)
    d = lambda p: jnp.dot(p, sel, preferred_element_type=f32)
    return d(hi) + d(mid) + d(lo)


def _dot_sel_lhs(sel, x):
    hi, mid, lo = _split3(x)
    d = lambda p: jnp.dot(sel, p, preferred_element_type=f32)
    return d(hi) + d(mid) + d(lo)


def _head_sums(x, head_ones):
    hi = x.astype(bf16)
    lo = (x - hi.astype(f32)).astype(bf16)
    return jnp.dot(jnp.concatenate([hi, lo], axis=1), jnp.concatenate([head_ones, head_ones], axis=0),
                   preferred_element_type=f32)


def _head_ones():
    r = lax.broadcasted_iota(jnp.int32, (LANES, LANES), 0) >> HEAD_SHIFT
    c = lax.broadcasted_iota(jnp.int32, (LANES, LANES), 1) >> HEAD_SHIFT
    return (r == c).astype(bf16)


def _first_head_mask():
    return lax.broadcasted_iota(jnp.int32, (1, LANES), 1) < HEAD_DIM


def _stack_heads(x, m0):
    z = jnp.zeros_like(x)
    return jnp.concatenate([jnp.where(m0, x, z), jnp.where(m0, z, x)], axis=0)


def _softplus(z):
    return jnp.maximum(z, 0.0) + jnp.log(1.0 + jnp.exp(-jnp.abs(z)))


def _rms(x, g):
    return x * lax.rsqrt(jnp.mean(x * x, axis=-1, keepdims=True) + RMS_EPS) * g


def _norm_mix_kernel(x_ref, xp_ref, g_ref, mix_ref, w1_ref, a1_ref, g1_ref,
                     xr_ref, xk_ref, xv_ref, zw_ref, za_ref, zg_ref, xw_s, xa_s, xg_s, *, tm):
    i = pl.program_id(0)
    d = x_ref.shape[1]
    rows_per_chunk = 2 * SUBLANES
    lane_blk = 4 * LANES
    seq_start = (i * tm) % SEQ == 0
    hp_row = _rms(xp_ref[...], g_ref[...])[SUBLANES - 1:SUBLANES, :]
    hp_row = jnp.where(seq_start, jnp.zeros_like(hp_row), hp_row)
    rid = lax.broadcasted_iota(jnp.int32, (rows_per_chunk, 1), 0)
    mix_dsts = (xr_ref, xk_ref, xv_ref, xw_s, xa_s, xg_s)

    def chunk(c, last_row):
        rows = pl.ds(pl.multiple_of(c * rows_per_chunk, rows_per_chunk), rows_per_chunk)
        x = x_ref[rows, :]
        inv = lax.rsqrt(jnp.mean(x * x, axis=-1, keepdims=True) + RMS_EPS)
        new_last = []
        for b in range(d // lane_blk):
            ln = slice(b * lane_blk, (b + 1) * lane_blk)
            h = x[:, ln] * inv * g_ref[:, ln]
            hprev = jnp.where(rid == 0, last_row[:, ln], pltpu.roll(h, 1, 0))
            xx = hprev - h
            for p, dst in enumerate(mix_dsts):
                dst[rows, ln] = (h + xx * mix_ref[p:p + 1, ln]).astype(dst.dtype)
            new_last.append(h[rows_per_chunk - 1:rows_per_chunk, :])
        return jnp.concatenate(new_last, axis=1)

    lax.fori_loop(0, tm // rows_per_chunk, chunk, hp_row, unroll=2)
    down = lambda xs, w: jnp.dot(xs[...], w[...], preferred_element_type=f32)
    zw_ref[...] = jnp.tanh(down(xw_s, w1_ref)).astype(zw_ref.dtype)
    za_ref[...] = down(xa_s, a1_ref).astype(za_ref.dtype)
    zg_ref[...] = jax.nn.sigmoid(down(xg_s, g1_ref)).astype(zg_ref.dtype)


def _norm_mix(x, g, mix, w1, a1, g1, tm=512):
    m, d = x.shape
    rb = tm // SUBLANES
    row_tile = lambda n: pl.BlockSpec((tm, n), lambda i: (i, 0))
    whole = lambda a: pl.BlockSpec(a.shape, lambda i: (0, 0))
    lows = (w1, a1, g1)
    return pl.pallas_call(
        functools.partial(_norm_mix_kernel, tm=tm),
        grid=(m // tm,),
        in_specs=[row_tile(d),
                  pl.BlockSpec((SUBLANES, d), lambda i: (jnp.maximum(i * rb - 1, 0), 0)),
                  pl.BlockSpec((1, d), lambda i: (0, 0)),
                  pl.BlockSpec((6, d), lambda i: (0, 0))] + [whole(w) for w in lows],
        out_specs=[row_tile(d)] * 3 + [row_tile(w.shape[1]) for w in lows],
        out_shape=[jax.ShapeDtypeStruct((m, d), bf16)] * 3
        + [jax.ShapeDtypeStruct((m, w.shape[1]), bf16) for w in lows],
        scratch_shapes=[pltpu.VMEM((tm, d), bf16)] * 3,
        compiler_params=pltpu.CompilerParams(dimension_semantics=("parallel",),
                                             vmem_limit_bytes=VMEM_LIMIT),
        name="norm_mix",
    )(x, x, g.reshape(1, d), mix, *lows)


def _mm_kernel(*refs, n_w, n_e, n_o, nk, epilogue):
    x_ref = refs[0]
    w_refs = refs[1:1 + n_w]
    e_refs = refs[1 + n_w:1 + n_w + n_e]
    o_refs = refs[1 + n_w + n_e:1 + n_w + n_e + n_o]
    acc_refs = refs[1 + n_w + n_e + n_o:]
    x = x_ref[...]

    def finish(accs):
        outs = epilogue(accs, [e[...] for e in e_refs])
        for o_ref, out in zip(o_refs, outs if isinstance(outs, tuple) else (outs,)):
            o_ref[...] = out.astype(o_ref.dtype)

    if nk == 1:
        finish([jnp.dot(x, w[...].astype(x.dtype), preferred_element_type=f32) for w in w_refs])
        return
    k = pl.program_id(2)

    @pl.when(k == 0)
    def _():
        for a in acc_refs:
            a[...] = jnp.zeros_like(a)

    for a, w in zip(acc_refs, w_refs):
        a[...] += jnp.dot(x, w[...], preferred_element_type=f32)

    @pl.when(k == nk - 1)
    def _():
        finish([a[...] for a in acc_refs])


def _matmul(x, w, layer, w_col_blocks, n_out, *, tm, tn, tk, epilogue, extras=(), out_dtype, name,
            resident_w=False):
    m, kdim = x.shape
    nk = kdim // tk
    n_w = len(w_col_blocks)
    nb = n_out // tn
    in_specs = [pl.BlockSpec((tm, tk), lambda i, j, k: (i, k))]
    args = [x]
    w_mode = pl.Buffered(1) if resident_w else None
    for cb in w_col_blocks:
        in_specs.append(pl.BlockSpec((None, tk, tn), lambda i, j, k, cb=cb: (layer, k, cb * nb + j),
                                     pipeline_mode=w_mode))
        args.append(w)
    for arr, kind in extras:
        if kind == "row":
            in_specs.append(pl.BlockSpec((1, tn), lambda i, j, k: (0, j)))
        else:
            in_specs.append(pl.BlockSpec((tm, tn), lambda i, j, k: (i, j)))
        args.append(arr)
    scratch = [pltpu.VMEM((tm, tn), f32) for _ in range(n_w)] if nk > 1 else []
    multi = isinstance(out_dtype, tuple)
    dtypes = out_dtype if multi else (out_dtype,)
    out_spec = pl.BlockSpec((tm, tn), lambda i, j, k: (i, j))
    outs = pl.pallas_call(
        functools.partial(_mm_kernel, n_w=n_w, n_e=len(extras), n_o=len(dtypes), nk=nk, epilogue=epilogue),
        grid=(m // tm, nb, nk),
        in_specs=in_specs,
        out_specs=[out_spec] * len(dtypes),
        out_shape=[jax.ShapeDtypeStruct((m, n_out), dt) for dt in dtypes],
        scratch_shapes=scratch,
        compiler_params=pltpu.CompilerParams(
            dimension_semantics=("parallel", "parallel", "arbitrary"),
            vmem_limit_bytes=VMEM_LIMIT),
        name=name,
    )(*args)
    return tuple(outs) if multi else outs[0]


def _epi_plain(accs, extras):
    return accs[0]


def _epi_residual_norm(accs, extras):
    y = extras[0] + accs[0]
    return y, _rms(y, extras[1])


def _epi_residual_norm_only(accs, extras):
    return _rms(extras[0] + accs[0], extras[1])


def _epi_swiglu(accs, extras):
    gate, up = accs
    return gate * jax.nn.sigmoid(gate) * up


def _log_decay(y, w0):
    return -jnp.exp(-_softplus(-(w0 + y)) - 0.5)


def _pad_cols(w, n):
    return jnp.pad(w, ((0, 0), (0, n - w.shape[1])))


def _pad_rows(w, n):
    return jnp.pad(w, ((0, n - w.shape[0]), (0, 0)))


def _each(fn, *lists):
    return [fn(*xs) for xs in zip(*lists)]


def _wkv_chunk_maps(tiles, consts):
    m0, strict, incl, eye = consts
    c = WKV_CHUNK
    n2 = 2 * c
    r, lw, lg, k2, v, av, bv = (list(x) for x in zip(*tiles))
    stack = lambda x: _stack_heads(x, m0)
    lg_end = _each(lambda x: x[c - 1:c, :], lg)
    g_inv = _each(lambda x: jnp.exp(-x), lg)
    g_rem = _each(lambda e, x: jnp.exp(e - x), lg_end, lg)
    rs = _each(lambda x, l: stack(x * jnp.exp(l)), r, lg)
    as_ = _each(lambda x, l, w: stack(x * jnp.exp(l - w)).astype(bf16), av, lg, lw)
    bs = _each(lambda x, g: stack(x * g).astype(bf16), bv, g_inv)
    ks = _each(lambda x, g: stack(x * g).astype(bf16), k2, g_inv)
    bhs = _each(lambda x, g: stack(x * g).astype(bf16), bv, g_rem)
    khs = _each(lambda x, g: stack(x * g).astype(bf16), k2, g_rem)
    vs = _each(stack, v)
    sc = _each(lambda a, rr, b, k: _dot_nt(jnp.concatenate([a, rr.astype(bf16)], axis=0),
                                           jnp.concatenate([b, k], axis=0)), as_, rs, bs, ks)
    zero = jnp.zeros((n2, n2), f32)
    a_ab = _each(lambda s: jnp.where(strict, s[:n2, :n2], zero), sc)
    a_ak = _each(lambda s: jnp.where(strict, s[:n2, n2:], zero), sc)
    a_r = _each(lambda s: jnp.where(jnp.concatenate([incl, incl], axis=1), s[n2:, :],
                                    jnp.zeros((n2, 2 * n2), f32)).astype(bf16), sc)
    side = lambda a, b: jnp.concatenate([a, b], axis=1)
    t = _each(lambda x: eye + x, a_ab)
    p = _each(lambda x: _dot(x, x), a_ab)
    for _ in range(c.bit_length() - 3):
        pt = _each(lambda pp, tt: _dot(pp, side(pp, tt)), p, t)
        p = _each(lambda x: x[:, :n2], pt)
        t = _each(lambda tt, x: tt + x[:, n2:], t, pt)
    t = _each(lambda tt, pp: (tt + _dot(pp, tt)).astype(bf16), t, p)
    akv = _each(_dot, a_ak, vs)
    hw = _each(lambda tt, a, x: _dot(tt, side(a, x)), t, as_, akv)
    ah = _each(lambda x: x[:, :LANES], hw)
    ws = _each(lambda x: x[:, LANES:], hw)
    pq = _each(lambda ar, x, vv: _dot(ar, jnp.concatenate([x, side(jnp.zeros_like(vv), vv)], axis=0)),
               a_r, hw, vs)
    pc = _each(lambda x, y: x + y[:, :LANES], rs, pq)
    qc = _each(lambda y: y[:, LANES:], pq)
    gm = _each(lambda h, b: _dot(h.T, b), ah, bhs)
    nc = _each(lambda w, x, b, k: _dot(jnp.concatenate([w, x], axis=0).T, jnp.concatenate([b, k], axis=0)),
               ws, vs, bhs, khs)
    decay = _each(jnp.exp, lg_end)
    return pc, qc, gm, nc, decay


def _wkv_kernel(r_ref, k_ref, v_ref, zw_ref, za_ref, zg_ref, w2_ref, a2_ref, g2_ref, w0_ref, a0_ref,
                kk_ref, ka_ref, rk_ref, lng_ref, lnb_ref, o_ref, s_ref):
    c = WKV_CHUNK
    n2 = 2 * c
    up = lambda z_ref, w_ref: jnp.dot(z_ref[...], w_ref[...], preferred_element_type=f32)
    lw_all = _log_decay(up(zw_ref, w2_ref), w0_ref[...])
    a_all = jax.nn.sigmoid(a0_ref[...] + up(za_ref, a2_ref))
    g_all = up(zg_ref, g2_ref)

    @pl.when(pl.program_id(2) == 0)
    def _():
        s_ref[...] = jnp.zeros_like(s_ref)

    ri = lax.broadcasted_iota(jnp.int32, (n2, n2), 0)
    ci = lax.broadcasted_iota(jnp.int32, (n2, n2), 1)
    tr = lax.broadcasted_iota(jnp.int32, (c, c), 0)
    tc = lax.broadcasted_iota(jnp.int32, (c, c), 1)
    consts = (_first_head_mask(), (ri & (c - 1)) > (ci & (c - 1)), (ri & (c - 1)) >= (ci & (c - 1)),
              (ri == ci).astype(f32))
    tri = (tr >= tc).astype(bf16)
    head_ones = _head_ones()
    inv_n = 1.0 / HEAD_DIM

    npairs = WKV_LANES // LANES
    nchunk = WKV_TBLK // c
    lanes = [slice(p * LANES, (p + 1) * LANES) for p in range(npairs)]
    r = [r_ref[:, ln] for ln in lanes]
    v = [v_ref[:, ln] for ln in lanes]
    k2, av, bv, bonus = [], [], [], []
    for p, ln in enumerate(lanes):
        k, a = k_ref[:, ln], a_all[:, ln]
        kk = k * kk_ref[:, ln]
        ss = _head_sums(kk * kk, head_ones)
        kk = kk / jnp.maximum(jnp.sqrt(ss), 1e-12)
        k2.append(k * (1.0 + (a - 1.0) * ka_ref[:, ln]))
        av.append(-kk)
        bv.append(kk * a)
        bonus.append(_head_sums(r[p] * k2[p] * rk_ref[:, ln], head_ones))
    tiles = []
    for ic in range(nchunk):
        rows = slice(ic * c, (ic + 1) * c)
        lw = lw_all[rows, :]
        lg = _dot_sel_lhs(tri, lw)
        for p, ln in enumerate(lanes):
            tiles.append((r[p][rows], lw[:, ln], lg[:, ln], k2[p][rows], v[p][rows], av[p][rows], bv[p][rows]))
    pc, qc, gm, nc, decay = _wkv_chunk_maps(tiles, consts)
    s = [s_ref[p] for p in range(npairs)]
    ys = [[] for _ in range(npairs)]
    for ic in range(nchunk):
        for p in range(npairs):
            i = ic * npairs + p
            y = _dot_nt(pc[i], s[p]) + qc[i]
            ys[p].append(y[:c] + y[c:])
            s[p] = s[p] * decay[i] + _dot(s[p], gm[i]) + nc[i]
    for p, ln in enumerate(lanes):
        s_ref[p] = s[p]
        y = jnp.concatenate(ys[p], axis=0)
        mu = _head_sums(y, head_ones) * inv_n
        yc = y - mu
        var = _head_sums(yc * yc, head_ones) * inv_n
        yn = yc * lax.rsqrt(var + GN_EPS) * lng_ref[:, ln] + lnb_ref[:, ln]
        o_ref[:, ln] = ((yn + bonus[p] * v[p]) * g_all[:, ln]).astype(o_ref.dtype)


def _wkv(r, k, v, zs, ups, w0, a0, k_k, k_a, r_k, lnx_g, lnx_b):
    m, d = r.shape
    tb = SEQ // WKV_TBLK
    tile = pl.BlockSpec((WKV_TBLK, WKV_LANES), lambda b, j, t: (b * tb + t, j))
    row = pl.BlockSpec((1, WKV_LANES), lambda b, j, t: (0, j))
    z_tile = lambda z: pl.BlockSpec((WKV_TBLK, z.shape[1]), lambda b, j, t: (b * tb + t, 0))
    up_tile = lambda w: pl.BlockSpec((w.shape[0], WKV_LANES), lambda b, j, t: (0, j))
    rows = [x.reshape(1, d) for x in (w0, a0, k_k, k_a, r_k, lnx_g, lnx_b)]
    return pl.pallas_call(
        _wkv_kernel,
        grid=(BATCH, d // WKV_LANES, tb),
        in_specs=[tile] * 3 + [z_tile(z) for z in zs] + [up_tile(w) for w in ups] + [row] * len(rows),
        out_specs=tile,
        out_shape=jax.ShapeDtypeStruct((m, d), bf16),
        scratch_shapes=[pltpu.VMEM((WKV_LANES // LANES, LANES, LANES), f32)],
        compiler_params=pltpu.CompilerParams(
            dimension_semantics=("parallel", "parallel", "arbitrary"),
            vmem_limit_bytes=VMEM_LIMIT),
        name="wkv7",
    )(r, k, v, *zs, *ups, *rows)


def _fox_gate_kernel(s_ref, bf_ref, cp_ref, sp_ref, run_ref, *, tc):
    @pl.when(pl.program_id(1) == 0)
    def _():
        run_ref[...] = jnp.zeros_like(run_ref)

    small = s_ref[...]
    ls = -_softplus(-(small + bf_ref[...]))
    tr = lax.broadcasted_iota(jnp.int32, (tc, tc), 0)
    tcc = lax.broadcasted_iota(jnp.int32, (tc, tc), 1)
    cs = _dot_sel_lhs((tr >= tcc).astype(bf16), ls) + run_ref[...]
    run_ref[...] = cs[tc - 1:tc, :]
    cp_ref[...] = jnp.concatenate(_split3(cs * LOG2_E), axis=1)
    sp_ref[...] = jnp.concatenate(_split3(small), axis=1)


def _fox_gate(small, b_f, tc=512):
    m, w = small.shape
    nt = SEQ // tc
    bias = jnp.pad(b_f, (0, w - b_f.shape[0])).reshape(1, w)
    pieces = pl.BlockSpec((tc, N_SPLIT * w), lambda b, t: (b * nt + t, 0))
    return pl.pallas_call(
        functools.partial(_fox_gate_kernel, tc=tc),
        grid=(BATCH, nt),
        in_specs=[pl.BlockSpec((tc, w), lambda b, t: (b * nt + t, 0)),
                  pl.BlockSpec((1, w), lambda b, t: (0, 0))],
        out_specs=[pieces, pieces],
        out_shape=[jax.ShapeDtypeStruct((m, N_SPLIT * w), bf16)] * 2,
        scratch_shapes=[pltpu.VMEM((1, w), f32)],
        compiler_params=pltpu.CompilerParams(dimension_semantics=("parallel", "arbitrary"),
                                             vmem_limit_bytes=VMEM_LIMIT),
        name="fox_gate_cumsum",
    )(small, bias)


def _fox_prep_kernel(q_ref, k_ref, v_ref, kp_ref, vp_ref, s_ref, c_ref, qg_ref, kg_ref,
                     qo_ref, ko_ref, vo_ref, *, tm):
    i = pl.program_id(0)
    npairs = PREP_LANES // LANES
    head_ones = _head_ones()
    seq_start = (i * tm) % SEQ == 0
    rid = lax.broadcasted_iota(jnp.int32, (tm, 1), 0)
    r = lax.broadcasted_iota(jnp.int32, (LANES, 2 * LANES), 0)
    col = lax.broadcasted_iota(jnp.int32, (LANES, 2 * LANES), 1)
    ra = lax.broadcasted_iota(jnp.int32, (LANES, LANES), 0)
    ca = lax.broadcasted_iota(jnp.int32, (LANES, LANES), 1)
    small_pieces = s_ref[...]
    c_pieces = c_ref[...]

    def shifted(x, prow):
        prow = jnp.where(seq_start, jnp.zeros_like(prow), prow)
        return jnp.where(rid == 0, prow, pltpu.roll(x, 1, 0))

    def head_rms(x, gain):
        ms = _head_sums(x * x, head_ones) * (1.0 / HEAD_DIM)
        return x * lax.rsqrt(ms + RMS_EPS) * gain

    for p in range(npairs):
        pair = pl.program_id(1) * npairs + p
        ln = slice(p * LANES, (p + 1) * LANES)
        src = (((col & (LANES - 1)) >> HEAD_SHIFT) + HEADS_PER_VREG * pair
               + N_HEADS * (1 + (col >> LANE_SHIFT)))
        sel = (r == src).astype(bf16)
        logits = jnp.dot(small_pieces, jnp.concatenate([sel] * N_SPLIT, axis=0), preferred_element_type=f32)
        ak = jax.nn.sigmoid(logits[:, :LANES])
        av = jax.nn.sigmoid(logits[:, LANES:])
        k = k_ref[:, ln]
        v = v_ref[:, ln]
        k = ak * shifted(k, kp_ref[SUBLANES - 1:SUBLANES, ln]) + (1.0 - ak) * k
        v = av * shifted(v, vp_ref[SUBLANES - 1:SUBLANES, ln]) + (1.0 - av) * v
        qo_ref[:, ln] = (head_rms(q_ref[:, ln], qg_ref[...])
                         * (HEAD_DIM ** -0.5 * LOG2_E)).astype(qo_ref.dtype)
        sel_aug = []
        for n in range(N_SPLIT):
            hit = jnp.zeros((LANES, LANES), jnp.bool_)
            for h in range(HEADS_PER_VREG):
                hit = hit | ((ca == N_SPLIT * h + n) & (ra == HEADS_PER_VREG * pair + h))
            sel_aug.append(hit.astype(bf16))
        aug = jnp.dot(c_pieces, jnp.concatenate(sel_aug, axis=0), preferred_element_type=f32)
        ko_ref[:, 2 * p * LANES:2 * (p + 1) * LANES] = jnp.concatenate(
            [head_rms(k, kg_ref[...]), aug], axis=1).astype(ko_ref.dtype)
        vo_ref[ln, :] = v.T.astype(vo_ref.dtype)


def _fox_prep(proj, small_pieces, c_pieces, qn_g, kn_g, tm=1024):
    m = proj.shape[0]
    rb = tm // SUBLANES
    w = small_pieces.shape[1]
    nb = D_MODEL // PREP_LANES
    tile = lambda cb: pl.BlockSpec((tm, PREP_LANES), lambda i, j, cb=cb: (i, cb * nb + j))
    prev = lambda cb: pl.BlockSpec(
        (SUBLANES, PREP_LANES), lambda i, j, cb=cb: (jnp.maximum(i * rb - 1, 0), cb * nb + j))
    gain = pl.BlockSpec((1, LANES), lambda i, j: (0, 0))
    small_tile = pl.BlockSpec((tm, w), lambda i, j: (i, 0))
    tile_gain = lambda x: jnp.tile(x, HEADS_PER_VREG).reshape(1, LANES)
    return pl.pallas_call(
        functools.partial(_fox_prep_kernel, tm=tm),
        grid=(m // tm, nb),
        in_specs=[tile(0), tile(1), tile(2), prev(1), prev(2), small_tile, small_tile, gain, gain],
        out_specs=[pl.BlockSpec((tm, PREP_LANES), lambda i, j: (i, j)),
                   pl.BlockSpec((tm, 2 * PREP_LANES), lambda i, j: (i, j)),
                   pl.BlockSpec((PREP_LANES, tm), lambda i, j: (j, i))],
        out_shape=[jax.ShapeDtypeStruct((m, D_MODEL), bf16),
                   jax.ShapeDtypeStruct((m, 2 * D_MODEL), bf16),
                   jax.ShapeDtypeStruct((D_MODEL, m), bf16)],
        compiler_params=pltpu.CompilerParams(dimension_semantics=("parallel", "parallel"),
                                             vmem_limit_bytes=VMEM_LIMIT),
        name="fox_prep",
    )(proj, proj, proj, proj, proj, small_pieces, c_pieces, tile_gain(qn_g), tile_gain(kn_g))


def _fox_attn_kernel(q_ref, k_ref, vt_ref, gate_ref, og_ref, o_ref):
    tq, tk = ATT_TQ, ATT_TK
    qi = pl.program_id(2)
    npairs = ATT_LANES // LANES
    chains = [(p, h) for p in range(npairs) for h in range(HEADS_PER_VREG)]
    lane = lax.broadcasted_iota(jnp.int32, (tq, LANES), 1)
    q_aug = []
    for p, h in chains:
        q = q_ref[:, p * LANES:(p + 1) * LANES]
        own = (lane >> HEAD_SHIFT) == h
        minus_one = (lane >= N_SPLIT * h) & (lane < N_SPLIT * (h + 1))
        q_aug.append(jnp.concatenate([jnp.where(own, q, jnp.zeros_like(q)),
                                      jnp.where(minus_one, -1.0, 0.0).astype(bf16)], axis=1))

    def step(key0, width, carry, diag):
        m_run, l_run, acc = carry
        keys = pl.ds(pl.multiple_of(key0, tq), width)
        s = [_dot_nt(k_ref[keys, 2 * p * LANES:2 * (p + 1) * LANES], qa)
             for (p, h), qa in zip(chains, q_aug)]
        if diag:
            kidx = key0 + lax.broadcasted_iota(jnp.int32, (width, tq), 0)
            qidx = qi * tq + lax.broadcasted_iota(jnp.int32, (width, tq), 1)
            s = [jnp.where(qidx >= kidx, x, NEG_BIG) for x in s]
        m_new = [jnp.maximum(mr, jnp.max(x, axis=0, keepdims=True)) for mr, x in zip(m_run, s)]
        alpha = [jnp.exp2(mr - mn) for mr, mn in zip(m_run, m_new)]
        pr = [jnp.exp2(x - mn) for x, mn in zip(s, m_new)]
        l_new = [a * lr + jnp.sum(x, axis=0, keepdims=True) for a, lr, x in zip(alpha, l_run, pr)]
        pv = [jnp.dot(vt_ref[pl.ds((p * HEADS_PER_VREG + h) * HEAD_DIM, HEAD_DIM), keys], x.astype(bf16),
                      preferred_element_type=f32) for (p, h), x in zip(chains, pr)]
        acc = [ac * a + x for ac, a, x in zip(acc, alpha, pv)]
        return m_new, l_new, acc

    n = len(chains)
    init = ([jnp.full((1, tq), NEG_BIG, f32)] * n, [jnp.zeros((1, tq), f32)] * n,
            [jnp.zeros((HEAD_DIM, tq), f32)] * n)
    n_full = (qi * tq) // tk
    carry = lax.fori_loop(0, n_full, lambda j, cy: step(j * tk, tk, cy, False), init)
    _, l_run, acc = lax.cond((qi * tq) % tk == 0,
                             lambda cy: step(n_full * tk, tq, cy, True),
                             lambda cy: step(n_full * tk, tk, cy, True), carry)
    o_t = []
    for ac, lr in zip(acc, l_run):
        o = ac / lr
        o_t.append(o * lax.rsqrt(jnp.mean(o * o, axis=0, keepdims=True) + RMS_EPS))
    for p in range(npairs):
        ln = slice(p * LANES, (p + 1) * LANES)
        o = jnp.concatenate(o_t[HEADS_PER_VREG * p:HEADS_PER_VREG * (p + 1)], axis=0).T
        o_ref[:, ln] = (o * og_ref[:, ln] * jax.nn.sigmoid(gate_ref[:, ln])).astype(o_ref.dtype)


def _fox_attn(q, k_aug, v_t, proj, on_g):
    m, d = q.shape
    nq = SEQ // ATT_TQ
    gate_col0 = 3 * D_MODEL // ATT_LANES
    return pl.pallas_call(
        _fox_attn_kernel,
        grid=(BATCH, d // ATT_LANES, nq),
        in_specs=[pl.BlockSpec((ATT_TQ, ATT_LANES), lambda b, p, i: (b * nq + i, p)),
                  pl.BlockSpec((SEQ, 2 * ATT_LANES), lambda b, p, i: (b, p)),
                  pl.BlockSpec((ATT_LANES, SEQ), lambda b, p, i: (p, b)),
                  pl.BlockSpec((ATT_TQ, ATT_LANES), lambda b, p, i: (b * nq + i, gate_col0 + p)),
                  pl.BlockSpec((1, ATT_LANES), lambda b, p, i: (0, p))],
        out_specs=pl.BlockSpec((ATT_TQ, ATT_LANES), lambda b, p, i: (b * nq + i, p)),
        out_shape=jax.ShapeDtypeStruct((m, d), bf16),
        compiler_params=pltpu.CompilerParams(
            dimension_semantics=("parallel", "parallel", "arbitrary"),
            vmem_limit_bytes=VMEM_LIMIT),
        name="fox_attention",
    )(q, k_aug, v_t, proj, on_g.reshape(1, d))


def _swiglu_block(x, hn, w_gu, w_d, layer, next_norm_g, last):
    act = _matmul(hn, w_gu, layer, (0, 1), D_FF, tm=2048, tn=512, tk=D_MODEL,
                  epilogue=_epi_swiglu, out_dtype=bf16, name="swiglu_gate_up")
    return _matmul(act, w_d, layer, (0,), D_MODEL, tm=256, tn=D_MODEL, tk=D_FF, resident_w=True,
                   epilogue=_epi_residual_norm_only if last else _epi_residual_norm,
                   extras=((x, "tile"), (next_norm_g.reshape(1, D_MODEL), "row")),
                   out_dtype=f32 if last else (f32, bf16), name="swiglu_down")


def _proj(x, w, layer=0, *, n_out=None, name):
    if w.ndim == 2:
        w = w[None]
    n = w.shape[2] if n_out is None else n_out
    resident = n <= D_MODEL
    return _matmul(x, w, layer, (0,), n, tm=1024 if resident else 2048, tn=n if resident else 1024,
                   tk=x.shape[1], epilogue=_epi_plain, out_dtype=f32, name=name, resident_w=resident)


def _out_proj_norm(y, w_o, x, next_norm_g, *, name):
    return _matmul(y, w_o.astype(bf16)[None], 0, (0,), D_MODEL, tm=512, tn=D_MODEL, tk=D_MODEL,
                   epilogue=_epi_residual_norm, extras=((x, "tile"), (next_norm_g.reshape(1, D_MODEL), "row")),
                   out_dtype=(f32, bf16), name=name)


def _rwkv7_block(x, norm_g, mix, w_rkv, w0, w1, w2, a0, a1, a2, g1, g2, k_k, k_a, r_k, lnx_g, lnx_b, w_o,
                 next_norm_g):
    rank = LANES * pl.cdiv(w1.shape[1], LANES)
    downs = (_pad_cols(w1, rank).astype(bf16), _pad_cols(a1, rank).astype(bf16), g1.astype(bf16))
    ups = (_pad_rows(w2, rank).astype(bf16), _pad_rows(a2, rank).astype(bf16), g2.astype(bf16))
    xr, xk, xv, *zs = _norm_mix(x, norm_g, mix, *downs)
    w_rkv = w_rkv.astype(bf16)
    r = _proj(xr, w_rkv, 0, name="rwkv_r")
    k = _proj(xk, w_rkv, 1, name="rwkv_k")
    v = _proj(xv, w_rkv, 2, name="rwkv_v")
    yg = _wkv(r, k, v, zs, ups, w0, a0, k_k, k_a, r_k.reshape(-1), lnx_g, lnx_b)
    return _out_proj_norm(yg, w_o, x, next_norm_g, name="rwkv_out")


def _fox_block(x, hn, w_in, b_f, qn_g, kn_g, on_g, w_o, next_norm_g):
    n_main = 4 * D_MODEL
    proj = _proj(hn, w_in.astype(bf16), n_out=n_main, name="fox_in")
    w_small = _pad_cols(w_in[:, n_main:], LANES).astype(bf16)
    small = _proj(hn, w_small, name="fox_in_gates")
    c_pieces, small_pieces = _fox_gate(small, b_f)
    q, k_aug, v_t = _fox_prep(proj, small_pieces, c_pieces, qn_g, kn_g)
    og = _fox_attn(q, k_aug, v_t, proj, on_g)
    return _out_proj_norm(og, w_o, x, next_norm_g, name="fox_out")


def kernel(x, a_norm_g, a_mix, a_w_rkv, a_w0, a_w1, a_w2, a_a0, a_a1, a_a2, a_g1, a_g2, a_k_k, a_k_a, a_r_k, a_lnx_g, a_lnx_b, a_w_o, b_norm_g, b_w_in, b_b_f, b_qn_g, b_kn_g, b_on_g, b_w_o, f_norm_g, f_w_gu, f_w_d, final_g):
    b, t, d = x.shape
    h = x.reshape(b * t, d)
    w_d = f_w_d.astype(bf16)
    h = _rwkv7_block(h, a_norm_g[0], a_mix[0], a_w_rkv[0], a_w0[0], a_w1[0], a_w2[0], a_a0[0], a_a1[0],
                     a_a2[0], a_g1[0], a_g2[0], a_k_k[0], a_k_a[0], a_r_k[0], a_lnx_g[0], a_lnx_b[0],
                     a_w_o[0], f_norm_g[0])
    h = _swiglu_block(*h, f_w_gu, w_d, 0, b_norm_g[0], False)
    h = _fox_block(*h, b_w_in[0], b_b_f[0], b_qn_g[0], b_kn_g[0], b_on_g[0], b_w_o[0],
                   f_norm_g[1])
    return _swiglu_block(*h, f_w_gu, w_d, 1, final_g, True).reshape(b, t, d)
```

```python
import functools

import jax
import jax.numpy as jnp
from jax import lax
from jax.experimental import pallas as pl
from jax.experimental.pallas import tpu as pltpu

D_MODEL = 2048
BATCH = 8
SEQ = 2048
N_TOK = BATCH * SEQ
HEAD_DIM = 64
HEAD_SHIFT = HEAD_DIM.bit_length() - 1
N_HEADS = D_MODEL // HEAD_DIM
D_FF = 5632
RMS_EPS = 1e-6
GN_EPS = 64e-5

LANES = 128
LANE_SHIFT = LANES.bit_length() - 1
SUBLANES = 8
HEADS_PER_VREG = LANES // HEAD_DIM
N_PAIRS = D_MODEL // LANES
VMEM_LIMIT = 56 * 1024 * 1024

NORM_MIX_DOT_ROWS = 128
WKV_CHUNK = 64
WKV_TBLK = 512
WKV_LANES = 512
ATT_TQ = 256
ATT_TK = 512
ATT_LANES = 512
PREP_LANES = 512
N_SPLIT = 3
NEG_BIG = -1e30
LOG2_E = 1.4426950408889634

f32 = jnp.float32
bf16 = jnp.bfloat16


def _dot(a, b):
    return jnp.dot(a.astype(bf16), b.astype(bf16), preferred_element_type=f32)


def _dot_nt(a, b):
    return lax.dot_general(a.astype(bf16), b.astype(bf16), (((1,), (1,)), ((), ())),
                           preferred_element_type=f32)


def _split3(x):
    hi = x.astype(bf16)
    r1 = x - hi.astype(f32)
    mid = r1.astype(bf16)
    lo = (r1 - mid.astype(f32)).astype(bf16)
    return hi, mid, lo


def _dot_sel_rhs(x, sel):
    hi, mid, lo = _split3(x)
    d = lambda p: jnp.dot(p, sel, preferred_element_type=f32)
    return d(hi) + d(mid) + d(lo)


def _dot_sel_lhs(sel, x):
    hi, mid, lo = _split3(x)
    d = lambda p: jnp.dot(sel, p, preferred_element_type=f32)
    return d(hi) + d(mid) + d(lo)


def _head_sums(x, head_ones):
    hi = x.astype(bf16)
    lo = (x - hi.astype(f32)).astype(bf16)
    return jnp.dot(jnp.concatenate([hi, lo], axis=1), jnp.concatenate([head_ones, head_ones], axis=0),
                   preferred_element_type=f32)


def _head_sums_lanes(x, m0):
    zero = jnp.zeros_like(x)
    s0 = jnp.sum(jnp.where(m0, x, zero), axis=-1, keepdims=True)
    s1 = jnp.sum(jnp.where(m0, zero, x), axis=-1, keepdims=True)
    return jnp.where(m0, s0, s1)


def _head_ones():
    r = lax.broadcasted_iota(jnp.int32, (LANES, LANES), 0) >> HEAD_SHIFT
    c = lax.broadcasted_iota(jnp.int32, (LANES, LANES), 1) >> HEAD_SHIFT
    return (r == c).astype(bf16)


def _first_head_mask():
    return lax.broadcasted_iota(jnp.int32, (1, LANES), 1) < HEAD_DIM


def _stack_heads(x, m0):
    z = jnp.zeros_like(x)
    return jnp.concatenate([jnp.where(m0, x, z), jnp.where(m0, z, x)], axis=0)


def _softplus(z):
    return jnp.maximum(z, 0.0) + jnp.log(1.0 + jnp.exp(-jnp.abs(z)))


def _rms(x, g):
    return x * lax.rsqrt(jnp.mean(x * x, axis=-1, keepdims=True) + RMS_EPS) * g


def _norm_mix_kernel(x_ref, xp_ref, g_ref, mix_ref, w1_ref, a1_ref, g1_ref,
                     xr_ref, xk_ref, xv_ref, zw_ref, za_ref, zg_ref, xw_s, xa_s, xg_s, *, tm):
    i = pl.program_id(0)
    d = x_ref.shape[1]
    rows_per_chunk = 2 * SUBLANES
    lane_blk = 4 * LANES
    seq_start = (i * tm) % SEQ == 0
    hp_row = _rms(xp_ref[...], g_ref[...])[SUBLANES - 1:SUBLANES, :]
    hp_row = jnp.where(seq_start, jnp.zeros_like(hp_row), hp_row)
    rid = lax.broadcasted_iota(jnp.int32, (rows_per_chunk, 1), 0)
    mix_dsts = (xr_ref, xk_ref, xv_ref, xw_s, xa_s, xg_s)

    def chunk(c, last_row):
        rows = slice(c * rows_per_chunk, (c + 1) * rows_per_chunk)
        x = x_ref[rows, :]
        inv = lax.rsqrt(jnp.mean(x * x, axis=-1, keepdims=True) + RMS_EPS)
        new_last = []
        for b in range(d // lane_blk):
            ln = slice(b * lane_blk, (b + 1) * lane_blk)
            h = x[:, ln] * inv * g_ref[:, ln]
            hprev = jnp.where(rid == 0, last_row[:, ln], pltpu.roll(h, 1, 0))
            xx = hprev - h
            for p, dst in enumerate(mix_dsts):
                dst[rows, ln] = (h + xx * mix_ref[p:p + 1, ln]).astype(dst.dtype)
            new_last.append(h[rows_per_chunk - 1:rows_per_chunk, :])
        return jnp.concatenate(new_last, axis=1)

    chunks_per_blk = NORM_MIX_DOT_ROWS // rows_per_chunk
    last_row = hp_row
    for blk in range(tm // NORM_MIX_DOT_ROWS):
        for c in range(blk * chunks_per_blk, (blk + 1) * chunks_per_blk):
            last_row = chunk(c, last_row)
        rows = slice(blk * NORM_MIX_DOT_ROWS, (blk + 1) * NORM_MIX_DOT_ROWS)
        down = lambda xs, w: jnp.dot(xs[rows, :], w[...], preferred_element_type=f32)
        zw_ref[rows, :] = jnp.tanh(down(xw_s, w1_ref)).astype(zw_ref.dtype)
        za_ref[rows, :] = down(xa_s, a1_ref).astype(za_ref.dtype)
        zg_ref[rows, :] = jax.nn.sigmoid(down(xg_s, g1_ref)).astype(zg_ref.dtype)


def _norm_mix(x, g, mix, w1, a1, g1, tm=512):
    m, d = x.shape
    rb = tm // SUBLANES
    row_tile = lambda n: pl.BlockSpec((tm, n), lambda i: (i, 0))
    whole = lambda a: pl.BlockSpec(a.shape, lambda i: (0, 0))
    lows = (w1, a1, g1)
    return pl.pallas_call(
        functools.partial(_norm_mix_kernel, tm=tm),
        grid=(m // tm,),
        in_specs=[row_tile(d),
                  pl.BlockSpec((SUBLANES, d), lambda i: (jnp.maximum(i * rb - 1, 0), 0)),
                  pl.BlockSpec((1, d), lambda i: (0, 0)),
                  pl.BlockSpec((6, d), lambda i: (0, 0))] + [whole(w) for w in lows],
        out_specs=[row_tile(d)] * 3 + [row_tile(w.shape[1]) for w in lows],
        out_shape=[jax.ShapeDtypeStruct((m, d), bf16)] * 3
        + [jax.ShapeDtypeStruct((m, w.shape[1]), bf16) for w in lows],
        scratch_shapes=[pltpu.VMEM((tm, d), bf16)] * 3,
        compiler_params=pltpu.CompilerParams(dimension_semantics=("parallel",),
                                             vmem_limit_bytes=VMEM_LIMIT),
        name="norm_mix",
    )(x, x, g.reshape(1, d), mix, *lows)


def _mm_kernel(*refs, n_w, n_e, n_o, nk, epilogue):
    x_ref = refs[0]
    w_refs = refs[1:1 + n_w]
    e_refs = refs[1 + n_w:1 + n_w + n_e]
    o_refs = refs[1 + n_w + n_e:1 + n_w + n_e + n_o]
    acc_refs = refs[1 + n_w + n_e + n_o:]
    x = x_ref[...]

    def finish(accs):
        outs = epilogue(accs, [e[...] for e in e_refs])
        for o_ref, out in zip(o_refs, outs if isinstance(outs, tuple) else (outs,)):
            o_ref[...] = out.astype(o_ref.dtype)

    if nk == 1:
        finish([jnp.dot(x, w[...].astype(x.dtype), preferred_element_type=f32) for w in w_refs])
        return
    k = pl.program_id(2)

    @pl.when(k == 0)
    def _():
        for a in acc_refs:
            a[...] = jnp.zeros_like(a)

    for a, w in zip(acc_refs, w_refs):
        a[...] += jnp.dot(x, w[...], preferred_element_type=f32)

    @pl.when(k == nk - 1)
    def _():
        finish([a[...] for a in acc_refs])


def _matmul(x, w, layer, w_col_blocks, n_out, *, tm, tn, tk, epilogue, extras=(), out_dtype, name,
            resident_w=False):
    m, kdim = x.shape
    nk = kdim // tk
    n_w = len(w_col_blocks)
    nb = n_out // tn
    in_specs = [pl.BlockSpec((tm, tk), lambda i, j, k: (i, k))]
    args = [x]
    w_mode = pl.Buffered(1) if resident_w else None
    for cb in w_col_blocks:
        in_specs.append(pl.BlockSpec((None, tk, tn), lambda i, j, k, cb=cb: (layer, k, cb * nb + j),
                                     pipeline_mode=w_mode))
        args.append(w)
    for arr, kind in extras:
        if kind == "row":
            in_specs.append(pl.BlockSpec((1, tn), lambda i, j, k: (0, j)))
        else:
            in_specs.append(pl.BlockSpec((tm, tn), lambda i, j, k: (i, j)))
        args.append(arr)
    scratch = [pltpu.VMEM((tm, tn), f32) for _ in range(n_w)] if nk > 1 else []
    multi = isinstance(out_dtype, tuple)
    dtypes = out_dtype if multi else (out_dtype,)
    out_spec = pl.BlockSpec((tm, tn), lambda i, j, k: (i, j))
    outs = pl.pallas_call(
        functools.partial(_mm_kernel, n_w=n_w, n_e=len(extras), n_o=len(dtypes), nk=nk, epilogue=epilogue),
        grid=(m // tm, nb, nk),
        in_specs=in_specs,
        out_specs=[out_spec] * len(dtypes),
        out_shape=[jax.ShapeDtypeStruct((m, n_out), dt) for dt in dtypes],
        scratch_shapes=scratch,
        compiler_params=pltpu.CompilerParams(
            dimension_semantics=("parallel", "parallel", "arbitrary"),
            vmem_limit_bytes=VMEM_LIMIT),
        name=name,
    )(*args)
    return tuple(outs) if multi else outs[0]


def _epi_plain(accs, extras):
    return accs[0]


def _epi_residual_norm(accs, extras):
    y = extras[0] + accs[0]
    return y, _rms(y, extras[1])


def _epi_residual_norm_only(accs, extras):
    return _rms(extras[0] + accs[0], extras[1])


def _epi_swiglu(accs, extras):
    gate, up = accs
    return gate * jax.nn.sigmoid(gate) * up


def _log_decay(y, w0):
    return -jnp.exp(-_softplus(-(w0 + y)) - 0.5)


def _pad_cols(w, n):
    return jnp.pad(w, ((0, 0), (0, n - w.shape[1])))


def _pad_rows(w, n):
    return jnp.pad(w, ((0, n - w.shape[0]), (0, 0)))


def _each(fn, *lists):
    return [fn(*xs) for xs in zip(*lists)]


def _wkv_chunk_maps(tiles, consts):
    m0, strict, incl, eye = consts
    c = WKV_CHUNK
    n2 = 2 * c
    r, lw, lg, k2, v, av, bv = (list(x) for x in zip(*tiles))
    stack = lambda x: _stack_heads(x, m0)
    lg_end = _each(lambda x: x[c - 1:c, :], lg)
    g_inv = _each(lambda x: jnp.exp(-x), lg)
    g_rem = _each(lambda e, x: jnp.exp(e - x), lg_end, lg)
    rs = _each(lambda x, l: stack(x * jnp.exp(l)), r, lg)
    as_ = _each(lambda x, l, w: stack(x * jnp.exp(l - w)).astype(bf16), av, lg, lw)
    bs = _each(lambda x, g: stack(x * g).astype(bf16), bv, g_inv)
    ks = _each(lambda x, g: stack(x * g).astype(bf16), k2, g_inv)
    bhs = _each(lambda x, g: stack(x * g).astype(bf16), bv, g_rem)
    khs = _each(lambda x, g: stack(x * g).astype(bf16), k2, g_rem)
    vs = _each(stack, v)
    sc = _each(lambda a, rr, b, k: _dot_nt(jnp.concatenate([a, rr.astype(bf16)], axis=0),
                                           jnp.concatenate([b, k], axis=0)), as_, rs, bs, ks)
    zero = jnp.zeros((n2, n2), f32)
    a_ab = _each(lambda s: jnp.where(strict, s[:n2, :n2], zero), sc)
    a_ak = _each(lambda s: jnp.where(strict, s[:n2, n2:], zero), sc)
    a_r = _each(lambda s: jnp.where(jnp.concatenate([incl, incl], axis=1), s[n2:, :],
                                    jnp.zeros((n2, 2 * n2), f32)).astype(bf16), sc)
    side = lambda a, b: jnp.concatenate([a, b], axis=1)
    t = _each(lambda x: eye + x, a_ab)
    p = _each(lambda x: _dot(x, x), a_ab)
    for _ in range(c.bit_length() - 3):
        pt = _each(lambda pp, tt: _dot(pp, side(pp, tt)), p, t)
        p = _each(lambda x: x[:, :n2], pt)
        t = _each(lambda tt, x: tt + x[:, n2:], t, pt)
    t = _each(lambda tt, pp: (tt + _dot(pp, tt)).astype(bf16), t, p)
    akv = _each(_dot, a_ak, vs)
    hw = _each(lambda tt, a, x: _dot(tt, side(a, x)), t, as_, akv)
    ah = _each(lambda x: x[:, :LANES], hw)
    ws = _each(lambda x: x[:, LANES:], hw)
    pq = _each(lambda ar, x, vv: _dot(ar, jnp.concatenate([x, side(jnp.zeros_like(vv), vv)], axis=0)),
               a_r, hw, vs)
    pc = _each(lambda x, y: x + y[:, :LANES], rs, pq)
    qc = _each(lambda y: y[:, LANES:], pq)
    gm = _each(lambda h, b: _dot(h.T, b), ah, bhs)
    nc = _each(lambda w, x, b, k: _dot(jnp.concatenate([w, x], axis=0).T, jnp.concatenate([b, k], axis=0)),
               ws, vs, bhs, khs)
    decay = _each(jnp.exp, lg_end)
    return pc, qc, gm, nc, decay


def _wkv_kernel(r_ref, k_ref, v_ref, zw_ref, za_ref, zg_ref, w2_ref, a2_ref, g2_ref, w0_ref, a0_ref,
                kk_ref, ka_ref, rk_ref, lng_ref, lnb_ref, o_ref, s_ref):
    c = WKV_CHUNK
    n2 = 2 * c
    up = lambda z_ref, w_ref: jnp.dot(z_ref[...], w_ref[...], preferred_element_type=f32)
    lw_all = _log_decay(up(zw_ref, w2_ref), w0_ref[...])
    a_all = jax.nn.sigmoid(a0_ref[...] + up(za_ref, a2_ref))
    g_all = up(zg_ref, g2_ref)

    @pl.when(pl.program_id(2) == 0)
    def _():
        s_ref[...] = jnp.zeros_like(s_ref)

    ri = lax.broadcasted_iota(jnp.int32, (n2, n2), 0)
    ci = lax.broadcasted_iota(jnp.int32, (n2, n2), 1)
    tr = lax.broadcasted_iota(jnp.int32, (c, c), 0)
    tc = lax.broadcasted_iota(jnp.int32, (c, c), 1)
    consts = (_first_head_mask(), (ri & (c - 1)) > (ci & (c - 1)), (ri & (c - 1)) >= (ci & (c - 1)),
              (ri == ci).astype(f32))
    tri = (tr >= tc).astype(bf16)
    m0 = consts[0]
    inv_n = 1.0 / HEAD_DIM

    npairs = WKV_LANES // LANES
    nchunk = WKV_TBLK // c
    lanes = [slice(p * LANES, (p + 1) * LANES) for p in range(npairs)]
    r = [r_ref[:, ln] for ln in lanes]
    v = [v_ref[:, ln] for ln in lanes]
    k2, av, bv, bonus = [], [], [], []
    for p, ln in enumerate(lanes):
        k, a = k_ref[:, ln], a_all[:, ln]
        kk = k * kk_ref[:, ln]
        ss = _head_sums_lanes(kk * kk, m0)
        kk = kk / jnp.maximum(jnp.sqrt(ss), 1e-12)
        k2.append(k * (1.0 + (a - 1.0) * ka_ref[:, ln]))
        av.append(-kk)
        bv.append(kk * a)
        bonus.append(_head_sums_lanes(r[p] * k2[p] * rk_ref[:, ln], m0))
    tiles = []
    for ic in range(nchunk):
        rows = slice(ic * c, (ic + 1) * c)
        lw = lw_all[rows, :]
        lg = _dot_sel_lhs(tri, lw)
        for p, ln in enumerate(lanes):
            tiles.append((r[p][rows], lw[:, ln], lg[:, ln], k2[p][rows], v[p][rows], av[p][rows], bv[p][rows]))
    pc, qc, gm, nc, decay = _wkv_chunk_maps(tiles, consts)
    s = [s_ref[p] for p in range(npairs)]
    ys = [[] for _ in range(npairs)]
    for ic in range(nchunk):
        for p in range(npairs):
            i = ic * npairs + p
            y = _dot_nt(pc[i], s[p]) + qc[i]
            ys[p].append(y[:c] + y[c:])
            s[p] = s[p] * decay[i] + _dot(s[p], gm[i]) + nc[i]
    for p, ln in enumerate(lanes):
        s_ref[p] = s[p]
        y = jnp.concatenate(ys[p], axis=0)
        mu = _head_sums_lanes(y, m0) * inv_n
        yc = y - mu
        var = _head_sums_lanes(yc * yc, m0) * inv_n
        yn = yc * lax.rsqrt(var + GN_EPS) * lng_ref[:, ln] + lnb_ref[:, ln]
        o_ref[:, ln] = ((yn + bonus[p] * v[p]) * g_all[:, ln]).astype(o_ref.dtype)


def _wkv(r, k, v, zs, ups, w0, a0, k_k, k_a, r_k, lnx_g, lnx_b):
    m, d = r.shape
    tb = SEQ // WKV_TBLK
    tile = pl.BlockSpec((WKV_TBLK, WKV_LANES), lambda b, j, t: (b * tb + t, j))
    row = pl.BlockSpec((1, WKV_LANES), lambda b, j, t: (0, j))
    z_tile = lambda z: pl.BlockSpec((WKV_TBLK, z.shape[1]), lambda b, j, t: (b * tb + t, 0))
    up_tile = lambda w: pl.BlockSpec((w.shape[0], WKV_LANES), lambda b, j, t: (0, j))
    rows = [x.reshape(1, d) for x in (w0, a0, k_k, k_a, r_k, lnx_g, lnx_b)]
    return pl.pallas_call(
        _wkv_kernel,
        grid=(BATCH, d // WKV_LANES, tb),
        in_specs=[tile] * 3 + [z_tile(z) for z in zs] + [up_tile(w) for w in ups] + [row] * len(rows),
        out_specs=tile,
        out_shape=jax.ShapeDtypeStruct((m, d), bf16),
        scratch_shapes=[pltpu.VMEM((WKV_LANES // LANES, LANES, LANES), f32)],
        compiler_params=pltpu.CompilerParams(
            dimension_semantics=("parallel", "parallel", "arbitrary"),
            vmem_limit_bytes=VMEM_LIMIT),
        name="wkv7",
    )(r, k, v, *zs, *ups, *rows)


def _fox_gate_kernel(s_ref, bf_ref, cp_ref, sp_ref, run_ref, *, tc):
    @pl.when(pl.program_id(1) == 0)
    def _():
        run_ref[...] = jnp.zeros_like(run_ref)

    small = s_ref[...]
    ls = -_softplus(-(small + bf_ref[...]))
    tr = lax.broadcasted_iota(jnp.int32, (tc, tc), 0)
    tcc = lax.broadcasted_iota(jnp.int32, (tc, tc), 1)
    cs = _dot_sel_lhs((tr >= tcc).astype(bf16), ls) + run_ref[...]
    run_ref[...] = cs[tc - 1:tc, :]
    cp_ref[...] = jnp.concatenate(_split3(cs * LOG2_E), axis=1)
    sp_ref[...] = jnp.concatenate(_split3(small), axis=1)


def _fox_gate(small, b_f, tc=512):
    m, w = small.shape
    nt = SEQ // tc
    bias = jnp.pad(b_f, (0, w - b_f.shape[0])).reshape(1, w)
    pieces = pl.BlockSpec((tc, N_SPLIT * w), lambda b, t: (b * nt + t, 0))
    return pl.pallas_call(
        functools.partial(_fox_gate_kernel, tc=tc),
        grid=(BATCH, nt),
        in_specs=[pl.BlockSpec((tc, w), lambda b, t: (b * nt + t, 0)),
                  pl.BlockSpec((1, w), lambda b, t: (0, 0))],
        out_specs=[pieces, pieces],
        out_shape=[jax.ShapeDtypeStruct((m, N_SPLIT * w), bf16)] * 2,
        scratch_shapes=[pltpu.VMEM((1, w), f32)],
        compiler_params=pltpu.CompilerParams(dimension_semantics=("parallel", "arbitrary"),
                                             vmem_limit_bytes=VMEM_LIMIT),
        name="fox_gate_cumsum",
    )(small, bias)


def _fox_prep_kernel(q_ref, k_ref, v_ref, kp_ref, vp_ref, s_ref, c_ref, qg_ref, kg_ref,
                     qo_ref, ko_ref, vo_ref, *, tm):
    i = pl.program_id(0)
    npairs = PREP_LANES // LANES
    head_ones = _head_ones()
    seq_start = (i * tm) % SEQ == 0
    rid = lax.broadcasted_iota(jnp.int32, (tm, 1), 0)
    r = lax.broadcasted_iota(jnp.int32, (LANES, 2 * LANES), 0)
    col = lax.broadcasted_iota(jnp.int32, (LANES, 2 * LANES), 1)
    ra = lax.broadcasted_iota(jnp.int32, (LANES, LANES), 0)
    ca = lax.broadcasted_iota(jnp.int32, (LANES, LANES), 1)
    small_pieces = s_ref[...]
    c_pieces = c_ref[...]

    def shifted(x, prow):
        prow = jnp.where(seq_start, jnp.zeros_like(prow), prow)
        return jnp.where(rid == 0, prow, pltpu.roll(x, 1, 0))

    def head_rms(x, gain):
        ms = _head_sums(x * x, head_ones) * (1.0 / HEAD_DIM)
        return x * lax.rsqrt(ms + RMS_EPS) * gain

    for p in range(npairs):
        pair = pl.program_id(1) * npairs + p
        ln = slice(p * LANES, (p + 1) * LANES)
        src = (((col & (LANES - 1)) >> HEAD_SHIFT) + HEADS_PER_VREG * pair
               + N_HEADS * (1 + (col >> LANE_SHIFT)))
        sel = (r == src).astype(bf16)
        logits = jnp.dot(small_pieces, jnp.concatenate([sel] * N_SPLIT, axis=0), preferred_element_type=f32)
        ak = jax.nn.sigmoid(logits[:, :LANES])
        av = jax.nn.sigmoid(logits[:, LANES:])
        k = k_ref[:, ln]
        v = v_ref[:, ln]
        k = ak * shifted(k, kp_ref[SUBLANES - 1:SUBLANES, ln]) + (1.0 - ak) * k
        v = av * shifted(v, vp_ref[SUBLANES - 1:SUBLANES, ln]) + (1.0 - av) * v
        qo_ref[:, ln] = (head_rms(q_ref[:, ln], qg_ref[...])
                         * (HEAD_DIM ** -0.5 * LOG2_E)).astype(qo_ref.dtype)
        sel_aug = []
        for n in range(N_SPLIT):
            hit = jnp.zeros((LANES, LANES), jnp.bool_)
            for h in range(HEADS_PER_VREG):
                hit = hit | ((ca == N_SPLIT * h + n) & (ra == HEADS_PER_VREG * pair + h))
            sel_aug.append(hit.astype(bf16))
        aug = jnp.dot(c_pieces, jnp.concatenate(sel_aug, axis=0), preferred_element_type=f32)
        ko_ref[:, 2 * p * LANES:2 * (p + 1) * LANES] = jnp.concatenate(
            [head_rms(k, kg_ref[...]), aug], axis=1).astype(ko_ref.dtype)
        vo_ref[ln, :] = v.T.astype(vo_ref.dtype)


def _fox_prep(proj, small_pieces, c_pieces, qn_g, kn_g, tm=1024):
    m = proj.shape[0]
    rb = tm // SUBLANES
    w = small_pieces.shape[1]
    nb = D_MODEL // PREP_LANES
    tile = lambda cb: pl.BlockSpec((tm, PREP_LANES), lambda i, j, cb=cb: (i, cb * nb + j))
    prev = lambda cb: pl.BlockSpec(
        (SUBLANES, PREP_LANES), lambda i, j, cb=cb: (jnp.maximum(i * rb - 1, 0), cb * nb + j))
    gain = pl.BlockSpec((1, LANES), lambda i, j: (0, 0))
    small_tile = pl.BlockSpec((tm, w), lambda i, j: (i, 0))
    tile_gain = lambda x: jnp.tile(x, HEADS_PER_VREG).reshape(1, LANES)
    return pl.pallas_call(
        functools.partial(_fox_prep_kernel, tm=tm),
        grid=(m // tm, nb),
        in_specs=[tile(0), tile(1), tile(2), prev(1), prev(2), small_tile, small_tile, gain, gain],
        out_specs=[pl.BlockSpec((tm, PREP_LANES), lambda i, j: (i, j)),
                   pl.BlockSpec((tm, 2 * PREP_LANES), lambda i, j: (i, j)),
                   pl.BlockSpec((PREP_LANES, tm), lambda i, j: (j, i))],
        out_shape=[jax.ShapeDtypeStruct((m, D_MODEL), bf16),
                   jax.ShapeDtypeStruct((m, 2 * D_MODEL), bf16),
                   jax.ShapeDtypeStruct((D_MODEL, m), bf16)],
        compiler_params=pltpu.CompilerParams(dimension_semantics=("parallel", "parallel"),
                                             vmem_limit_bytes=VMEM_LIMIT),
        name="fox_prep",
    )(proj, proj, proj, proj, proj, small_pieces, c_pieces, tile_gain(qn_g), tile_gain(kn_g))


def _fox_attn_kernel(q_ref, k_ref, vt_ref, gate_ref, og_ref, o_ref):
    tq, tk = ATT_TQ, ATT_TK
    qi = pl.program_id(2)
    npairs = ATT_LANES // LANES
    chains = [(p, h) for p in range(npairs) for h in range(HEADS_PER_VREG)]
    lane = lax.broadcasted_iota(jnp.int32, (tq, LANES), 1)
    q_aug = []
    for p, h in chains:
        q = q_ref[:, p * LANES:(p + 1) * LANES]
        own = (lane >> HEAD_SHIFT) == h
        minus_one = (lane >= N_SPLIT * h) & (lane < N_SPLIT * (h + 1))
        q_aug.append(jnp.concatenate([jnp.where(own, q, jnp.zeros_like(q)),
                                      jnp.where(minus_one, -1.0, 0.0).astype(bf16)], axis=1))

    def step(key0, width, carry, diag):
        m_run, l_run, acc = carry
        keys = pl.ds(pl.multiple_of(key0, tq), width)
        s = [_dot_nt(k_ref[keys, 2 * p * LANES:2 * (p + 1) * LANES], qa)
             for (p, h), qa in zip(chains, q_aug)]
        if diag:
            kidx = key0 + lax.broadcasted_iota(jnp.int32, (width, tq), 0)
            qidx = qi * tq + lax.broadcasted_iota(jnp.int32, (width, tq), 1)
            s = [jnp.where(qidx >= kidx, x, NEG_BIG) for x in s]
        m_new = [jnp.maximum(mr, jnp.max(x, axis=0, keepdims=True)) for mr, x in zip(m_run, s)]
        alpha = [jnp.exp2(mr - mn) for mr, mn in zip(m_run, m_new)]
        pr = [jnp.exp2(x - mn) for x, mn in zip(s, m_new)]
        l_new = [a * lr + jnp.sum(x, axis=0, keepdims=True) for a, lr, x in zip(alpha, l_run, pr)]
        pv = [jnp.dot(vt_ref[pl.ds((p * HEADS_PER_VREG + h) * HEAD_DIM, HEAD_DIM), keys], x.astype(bf16),
                      preferred_element_type=f32) for (p, h), x in zip(chains, pr)]
        acc = [ac * a + x for ac, a, x in zip(acc, alpha, pv)]
        return m_new, l_new, acc

    n = len(chains)
    init = ([jnp.full((1, tq), NEG_BIG, f32)] * n, [jnp.zeros((1, tq), f32)] * n,
            [jnp.zeros((HEAD_DIM, tq), f32)] * n)
    n_full = (qi * tq) // tk
    carry = lax.fori_loop(0, n_full, lambda j, cy: step(j * tk, tk, cy, False), init)
    _, l_run, acc = lax.cond((qi * tq) % tk == 0,
                             lambda cy: step(n_full * tk, tq, cy, True),
                             lambda cy: step(n_full * tk, tk, cy, True), carry)
    o_t = []
    for ac, lr in zip(acc, l_run):
        o = ac / lr
        o_t.append(o * lax.rsqrt(jnp.mean(o * o, axis=0, keepdims=True) + RMS_EPS))
    for p in range(npairs):
        ln = slice(p * LANES, (p + 1) * LANES)
        o = jnp.concatenate(o_t[HEADS_PER_VREG * p:HEADS_PER_VREG * (p + 1)], axis=0).T
        o_ref[:, ln] = (o * og_ref[:, ln] * jax.nn.sigmoid(gate_ref[:, ln])).astype(o_ref.dtype)


def _fox_attn(q, k_aug, v_t, proj, on_g):
    m, d = q.shape
    nq = SEQ // ATT_TQ
    gate_col0 = 3 * D_MODEL // ATT_LANES
    return pl.pallas_call(
        _fox_attn_kernel,
        grid=(BATCH, d // ATT_LANES, nq),
        in_specs=[pl.BlockSpec((ATT_TQ, ATT_LANES), lambda b, p, i: (b * nq + i, p)),
                  pl.BlockSpec((SEQ, 2 * ATT_LANES), lambda b, p, i: (b, p)),
                  pl.BlockSpec((ATT_LANES, SEQ), lambda b, p, i: (p, b)),
                  pl.BlockSpec((ATT_TQ, ATT_LANES), lambda b, p, i: (b * nq + i, gate_col0 + p)),
                  pl.BlockSpec((1, ATT_LANES), lambda b, p, i: (0, p))],
        out_specs=pl.BlockSpec((ATT_TQ, ATT_LANES), lambda b, p, i: (b * nq + i, p)),
        out_shape=jax.ShapeDtypeStruct((m, d), bf16),
        compiler_params=pltpu.CompilerParams(
            dimension_semantics=("parallel", "parallel", "arbitrary"),
            vmem_limit_bytes=VMEM_LIMIT),
        name="fox_attention",
    )(q, k_aug, v_t, proj, on_g.reshape(1, d))


def _swiglu_block(x, hn, w_gu, w_d, layer, next_norm_g, last):
    act = _matmul(hn, w_gu, layer, (0, 1), D_FF, tm=2048, tn=512, tk=D_MODEL,
                  epilogue=_epi_swiglu, out_dtype=bf16, name="swiglu_gate_up")
    return _matmul(act, w_d, layer, (0,), D_MODEL, tm=256, tn=D_MODEL, tk=D_FF, resident_w=True,
                   epilogue=_epi_residual_norm_only if last else _epi_residual_norm,
                   extras=((x, "tile"), (next_norm_g.reshape(1, D_MODEL), "row")),
                   out_dtype=f32 if last else (f32, bf16), name="swiglu_down")


def _proj(x, w, layer=0, *, n_out=None, name):
    if w.ndim == 2:
        w = w[None]
    n = w.shape[2] if n_out is None else n_out
    resident = n <= D_MODEL
    return _matmul(x, w, layer, (0,), n, tm=1024 if resident else 2048, tn=n if resident else 1024,
                   tk=x.shape[1], epilogue=_epi_plain, out_dtype=f32, name=name, resident_w=resident)


def _out_proj_norm(y, w_o, x, next_norm_g, *, name):
    return _matmul(y, w_o.astype(bf16)[None], 0, (0,), D_MODEL, tm=512, tn=D_MODEL, tk=D_MODEL,
                   epilogue=_epi_residual_norm, extras=((x, "tile"), (next_norm_g.reshape(1, D_MODEL), "row")),
                   out_dtype=(f32, bf16), name=name)


def _rwkv7_block(x, norm_g, mix, w_rkv, w0, w1, w2, a0, a1, a2, g1, g2, k_k, k_a, r_k, lnx_g, lnx_b, w_o,
                 next_norm_g):
    rank = LANES * pl.cdiv(w1.shape[1], LANES)
    downs = (_pad_cols(w1, rank).astype(bf16), _pad_cols(a1, rank).astype(bf16), g1.astype(bf16))
    ups = (_pad_rows(w2, rank).astype(bf16), _pad_rows(a2, rank).astype(bf16), g2.astype(bf16))
    xr, xk, xv, *zs = _norm_mix(x, norm_g, mix, *downs)
    w_rkv = w_rkv.astype(bf16)
    r = _proj(xr, w_rkv, 0, name="rwkv_r")
    k = _proj(xk, w_rkv, 1, name="rwkv_k")
    v = _proj(xv, w_rkv, 2, name="rwkv_v")
    yg = _wkv(r, k, v, zs, ups, w0, a0, k_k, k_a, r_k.reshape(-1), lnx_g, lnx_b)
    return _out_proj_norm(yg, w_o, x, next_norm_g, name="rwkv_out")


def _fox_block(x, hn, w_in, b_f, qn_g, kn_g, on_g, w_o, next_norm_g):
    n_main = 4 * D_MODEL
    proj = _proj(hn, w_in.astype(bf16), n_out=n_main, name="fox_in")
    w_small = _pad_cols(w_in[:, n_main:], LANES).astype(bf16)
    small = _proj(hn, w_small, name="fox_in_gates")
    c_pieces, small_pieces = _fox_gate(small, b_f)
    q, k_aug, v_t = _fox_prep(proj, small_pieces, c_pieces, qn_g, kn_g)
    og = _fox_attn(q, k_aug, v_t, proj, on_g)
    return _out_proj_norm(og, w_o, x, next_norm_g, name="fox_out")


def kernel(x, a_norm_g, a_mix, a_w_rkv, a_w0, a_w1, a_w2, a_a0, a_a1, a_a2, a_g1, a_g2, a_k_k, a_k_a, a_r_k, a_lnx_g, a_lnx_b, a_w_o, b_norm_g, b_w_in, b_b_f, b_qn_g, b_kn_g, b_on_g, b_w_o, f_norm_g, f_w_gu, f_w_d, final_g):
    b, t, d = x.shape
    h = x.reshape(b * t, d)
    w_d = f_w_d.astype(bf16)
    h = _rwkv7_block(h, a_norm_g[0], a_mix[0], a_w_rkv[0], a_w0[0], a_w1[0], a_w2[0], a_a0[0], a_a1[0],
                     a_a2[0], a_g1[0], a_g2[0], a_k_k[0], a_k_a[0], a_r_k[0], a_lnx_g[0], a_lnx_b[0],
                     a_w_o[0], f_norm_g[0])
    h = _swiglu_block(*h, f_w_gu, w_d, 0, b_norm_g[0], False)
    h = _fox_block(*h, b_w_in[0], b_b_f[0], b_qn_g[0], b_kn_g[0], b_on_g[0], b_w_o[0],
                   f_norm_g[1])
    return _swiglu_block(*h, f_w_gu, w_d, 1, final_g, True).reshape(b, t, d)
```

```python
import functools

import jax
import jax.numpy as jnp
from jax import lax
from jax.experimental import pallas as pl
from jax.experimental.pallas import tpu as pltpu

D_MODEL = 2048
BATCH = 8
SEQ = 2048
N_TOK = BATCH * SEQ
HEAD_DIM = 64
HEAD_SHIFT = HEAD_DIM.bit_length() - 1
N_HEADS = D_MODEL // HEAD_DIM
D_FF = 5632
RMS_EPS = 1e-6
GN_EPS = 64e-5

LANES = 128
LANE_SHIFT = LANES.bit_length() - 1
SUBLANES = 8
HEADS_PER_VREG = LANES // HEAD_DIM
N_PAIRS = D_MODEL // LANES
VMEM_LIMIT = 56 * 1024 * 1024

NORM_MIX_DOT_ROWS = 128
WKV_CHUNK = 64
WKV_TBLK = 512
WKV_LANES = 512
ATT_TQ = 256
ATT_TK = 512
ATT_LANES = 1024
PREP_LANES = 512
N_SPLIT = 3
LOGIT_SPLIT = 2
NEG_BIG = -1e30
LOG2_E = 1.4426950408889634

f32 = jnp.float32
bf16 = jnp.bfloat16


def _dot(a, b):
    return jnp.dot(a.astype(bf16), b.astype(bf16), preferred_element_type=f32)


def _dot_nt(a, b):
    return lax.dot_general(a.astype(bf16), b.astype(bf16), (((1,), (1,)), ((), ())),
                           preferred_element_type=f32)


def _split3(x):
    hi = x.astype(bf16)
    r1 = x - hi.astype(f32)
    mid = r1.astype(bf16)
    lo = (r1 - mid.astype(f32)).astype(bf16)
    return hi, mid, lo


def _dot_sel_rhs(x, sel):
    hi, mid, lo = _split3(x)
    d = lambda p: jnp.dot(p, sel, preferred_element_type=f32)
    return d(hi) + d(mid) + d(lo)


def _dot_sel_lhs(sel, x):
    hi, mid, lo = _split3(x)
    d = lambda p: jnp.dot(sel, p, preferred_element_type=f32)
    return d(hi) + d(mid) + d(lo)


def _head_sums(x, head_ones):
    hi = x.astype(bf16)
    lo = (x - hi.astype(f32)).astype(bf16)
    return jnp.dot(jnp.concatenate([hi, lo], axis=1), jnp.concatenate([head_ones, head_ones], axis=0),
                   preferred_element_type=f32)


def _head_sums_lanes(x, m0):
    zero = jnp.zeros_like(x)
    s0 = jnp.sum(jnp.where(m0, x, zero), axis=-1, keepdims=True)
    s1 = jnp.sum(jnp.where(m0, zero, x), axis=-1, keepdims=True)
    return jnp.where(m0, s0, s1)


def _head_ones():
    r = lax.broadcasted_iota(jnp.int32, (LANES, LANES), 0) >> HEAD_SHIFT
    c = lax.broadcasted_iota(jnp.int32, (LANES, LANES), 1) >> HEAD_SHIFT
    return (r == c).astype(bf16)


def _first_head_mask():
    return lax.broadcasted_iota(jnp.int32, (1, LANES), 1) < HEAD_DIM


def _stack_heads(x, m0):
    z = jnp.zeros_like(x)
    return jnp.concatenate([jnp.where(m0, x, z), jnp.where(m0, z, x)], axis=0)


def _softplus(z):
    return jnp.maximum(z, 0.0) + jnp.log(1.0 + jnp.exp(-jnp.abs(z)))


def _rms(x, g):
    return x * lax.rsqrt(jnp.mean(x * x, axis=-1, keepdims=True) + RMS_EPS) * g


def _norm_mix_kernel(x_ref, xp_ref, g_ref, mix_ref, w1_ref, a1_ref, g1_ref,
                     xr_ref, xk_ref, xv_ref, zw_ref, za_ref, zg_ref, xw_s, xa_s, xg_s, *, tm):
    i = pl.program_id(0)
    d = x_ref.shape[1]
    rows_per_chunk = 2 * SUBLANES
    lane_blk = 4 * LANES
    seq_start = (i * tm) % SEQ == 0
    hp_row = _rms(xp_ref[...], g_ref[...])[SUBLANES - 1:SUBLANES, :]
    hp_row = jnp.where(seq_start, jnp.zeros_like(hp_row), hp_row)
    rid = lax.broadcasted_iota(jnp.int32, (rows_per_chunk, 1), 0)
    mix_dsts = (xr_ref, xk_ref, xv_ref, xw_s, xa_s, xg_s)

    def chunk(c, last_row):
        rows = slice(c * rows_per_chunk, (c + 1) * rows_per_chunk)
        x = x_ref[rows, :]
        inv = lax.rsqrt(jnp.mean(x * x, axis=-1, keepdims=True) + RMS_EPS)
        new_last = []
        for b in range(d // lane_blk):
            ln = slice(b * lane_blk, (b + 1) * lane_blk)
            h = x[:, ln] * inv * g_ref[:, ln]
            hprev = jnp.where(rid == 0, last_row[:, ln], pltpu.roll(h, 1, 0))
            xx = hprev - h
            for p, dst in enumerate(mix_dsts):
                dst[rows, ln] = (h + xx * mix_ref[p:p + 1, ln]).astype(dst.dtype)
            new_last.append(h[rows_per_chunk - 1:rows_per_chunk, :])
        return jnp.concatenate(new_last, axis=1)

    chunks_per_blk = NORM_MIX_DOT_ROWS // rows_per_chunk
    last_row = hp_row
    for blk in range(tm // NORM_MIX_DOT_ROWS):
        for c in range(blk * chunks_per_blk, (blk + 1) * chunks_per_blk):
            last_row = chunk(c, last_row)
        rows = slice(blk * NORM_MIX_DOT_ROWS, (blk + 1) * NORM_MIX_DOT_ROWS)
        down = lambda xs, w: jnp.dot(xs[rows, :], w[...], preferred_element_type=f32)
        zw_ref[rows, :] = jnp.tanh(down(xw_s, w1_ref)).astype(zw_ref.dtype)
        za_ref[rows, :] = down(xa_s, a1_ref).astype(za_ref.dtype)
        zg_ref[rows, :] = jax.nn.sigmoid(down(xg_s, g1_ref)).astype(zg_ref.dtype)


def _norm_mix(x, g, mix, w1, a1, g1, tm=512):
    m, d = x.shape
    rb = tm // SUBLANES
    row_tile = lambda n: pl.BlockSpec((tm, n), lambda i: (i, 0))
    whole = lambda a: pl.BlockSpec(a.shape, lambda i: (0, 0))
    lows = (w1, a1, g1)
    return pl.pallas_call(
        functools.partial(_norm_mix_kernel, tm=tm),
        grid=(m // tm,),
        in_specs=[row_tile(d),
                  pl.BlockSpec((SUBLANES, d), lambda i: (jnp.maximum(i * rb - 1, 0), 0)),
                  pl.BlockSpec((1, d), lambda i: (0, 0)),
                  pl.BlockSpec((6, d), lambda i: (0, 0))] + [whole(w) for w in lows],
        out_specs=[row_tile(d)] * 3 + [row_tile(w.shape[1]) for w in lows],
        out_shape=[jax.ShapeDtypeStruct((m, d), bf16)] * 3
        + [jax.ShapeDtypeStruct((m, w.shape[1]), bf16) for w in lows],
        scratch_shapes=[pltpu.VMEM((tm, d), bf16)] * 3,
        compiler_params=pltpu.CompilerParams(dimension_semantics=("parallel",),
                                             vmem_limit_bytes=VMEM_LIMIT),
        name="norm_mix",
    )(x, x, g.reshape(1, d), mix, *lows)


def _mm_kernel(*refs, n_w, n_e, n_o, nk, epilogue):
    x_ref = refs[0]
    w_refs = refs[1:1 + n_w]
    e_refs = refs[1 + n_w:1 + n_w + n_e]
    o_refs = refs[1 + n_w + n_e:1 + n_w + n_e + n_o]
    acc_refs = refs[1 + n_w + n_e + n_o:]
    x = x_ref[...]

    def finish(accs):
        outs = epilogue(accs, [e[...] for e in e_refs])
        for o_ref, out in zip(o_refs, outs if isinstance(outs, tuple) else (outs,)):
            o_ref[...] = out.astype(o_ref.dtype)

    if nk == 1:
        finish([jnp.dot(x, w[...].astype(x.dtype), preferred_element_type=f32) for w in w_refs])
        return
    k = pl.program_id(2)

    @pl.when(k == 0)
    def _():
        for a in acc_refs:
            a[...] = jnp.zeros_like(a)

    for a, w in zip(acc_refs, w_refs):
        a[...] += jnp.dot(x, w[...], preferred_element_type=f32)

    @pl.when(k == nk - 1)
    def _():
        finish([a[...] for a in acc_refs])


def _matmul(x, w, layer, w_col_blocks, n_out, *, tm, tn, tk, epilogue, extras=(), out_dtype, name,
            resident_w=False):
    m, kdim = x.shape
    nk = kdim // tk
    n_w = len(w_col_blocks)
    nb = n_out // tn
    in_specs = [pl.BlockSpec((tm, tk), lambda i, j, k: (i, k))]
    args = [x]
    w_mode = pl.Buffered(1) if resident_w else None
    for cb in w_col_blocks:
        in_specs.append(pl.BlockSpec((None, tk, tn), lambda i, j, k, cb=cb: (layer, k, cb * nb + j),
                                     pipeline_mode=w_mode))
        args.append(w)
    for arr, kind in extras:
        if kind == "row":
            in_specs.append(pl.BlockSpec((1, tn), lambda i, j, k: (0, j)))
        else:
            in_specs.append(pl.BlockSpec((tm, tn), lambda i, j, k: (i, j)))
        args.append(arr)
    scratch = [pltpu.VMEM((tm, tn), f32) for _ in range(n_w)] if nk > 1 else []
    multi = isinstance(out_dtype, tuple)
    dtypes = out_dtype if multi else (out_dtype,)
    out_spec = pl.BlockSpec((tm, tn), lambda i, j, k: (i, j))
    outs = pl.pallas_call(
        functools.partial(_mm_kernel, n_w=n_w, n_e=len(extras), n_o=len(dtypes), nk=nk, epilogue=epilogue),
        grid=(m // tm, nb, nk),
        in_specs=in_specs,
        out_specs=[out_spec] * len(dtypes),
        out_shape=[jax.ShapeDtypeStruct((m, n_out), dt) for dt in dtypes],
        scratch_shapes=scratch,
        compiler_params=pltpu.CompilerParams(
            dimension_semantics=("parallel", "parallel", "arbitrary"),
            vmem_limit_bytes=VMEM_LIMIT),
        name=name,
    )(*args)
    return tuple(outs) if multi else outs[0]


def _epi_plain(accs, extras):
    return accs[0]


def _epi_residual_norm(accs, extras):
    y = extras[0] + accs[0]
    return y, _rms(y, extras[1])


def _epi_residual_norm_only(accs, extras):
    return _rms(extras[0] + accs[0], extras[1])


def _epi_swiglu(accs, extras):
    gate, up = accs
    return gate * jax.nn.sigmoid(gate) * up


def _log_decay(y, w0):
    return -jnp.exp(-_softplus(-(w0 + y)) - 0.5)


def _pad_cols(w, n):
    return jnp.pad(w, ((0, 0), (0, n - w.shape[1])))


def _pad_rows(w, n):
    return jnp.pad(w, ((0, n - w.shape[0]), (0, 0)))


def _each(fn, *lists):
    return [fn(*xs) for xs in zip(*lists)]


def _wkv_chunk_maps(tiles, consts):
    m0, strict, incl, eye = consts
    c = WKV_CHUNK
    n2 = 2 * c
    r, lw, lg, k2, v, av, bv = (list(x) for x in zip(*tiles))
    stack = lambda x: _stack_heads(x, m0)
    lg_end = _each(lambda x: x[c - 1:c, :], lg)
    g_inv = _each(lambda x: jnp.exp(-x), lg)
    g_rem = _each(lambda e, x: jnp.exp(e - x), lg_end, lg)
    rs = _each(lambda x, l: stack(x * jnp.exp(l)), r, lg)
    as_ = _each(lambda x, l, w: stack(x * jnp.exp(l - w)).astype(bf16), av, lg, lw)
    bs = _each(lambda x, g: stack(x * g).astype(bf16), bv, g_inv)
    ks = _each(lambda x, g: stack(x * g).astype(bf16), k2, g_inv)
    bhs = _each(lambda x, g: stack(x * g).astype(bf16), bv, g_rem)
    khs = _each(lambda x, g: stack(x * g).astype(bf16), k2, g_rem)
    vs = _each(stack, v)
    sc = _each(lambda a, rr, b, k: _dot_nt(jnp.concatenate([a, rr.astype(bf16)], axis=0),
                                           jnp.concatenate([b, k], axis=0)), as_, rs, bs, ks)
    zero = jnp.zeros((n2, n2), f32)
    a_ab = _each(lambda s: jnp.where(strict, s[:n2, :n2], zero), sc)
    a_ak = _each(lambda s: jnp.where(strict, s[:n2, n2:], zero), sc)
    a_r = _each(lambda s: jnp.where(jnp.concatenate([incl, incl], axis=1), s[n2:, :],
                                    jnp.zeros((n2, 2 * n2), f32)).astype(bf16), sc)
    side = lambda a, b: jnp.concatenate([a, b], axis=1)
    t = _each(lambda x: eye + x, a_ab)
    p = _each(lambda x: _dot(x, x), a_ab)
    for _ in range(c.bit_length() - 3):
        pt = _each(lambda pp, tt: _dot(pp, side(pp, tt)), p, t)
        p = _each(lambda x: x[:, :n2], pt)
        t = _each(lambda tt, x: tt + x[:, n2:], t, pt)
    t = _each(lambda tt, pp: (tt + _dot(pp, tt)).astype(bf16), t, p)
    akv = _each(_dot, a_ak, vs)
    hw = _each(lambda tt, a, x: _dot(tt, side(a, x)), t, as_, akv)
    ah = _each(lambda x: x[:, :LANES], hw)
    ws = _each(lambda x: x[:, LANES:], hw)
    pq = _each(lambda ar, x, vv: _dot(ar, jnp.concatenate([x, side(jnp.zeros_like(vv), vv)], axis=0)),
               a_r, hw, vs)
    pc = _each(lambda x, y: x + y[:, :LANES], rs, pq)
    qc = _each(lambda y: y[:, LANES:], pq)
    gm = _each(lambda h, b: _dot(h.T, b), ah, bhs)
    nc = _each(lambda w, x, b, k: _dot(jnp.concatenate([w, x], axis=0).T, jnp.concatenate([b, k], axis=0)),
               ws, vs, bhs, khs)
    decay = _each(jnp.exp, lg_end)
    return pc, qc, gm, nc, decay


def _wkv_kernel(r_ref, k_ref, v_ref, zw_ref, za_ref, zg_ref, w2_ref, a2_ref, g2_ref, w0_ref, a0_ref,
                kk_ref, ka_ref, rk_ref, lng_ref, lnb_ref, o_ref, s_ref):
    c = WKV_CHUNK
    n2 = 2 * c
    up = lambda z_ref, w_ref: jnp.dot(z_ref[...], w_ref[...], preferred_element_type=f32)
    lw_all = _log_decay(up(zw_ref, w2_ref), w0_ref[...])
    a_all = jax.nn.sigmoid(a0_ref[...] + up(za_ref, a2_ref))
    g_all = up(zg_ref, g2_ref)

    @pl.when(pl.program_id(2) == 0)
    def _():
        s_ref[...] = jnp.zeros_like(s_ref)

    ri = lax.broadcasted_iota(jnp.int32, (n2, n2), 0)
    ci = lax.broadcasted_iota(jnp.int32, (n2, n2), 1)
    tr = lax.broadcasted_iota(jnp.int32, (c, c), 0)
    tc = lax.broadcasted_iota(jnp.int32, (c, c), 1)
    consts = (_first_head_mask(), (ri & (c - 1)) > (ci & (c - 1)), (ri & (c - 1)) >= (ci & (c - 1)),
              (ri == ci).astype(f32))
    tri = (tr >= tc).astype(bf16)
    m0 = consts[0]
    inv_n = 1.0 / HEAD_DIM

    npairs = WKV_LANES // LANES
    nchunk = WKV_TBLK // c
    lanes = [slice(p * LANES, (p + 1) * LANES) for p in range(npairs)]
    r = [r_ref[:, ln] for ln in lanes]
    v = [v_ref[:, ln] for ln in lanes]
    k2, av, bv, bonus = [], [], [], []
    for p, ln in enumerate(lanes):
        k, a = k_ref[:, ln], a_all[:, ln]
        kk = k * kk_ref[:, ln]
        ss = _head_sums_lanes(kk * kk, m0)
        kk = kk / jnp.maximum(jnp.sqrt(ss), 1e-12)
        k2.append(k * (1.0 + (a - 1.0) * ka_ref[:, ln]))
        av.append(-kk)
        bv.append(kk * a)
        bonus.append(_head_sums_lanes(r[p] * k2[p] * rk_ref[:, ln], m0))
    tiles = []
    for ic in range(nchunk):
        rows = slice(ic * c, (ic + 1) * c)
        lw = lw_all[rows, :]
        lg = _dot_sel_lhs(tri, lw)
        for p, ln in enumerate(lanes):
            tiles.append((r[p][rows], lw[:, ln], lg[:, ln], k2[p][rows], v[p][rows], av[p][rows], bv[p][rows]))
    pc, qc, gm, nc, decay = _wkv_chunk_maps(tiles, consts)
    s = [s_ref[p] for p in range(npairs)]
    ys = [[] for _ in range(npairs)]
    for ic in range(nchunk):
        for p in range(npairs):
            i = ic * npairs + p
            y = _dot_nt(pc[i], s[p]) + qc[i]
            ys[p].append(y[:c] + y[c:])
            s[p] = s[p] * decay[i] + _dot(s[p], gm[i]) + nc[i]
    for p, ln in enumerate(lanes):
        s_ref[p] = s[p]
        y = jnp.concatenate(ys[p], axis=0)
        mu = _head_sums_lanes(y, m0) * inv_n
        yc = y - mu
        var = _head_sums_lanes(yc * yc, m0) * inv_n
        yn = yc * lax.rsqrt(var + GN_EPS) * lng_ref[:, ln] + lnb_ref[:, ln]
        o_ref[:, ln] = ((yn + bonus[p] * v[p]) * g_all[:, ln]).astype(o_ref.dtype)


def _wkv(r, k, v, zs, ups, w0, a0, k_k, k_a, r_k, lnx_g, lnx_b):
    m, d = r.shape
    tb = SEQ // WKV_TBLK
    tile = pl.BlockSpec((WKV_TBLK, WKV_LANES), lambda b, j, t: (b * tb + t, j))
    row = pl.BlockSpec((1, WKV_LANES), lambda b, j, t: (0, j))
    z_tile = lambda z: pl.BlockSpec((WKV_TBLK, z.shape[1]), lambda b, j, t: (b * tb + t, 0))
    up_tile = lambda w: pl.BlockSpec((w.shape[0], WKV_LANES), lambda b, j, t: (0, j))
    rows = [x.reshape(1, d) for x in (w0, a0, k_k, k_a, r_k, lnx_g, lnx_b)]
    return pl.pallas_call(
        _wkv_kernel,
        grid=(BATCH, d // WKV_LANES, tb),
        in_specs=[tile] * 3 + [z_tile(z) for z in zs] + [up_tile(w) for w in ups] + [row] * len(rows),
        out_specs=tile,
        out_shape=jax.ShapeDtypeStruct((m, d), bf16),
        scratch_shapes=[pltpu.VMEM((WKV_LANES // LANES, LANES, LANES), f32)],
        compiler_params=pltpu.CompilerParams(
            dimension_semantics=("parallel", "parallel", "arbitrary"),
            vmem_limit_bytes=VMEM_LIMIT),
        name="wkv7",
    )(r, k, v, *zs, *ups, *rows)


def _fox_gate_kernel(s_ref, bf_ref, cp_ref, sp_ref, run_ref, *, tc):
    @pl.when(pl.program_id(1) == 0)
    def _():
        run_ref[...] = jnp.zeros_like(run_ref)

    small = s_ref[...]
    ls = -_softplus(-(small + bf_ref[...]))
    tr = lax.broadcasted_iota(jnp.int32, (tc, tc), 0)
    tcc = lax.broadcasted_iota(jnp.int32, (tc, tc), 1)
    cs = _dot_sel_lhs((tr >= tcc).astype(bf16), ls) + run_ref[...]
    run_ref[...] = cs[tc - 1:tc, :]
    cp_ref[...] = jnp.concatenate(_split3(cs * LOG2_E), axis=1)
    sp_ref[...] = jnp.concatenate(_split3(small)[:LOGIT_SPLIT], axis=1)


def _fox_gate(small, b_f, tc=512):
    m, w = small.shape
    nt = SEQ // tc
    bias = jnp.pad(b_f, (0, w - b_f.shape[0])).reshape(1, w)
    pieces = lambda n: pl.BlockSpec((tc, n * w), lambda b, t: (b * nt + t, 0))
    return pl.pallas_call(
        functools.partial(_fox_gate_kernel, tc=tc),
        grid=(BATCH, nt),
        in_specs=[pl.BlockSpec((tc, w), lambda b, t: (b * nt + t, 0)),
                  pl.BlockSpec((1, w), lambda b, t: (0, 0))],
        out_specs=[pieces(N_SPLIT), pieces(LOGIT_SPLIT)],
        out_shape=[jax.ShapeDtypeStruct((m, n * w), bf16) for n in (N_SPLIT, LOGIT_SPLIT)],
        scratch_shapes=[pltpu.VMEM((1, w), f32)],
        compiler_params=pltpu.CompilerParams(dimension_semantics=("parallel", "arbitrary"),
                                             vmem_limit_bytes=VMEM_LIMIT),
        name="fox_gate_cumsum",
    )(small, bias)


def _fox_prep_kernel(q_ref, k_ref, v_ref, kp_ref, vp_ref, s_ref, c_ref, qg_ref, kg_ref,
                     qo_ref, ko_ref, vo_ref, *, tm):
    i = pl.program_id(0)
    npairs = PREP_LANES // LANES
    head_ones = _head_ones()
    seq_start = (i * tm) % SEQ == 0
    rid = lax.broadcasted_iota(jnp.int32, (tm, 1), 0)
    r = lax.broadcasted_iota(jnp.int32, (LANES, 2 * LANES), 0)
    col = lax.broadcasted_iota(jnp.int32, (LANES, 2 * LANES), 1)
    ra = lax.broadcasted_iota(jnp.int32, (LANES, LANES), 0)
    ca = lax.broadcasted_iota(jnp.int32, (LANES, LANES), 1)
    small_pieces = s_ref[...]
    c_pieces = c_ref[...]

    def shifted(x, prow):
        prow = jnp.where(seq_start, jnp.zeros_like(prow), prow)
        return jnp.where(rid == 0, prow, pltpu.roll(x, 1, 0))

    def head_rms(x, gain):
        ms = _head_sums(x * x, head_ones) * (1.0 / HEAD_DIM)
        return x * lax.rsqrt(ms + RMS_EPS) * gain

    for p in range(npairs):
        pair = pl.program_id(1) * npairs + p
        ln = slice(p * LANES, (p + 1) * LANES)
        src = (((col & (LANES - 1)) >> HEAD_SHIFT) + HEADS_PER_VREG * pair
               + N_HEADS * (1 + (col >> LANE_SHIFT)))
        sel = (r == src).astype(bf16)
        logits = jnp.dot(small_pieces, jnp.concatenate([sel] * LOGIT_SPLIT, axis=0), preferred_element_type=f32)
        ak = jax.nn.sigmoid(logits[:, :LANES])
        av = jax.nn.sigmoid(logits[:, LANES:])
        k = k_ref[:, ln]
        v = v_ref[:, ln]
        k = ak * shifted(k, kp_ref[SUBLANES - 1:SUBLANES, ln]) + (1.0 - ak) * k
        v = av * shifted(v, vp_ref[SUBLANES - 1:SUBLANES, ln]) + (1.0 - av) * v
        qo_ref[:, ln] = (head_rms(q_ref[:, ln], qg_ref[...])
                         * (HEAD_DIM ** -0.5 * LOG2_E)).astype(qo_ref.dtype)
        sel_aug = []
        for n in range(N_SPLIT):
            hit = jnp.zeros((LANES, LANES), jnp.bool_)
            for h in range(HEADS_PER_VREG):
                hit = hit | ((ca == N_SPLIT * h + n) & (ra == HEADS_PER_VREG * pair + h))
            sel_aug.append(hit.astype(bf16))
        aug = jnp.dot(c_pieces, jnp.concatenate(sel_aug, axis=0), preferred_element_type=f32)
        ko_ref[:, 2 * p * LANES:2 * (p + 1) * LANES] = jnp.concatenate(
            [head_rms(k, kg_ref[...]), aug], axis=1).astype(ko_ref.dtype)
        vo_ref[ln, :] = v.T.astype(vo_ref.dtype)


def _fox_prep(proj, small_pieces, c_pieces, qn_g, kn_g, tm=1024):
    m = proj.shape[0]
    rb = tm // SUBLANES
    nb = D_MODEL // PREP_LANES
    tile = lambda cb: pl.BlockSpec((tm, PREP_LANES), lambda i, j, cb=cb: (i, cb * nb + j))
    prev = lambda cb: pl.BlockSpec(
        (SUBLANES, PREP_LANES), lambda i, j, cb=cb: (jnp.maximum(i * rb - 1, 0), cb * nb + j))
    gain = pl.BlockSpec((1, LANES), lambda i, j: (0, 0))
    small_tile = lambda a: pl.BlockSpec((tm, a.shape[1]), lambda i, j: (i, 0))
    tile_gain = lambda x: jnp.tile(x, HEADS_PER_VREG).reshape(1, LANES)
    return pl.pallas_call(
        functools.partial(_fox_prep_kernel, tm=tm),
        grid=(m // tm, nb),
        in_specs=[tile(0), tile(1), tile(2), prev(1), prev(2), small_tile(small_pieces), small_tile(c_pieces),
                  gain, gain],
        out_specs=[pl.BlockSpec((tm, PREP_LANES), lambda i, j: (i, j)),
                   pl.BlockSpec((tm, 2 * PREP_LANES), lambda i, j: (i, j)),
                   pl.BlockSpec((PREP_LANES, tm), lambda i, j: (j, i))],
        out_shape=[jax.ShapeDtypeStruct((m, D_MODEL), bf16),
                   jax.ShapeDtypeStruct((m, 2 * D_MODEL), bf16),
                   jax.ShapeDtypeStruct((D_MODEL, m), bf16)],
        compiler_params=pltpu.CompilerParams(dimension_semantics=("parallel", "parallel"),
                                             vmem_limit_bytes=VMEM_LIMIT),
        name="fox_prep",
    )(proj, proj, proj, proj, proj, small_pieces, c_pieces, tile_gain(qn_g), tile_gain(kn_g))


def _fox_attn_kernel(q_ref, k_ref, vt_ref, gate_ref, og_ref, o_ref):
    tq, tk = ATT_TQ, ATT_TK
    qi = pl.program_id(2)
    npairs = ATT_LANES // LANES
    chains = [(p, h) for p in range(npairs) for h in range(HEADS_PER_VREG)]
    lane = lax.broadcasted_iota(jnp.int32, (tq, LANES), 1)
    q_aug = []
    for p, h in chains:
        q = q_ref[:, p * LANES:(p + 1) * LANES]
        own = (lane >> HEAD_SHIFT) == h
        minus_one = (lane >= N_SPLIT * h) & (lane < N_SPLIT * (h + 1))
        q_aug.append(jnp.concatenate([jnp.where(own, q, jnp.zeros_like(q)),
                                      jnp.where(minus_one, -1.0, 0.0).astype(bf16)], axis=1))

    def step(key0, width, carry, diag):
        m_run, l_run, acc = carry
        keys = pl.ds(pl.multiple_of(key0, tq), width)
        s = [_dot_nt(k_ref[keys, 2 * p * LANES:2 * (p + 1) * LANES], qa)
             for (p, h), qa in zip(chains, q_aug)]
        if diag:
            kidx = key0 + lax.broadcasted_iota(jnp.int32, (width, tq), 0)
            qidx = qi * tq + lax.broadcasted_iota(jnp.int32, (width, tq), 1)
            s = [jnp.where(qidx >= kidx, x, NEG_BIG) for x in s]
        m_new = [jnp.maximum(mr, jnp.max(x, axis=0, keepdims=True)) for mr, x in zip(m_run, s)]
        alpha = [jnp.exp2(mr - mn) for mr, mn in zip(m_run, m_new)]
        pr = [jnp.exp2(x - mn) for x, mn in zip(s, m_new)]
        l_new = [a * lr + jnp.sum(x, axis=0, keepdims=True) for a, lr, x in zip(alpha, l_run, pr)]
        pv = [jnp.dot(vt_ref[pl.ds((p * HEADS_PER_VREG + h) * HEAD_DIM, HEAD_DIM), keys], x.astype(bf16),
                      preferred_element_type=f32) for (p, h), x in zip(chains, pr)]
        acc = [ac * a + x for ac, a, x in zip(acc, alpha, pv)]
        return m_new, l_new, acc

    n = len(chains)
    init = ([jnp.full((1, tq), NEG_BIG, f32)] * n, [jnp.zeros((1, tq), f32)] * n,
            [jnp.zeros((HEAD_DIM, tq), f32)] * n)
    n_full = (qi * tq) // tk
    carry = lax.fori_loop(0, n_full, lambda j, cy: step(j * tk, tk, cy, False), init)
    _, l_run, acc = lax.cond((qi * tq) % tk == 0,
                             lambda cy: step(n_full * tk, tq, cy, True),
                             lambda cy: step(n_full * tk, tk, cy, True), carry)
    o_t = []
    for ac, lr in zip(acc, l_run):
        o = ac / lr
        o_t.append(o * lax.rsqrt(jnp.mean(o * o, axis=0, keepdims=True) + RMS_EPS))
    for p in range(npairs):
        ln = slice(p * LANES, (p + 1) * LANES)
        o = jnp.concatenate(o_t[HEADS_PER_VREG * p:HEADS_PER_VREG * (p + 1)], axis=0).T
        o_ref[:, ln] = (o * og_ref[:, ln] * jax.nn.sigmoid(gate_ref[:, ln])).astype(o_ref.dtype)


def _fox_attn(q, k_aug, v_t, proj, on_g):
    m, d = q.shape
    nq = SEQ // ATT_TQ
    gate_col0 = 3 * D_MODEL // ATT_LANES
    return pl.pallas_call(
        _fox_attn_kernel,
        grid=(BATCH, d // ATT_LANES, nq),
        in_specs=[pl.BlockSpec((ATT_TQ, ATT_LANES), lambda b, p, i: (b * nq + i, p)),
                  pl.BlockSpec((SEQ, 2 * ATT_LANES), lambda b, p, i: (b, p)),
                  pl.BlockSpec((ATT_LANES, SEQ), lambda b, p, i: (p, b)),
                  pl.BlockSpec((ATT_TQ, ATT_LANES), lambda b, p, i: (b * nq + i, gate_col0 + p)),
                  pl.BlockSpec((1, ATT_LANES), lambda b, p, i: (0, p))],
        out_specs=pl.BlockSpec((ATT_TQ, ATT_LANES), lambda b, p, i: (b * nq + i, p)),
        out_shape=jax.ShapeDtypeStruct((m, d), bf16),
        compiler_params=pltpu.CompilerParams(
            dimension_semantics=("parallel", "parallel", "arbitrary"),
            vmem_limit_bytes=VMEM_LIMIT),
        name="fox_attention",
    )(q, k_aug, v_t, proj, on_g.reshape(1, d))


def _swiglu_block(x, hn, w_gu, w_d, layer, next_norm_g, last):
    act = _matmul(hn, w_gu, layer, (0, 1), D_FF, tm=2048, tn=512, tk=D_MODEL,
                  epilogue=_epi_swiglu, out_dtype=bf16, name="swiglu_gate_up")
    return _matmul(act, w_d, layer, (0,), D_MODEL, tm=256, tn=D_MODEL, tk=D_FF, resident_w=True,
                   epilogue=_epi_residual_norm_only if last else _epi_residual_norm,
                   extras=((x, "tile"), (next_norm_g.reshape(1, D_MODEL), "row")),
                   out_dtype=f32 if last else (f32, bf16), name="swiglu_down")


def _proj(x, w, layer=0, *, name):
    if w.ndim == 2:
        w = w[None]
    n = w.shape[2]
    resident = n <= D_MODEL
    return _matmul(x, w, layer, (0,), n, tm=1024 if resident else 2048, tn=n if resident else 1024,
                   tk=x.shape[1], epilogue=_epi_plain, out_dtype=f32, name=name, resident_w=resident)


def _out_proj_norm(y, w_o, x, next_norm_g, *, name):
    return _matmul(y, w_o.astype(bf16)[None], 0, (0,), D_MODEL, tm=512, tn=D_MODEL, tk=D_MODEL,
                   epilogue=_epi_residual_norm, extras=((x, "tile"), (next_norm_g.reshape(1, D_MODEL), "row")),
                   out_dtype=(f32, bf16), name=name)


def _rwkv7_block(x, norm_g, mix, w_rkv, w0, w1, w2, a0, a1, a2, g1, g2, k_k, k_a, r_k, lnx_g, lnx_b, w_o,
                 next_norm_g):
    rank = LANES * pl.cdiv(w1.shape[1], LANES)
    downs = (_pad_cols(w1, rank).astype(bf16), _pad_cols(a1, rank).astype(bf16), g1.astype(bf16))
    ups = (_pad_rows(w2, rank).astype(bf16), _pad_rows(a2, rank).astype(bf16), g2.astype(bf16))
    xr, xk, xv, *zs = _norm_mix(x, norm_g, mix, *downs)
    w_rkv = w_rkv.astype(bf16)
    r = _proj(xr, w_rkv, 0, name="rwkv_r")
    k = _proj(xk, w_rkv, 1, name="rwkv_k")
    v = _proj(xv, w_rkv, 2, name="rwkv_v")
    yg = _wkv(r, k, v, zs, ups, w0, a0, k_k, k_a, r_k.reshape(-1), lnx_g, lnx_b)
    return _out_proj_norm(yg, w_o, x, next_norm_g, name="rwkv_out")


def _fox_block(x, hn, w_in, b_f, qn_g, kn_g, on_g, w_o, next_norm_g):
    n_main = 4 * D_MODEL
    proj = _proj(hn, w_in[:, :n_main].astype(bf16), name="fox_in")
    w_small = _pad_cols(w_in[:, n_main:], LANES).astype(bf16)
    small = _proj(hn, w_small, name="fox_in_gates")
    c_pieces, small_pieces = _fox_gate(small, b_f)
    q, k_aug, v_t = _fox_prep(proj, small_pieces, c_pieces, qn_g, kn_g)
    og = _fox_attn(q, k_aug, v_t, proj, on_g)
    return _out_proj_norm(og, w_o, x, next_norm_g, name="fox_out")


def kernel(x, a_norm_g, a_mix, a_w_rkv, a_w0, a_w1, a_w2, a_a0, a_a1, a_a2, a_g1, a_g2, a_k_k, a_k_a, a_r_k, a_lnx_g, a_lnx_b, a_w_o, b_norm_g, b_w_in, b_b_f, b_qn_g, b_kn_g, b_on_g, b_w_o, f_norm_g, f_w_gu, f_w_d, final_g):
    b, t, d = x.shape
    h = x.reshape(b * t, d)
    w_d = f_w_d.astype(bf16)
    h = _rwkv7_block(h, a_norm_g[0], a_mix[0], a_w_rkv[0], a_w0[0], a_w1[0], a_w2[0], a_a0[0], a_a1[0],
                     a_a2[0], a_g1[0], a_g2[0], a_k_k[0], a_k_a[0], a_r_k[0], a_lnx_g[0], a_lnx_b[0],
                     a_w_o[0], f_norm_g[0])
    h = _swiglu_block(*h, f_w_gu, w_d, 0, b_norm_g[0], False)
    h = _fox_block(*h, b_w_in[0], b_b_f[0], b_qn_g[0], b_kn_g[0], b_on_g[0], b_w_o[0],
                   f_norm_g[1])
    return _swiglu_block(*h, f_w_gu, w_d, 1, final_g, True).reshape(b, t, d)
```

```python
import functools

import jax
import jax.numpy as jnp
from jax import lax
from jax.experimental import pallas as pl
from jax.experimental.pallas import tpu as pltpu

D_MODEL = 2048
BATCH = 8
SEQ = 2048
N_TOK = BATCH * SEQ
HEAD_DIM = 64
HEAD_SHIFT = HEAD_DIM.bit_length() - 1
N_HEADS = D_MODEL // HEAD_DIM
D_FF = 5632
RMS_EPS = 1e-6
GN_EPS = 64e-5

LANES = 128
LANE_SHIFT = LANES.bit_length() - 1
SUBLANES = 8
HEADS_PER_VREG = LANES // HEAD_DIM
N_PAIRS = D_MODEL // LANES
VMEM_LIMIT = 56 * 1024 * 1024

NORM_MIX_DOT_ROWS = 128
WKV_CHUNK = 64
WKV_TBLK = 512
WKV_LANES = 512
ATT_TQ = 256
ATT_TK = 512
ATT_LANES = 1024
PREP_LANES = 512
N_SPLIT = 3
LOGIT_SPLIT = 2
NEG_BIG = -1e30
LOG2_E = 1.4426950408889634

f32 = jnp.float32
bf16 = jnp.bfloat16


def _dot(a, b):
    return jnp.dot(a.astype(bf16), b.astype(bf16), preferred_element_type=f32)


def _dot_nt(a, b):
    return lax.dot_general(a.astype(bf16), b.astype(bf16), (((1,), (1,)), ((), ())),
                           preferred_element_type=f32)


def _split3(x):
    hi = x.astype(bf16)
    r1 = x - hi.astype(f32)
    mid = r1.astype(bf16)
    lo = (r1 - mid.astype(f32)).astype(bf16)
    return hi, mid, lo


def _dot_sel_rhs(x, sel):
    hi, mid, lo = _split3(x)
    d = lambda p: jnp.dot(p, sel, preferred_element_type=f32)
    return d(hi) + d(mid) + d(lo)


def _dot_sel_lhs(sel, x):
    hi, mid, lo = _split3(x)
    d = lambda p: jnp.dot(sel, p, preferred_element_type=f32)
    return d(hi) + d(mid) + d(lo)


def _head_sums(x, head_ones):
    hi = x.astype(bf16)
    lo = (x - hi.astype(f32)).astype(bf16)
    return jnp.dot(jnp.concatenate([hi, lo], axis=1), jnp.concatenate([head_ones, head_ones], axis=0),
                   preferred_element_type=f32)


def _head_sums_lanes(x, m0):
    zero = jnp.zeros_like(x)
    s0 = jnp.sum(jnp.where(m0, x, zero), axis=-1, keepdims=True)
    s1 = jnp.sum(jnp.where(m0, zero, x), axis=-1, keepdims=True)
    return jnp.where(m0, s0, s1)


def _head_ones():
    r = lax.broadcasted_iota(jnp.int32, (LANES, LANES), 0) >> HEAD_SHIFT
    c = lax.broadcasted_iota(jnp.int32, (LANES, LANES), 1) >> HEAD_SHIFT
    return (r == c).astype(bf16)


def _first_head_mask():
    return lax.broadcasted_iota(jnp.int32, (1, LANES), 1) < HEAD_DIM


def _stack_heads(x, m0):
    z = jnp.zeros_like(x)
    return jnp.concatenate([jnp.where(m0, x, z), jnp.where(m0, z, x)], axis=0)


def _softplus(z):
    return jnp.maximum(z, 0.0) + jnp.log(1.0 + jnp.exp(-jnp.abs(z)))


def _rms(x, g):
    return x * lax.rsqrt(jnp.mean(x * x, axis=-1, keepdims=True) + RMS_EPS) * g


def _norm_mix_kernel(x_ref, xp_ref, g_ref, mix_ref, w1_ref, a1_ref, g1_ref,
                     xr_ref, xk_ref, xv_ref, zw_ref, za_ref, zg_ref, xw_s, xa_s, xg_s, *, tm):
    i = pl.program_id(0)
    d = x_ref.shape[1]
    rows_per_chunk = 2 * SUBLANES
    lane_blk = 4 * LANES
    seq_start = (i * tm) % SEQ == 0
    hp_row = _rms(xp_ref[...], g_ref[...])[SUBLANES - 1:SUBLANES, :]
    hp_row = jnp.where(seq_start, jnp.zeros_like(hp_row), hp_row)
    rid = lax.broadcasted_iota(jnp.int32, (rows_per_chunk, 1), 0)
    mix_dsts = (xr_ref, xk_ref, xv_ref, xw_s, xa_s, xg_s)

    def chunk(c, last_row):
        rows = slice(c * rows_per_chunk, (c + 1) * rows_per_chunk)
        x = x_ref[rows, :]
        inv = lax.rsqrt(jnp.mean(x * x, axis=-1, keepdims=True) + RMS_EPS)
        new_last = []
        for b in range(d // lane_blk):
            ln = slice(b * lane_blk, (b + 1) * lane_blk)
            h = x[:, ln] * inv * g_ref[:, ln]
            hprev = jnp.where(rid == 0, last_row[:, ln], pltpu.roll(h, 1, 0))
            xx = hprev - h
            for p, dst in enumerate(mix_dsts):
                dst[rows, ln] = (h + xx * mix_ref[p:p + 1, ln]).astype(dst.dtype)
            new_last.append(h[rows_per_chunk - 1:rows_per_chunk, :])
        return jnp.concatenate(new_last, axis=1)

    chunks_per_blk = NORM_MIX_DOT_ROWS // rows_per_chunk
    last_row = hp_row
    for blk in range(tm // NORM_MIX_DOT_ROWS):
        for c in range(blk * chunks_per_blk, (blk + 1) * chunks_per_blk):
            last_row = chunk(c, last_row)
        rows = slice(blk * NORM_MIX_DOT_ROWS, (blk + 1) * NORM_MIX_DOT_ROWS)
        down = lambda xs, w: jnp.dot(xs[rows, :], w[...], preferred_element_type=f32)
        zw_ref[rows, :] = jnp.tanh(down(xw_s, w1_ref)).astype(zw_ref.dtype)
        za_ref[rows, :] = down(xa_s, a1_ref).astype(za_ref.dtype)
        zg_ref[rows, :] = jax.nn.sigmoid(down(xg_s, g1_ref)).astype(zg_ref.dtype)


def _norm_mix(x, g, mix, w1, a1, g1, tm=512):
    m, d = x.shape
    rb = tm // SUBLANES
    row_tile = lambda n: pl.BlockSpec((tm, n), lambda i: (i, 0))
    whole = lambda a: pl.BlockSpec(a.shape, lambda i: (0, 0))
    lows = (w1, a1, g1)
    return pl.pallas_call(
        functools.partial(_norm_mix_kernel, tm=tm),
        grid=(m // tm,),
        in_specs=[row_tile(d),
                  pl.BlockSpec((SUBLANES, d), lambda i: (jnp.maximum(i * rb - 1, 0), 0)),
                  pl.BlockSpec((1, d), lambda i: (0, 0)),
                  pl.BlockSpec((6, d), lambda i: (0, 0))] + [whole(w) for w in lows],
        out_specs=[row_tile(d)] * 3 + [row_tile(w.shape[1]) for w in lows],
        out_shape=[jax.ShapeDtypeStruct((m, d), bf16)] * 3
        + [jax.ShapeDtypeStruct((m, w.shape[1]), bf16) for w in lows],
        scratch_shapes=[pltpu.VMEM((tm, d), bf16)] * 3,
        compiler_params=pltpu.CompilerParams(dimension_semantics=("parallel",),
                                             vmem_limit_bytes=VMEM_LIMIT),
        name="norm_mix",
    )(x, x, g.reshape(1, d), mix, *lows)


def _mm_kernel(*refs, n_w, n_e, n_o, nk, epilogue):
    x_ref = refs[0]
    w_refs = refs[1:1 + n_w]
    e_refs = refs[1 + n_w:1 + n_w + n_e]
    o_refs = refs[1 + n_w + n_e:1 + n_w + n_e + n_o]
    acc_refs = refs[1 + n_w + n_e + n_o:]
    x = x_ref[...]

    def finish(accs):
        outs = epilogue(accs, [e[...] for e in e_refs])
        for o_ref, out in zip(o_refs, outs if isinstance(outs, tuple) else (outs,)):
            o_ref[...] = out.astype(o_ref.dtype)

    if nk == 1:
        finish([jnp.dot(x, w[...].astype(x.dtype), preferred_element_type=f32) for w in w_refs])
        return
    k = pl.program_id(2)

    @pl.when(k == 0)
    def _():
        for a in acc_refs:
            a[...] = jnp.zeros_like(a)

    for a, w in zip(acc_refs, w_refs):
        a[...] += jnp.dot(x, w[...], preferred_element_type=f32)

    @pl.when(k == nk - 1)
    def _():
        finish([a[...] for a in acc_refs])


def _matmul(x, w, layer, w_col_blocks, n_out, *, tm, tn, tk, epilogue, extras=(), out_dtype, name,
            resident_w=False):
    m, kdim = x.shape
    nk = kdim // tk
    n_w = len(w_col_blocks)
    nb = n_out // tn
    in_specs = [pl.BlockSpec((tm, tk), lambda i, j, k: (i, k))]
    args = [x]
    w_mode = pl.Buffered(1) if resident_w else None
    for cb in w_col_blocks:
        in_specs.append(pl.BlockSpec((None, tk, tn), lambda i, j, k, cb=cb: (layer, k, cb * nb + j),
                                     pipeline_mode=w_mode))
        args.append(w)
    for arr, kind in extras:
        if kind == "row":
            in_specs.append(pl.BlockSpec((1, tn), lambda i, j, k: (0, j)))
        else:
            in_specs.append(pl.BlockSpec((tm, tn), lambda i, j, k: (i, j)))
        args.append(arr)
    scratch = [pltpu.VMEM((tm, tn), f32) for _ in range(n_w)] if nk > 1 else []
    multi = isinstance(out_dtype, tuple)
    dtypes = out_dtype if multi else (out_dtype,)
    out_spec = pl.BlockSpec((tm, tn), lambda i, j, k: (i, j))
    outs = pl.pallas_call(
        functools.partial(_mm_kernel, n_w=n_w, n_e=len(extras), n_o=len(dtypes), nk=nk, epilogue=epilogue),
        grid=(m // tm, nb, nk),
        in_specs=in_specs,
        out_specs=[out_spec] * len(dtypes),
        out_shape=[jax.ShapeDtypeStruct((m, n_out), dt) for dt in dtypes],
        scratch_shapes=scratch,
        compiler_params=pltpu.CompilerParams(
            dimension_semantics=("parallel", "parallel", "arbitrary"),
            vmem_limit_bytes=VMEM_LIMIT),
        name=name,
    )(*args)
    return tuple(outs) if multi else outs[0]


def _epi_plain(accs, extras):
    return accs[0]


def _epi_residual_norm(accs, extras):
    y = extras[0] + accs[0]
    return y, _rms(y, extras[1])


def _epi_residual_norm_only(accs, extras):
    return _rms(extras[0] + accs[0], extras[1])


def _epi_swiglu(accs, extras):
    gate, up = accs
    return gate * jax.nn.sigmoid(gate) * up


def _log_decay(y, w0):
    return -jnp.exp(-_softplus(-(w0 + y)) - 0.5)


def _pad_cols(w, n):
    return jnp.pad(w, ((0, 0), (0, n - w.shape[1])))


def _pad_rows(w, n):
    return jnp.pad(w, ((0, n - w.shape[0]), (0, 0)))


def _each(fn, *lists):
    return [fn(*xs) for xs in zip(*lists)]


def _wkv_chunk_maps(tiles, consts):
    m0, strict, incl, eye = consts
    c = WKV_CHUNK
    n2 = 2 * c
    r, lw, lg, k2, v, av, bv = (list(x) for x in zip(*tiles))
    stack = lambda x: _stack_heads(x, m0)
    lg_end = _each(lambda x: x[c - 1:c, :], lg)
    g_inv = _each(lambda x: jnp.exp(-x), lg)
    g_rem = _each(lambda e, x: jnp.exp(e - x), lg_end, lg)
    rs = _each(lambda x, l: stack(x * jnp.exp(l)), r, lg)
    as_ = _each(lambda x, l, w: stack(x * jnp.exp(l - w)).astype(bf16), av, lg, lw)
    bs = _each(lambda x, g: stack(x * g).astype(bf16), bv, g_inv)
    ks = _each(lambda x, g: stack(x * g).astype(bf16), k2, g_inv)
    bhs = _each(lambda x, g: stack(x * g).astype(bf16), bv, g_rem)
    khs = _each(lambda x, g: stack(x * g).astype(bf16), k2, g_rem)
    vs = _each(stack, v)
    sc = _each(lambda a, rr, b, k: _dot_nt(jnp.concatenate([a, rr.astype(bf16)], axis=0),
                                           jnp.concatenate([b, k], axis=0)), as_, rs, bs, ks)
    zero = jnp.zeros((n2, n2), f32)
    a_ab = _each(lambda s: jnp.where(strict, s[:n2, :n2], zero), sc)
    a_ak = _each(lambda s: jnp.where(strict, s[:n2, n2:], zero), sc)
    a_r = _each(lambda s: jnp.where(jnp.concatenate([incl, incl], axis=1), s[n2:, :],
                                    jnp.zeros((n2, 2 * n2), f32)).astype(bf16), sc)
    side = lambda a, b: jnp.concatenate([a, b], axis=1)
    t = _each(lambda x: eye + x, a_ab)
    p = _each(lambda x: _dot(x, x), a_ab)
    for _ in range(c.bit_length() - 3):
        pt = _each(lambda pp, tt: _dot(pp, side(pp, tt)), p, t)
        p = _each(lambda x: x[:, :n2], pt)
        t = _each(lambda tt, x: tt + x[:, n2:], t, pt)
    t = _each(lambda tt, pp: (tt + _dot(pp, tt)).astype(bf16), t, p)
    akv = _each(_dot, a_ak, vs)
    hw = _each(lambda tt, a, x: _dot(tt, side(a, x)), t, as_, akv)
    ah = _each(lambda x: x[:, :LANES], hw)
    ws = _each(lambda x: x[:, LANES:], hw)
    pq = _each(lambda ar, x, vv: _dot(ar, jnp.concatenate([x, side(jnp.zeros_like(vv), vv)], axis=0)),
               a_r, hw, vs)
    pc = _each(lambda x, y: x + y[:, :LANES], rs, pq)
    qc = _each(lambda y: y[:, LANES:], pq)
    gm = _each(lambda h, b: _dot(h.T, b), ah, bhs)
    nc = _each(lambda w, x, b, k: _dot(jnp.concatenate([w, x], axis=0).T, jnp.concatenate([b, k], axis=0)),
               ws, vs, bhs, khs)
    decay = _each(jnp.exp, lg_end)
    return pc, qc, gm, nc, decay


def _wkv_kernel(r_ref, k_ref, v_ref, zw_ref, za_ref, zg_ref, w2_ref, a2_ref, g2_ref, w0_ref, a0_ref,
                kk_ref, ka_ref, rk_ref, lng_ref, lnb_ref, o_ref, s_ref):
    c = WKV_CHUNK
    n2 = 2 * c
    up = lambda z_ref, w_ref: jnp.dot(z_ref[...], w_ref[...], preferred_element_type=f32)
    lw_all = _log_decay(up(zw_ref, w2_ref), w0_ref[...])
    a_all = jax.nn.sigmoid(a0_ref[...] + up(za_ref, a2_ref))
    g_all = up(zg_ref, g2_ref)

    @pl.when(pl.program_id(2) == 0)
    def _():
        s_ref[...] = jnp.zeros_like(s_ref)

    ri = lax.broadcasted_iota(jnp.int32, (n2, n2), 0)
    ci = lax.broadcasted_iota(jnp.int32, (n2, n2), 1)
    tr = lax.broadcasted_iota(jnp.int32, (c, c), 0)
    tc = lax.broadcasted_iota(jnp.int32, (c, c), 1)
    consts = (_first_head_mask(), (ri & (c - 1)) > (ci & (c - 1)), (ri & (c - 1)) >= (ci & (c - 1)),
              (ri == ci).astype(f32))
    tri = (tr >= tc).astype(bf16)
    m0 = consts[0]
    inv_n = 1.0 / HEAD_DIM

    npairs = WKV_LANES // LANES
    nchunk = WKV_TBLK // c
    lanes = [slice(p * LANES, (p + 1) * LANES) for p in range(npairs)]
    r = [r_ref[:, ln] for ln in lanes]
    v = [v_ref[:, ln] for ln in lanes]
    k2, av, bv, bonus = [], [], [], []
    for p, ln in enumerate(lanes):
        k, a = k_ref[:, ln], a_all[:, ln]
        kk = k * kk_ref[:, ln]
        ss = _head_sums_lanes(kk * kk, m0)
        kk = kk / jnp.maximum(jnp.sqrt(ss), 1e-12)
        k2.append(k * (1.0 + (a - 1.0) * ka_ref[:, ln]))
        av.append(-kk)
        bv.append(kk * a)
        bonus.append(_head_sums_lanes(r[p] * k2[p] * rk_ref[:, ln], m0))
    tiles = []
    for ic in range(nchunk):
        rows = slice(ic * c, (ic + 1) * c)
        lw = lw_all[rows, :]
        lg = _dot_sel_lhs(tri, lw)
        for p, ln in enumerate(lanes):
            tiles.append((r[p][rows], lw[:, ln], lg[:, ln], k2[p][rows], v[p][rows], av[p][rows], bv[p][rows]))
    pc, qc, gm, nc, decay = _wkv_chunk_maps(tiles, consts)
    s = [s_ref[p] for p in range(npairs)]
    ys = [[] for _ in range(npairs)]
    for ic in range(nchunk):
        for p in range(npairs):
            i = ic * npairs + p
            y = _dot_nt(pc[i], s[p]) + qc[i]
            ys[p].append(y[:c] + y[c:])
            s[p] = s[p] * decay[i] + _dot(s[p], gm[i]) + nc[i]
    for p, ln in enumerate(lanes):
        s_ref[p] = s[p]
        y = jnp.concatenate(ys[p], axis=0)
        mu = _head_sums_lanes(y, m0) * inv_n
        yc = y - mu
        var = _head_sums_lanes(yc * yc, m0) * inv_n
        yn = yc * lax.rsqrt(var + GN_EPS) * lng_ref[:, ln] + lnb_ref[:, ln]
        o_ref[:, ln] = ((yn + bonus[p] * v[p]) * g_all[:, ln]).astype(o_ref.dtype)


def _wkv(r, k, v, zs, ups, w0, a0, k_k, k_a, r_k, lnx_g, lnx_b):
    m, d = r.shape
    tb = SEQ // WKV_TBLK
    tile = pl.BlockSpec((WKV_TBLK, WKV_LANES), lambda b, j, t: (b * tb + t, j))
    row = pl.BlockSpec((1, WKV_LANES), lambda b, j, t: (0, j))
    z_tile = lambda z: pl.BlockSpec((WKV_TBLK, z.shape[1]), lambda b, j, t: (b * tb + t, 0))
    up_tile = lambda w: pl.BlockSpec((w.shape[0], WKV_LANES), lambda b, j, t: (0, j))
    rows = [x.reshape(1, d) for x in (w0, a0, k_k, k_a, r_k, lnx_g, lnx_b)]
    return pl.pallas_call(
        _wkv_kernel,
        grid=(BATCH, d // WKV_LANES, tb),
        in_specs=[tile] * 3 + [z_tile(z) for z in zs] + [up_tile(w) for w in ups] + [row] * len(rows),
        out_specs=tile,
        out_shape=jax.ShapeDtypeStruct((m, d), bf16),
        scratch_shapes=[pltpu.VMEM((WKV_LANES // LANES, LANES, LANES), f32)],
        compiler_params=pltpu.CompilerParams(
            dimension_semantics=("parallel", "parallel", "arbitrary"),
            vmem_limit_bytes=VMEM_LIMIT),
        name="wkv7",
    )(r, k, v, *zs, *ups, *rows)


def _fox_gate_kernel(s_ref, bf_ref, cp_ref, sp_ref, run_ref, *, tc):
    @pl.when(pl.program_id(1) == 0)
    def _():
        run_ref[...] = jnp.zeros_like(run_ref)

    small = s_ref[...]
    ls = -_softplus(-(small + bf_ref[...]))
    tr = lax.broadcasted_iota(jnp.int32, (tc, tc), 0)
    tcc = lax.broadcasted_iota(jnp.int32, (tc, tc), 1)
    cs = _dot_sel_lhs((tr >= tcc).astype(bf16), ls) + run_ref[...]
    run_ref[...] = cs[tc - 1:tc, :]
    w = cs.shape[1]
    src = lax.broadcasted_iota(jnp.int32, (w, w), 0)
    dst = lax.broadcasted_iota(jnp.int32, (w, w), 1)
    sel = [((dst == N_SPLIT * src + n) & (src < N_HEADS)).astype(bf16) for n in range(N_SPLIT)]
    cp_ref[...] = jnp.dot(jnp.concatenate(_split3(cs * LOG2_E), axis=1), jnp.concatenate(sel, axis=0),
                          preferred_element_type=f32).astype(cp_ref.dtype)
    sp_ref[...] = jnp.concatenate(_split3(small)[:LOGIT_SPLIT], axis=1)


def _fox_gate(small, b_f, tc=512):
    m, w = small.shape
    nt = SEQ // tc
    bias = jnp.pad(b_f, (0, w - b_f.shape[0])).reshape(1, w)
    pieces = lambda n: pl.BlockSpec((tc, n * w), lambda b, t: (b * nt + t, 0))
    return pl.pallas_call(
        functools.partial(_fox_gate_kernel, tc=tc),
        grid=(BATCH, nt),
        in_specs=[pl.BlockSpec((tc, w), lambda b, t: (b * nt + t, 0)),
                  pl.BlockSpec((1, w), lambda b, t: (0, 0))],
        out_specs=[pieces(1), pieces(LOGIT_SPLIT)],
        out_shape=[jax.ShapeDtypeStruct((m, n * w), bf16) for n in (1, LOGIT_SPLIT)],
        scratch_shapes=[pltpu.VMEM((1, w), f32)],
        compiler_params=pltpu.CompilerParams(dimension_semantics=("parallel", "arbitrary"),
                                             vmem_limit_bytes=VMEM_LIMIT),
        name="fox_gate_cumsum",
    )(small, bias)


def _fox_prep_kernel(q_ref, k_ref, v_ref, kp_ref, vp_ref, s_ref, c_ref, qg_ref, kg_ref,
                     qo_ref, ko_ref, vo_ref, *, tm):
    i = pl.program_id(0)
    npairs = PREP_LANES // LANES
    head_ones = _head_ones()
    seq_start = (i * tm) % SEQ == 0
    rid = lax.broadcasted_iota(jnp.int32, (tm, 1), 0)
    r = lax.broadcasted_iota(jnp.int32, (LANES, 2 * LANES), 0)
    col = lax.broadcasted_iota(jnp.int32, (LANES, 2 * LANES), 1)
    small_pieces = s_ref[...]
    aug = c_ref[...]

    def shifted(x, prow):
        prow = jnp.where(seq_start, jnp.zeros_like(prow), prow)
        return jnp.where(rid == 0, prow, pltpu.roll(x, 1, 0))

    def head_rms(x, gain):
        ms = _head_sums(x * x, head_ones) * (1.0 / HEAD_DIM)
        return x * lax.rsqrt(ms + RMS_EPS) * gain

    for p in range(npairs):
        pair = pl.program_id(1) * npairs + p
        ln = slice(p * LANES, (p + 1) * LANES)
        src = (((col & (LANES - 1)) >> HEAD_SHIFT) + HEADS_PER_VREG * pair
               + N_HEADS * (1 + (col >> LANE_SHIFT)))
        sel = (r == src).astype(bf16)
        logits = jnp.dot(small_pieces, jnp.concatenate([sel] * LOGIT_SPLIT, axis=0), preferred_element_type=f32)
        ak = jax.nn.sigmoid(logits[:, :LANES])
        av = jax.nn.sigmoid(logits[:, LANES:])
        k = k_ref[:, ln]
        v = v_ref[:, ln]
        k = ak * shifted(k, kp_ref[SUBLANES - 1:SUBLANES, ln]) + (1.0 - ak) * k
        v = av * shifted(v, vp_ref[SUBLANES - 1:SUBLANES, ln]) + (1.0 - av) * v
        qo_ref[:, ln] = (head_rms(q_ref[:, ln], qg_ref[...])
                         * (HEAD_DIM ** -0.5 * LOG2_E)).astype(qo_ref.dtype)
        ko_ref[:, 2 * p * LANES:2 * (p + 1) * LANES] = jnp.concatenate(
            [head_rms(k, kg_ref[...]).astype(ko_ref.dtype), aug], axis=1)
        vo_ref[ln, :] = v.T.astype(vo_ref.dtype)


def _fox_prep(proj, small_pieces, c_pieces, qn_g, kn_g, tm=1024):
    m = proj.shape[0]
    rb = tm // SUBLANES
    nb = D_MODEL // PREP_LANES
    tile = lambda cb: pl.BlockSpec((tm, PREP_LANES), lambda i, j, cb=cb: (i, cb * nb + j))
    prev = lambda cb: pl.BlockSpec(
        (SUBLANES, PREP_LANES), lambda i, j, cb=cb: (jnp.maximum(i * rb - 1, 0), cb * nb + j))
    gain = pl.BlockSpec((1, LANES), lambda i, j: (0, 0))
    small_tile = lambda a: pl.BlockSpec((tm, a.shape[1]), lambda i, j: (i, 0))
    tile_gain = lambda x: jnp.tile(x, HEADS_PER_VREG).reshape(1, LANES)
    return pl.pallas_call(
        functools.partial(_fox_prep_kernel, tm=tm),
        grid=(m // tm, nb),
        in_specs=[tile(0), tile(1), tile(2), prev(1), prev(2), small_tile(small_pieces), small_tile(c_pieces),
                  gain, gain],
        out_specs=[pl.BlockSpec((tm, PREP_LANES), lambda i, j: (i, j)),
                   pl.BlockSpec((tm, 2 * PREP_LANES), lambda i, j: (i, j)),
                   pl.BlockSpec((PREP_LANES, tm), lambda i, j: (j, i))],
        out_shape=[jax.ShapeDtypeStruct((m, D_MODEL), bf16),
                   jax.ShapeDtypeStruct((m, 2 * D_MODEL), bf16),
                   jax.ShapeDtypeStruct((D_MODEL, m), bf16)],
        compiler_params=pltpu.CompilerParams(dimension_semantics=("parallel", "parallel"),
                                             vmem_limit_bytes=VMEM_LIMIT),
        name="fox_prep",
    )(proj, proj, proj, proj, proj, small_pieces, c_pieces, tile_gain(qn_g), tile_gain(kn_g))


def _fox_attn_kernel(q_ref, k_ref, vt_ref, gate_ref, og_ref, o_ref):
    tq, tk = ATT_TQ, ATT_TK
    qi = pl.program_id(2)
    npairs = ATT_LANES // LANES
    chains = [(p, h) for p in range(npairs) for h in range(HEADS_PER_VREG)]
    lane = lax.broadcasted_iota(jnp.int32, (tq, LANES), 1)
    q_aug = []
    for p, h in chains:
        q = q_ref[:, p * LANES:(p + 1) * LANES]
        own = (lane >> HEAD_SHIFT) == h
        head = (pl.program_id(1) * npairs + p) * HEADS_PER_VREG + h
        minus_one = (lane >= N_SPLIT * head) & (lane < N_SPLIT * (head + 1))
        q_aug.append(jnp.concatenate([jnp.where(own, q, jnp.zeros_like(q)),
                                      jnp.where(minus_one, -1.0, 0.0).astype(bf16)], axis=1))

    def step(key0, width, carry, diag):
        m_run, l_run, acc = carry
        keys = pl.ds(pl.multiple_of(key0, tq), width)
        s = [_dot_nt(k_ref[keys, 2 * p * LANES:2 * (p + 1) * LANES], qa)
             for (p, h), qa in zip(chains, q_aug)]
        if diag:
            kidx = key0 + lax.broadcasted_iota(jnp.int32, (width, tq), 0)
            qidx = qi * tq + lax.broadcasted_iota(jnp.int32, (width, tq), 1)
            s = [jnp.where(qidx >= kidx, x, NEG_BIG) for x in s]
        m_new = [jnp.maximum(mr, jnp.max(x, axis=0, keepdims=True)) for mr, x in zip(m_run, s)]
        alpha = [jnp.exp2(mr - mn) for mr, mn in zip(m_run, m_new)]
        pr = [jnp.exp2(x - mn) for x, mn in zip(s, m_new)]
        l_new = [a * lr + jnp.sum(x, axis=0, keepdims=True) for a, lr, x in zip(alpha, l_run, pr)]
        pv = [jnp.dot(vt_ref[pl.ds((p * HEADS_PER_VREG + h) * HEAD_DIM, HEAD_DIM), keys], x.astype(bf16),
                      preferred_element_type=f32) for (p, h), x in zip(chains, pr)]
        acc = [ac * a + x for ac, a, x in zip(acc, alpha, pv)]
        return m_new, l_new, acc

    n = len(chains)
    init = ([jnp.full((1, tq), NEG_BIG, f32)] * n, [jnp.zeros((1, tq), f32)] * n,
            [jnp.zeros((HEAD_DIM, tq), f32)] * n)
    n_full = (qi * tq) // tk
    carry = lax.fori_loop(0, n_full, lambda j, cy: step(j * tk, tk, cy, False), init)
    _, l_run, acc = lax.cond((qi * tq) % tk == 0,
                             lambda cy: step(n_full * tk, tq, cy, True),
                             lambda cy: step(n_full * tk, tk, cy, True), carry)
    o_t = []
    for ac, lr in zip(acc, l_run):
        o = ac / lr
        o_t.append(o * lax.rsqrt(jnp.mean(o * o, axis=0, keepdims=True) + RMS_EPS))
    for p in range(npairs):
        ln = slice(p * LANES, (p + 1) * LANES)
        o = jnp.concatenate(o_t[HEADS_PER_VREG * p:HEADS_PER_VREG * (p + 1)], axis=0).T
        o_ref[:, ln] = (o * og_ref[:, ln] * jax.nn.sigmoid(gate_ref[:, ln])).astype(o_ref.dtype)


def _fox_attn(q, k_aug, v_t, proj, on_g):
    m, d = q.shape
    nq = SEQ // ATT_TQ
    gate_col0 = 3 * D_MODEL // ATT_LANES
    return pl.pallas_call(
        _fox_attn_kernel,
        grid=(BATCH, d // ATT_LANES, nq),
        in_specs=[pl.BlockSpec((ATT_TQ, ATT_LANES), lambda b, p, i: (b * nq + i, p)),
                  pl.BlockSpec((SEQ, 2 * ATT_LANES), lambda b, p, i: (b, p)),
                  pl.BlockSpec((ATT_LANES, SEQ), lambda b, p, i: (p, b)),
                  pl.BlockSpec((ATT_TQ, ATT_LANES), lambda b, p, i: (b * nq + i, gate_col0 + p)),
                  pl.BlockSpec((1, ATT_LANES), lambda b, p, i: (0, p))],
        out_specs=pl.BlockSpec((ATT_TQ, ATT_LANES), lambda b, p, i: (b * nq + i, p)),
        out_shape=jax.ShapeDtypeStruct((m, d), bf16),
        compiler_params=pltpu.CompilerParams(
            dimension_semantics=("parallel", "parallel", "arbitrary"),
            vmem_limit_bytes=VMEM_LIMIT),
        name="fox_attention",
    )(q, k_aug, v_t, proj, on_g.reshape(1, d))


def _swiglu_block(x, hn, w_gu, w_d, layer, next_norm_g, last):
    act = _matmul(hn, w_gu, layer, (0, 1), D_FF, tm=2048, tn=512, tk=D_MODEL,
                  epilogue=_epi_swiglu, out_dtype=bf16, name="swiglu_gate_up")
    return _matmul(act, w_d, layer, (0,), D_MODEL, tm=256, tn=D_MODEL, tk=D_FF, resident_w=True,
                   epilogue=_epi_residual_norm_only if last else _epi_residual_norm,
                   extras=((x, "tile"), (next_norm_g.reshape(1, D_MODEL), "row")),
                   out_dtype=f32 if last else (f32, bf16), name="swiglu_down")


def _proj(x, w, layer=0, *, name):
    if w.ndim == 2:
        w = w[None]
    n = w.shape[2]
    resident = n <= D_MODEL
    return _matmul(x, w, layer, (0,), n, tm=1024 if resident else 2048, tn=n if resident else 1024,
                   tk=x.shape[1], epilogue=_epi_plain, out_dtype=f32, name=name, resident_w=resident)


def _out_proj_norm(y, w_o, x, next_norm_g, *, name):
    return _matmul(y, w_o.astype(bf16)[None], 0, (0,), D_MODEL, tm=512, tn=D_MODEL, tk=D_MODEL,
                   epilogue=_epi_residual_norm, extras=((x, "tile"), (next_norm_g.reshape(1, D_MODEL), "row")),
                   out_dtype=(f32, bf16), name=name)


def _rwkv7_block(x, norm_g, mix, w_rkv, w0, w1, w2, a0, a1, a2, g1, g2, k_k, k_a, r_k, lnx_g, lnx_b, w_o,
                 next_norm_g):
    rank = LANES * pl.cdiv(w1.shape[1], LANES)
    downs = (_pad_cols(w1, rank).astype(bf16), _pad_cols(a1, rank).astype(bf16), g1.astype(bf16))
    ups = (_pad_rows(w2, rank).astype(bf16), _pad_rows(a2, rank).astype(bf16), g2.astype(bf16))
    xr, xk, xv, *zs = _norm_mix(x, norm_g, mix, *downs)
    w_rkv = w_rkv.astype(bf16)
    r = _proj(xr, w_rkv, 0, name="rwkv_r")
    k = _proj(xk, w_rkv, 1, name="rwkv_k")
    v = _proj(xv, w_rkv, 2, name="rwkv_v")
    yg = _wkv(r, k, v, zs, ups, w0, a0, k_k, k_a, r_k.reshape(-1), lnx_g, lnx_b)
    return _out_proj_norm(yg, w_o, x, next_norm_g, name="rwkv_out")


def _fox_block(x, hn, w_in, b_f, qn_g, kn_g, on_g, w_o, next_norm_g):
    n_main = 4 * D_MODEL
    proj = _proj(hn, w_in[:, :n_main].astype(bf16), name="fox_in")
    w_small = _pad_cols(w_in[:, n_main:], LANES).astype(bf16)
    small = _proj(hn, w_small, name="fox_in_gates")
    c_pieces, small_pieces = _fox_gate(small, b_f)
    q, k_aug, v_t = _fox_prep(proj, small_pieces, c_pieces, qn_g, kn_g)
    og = _fox_attn(q, k_aug, v_t, proj, on_g)
    return _out_proj_norm(og, w_o, x, next_norm_g, name="fox_out")


def kernel(x, a_norm_g, a_mix, a_w_rkv, a_w0, a_w1, a_w2, a_a0, a_a1, a_a2, a_g1, a_g2, a_k_k, a_k_a, a_r_k, a_lnx_g, a_lnx_b, a_w_o, b_norm_g, b_w_in, b_b_f, b_qn_g, b_kn_g, b_on_g, b_w_o, f_norm_g, f_w_gu, f_w_d, final_g):
    b, t, d = x.shape
    h = x.reshape(b * t, d)
    w_d = f_w_d.astype(bf16)
    h = _rwkv7_block(h, a_norm_g[0], a_mix[0], a_w_rkv[0], a_w0[0], a_w1[0], a_w2[0], a_a0[0], a_a1[0],
                     a_a2[0], a_g1[0], a_g2[0], a_k_k[0], a_k_a[0], a_r_k[0], a_lnx_g[0], a_lnx_b[0],
                     a_w_o[0], f_norm_g[0])
    h = _swiglu_block(*h, f_w_gu, w_d, 0, b_norm_g[0], False)
    h = _fox_block(*h, b_w_in[0], b_b_f[0], b_qn_g[0], b_kn_g[0], b_on_g[0], b_w_o[0],
                   f_norm_g[1])
    return _swiglu_block(*h, f_w_gu, w_d, 1, final_g, True).reshape(b, t, d)
```

```python
import functools

import jax
import jax.numpy as jnp
from jax import lax
from jax.experimental import pallas as pl
from jax.experimental.pallas import tpu as pltpu

D_MODEL = 2048
BATCH = 8
SEQ = 2048
N_TOK = BATCH * SEQ
HEAD_DIM = 64
HEAD_SHIFT = HEAD_DIM.bit_length() - 1
N_HEADS = D_MODEL // HEAD_DIM
D_FF = 5632
RMS_EPS = 1e-6
GN_EPS = 64e-5

LANES = 128
LANE_SHIFT = LANES.bit_length() - 1
SUBLANES = 8
HEADS_PER_VREG = LANES // HEAD_DIM
N_PAIRS = D_MODEL // LANES
VMEM_LIMIT = 56 * 1024 * 1024

NORM_MIX_DOT_ROWS = 128
WKV_CHUNK = 64
WKV_TBLK = 512
WKV_GROUPS = 2
WKV_LANES = 512
ATT_TQ = 256
ATT_TK = 512
ATT_LANES = 1024
PREP_LANES = 512
N_SPLIT = 3
LOGIT_SPLIT = 2
NEG_BIG = -1e30
LOG2_E = 1.4426950408889634

f32 = jnp.float32
bf16 = jnp.bfloat16


def _dot(a, b):
    return jnp.dot(a.astype(bf16), b.astype(bf16), preferred_element_type=f32)


def _dot_nt(a, b):
    return lax.dot_general(a.astype(bf16), b.astype(bf16), (((1,), (1,)), ((), ())),
                           preferred_element_type=f32)


def _split3(x):
    hi = x.astype(bf16)
    r1 = x - hi.astype(f32)
    mid = r1.astype(bf16)
    lo = (r1 - mid.astype(f32)).astype(bf16)
    return hi, mid, lo


def _dot_sel_rhs(x, sel):
    hi, mid, lo = _split3(x)
    d = lambda p: jnp.dot(p, sel, preferred_element_type=f32)
    return d(hi) + d(mid) + d(lo)


def _dot_sel_lhs(sel, x):
    hi, mid, lo = _split3(x)
    d = lambda p: jnp.dot(sel, p, preferred_element_type=f32)
    return d(hi) + d(mid) + d(lo)


def _head_sums(x, head_ones):
    hi = x.astype(bf16)
    lo = (x - hi.astype(f32)).astype(bf16)
    return jnp.dot(jnp.concatenate([hi, lo], axis=1), jnp.concatenate([head_ones, head_ones], axis=0),
                   preferred_element_type=f32)


def _head_sums_lanes(x, m0):
    zero = jnp.zeros_like(x)
    s0 = jnp.sum(jnp.where(m0, x, zero), axis=-1, keepdims=True)
    s1 = jnp.sum(jnp.where(m0, zero, x), axis=-1, keepdims=True)
    return jnp.where(m0, s0, s1)


def _head_ones():
    r = lax.broadcasted_iota(jnp.int32, (LANES, LANES), 0) >> HEAD_SHIFT
    c = lax.broadcasted_iota(jnp.int32, (LANES, LANES), 1) >> HEAD_SHIFT
    return (r == c).astype(bf16)


def _first_head_mask():
    return lax.broadcasted_iota(jnp.int32, (1, LANES), 1) < HEAD_DIM


def _stack_heads(x, m0):
    z = jnp.zeros_like(x)
    return jnp.concatenate([jnp.where(m0, x, z), jnp.where(m0, z, x)], axis=0)


def _softplus(z):
    return jnp.maximum(z, 0.0) + jnp.log(1.0 + jnp.exp(-jnp.abs(z)))


def _rms(x, g):
    return x * lax.rsqrt(jnp.mean(x * x, axis=-1, keepdims=True) + RMS_EPS) * g


def _norm_mix_kernel(x_ref, xp_ref, g_ref, mix_ref, w1_ref, a1_ref, g1_ref,
                     xr_ref, xk_ref, xv_ref, zw_ref, za_ref, zg_ref, xw_s, xa_s, xg_s, *, tm):
    i = pl.program_id(0)
    d = x_ref.shape[1]
    rows_per_chunk = 2 * SUBLANES
    lane_blk = 4 * LANES
    seq_start = (i * tm) % SEQ == 0
    hp_row = _rms(xp_ref[...], g_ref[...])[SUBLANES - 1:SUBLANES, :]
    hp_row = jnp.where(seq_start, jnp.zeros_like(hp_row), hp_row)
    rid = lax.broadcasted_iota(jnp.int32, (rows_per_chunk, 1), 0)
    mix_dsts = (xr_ref, xk_ref, xv_ref, xw_s, xa_s, xg_s)

    def chunk(c, last_row):
        rows = slice(c * rows_per_chunk, (c + 1) * rows_per_chunk)
        x = x_ref[rows, :]
        inv = lax.rsqrt(jnp.mean(x * x, axis=-1, keepdims=True) + RMS_EPS)
        new_last = []
        for b in range(d // lane_blk):
            ln = slice(b * lane_blk, (b + 1) * lane_blk)
            h = x[:, ln] * inv * g_ref[:, ln]
            hprev = jnp.where(rid == 0, last_row[:, ln], pltpu.roll(h, 1, 0))
            xx = hprev - h
            for p, dst in enumerate(mix_dsts):
                dst[rows, ln] = (h + xx * mix_ref[p:p + 1, ln]).astype(dst.dtype)
            new_last.append(h[rows_per_chunk - 1:rows_per_chunk, :])
        return jnp.concatenate(new_last, axis=1)

    chunks_per_blk = NORM_MIX_DOT_ROWS // rows_per_chunk
    last_row = hp_row
    for blk in range(tm // NORM_MIX_DOT_ROWS):
        for c in range(blk * chunks_per_blk, (blk + 1) * chunks_per_blk):
            last_row = chunk(c, last_row)
        rows = slice(blk * NORM_MIX_DOT_ROWS, (blk + 1) * NORM_MIX_DOT_ROWS)
        down = lambda xs, w: jnp.dot(xs[rows, :], w[...], preferred_element_type=f32)
        zw_ref[rows, :] = jnp.tanh(down(xw_s, w1_ref)).astype(zw_ref.dtype)
        za_ref[rows, :] = down(xa_s, a1_ref).astype(za_ref.dtype)
        zg_ref[rows, :] = jax.nn.sigmoid(down(xg_s, g1_ref)).astype(zg_ref.dtype)


def _norm_mix(x, g, mix, w1, a1, g1, tm=512):
    m, d = x.shape
    rb = tm // SUBLANES
    row_tile = lambda n: pl.BlockSpec((tm, n), lambda i: (i, 0))
    whole = lambda a: pl.BlockSpec(a.shape, lambda i: (0, 0))
    lows = (w1, a1, g1)
    return pl.pallas_call(
        functools.partial(_norm_mix_kernel, tm=tm),
        grid=(m // tm,),
        in_specs=[row_tile(d),
                  pl.BlockSpec((SUBLANES, d), lambda i: (jnp.maximum(i * rb - 1, 0), 0)),
                  pl.BlockSpec((1, d), lambda i: (0, 0)),
                  pl.BlockSpec((6, d), lambda i: (0, 0))] + [whole(w) for w in lows],
        out_specs=[row_tile(d)] * 3 + [row_tile(w.shape[1]) for w in lows],
        out_shape=[jax.ShapeDtypeStruct((m, d), bf16)] * 3
        + [jax.ShapeDtypeStruct((m, w.shape[1]), bf16) for w in lows],
        scratch_shapes=[pltpu.VMEM((tm, d), bf16)] * 3,
        compiler_params=pltpu.CompilerParams(dimension_semantics=("parallel",),
                                             vmem_limit_bytes=VMEM_LIMIT),
        name="norm_mix",
    )(x, x, g.reshape(1, d), mix, *lows)


def _mm_kernel(*refs, n_w, n_e, n_o, nk, epilogue):
    x_ref = refs[0]
    w_refs = refs[1:1 + n_w]
    e_refs = refs[1 + n_w:1 + n_w + n_e]
    o_refs = refs[1 + n_w + n_e:1 + n_w + n_e + n_o]
    acc_refs = refs[1 + n_w + n_e + n_o:]
    x = x_ref[...]

    def finish(accs):
        outs = epilogue(accs, [e[...] for e in e_refs])
        for o_ref, out in zip(o_refs, outs if isinstance(outs, tuple) else (outs,)):
            o_ref[...] = out.astype(o_ref.dtype)

    if nk == 1:
        finish([jnp.dot(x, w[...].astype(x.dtype), preferred_element_type=f32) for w in w_refs])
        return
    k = pl.program_id(2)

    @pl.when(k == 0)
    def _():
        for a in acc_refs:
            a[...] = jnp.zeros_like(a)

    for a, w in zip(acc_refs, w_refs):
        a[...] += jnp.dot(x, w[...], preferred_element_type=f32)

    @pl.when(k == nk - 1)
    def _():
        finish([a[...] for a in acc_refs])


def _matmul(x, w, layer, w_col_blocks, n_out, *, tm, tn, tk, epilogue, extras=(), out_dtype, name,
            resident_w=False):
    m, kdim = x.shape
    nk = kdim // tk
    n_w = len(w_col_blocks)
    nb = n_out // tn
    in_specs = [pl.BlockSpec((tm, tk), lambda i, j, k: (i, k))]
    args = [x]
    w_mode = pl.Buffered(1) if resident_w else None
    for cb in w_col_blocks:
        in_specs.append(pl.BlockSpec((None, tk, tn), lambda i, j, k, cb=cb: (layer, k, cb * nb + j),
                                     pipeline_mode=w_mode))
        args.append(w)
    for arr, kind in extras:
        if kind == "row":
            in_specs.append(pl.BlockSpec((1, tn), lambda i, j, k: (0, j)))
        else:
            in_specs.append(pl.BlockSpec((tm, tn), lambda i, j, k: (i, j)))
        args.append(arr)
    scratch = [pltpu.VMEM((tm, tn), f32) for _ in range(n_w)] if nk > 1 else []
    multi = isinstance(out_dtype, tuple)
    dtypes = out_dtype if multi else (out_dtype,)
    out_spec = pl.BlockSpec((tm, tn), lambda i, j, k: (i, j))
    outs = pl.pallas_call(
        functools.partial(_mm_kernel, n_w=n_w, n_e=len(extras), n_o=len(dtypes), nk=nk, epilogue=epilogue),
        grid=(m // tm, nb, nk),
        in_specs=in_specs,
        out_specs=[out_spec] * len(dtypes),
        out_shape=[jax.ShapeDtypeStruct((m, n_out), dt) for dt in dtypes],
        scratch_shapes=scratch,
        compiler_params=pltpu.CompilerParams(
            dimension_semantics=("parallel", "parallel", "arbitrary"),
            vmem_limit_bytes=VMEM_LIMIT),
        name=name,
    )(*args)
    return tuple(outs) if multi else outs[0]


def _epi_plain(accs, extras):
    return accs[0]


def _epi_residual_norm(accs, extras):
    y = extras[0] + accs[0]
    return y, _rms(y, extras[1])


def _epi_residual_norm_only(accs, extras):
    return _rms(extras[0] + accs[0], extras[1])


def _epi_swiglu(accs, extras):
    gate, up = accs
    return gate * jax.nn.sigmoid(gate) * up


def _log_decay(y, w0):
    return -jnp.exp(-_softplus(-(w0 + y)) - 0.5)


def _pad_cols(w, n):
    return jnp.pad(w, ((0, 0), (0, n - w.shape[1])))


def _pad_rows(w, n):
    return jnp.pad(w, ((0, n - w.shape[0]), (0, 0)))


def _each(fn, *lists):
    return [fn(*xs) for xs in zip(*lists)]


def _wkv_chunk_maps(tiles, consts):
    m0, strict, incl, eye = consts
    c = WKV_CHUNK
    n2 = 2 * c
    r, lw, lg, k2, v, av, bv = (list(x) for x in zip(*tiles))
    stack = lambda x: _stack_heads(x, m0)
    lg_end = _each(lambda x: x[c - 1:c, :], lg)
    g_inv = _each(lambda x: jnp.exp(-x), lg)
    g_rem = _each(lambda e, x: jnp.exp(e - x), lg_end, lg)
    rs = _each(lambda x, l: stack(x * jnp.exp(l)), r, lg)
    as_ = _each(lambda x, l, w: stack(x * jnp.exp(l - w)).astype(bf16), av, lg, lw)
    bs = _each(lambda x, g: stack(x * g).astype(bf16), bv, g_inv)
    ks = _each(lambda x, g: stack(x * g).astype(bf16), k2, g_inv)
    bhs = _each(lambda x, g: stack(x * g).astype(bf16), bv, g_rem)
    khs = _each(lambda x, g: stack(x * g).astype(bf16), k2, g_rem)
    vs = _each(stack, v)
    sc = _each(lambda a, rr, b, k: _dot_nt(jnp.concatenate([a, rr.astype(bf16)], axis=0),
                                           jnp.concatenate([b, k], axis=0)), as_, rs, bs, ks)
    zero = jnp.zeros((n2, n2), f32)
    yield
    a_ab = _each(lambda s: jnp.where(strict, s[:n2, :n2], zero), sc)
    a_ak = _each(lambda s: jnp.where(strict, s[:n2, n2:], zero), sc)
    a_r = _each(lambda s: jnp.where(jnp.concatenate([incl, incl], axis=1), s[n2:, :],
                                    jnp.zeros((n2, 2 * n2), f32)).astype(bf16), sc)
    side = lambda a, b: jnp.concatenate([a, b], axis=1)
    t = _each(lambda x: eye + x, a_ab)
    p = _each(lambda x: _dot(x, x), a_ab)
    yield
    for _ in range(c.bit_length() - 3):
        pt = _each(lambda pp, tt: _dot(pp, side(pp, tt)), p, t)
        yield
        p = _each(lambda x: x[:, :n2], pt)
        t = _each(lambda tt, x: tt + x[:, n2:], t, pt)
    t = _each(lambda tt, pp: (tt + _dot(pp, tt)).astype(bf16), t, p)
    yield
    akv = _each(_dot, a_ak, vs)
    yield
    hw = _each(lambda tt, a, x: _dot(tt, side(a, x)), t, as_, akv)
    yield
    ah = _each(lambda x: x[:, :LANES], hw)
    ws = _each(lambda x: x[:, LANES:], hw)
    pq = _each(lambda ar, x, vv: _dot(ar, jnp.concatenate([x, side(jnp.zeros_like(vv), vv)], axis=0)),
               a_r, hw, vs)
    yield
    pc = _each(lambda x, y: x + y[:, :LANES], rs, pq)
    qc = _each(lambda y: y[:, LANES:], pq)
    gm = _each(lambda h, b: _dot(h.T, b), ah, bhs)
    yield
    nc = _each(lambda w, x, b, k: _dot(jnp.concatenate([w, x], axis=0).T, jnp.concatenate([b, k], axis=0)),
               ws, vs, bhs, khs)
    decay = _each(jnp.exp, lg_end)
    return pc, qc, gm, nc, decay


def _drive(staged, between=()):
    pending = list(between)
    while True:
        try:
            next(staged)
        except StopIteration as done:
            for fn in pending:
                if fn is not None:
                    fn()
            return done.value
        if pending:
            fn = pending.pop(0)
            if fn is not None:
                fn()


def _wkv_kernel(r_ref, k_ref, v_ref, zw_ref, za_ref, zg_ref, w2_ref, a2_ref, g2_ref, w0_ref, a0_ref,
                kk_ref, ka_ref, rk_ref, lng_ref, lnb_ref, o_ref, s_ref):
    c = WKV_CHUNK
    n2 = 2 * c
    up = lambda z_ref, w_ref: jnp.dot(z_ref[...], w_ref[...], preferred_element_type=f32)
    lw_all = _log_decay(up(zw_ref, w2_ref), w0_ref[...])
    a_all = jax.nn.sigmoid(a0_ref[...] + up(za_ref, a2_ref))
    g_all = up(zg_ref, g2_ref)

    @pl.when(pl.program_id(2) == 0)
    def _():
        s_ref[...] = jnp.zeros_like(s_ref)

    ri = lax.broadcasted_iota(jnp.int32, (n2, n2), 0)
    ci = lax.broadcasted_iota(jnp.int32, (n2, n2), 1)
    tr = lax.broadcasted_iota(jnp.int32, (c, c), 0)
    tc = lax.broadcasted_iota(jnp.int32, (c, c), 1)
    consts = (_first_head_mask(), (ri & (c - 1)) > (ci & (c - 1)), (ri & (c - 1)) >= (ci & (c - 1)),
              (ri == ci).astype(f32))
    tri = (tr >= tc).astype(bf16)
    m0 = consts[0]
    inv_n = 1.0 / HEAD_DIM

    npairs = WKV_LANES // LANES
    nchunk = WKV_TBLK // c
    lanes = [slice(p * LANES, (p + 1) * LANES) for p in range(npairs)]
    r = [r_ref[:, ln] for ln in lanes]
    v = [v_ref[:, ln] for ln in lanes]
    k2, av, bv, bonus = [], [], [], []
    for p, ln in enumerate(lanes):
        k, a = k_ref[:, ln], a_all[:, ln]
        kk = k * kk_ref[:, ln]
        ss = _head_sums_lanes(kk * kk, m0)
        kk = kk / jnp.maximum(jnp.sqrt(ss), 1e-12)
        k2.append(k * (1.0 + (a - 1.0) * ka_ref[:, ln]))
        av.append(-kk)
        bv.append(kk * a)
        bonus.append(_head_sums_lanes(r[p] * k2[p] * rk_ref[:, ln], m0))
    tiles = []
    for ic in range(nchunk):
        rows = slice(ic * c, (ic + 1) * c)
        lw = lw_all[rows, :]
        lg = _dot_sel_lhs(tri, lw)
        for p, ln in enumerate(lanes):
            tiles.append((r[p][rows], lw[:, ln], lg[:, ln], k2[p][rows], v[p][rows], av[p][rows], bv[p][rows]))
    s = [s_ref[p] for p in range(npairs)]
    ys = [[] for _ in range(npairs)]

    def advance(maps, j):
        pc, qc, gm, nc, decay = maps
        for p in range(npairs):
            i = j * npairs + p
            y = _dot_nt(pc[i], s[p]) + qc[i]
            ys[p].append(y[:c] + y[c:])
            s[p] = s[p] * decay[i] + _dot(s[p], gm[i]) + nc[i]

    per_group = nchunk // WKV_GROUPS
    slots = []
    for gi in range(WKV_GROUPS):
        maps = _drive(_wkv_chunk_maps(tiles[gi * per_group * npairs:(gi + 1) * per_group * npairs], consts), slots)
        slots = []
        for j in range(per_group):
            slots += [functools.partial(advance, maps, j), None, None]
    _drive(iter(()), slots)
    for p, ln in enumerate(lanes):
        s_ref[p] = s[p]
        y = jnp.concatenate(ys[p], axis=0)
        mu = _head_sums_lanes(y, m0) * inv_n
        yc = y - mu
        var = _head_sums_lanes(yc * yc, m0) * inv_n
        yn = yc * lax.rsqrt(var + GN_EPS) * lng_ref[:, ln] + lnb_ref[:, ln]
        o_ref[:, ln] = ((yn + bonus[p] * v[p]) * g_all[:, ln]).astype(o_ref.dtype)


def _wkv(r, k, v, zs, ups, w0, a0, k_k, k_a, r_k, lnx_g, lnx_b):
    m, d = r.shape
    tb = SEQ // WKV_TBLK
    tile = pl.BlockSpec((WKV_TBLK, WKV_LANES), lambda b, j, t: (b * tb + t, j))
    row = pl.BlockSpec((1, WKV_LANES), lambda b, j, t: (0, j))
    z_tile = lambda z: pl.BlockSpec((WKV_TBLK, z.shape[1]), lambda b, j, t: (b * tb + t, 0))
    up_tile = lambda w: pl.BlockSpec((w.shape[0], WKV_LANES), lambda b, j, t: (0, j))
    rows = [x.reshape(1, d) for x in (w0, a0, k_k, k_a, r_k, lnx_g, lnx_b)]
    return pl.pallas_call(
        _wkv_kernel,
        grid=(BATCH, d // WKV_LANES, tb),
        in_specs=[tile] * 3 + [z_tile(z) for z in zs] + [up_tile(w) for w in ups] + [row] * len(rows),
        out_specs=tile,
        out_shape=jax.ShapeDtypeStruct((m, d), bf16),
        scratch_shapes=[pltpu.VMEM((WKV_LANES // LANES, LANES, LANES), f32)],
        compiler_params=pltpu.CompilerParams(
            dimension_semantics=("parallel", "parallel", "arbitrary"),
            vmem_limit_bytes=VMEM_LIMIT),
        name="wkv7",
    )(r, k, v, *zs, *ups, *rows)


def _fox_gate_kernel(s_ref, bf_ref, cp_ref, sp_ref, run_ref, *, tc):
    @pl.when(pl.program_id(1) == 0)
    def _():
        run_ref[...] = jnp.zeros_like(run_ref)

    small = s_ref[...]
    ls = -_softplus(-(small + bf_ref[...]))
    tr = lax.broadcasted_iota(jnp.int32, (tc, tc), 0)
    tcc = lax.broadcasted_iota(jnp.int32, (tc, tc), 1)
    cs = _dot_sel_lhs((tr >= tcc).astype(bf16), ls) + run_ref[...]
    run_ref[...] = cs[tc - 1:tc, :]
    w = cs.shape[1]
    src = lax.broadcasted_iota(jnp.int32, (w, w), 0)
    dst = lax.broadcasted_iota(jnp.int32, (w, w), 1)
    sel = [((dst == N_SPLIT * src + n) & (src < N_HEADS)).astype(bf16) for n in range(N_SPLIT)]
    cp_ref[...] = jnp.dot(jnp.concatenate(_split3(cs * LOG2_E), axis=1), jnp.concatenate(sel, axis=0),
                          preferred_element_type=f32).astype(cp_ref.dtype)
    sp_ref[...] = jnp.concatenate(_split3(small)[:LOGIT_SPLIT], axis=1)


def _fox_gate(small, b_f, tc=512):
    m, w = small.shape
    nt = SEQ // tc
    bias = jnp.pad(b_f, (0, w - b_f.shape[0])).reshape(1, w)
    pieces = lambda n: pl.BlockSpec((tc, n * w), lambda b, t: (b * nt + t, 0))
    return pl.pallas_call(
        functools.partial(_fox_gate_kernel, tc=tc),
        grid=(BATCH, nt),
        in_specs=[pl.BlockSpec((tc, w), lambda b, t: (b * nt + t, 0)),
                  pl.BlockSpec((1, w), lambda b, t: (0, 0))],
        out_specs=[pieces(1), pieces(LOGIT_SPLIT)],
        out_shape=[jax.ShapeDtypeStruct((m, n * w), bf16) for n in (1, LOGIT_SPLIT)],
        scratch_shapes=[pltpu.VMEM((1, w), f32)],
        compiler_params=pltpu.CompilerParams(dimension_semantics=("parallel", "arbitrary"),
                                             vmem_limit_bytes=VMEM_LIMIT),
        name="fox_gate_cumsum",
    )(small, bias)


def _fox_prep_kernel(q_ref, k_ref, v_ref, kp_ref, vp_ref, s_ref, c_ref, qg_ref, kg_ref,
                     qo_ref, ko_ref, vo_ref, *, tm):
    i = pl.program_id(0)
    npairs = PREP_LANES // LANES
    head_ones = _head_ones()
    seq_start = (i * tm) % SEQ == 0
    rid = lax.broadcasted_iota(jnp.int32, (tm, 1), 0)
    r = lax.broadcasted_iota(jnp.int32, (LANES, 2 * LANES), 0)
    col = lax.broadcasted_iota(jnp.int32, (LANES, 2 * LANES), 1)
    small_pieces = s_ref[...]
    aug = c_ref[...]

    def shifted(x, prow):
        prow = jnp.where(seq_start, jnp.zeros_like(prow), prow)
        return jnp.where(rid == 0, prow, pltpu.roll(x, 1, 0))

    def head_rms(x, gain):
        ms = _head_sums(x * x, head_ones) * (1.0 / HEAD_DIM)
        return x * lax.rsqrt(ms + RMS_EPS) * gain

    for p in range(npairs):
        pair = pl.program_id(1) * npairs + p
        ln = slice(p * LANES, (p + 1) * LANES)
        src = (((col & (LANES - 1)) >> HEAD_SHIFT) + HEADS_PER_VREG * pair
               + N_HEADS * (1 + (col >> LANE_SHIFT)))
        sel = (r == src).astype(bf16)
        logits = jnp.dot(small_pieces, jnp.concatenate([sel] * LOGIT_SPLIT, axis=0), preferred_element_type=f32)
        ak = jax.nn.sigmoid(logits[:, :LANES])
        av = jax.nn.sigmoid(logits[:, LANES:])
        k = k_ref[:, ln]
        v = v_ref[:, ln]
        k = ak * shifted(k, kp_ref[SUBLANES - 1:SUBLANES, ln]) + (1.0 - ak) * k
        v = av * shifted(v, vp_ref[SUBLANES - 1:SUBLANES, ln]) + (1.0 - av) * v
        qo_ref[:, ln] = (head_rms(q_ref[:, ln], qg_ref[...])
                         * (HEAD_DIM ** -0.5 * LOG2_E)).astype(qo_ref.dtype)
        ko_ref[:, 2 * p * LANES:2 * (p + 1) * LANES] = jnp.concatenate(
            [head_rms(k, kg_ref[...]).astype(ko_ref.dtype), aug], axis=1)
        vo_ref[ln, :] = v.T.astype(vo_ref.dtype)


def _fox_prep(proj, small_pieces, c_pieces, qn_g, kn_g, tm=1024):
    m = proj.shape[0]
    rb = tm // SUBLANES
    nb = D_MODEL // PREP_LANES
    tile = lambda cb: pl.BlockSpec((tm, PREP_LANES), lambda i, j, cb=cb: (i, cb * nb + j))
    prev = lambda cb: pl.BlockSpec(
        (SUBLANES, PREP_LANES), lambda i, j, cb=cb: (jnp.maximum(i * rb - 1, 0), cb * nb + j))
    gain = pl.BlockSpec((1, LANES), lambda i, j: (0, 0))
    small_tile = lambda a: pl.BlockSpec((tm, a.shape[1]), lambda i, j: (i, 0))
    tile_gain = lambda x: jnp.tile(x, HEADS_PER_VREG).reshape(1, LANES)
    return pl.pallas_call(
        functools.partial(_fox_prep_kernel, tm=tm),
        grid=(m // tm, nb),
        in_specs=[tile(0), tile(1), tile(2), prev(1), prev(2), small_tile(small_pieces), small_tile(c_pieces),
                  gain, gain],
        out_specs=[pl.BlockSpec((tm, PREP_LANES), lambda i, j: (i, j)),
                   pl.BlockSpec((tm, 2 * PREP_LANES), lambda i, j: (i, j)),
                   pl.BlockSpec((PREP_LANES, tm), lambda i, j: (j, i))],
        out_shape=[jax.ShapeDtypeStruct((m, D_MODEL), bf16),
                   jax.ShapeDtypeStruct((m, 2 * D_MODEL), bf16),
                   jax.ShapeDtypeStruct((D_MODEL, m), bf16)],
        compiler_params=pltpu.CompilerParams(dimension_semantics=("parallel", "parallel"),
                                             vmem_limit_bytes=VMEM_LIMIT),
        name="fox_prep",
    )(proj, proj, proj, proj, proj, small_pieces, c_pieces, tile_gain(qn_g), tile_gain(kn_g))


def _fox_attn_kernel(q_ref, k_ref, vt_ref, gate_ref, og_ref, o_ref):
    tq, tk = ATT_TQ, ATT_TK
    qi = pl.program_id(2)
    npairs = ATT_LANES // LANES
    chains = [(p, h) for p in range(npairs) for h in range(HEADS_PER_VREG)]
    lane = lax.broadcasted_iota(jnp.int32, (tq, LANES), 1)
    q_aug = []
    for p, h in chains:
        q = q_ref[:, p * LANES:(p + 1) * LANES]
        own = (lane >> HEAD_SHIFT) == h
        head = (pl.program_id(1) * npairs + p) * HEADS_PER_VREG + h
        minus_one = (lane >= N_SPLIT * head) & (lane < N_SPLIT * (head + 1))
        q_aug.append(jnp.concatenate([jnp.where(own, q, jnp.zeros_like(q)),
                                      jnp.where(minus_one, -1.0, 0.0).astype(bf16)], axis=1))

    def step(key0, width, carry, diag):
        m_run, l_run, acc = carry
        keys = pl.ds(pl.multiple_of(key0, tq), width)
        s = [_dot_nt(k_ref[keys, 2 * p * LANES:2 * (p + 1) * LANES], qa)
             for (p, h), qa in zip(chains, q_aug)]
        if diag:
            kidx = key0 + lax.broadcasted_iota(jnp.int32, (width, tq), 0)
            qidx = qi * tq + lax.broadcasted_iota(jnp.int32, (width, tq), 1)
            s = [jnp.where(qidx >= kidx, x, NEG_BIG) for x in s]
        m_new = [jnp.maximum(mr, jnp.max(x, axis=0, keepdims=True)) for mr, x in zip(m_run, s)]
        alpha = [jnp.exp2(mr - mn) for mr, mn in zip(m_run, m_new)]
        pr = [jnp.exp2(x - mn) for x, mn in zip(s, m_new)]
        l_new = [a * lr + jnp.sum(x, axis=0, keepdims=True) for a, lr, x in zip(alpha, l_run, pr)]
        pv = [jnp.dot(vt_ref[pl.ds((p * HEADS_PER_VREG + h) * HEAD_DIM, HEAD_DIM), keys], x.astype(bf16),
                      preferred_element_type=f32) for (p, h), x in zip(chains, pr)]
        acc = [ac * a + x for ac, a, x in zip(acc, alpha, pv)]
        return m_new, l_new, acc

    n = len(chains)
    init = ([jnp.full((1, tq), NEG_BIG, f32)] * n, [jnp.zeros((1, tq), f32)] * n,
            [jnp.zeros((HEAD_DIM, tq), f32)] * n)
    n_full = (qi * tq) // tk
    carry = lax.fori_loop(0, n_full, lambda j, cy: step(j * tk, tk, cy, False), init)
    _, l_run, acc = lax.cond((qi * tq) % tk == 0,
                             lambda cy: step(n_full * tk, tq, cy, True),
                             lambda cy: step(n_full * tk, tk, cy, True), carry)
    o_t = []
    for ac, lr in zip(acc, l_run):
        o = ac / lr
        o_t.append(o * lax.rsqrt(jnp.mean(o * o, axis=0, keepdims=True) + RMS_EPS))
    for p in range(npairs):
        ln = slice(p * LANES, (p + 1) * LANES)
        o = jnp.concatenate(o_t[HEADS_PER_VREG * p:HEADS_PER_VREG * (p + 1)], axis=0).T
        o_ref[:, ln] = (o * og_ref[:, ln] * jax.nn.sigmoid(gate_ref[:, ln])).astype(o_ref.dtype)


def _fox_attn(q, k_aug, v_t, proj, on_g):
    m, d = q.shape
    nq = SEQ // ATT_TQ
    gate_col0 = 3 * D_MODEL // ATT_LANES
    return pl.pallas_call(
        _fox_attn_kernel,
        grid=(BATCH, d // ATT_LANES, nq),
        in_specs=[pl.BlockSpec((ATT_TQ, ATT_LANES), lambda b, p, i: (b * nq + i, p)),
                  pl.BlockSpec((SEQ, 2 * ATT_LANES), lambda b, p, i: (b, p)),
                  pl.BlockSpec((ATT_LANES, SEQ), lambda b, p, i: (p, b)),
                  pl.BlockSpec((ATT_TQ, ATT_LANES), lambda b, p, i: (b * nq + i, gate_col0 + p)),
                  pl.BlockSpec((1, ATT_LANES), lambda b, p, i: (0, p))],
        out_specs=pl.BlockSpec((ATT_TQ, ATT_LANES), lambda b, p, i: (b * nq + i, p)),
        out_shape=jax.ShapeDtypeStruct((m, d), bf16),
        compiler_params=pltpu.CompilerParams(
            dimension_semantics=("parallel", "parallel", "arbitrary"),
            vmem_limit_bytes=VMEM_LIMIT),
        name="fox_attention",
    )(q, k_aug, v_t, proj, on_g.reshape(1, d))


def _swiglu_block(x, hn, w_gu, w_d, layer, next_norm_g, last):
    act = _matmul(hn, w_gu, layer, (0, 1), D_FF, tm=2048, tn=512, tk=D_MODEL,
                  epilogue=_epi_swiglu, out_dtype=bf16, name="swiglu_gate_up")
    return _matmul(act, w_d, layer, (0,), D_MODEL, tm=256, tn=D_MODEL, tk=D_FF, resident_w=True,
                   epilogue=_epi_residual_norm_only if last else _epi_residual_norm,
                   extras=((x, "tile"), (next_norm_g.reshape(1, D_MODEL), "row")),
                   out_dtype=f32 if last else (f32, bf16), name="swiglu_down")


def _proj(x, w, layer=0, *, name):
    if w.ndim == 2:
        w = w[None]
    n = w.shape[2]
    resident = n <= D_MODEL
    return _matmul(x, w, layer, (0,), n, tm=1024 if resident else 2048, tn=n if resident else 1024,
                   tk=x.shape[1], epilogue=_epi_plain, out_dtype=f32, name=name, resident_w=resident)


def _out_proj_norm(y, w_o, x, next_norm_g, *, name):
    return _matmul(y, w_o.astype(bf16)[None], 0, (0,), D_MODEL, tm=512, tn=D_MODEL, tk=D_MODEL,
                   epilogue=_epi_residual_norm, extras=((x, "tile"), (next_norm_g.reshape(1, D_MODEL), "row")),
                   out_dtype=(f32, bf16), name=name)


def _rwkv7_block(x, norm_g, mix, w_rkv, w0, w1, w2, a0, a1, a2, g1, g2, k_k, k_a, r_k, lnx_g, lnx_b, w_o,
                 next_norm_g):
    rank = LANES * pl.cdiv(w1.shape[1], LANES)
    downs = (_pad_cols(w1, rank).astype(bf16), _pad_cols(a1, rank).astype(bf16), g1.astype(bf16))
    ups = (_pad_rows(w2, rank).astype(bf16), _pad_rows(a2, rank).astype(bf16), g2.astype(bf16))
    xr, xk, xv, *zs = _norm_mix(x, norm_g, mix, *downs)
    w_rkv = w_rkv.astype(bf16)
    r = _proj(xr, w_rkv, 0, name="rwkv_r")
    k = _proj(xk, w_rkv, 1, name="rwkv_k")
    v = _proj(xv, w_rkv, 2, name="rwkv_v")
    yg = _wkv(r, k, v, zs, ups, w0, a0, k_k, k_a, r_k.reshape(-1), lnx_g, lnx_b)
    return _out_proj_norm(yg, w_o, x, next_norm_g, name="rwkv_out")


def _fox_block(x, hn, w_in, b_f, qn_g, kn_g, on_g, w_o, next_norm_g):
    n_main = 4 * D_MODEL
    proj = _proj(hn, w_in[:, :n_main].astype(bf16), name="fox_in")
    w_small = _pad_cols(w_in[:, n_main:], LANES).astype(bf16)
    small = _proj(hn, w_small, name="fox_in_gates")
    c_pieces, small_pieces = _fox_gate(small, b_f)
    q, k_aug, v_t = _fox_prep(proj, small_pieces, c_pieces, qn_g, kn_g)
    og = _fox_attn(q, k_aug, v_t, proj, on_g)
    return _out_proj_norm(og, w_o, x, next_norm_g, name="fox_out")


def kernel(x, a_norm_g, a_mix, a_w_rkv, a_w0, a_w1, a_w2, a_a0, a_a1, a_a2, a_g1, a_g2, a_k_k, a_k_a, a_r_k, a_lnx_g, a_lnx_b, a_w_o, b_norm_g, b_w_in, b_b_f, b_qn_g, b_kn_g, b_on_g, b_w_o, f_norm_g, f_w_gu, f_w_d, final_g):
    b, t, d = x.shape
    h = x.reshape(b * t, d)
    w_d = f_w_d.astype(bf16)
    h = _rwkv7_block(h, a_norm_g[0], a_mix[0], a_w_rkv[0], a_w0[0], a_w1[0], a_w2[0], a_a0[0], a_a1[0],
                     a_a2[0], a_g1[0], a_g2[0], a_k_k[0], a_k_a[0], a_r_k[0], a_lnx_g[0], a_lnx_b[0],
                     a_w_o[0], f_norm_g[0])
    h = _swiglu_block(*h, f_w_gu, w_d, 0, b_norm_g[0], False)
    h = _fox_block(*h, b_w_in[0], b_b_f[0], b_qn_g[0], b_kn_g[0], b_on_g[0], b_w_o[0],
                   f_norm_g[1])
    return _swiglu_block(*h, f_w_gu, w_d, 1, final_g, True).reshape(b, t, d)
```

```python
import functools

import jax
import jax.numpy as jnp
from jax import lax
from jax.experimental import pallas as pl
from jax.experimental.pallas import tpu as pltpu

D_MODEL = 2048
BATCH = 8
SEQ = 2048
N_TOK = BATCH * SEQ
HEAD_DIM = 64
HEAD_SHIFT = HEAD_DIM.bit_length() - 1
N_HEADS = D_MODEL // HEAD_DIM
D_FF = 5632
RMS_EPS = 1e-6
GN_EPS = 64e-5

LANES = 128
LANE_SHIFT = LANES.bit_length() - 1
SUBLANES = 8
HEADS_PER_VREG = LANES // HEAD_DIM
N_PAIRS = D_MODEL // LANES
VMEM_LIMIT = 56 * 1024 * 1024

NORM_MIX_DOT_ROWS = 128
WKV_CHUNK = 64
WKV_TBLK = 512
WKV_GROUPS = 2
WKV_LANES = 512
ATT_TQ = 256
ATT_TK = 512
ATT_LANES = 1024
PREP_LANES = 512
N_SPLIT = 3
LOGIT_SPLIT = 2
NEG_BIG = -1e30
LOG2_E = 1.4426950408889634
EXP_NEG_HALF = 0.6065306597126334

f32 = jnp.float32
bf16 = jnp.bfloat16


def _dot(a, b):
    return jnp.dot(a.astype(bf16), b.astype(bf16), preferred_element_type=f32)


def _dot_nt(a, b):
    return lax.dot_general(a.astype(bf16), b.astype(bf16), (((1,), (1,)), ((), ())),
                           preferred_element_type=f32)


def _split3(x):
    hi = x.astype(bf16)
    r1 = x - hi.astype(f32)
    mid = r1.astype(bf16)
    lo = (r1 - mid.astype(f32)).astype(bf16)
    return hi, mid, lo


def _dot_sel_rhs(x, sel):
    hi, mid, lo = _split3(x)
    d = lambda p: jnp.dot(p, sel, preferred_element_type=f32)
    return d(hi) + d(mid) + d(lo)


def _dot_sel_lhs(sel, x):
    hi, mid, lo = _split3(x)
    d = lambda p: jnp.dot(sel, p, preferred_element_type=f32)
    return d(hi) + d(mid) + d(lo)


def _head_sums(x, head_ones):
    hi = x.astype(bf16)
    lo = (x - hi.astype(f32)).astype(bf16)
    return jnp.dot(jnp.concatenate([hi, lo], axis=1), jnp.concatenate([head_ones, head_ones], axis=0),
                   preferred_element_type=f32)


def _head_sums_lanes(x, m0):
    zero = jnp.zeros_like(x)
    s0 = jnp.sum(jnp.where(m0, x, zero), axis=-1, keepdims=True)
    s1 = jnp.sum(jnp.where(m0, zero, x), axis=-1, keepdims=True)
    return jnp.where(m0, s0, s1)


def _head_ones():
    r = lax.broadcasted_iota(jnp.int32, (LANES, LANES), 0) >> HEAD_SHIFT
    c = lax.broadcasted_iota(jnp.int32, (LANES, LANES), 1) >> HEAD_SHIFT
    return (r == c).astype(bf16)


def _first_head_mask():
    return lax.broadcasted_iota(jnp.int32, (1, LANES), 1) < HEAD_DIM


def _stack_heads(x, m0):
    z = jnp.zeros_like(x)
    return jnp.concatenate([jnp.where(m0, x, z), jnp.where(m0, z, x)], axis=0)


def _softplus(z):
    return jnp.maximum(z, 0.0) + jnp.log(1.0 + jnp.exp(-jnp.abs(z)))


def _rms(x, g):
    return x * lax.rsqrt(jnp.mean(x * x, axis=-1, keepdims=True) + RMS_EPS) * g


def _norm_mix_kernel(x_ref, xp_ref, g_ref, mix_ref, w1_ref, a1_ref, g1_ref,
                     xr_ref, xk_ref, xv_ref, zw_ref, za_ref, zg_ref, xw_s, xa_s, xg_s, *, tm):
    i = pl.program_id(0)
    d = x_ref.shape[1]
    rows_per_chunk = 2 * SUBLANES
    lane_blk = 4 * LANES
    seq_start = (i * tm) % SEQ == 0
    hp_row = _rms(xp_ref[...], g_ref[...])[SUBLANES - 1:SUBLANES, :]
    hp_row = jnp.where(seq_start, jnp.zeros_like(hp_row), hp_row)
    rid = lax.broadcasted_iota(jnp.int32, (rows_per_chunk, 1), 0)
    mix_dsts = (xr_ref, xk_ref, xv_ref, xw_s, xa_s, xg_s)

    def chunk(c, last_row):
        rows = slice(c * rows_per_chunk, (c + 1) * rows_per_chunk)
        x = x_ref[rows, :]
        inv = lax.rsqrt(jnp.mean(x * x, axis=-1, keepdims=True) + RMS_EPS)
        new_last = []
        for b in range(d // lane_blk):
            ln = slice(b * lane_blk, (b + 1) * lane_blk)
            h = x[:, ln] * inv * g_ref[:, ln]
            hprev = jnp.where(rid == 0, last_row[:, ln], pltpu.roll(h, 1, 0))
            xx = hprev - h
            for p, dst in enumerate(mix_dsts):
                dst[rows, ln] = (h + xx * mix_ref[p:p + 1, ln]).astype(dst.dtype)
            new_last.append(h[rows_per_chunk - 1:rows_per_chunk, :])
        return jnp.concatenate(new_last, axis=1)

    chunks_per_blk = NORM_MIX_DOT_ROWS // rows_per_chunk
    last_row = hp_row
    for blk in range(tm // NORM_MIX_DOT_ROWS):
        for c in range(blk * chunks_per_blk, (blk + 1) * chunks_per_blk):
            last_row = chunk(c, last_row)
        rows = slice(blk * NORM_MIX_DOT_ROWS, (blk + 1) * NORM_MIX_DOT_ROWS)
        down = lambda xs, w: jnp.dot(xs[rows, :], w[...], preferred_element_type=f32)
        zw_ref[rows, :] = jnp.tanh(down(xw_s, w1_ref)).astype(zw_ref.dtype)
        za_ref[rows, :] = down(xa_s, a1_ref).astype(za_ref.dtype)
        zg_ref[rows, :] = jax.nn.sigmoid(down(xg_s, g1_ref)).astype(zg_ref.dtype)


def _norm_mix(x, g, mix, w1, a1, g1, tm=512):
    m, d = x.shape
    rb = tm // SUBLANES
    row_tile = lambda n: pl.BlockSpec((tm, n), lambda i: (i, 0))
    whole = lambda a: pl.BlockSpec(a.shape, lambda i: (0, 0))
    lows = (w1, a1, g1)
    return pl.pallas_call(
        functools.partial(_norm_mix_kernel, tm=tm),
        grid=(m // tm,),
        in_specs=[row_tile(d),
                  pl.BlockSpec((SUBLANES, d), lambda i: (jnp.maximum(i * rb - 1, 0), 0)),
                  pl.BlockSpec((1, d), lambda i: (0, 0)),
                  pl.BlockSpec((6, d), lambda i: (0, 0))] + [whole(w) for w in lows],
        out_specs=[row_tile(d)] * 3 + [row_tile(w.shape[1]) for w in lows],
        out_shape=[jax.ShapeDtypeStruct((m, d), bf16)] * 3
        + [jax.ShapeDtypeStruct((m, w.shape[1]), bf16) for w in lows],
        scratch_shapes=[pltpu.VMEM((tm, d), bf16)] * 3,
        compiler_params=pltpu.CompilerParams(dimension_semantics=("parallel",),
                                             vmem_limit_bytes=VMEM_LIMIT),
        name="norm_mix",
    )(x, x, g.reshape(1, d), mix, *lows)


def _mm_kernel(*refs, n_w, n_e, n_o, nk, epilogue):
    x_ref = refs[0]
    w_refs = refs[1:1 + n_w]
    e_refs = refs[1 + n_w:1 + n_w + n_e]
    o_refs = refs[1 + n_w + n_e:1 + n_w + n_e + n_o]
    acc_refs = refs[1 + n_w + n_e + n_o:]
    x = x_ref[...]

    def finish(accs):
        outs = epilogue(accs, [e[...] for e in e_refs])
        for o_ref, out in zip(o_refs, outs if isinstance(outs, tuple) else (outs,)):
            o_ref[...] = out.astype(o_ref.dtype)

    if nk == 1:
        finish([jnp.dot(x, w[...].astype(x.dtype), preferred_element_type=f32) for w in w_refs])
        return
    k = pl.program_id(2)

    @pl.when(k == 0)
    def _():
        for a in acc_refs:
            a[...] = jnp.zeros_like(a)

    for a, w in zip(acc_refs, w_refs):
        a[...] += jnp.dot(x, w[...], preferred_element_type=f32)

    @pl.when(k == nk - 1)
    def _():
        finish([a[...] for a in acc_refs])


def _matmul(x, w, layer, w_col_blocks, n_out, *, tm, tn, tk, epilogue, extras=(), out_dtype, name,
            resident_w=False):
    m, kdim = x.shape
    nk = kdim // tk
    n_w = len(w_col_blocks)
    nb = n_out // tn
    in_specs = [pl.BlockSpec((tm, tk), lambda i, j, k: (i, k))]
    args = [x]
    w_mode = pl.Buffered(1) if resident_w else None
    for cb in w_col_blocks:
        in_specs.append(pl.BlockSpec((None, tk, tn), lambda i, j, k, cb=cb: (layer, k, cb * nb + j),
                                     pipeline_mode=w_mode))
        args.append(w)
    for arr, kind in extras:
        if kind == "row":
            in_specs.append(pl.BlockSpec((1, tn), lambda i, j, k: (0, j)))
        else:
            in_specs.append(pl.BlockSpec((tm, tn), lambda i, j, k: (i, j)))
        args.append(arr)
    scratch = [pltpu.VMEM((tm, tn), f32) for _ in range(n_w)] if nk > 1 else []
    multi = isinstance(out_dtype, tuple)
    dtypes = out_dtype if multi else (out_dtype,)
    out_spec = pl.BlockSpec((tm, tn), lambda i, j, k: (i, j))
    outs = pl.pallas_call(
        functools.partial(_mm_kernel, n_w=n_w, n_e=len(extras), n_o=len(dtypes), nk=nk, epilogue=epilogue),
        grid=(m // tm, nb, nk),
        in_specs=in_specs,
        out_specs=[out_spec] * len(dtypes),
        out_shape=[jax.ShapeDtypeStruct((m, n_out), dt) for dt in dtypes],
        scratch_shapes=scratch,
        compiler_params=pltpu.CompilerParams(
            dimension_semantics=("parallel", "parallel", "arbitrary"),
            vmem_limit_bytes=VMEM_LIMIT),
        name=name,
    )(*args)
    return tuple(outs) if multi else outs[0]


def _epi_plain(accs, extras):
    return accs[0]


def _epi_residual_norm(accs, extras):
    y = extras[0] + accs[0]
    return y, _rms(y, extras[1])


def _epi_residual_norm_only(accs, extras):
    return _rms(extras[0] + accs[0], extras[1])


def _epi_swiglu(accs, extras):
    gate, up = accs
    return gate * jax.nn.sigmoid(gate) * up


def _log_decay(y, w0):
    return -EXP_NEG_HALF * jax.nn.sigmoid(w0 + y)


def _pad_cols(w, n):
    return jnp.pad(w, ((0, 0), (0, n - w.shape[1])))


def _pad_rows(w, n):
    return jnp.pad(w, ((0, n - w.shape[0]), (0, 0)))


def _each(fn, *lists):
    return [fn(*xs) for xs in zip(*lists)]


def _wkv_chunk_maps(tiles, consts):
    m0, strict, incl, eye = consts
    c = WKV_CHUNK
    n2 = 2 * c
    r, lw, lg, k2, v, av, bv = (list(x) for x in zip(*tiles))
    stack = lambda x: _stack_heads(x, m0)
    lg_end = _each(lambda x: x[c - 1:c, :], lg)
    g_inv = _each(lambda x: jnp.exp(-x), lg)
    g_rem = _each(lambda e, x: jnp.exp(e - x), lg_end, lg)
    rs = _each(lambda x, l: stack(x * jnp.exp(l)), r, lg)
    as_ = _each(lambda x, l, w: stack(x * jnp.exp(l - w)).astype(bf16), av, lg, lw)
    bs = _each(lambda x, g: stack(x * g).astype(bf16), bv, g_inv)
    ks = _each(lambda x, g: stack(x * g).astype(bf16), k2, g_inv)
    bhs = _each(lambda x, g: stack(x * g).astype(bf16), bv, g_rem)
    khs = _each(lambda x, g: stack(x * g).astype(bf16), k2, g_rem)
    vs = _each(stack, v)
    sc = _each(lambda a, rr, b, k: _dot_nt(jnp.concatenate([a, rr.astype(bf16)], axis=0),
                                           jnp.concatenate([b, k], axis=0)), as_, rs, bs, ks)
    zero = jnp.zeros((n2, n2), f32)
    yield
    a_ab = _each(lambda s: jnp.where(strict, s[:n2, :n2], zero), sc)
    a_ak = _each(lambda s: jnp.where(strict, s[:n2, n2:], zero), sc)
    a_r = _each(lambda s: jnp.where(jnp.concatenate([incl, incl], axis=1), s[n2:, :],
                                    jnp.zeros((n2, 2 * n2), f32)).astype(bf16), sc)
    side = lambda a, b: jnp.concatenate([a, b], axis=1)
    t = _each(lambda x: eye + x, a_ab)
    p = _each(lambda x: _dot(x, x), a_ab)
    yield
    for _ in range(c.bit_length() - 3):
        pt = _each(lambda pp, tt: _dot(pp, side(pp, tt)), p, t)
        yield
        p = _each(lambda x: x[:, :n2], pt)
        t = _each(lambda tt, x: tt + x[:, n2:], t, pt)
    t = _each(lambda tt, pp: (tt + _dot(pp, tt)).astype(bf16), t, p)
    yield
    akv = _each(_dot, a_ak, vs)
    yield
    hw = _each(lambda tt, a, x: _dot(tt, side(a, x)), t, as_, akv)
    yield
    ah = _each(lambda x: x[:, :LANES], hw)
    ws = _each(lambda x: x[:, LANES:], hw)
    pq = _each(lambda ar, x, vv: _dot(ar, jnp.concatenate([x, side(jnp.zeros_like(vv), vv)], axis=0)),
               a_r, hw, vs)
    yield
    pc = _each(lambda x, y: x + y[:, :LANES], rs, pq)
    qc = _each(lambda y: y[:, LANES:], pq)
    gm = _each(lambda h, b: _dot(h.T, b), ah, bhs)
    yield
    nc = _each(lambda w, x, b, k: _dot(jnp.concatenate([w, x], axis=0).T, jnp.concatenate([b, k], axis=0)),
               ws, vs, bhs, khs)
    decay = _each(jnp.exp, lg_end)
    return pc, qc, gm, nc, decay


def _drive(staged, between=()):
    pending = list(between)
    while True:
        try:
            next(staged)
        except StopIteration as done:
            for fn in pending:
                if fn is not None:
                    fn()
            return done.value
        if pending:
            fn = pending.pop(0)
            if fn is not None:
                fn()


def _wkv_kernel(r_ref, k_ref, v_ref, zw_ref, za_ref, zg_ref, w2_ref, a2_ref, g2_ref, w0_ref, a0_ref,
                kk_ref, ka_ref, rk_ref, lng_ref, lnb_ref, o_ref, s_ref):
    c = WKV_CHUNK
    n2 = 2 * c
    up = lambda z_ref, w_ref: jnp.dot(z_ref[...], w_ref[...], preferred_element_type=f32)
    lw_all = _log_decay(up(zw_ref, w2_ref), w0_ref[...])
    a_all = jax.nn.sigmoid(a0_ref[...] + up(za_ref, a2_ref))
    g_all = up(zg_ref, g2_ref)

    @pl.when(pl.program_id(2) == 0)
    def _():
        s_ref[...] = jnp.zeros_like(s_ref)

    ri = lax.broadcasted_iota(jnp.int32, (n2, n2), 0)
    ci = lax.broadcasted_iota(jnp.int32, (n2, n2), 1)
    tr = lax.broadcasted_iota(jnp.int32, (c, c), 0)
    tc = lax.broadcasted_iota(jnp.int32, (c, c), 1)
    consts = (_first_head_mask(), (ri & (c - 1)) > (ci & (c - 1)), (ri & (c - 1)) >= (ci & (c - 1)),
              (ri == ci).astype(f32))
    tri = (tr >= tc).astype(bf16)
    m0 = consts[0]
    inv_n = 1.0 / HEAD_DIM

    npairs = WKV_LANES // LANES
    nchunk = WKV_TBLK // c
    lanes = [slice(p * LANES, (p + 1) * LANES) for p in range(npairs)]
    r = [r_ref[:, ln] for ln in lanes]
    v = [v_ref[:, ln] for ln in lanes]
    k2, av, bv, bonus = [], [], [], []
    for p, ln in enumerate(lanes):
        k, a = k_ref[:, ln], a_all[:, ln]
        kk = k * kk_ref[:, ln]
        ss = _head_sums_lanes(kk * kk, m0)
        kk = kk / jnp.maximum(jnp.sqrt(ss), 1e-12)
        k2.append(k * (1.0 + (a - 1.0) * ka_ref[:, ln]))
        av.append(-kk)
        bv.append(kk * a)
        bonus.append(_head_sums_lanes(r[p] * k2[p] * rk_ref[:, ln], m0))
    tiles = []
    for ic in range(nchunk):
        rows = slice(ic * c, (ic + 1) * c)
        lw = lw_all[rows, :]
        lg = _dot_sel_lhs(tri, lw)
        for p, ln in enumerate(lanes):
            tiles.append((r[p][rows], lw[:, ln], lg[:, ln], k2[p][rows], v[p][rows], av[p][rows], bv[p][rows]))
    s = [s_ref[p] for p in range(npairs)]
    ys = [[] for _ in range(npairs)]

    def advance(maps, j):
        pc, qc, gm, nc, decay = maps
        for p in range(npairs):
            i = j * npairs + p
            y = _dot_nt(pc[i], s[p]) + qc[i]
            ys[p].append(y[:c] + y[c:])
            s[p] = s[p] * decay[i] + _dot(s[p], gm[i]) + nc[i]

    per_group = nchunk // WKV_GROUPS
    slots = []
    for gi in range(WKV_GROUPS):
        maps = _drive(_wkv_chunk_maps(tiles[gi * per_group * npairs:(gi + 1) * per_group * npairs], consts), slots)
        slots = []
        for j in range(per_group):
            slots += [functools.partial(advance, maps, j), None, None]
    _drive(iter(()), slots)
    for p, ln in enumerate(lanes):
        s_ref[p] = s[p]
        y = jnp.concatenate(ys[p], axis=0)
        mu = _head_sums_lanes(y, m0) * inv_n
        yc = y - mu
        var = _head_sums_lanes(yc * yc, m0) * inv_n
        yn = yc * lax.rsqrt(var + GN_EPS) * lng_ref[:, ln] + lnb_ref[:, ln]
        o_ref[:, ln] = ((yn + bonus[p] * v[p]) * g_all[:, ln]).astype(o_ref.dtype)


def _wkv(r, k, v, zs, ups, w0, a0, k_k, k_a, r_k, lnx_g, lnx_b):
    m, d = r.shape
    tb = SEQ // WKV_TBLK
    tile = pl.BlockSpec((WKV_TBLK, WKV_LANES), lambda b, j, t: (b * tb + t, j))
    row = pl.BlockSpec((1, WKV_LANES), lambda b, j, t: (0, j))
    z_tile = lambda z: pl.BlockSpec((WKV_TBLK, z.shape[1]), lambda b, j, t: (b * tb + t, 0))
    up_tile = lambda w: pl.BlockSpec((w.shape[0], WKV_LANES), lambda b, j, t: (0, j))
    rows = [x.reshape(1, d) for x in (w0, a0, k_k, k_a, r_k, lnx_g, lnx_b)]
    return pl.pallas_call(
        _wkv_kernel,
        grid=(BATCH, d // WKV_LANES, tb),
        in_specs=[tile] * 3 + [z_tile(z) for z in zs] + [up_tile(w) for w in ups] + [row] * len(rows),
        out_specs=tile,
        out_shape=jax.ShapeDtypeStruct((m, d), bf16),
        scratch_shapes=[pltpu.VMEM((WKV_LANES // LANES, LANES, LANES), f32)],
        compiler_params=pltpu.CompilerParams(
            dimension_semantics=("parallel", "parallel", "arbitrary"),
            vmem_limit_bytes=VMEM_LIMIT),
        name="wkv7",
    )(r, k, v, *zs, *ups, *rows)


def _fox_gate_kernel(s_ref, bf_ref, cp_ref, sp_ref, run_ref, *, tc):
    @pl.when(pl.program_id(1) == 0)
    def _():
        run_ref[...] = jnp.zeros_like(run_ref)

    small = s_ref[...]
    ls = -_softplus(-(small + bf_ref[...]))
    tr = lax.broadcasted_iota(jnp.int32, (tc, tc), 0)
    tcc = lax.broadcasted_iota(jnp.int32, (tc, tc), 1)
    cs = _dot_sel_lhs((tr >= tcc).astype(bf16), ls) + run_ref[...]
    run_ref[...] = cs[tc - 1:tc, :]
    w = cs.shape[1]
    src = lax.broadcasted_iota(jnp.int32, (w, w), 0)
    dst = lax.broadcasted_iota(jnp.int32, (w, w), 1)
    sel = [((dst == N_SPLIT * src + n) & (src < N_HEADS)).astype(bf16) for n in range(N_SPLIT)]
    cp_ref[...] = jnp.dot(jnp.concatenate(_split3(cs * LOG2_E), axis=1), jnp.concatenate(sel, axis=0),
                          preferred_element_type=f32).astype(cp_ref.dtype)
    sp_ref[...] = jnp.concatenate(_split3(small)[:LOGIT_SPLIT], axis=1)


def _fox_gate(small, b_f, tc=512):
    m, w = small.shape
    nt = SEQ // tc
    bias = jnp.pad(b_f, (0, w - b_f.shape[0])).reshape(1, w)
    pieces = lambda n: pl.BlockSpec((tc, n * w), lambda b, t: (b * nt + t, 0))
    return pl.pallas_call(
        functools.partial(_fox_gate_kernel, tc=tc),
        grid=(BATCH, nt),
        in_specs=[pl.BlockSpec((tc, w), lambda b, t: (b * nt + t, 0)),
                  pl.BlockSpec((1, w), lambda b, t: (0, 0))],
        out_specs=[pieces(1), pieces(LOGIT_SPLIT)],
        out_shape=[jax.ShapeDtypeStruct((m, n * w), bf16) for n in (1, LOGIT_SPLIT)],
        scratch_shapes=[pltpu.VMEM((1, w), f32)],
        compiler_params=pltpu.CompilerParams(dimension_semantics=("parallel", "arbitrary"),
                                             vmem_limit_bytes=VMEM_LIMIT),
        name="fox_gate_cumsum",
    )(small, bias)


def _fox_prep_kernel(q_ref, k_ref, v_ref, kp_ref, vp_ref, s_ref, c_ref, qg_ref, kg_ref,
                     qo_ref, ko_ref, vo_ref, *, tm):
    i = pl.program_id(0)
    npairs = PREP_LANES // LANES
    head_ones = _head_ones()
    seq_start = (i * tm) % SEQ == 0
    rid = lax.broadcasted_iota(jnp.int32, (tm, 1), 0)
    r = lax.broadcasted_iota(jnp.int32, (LANES, 2 * LANES), 0)
    col = lax.broadcasted_iota(jnp.int32, (LANES, 2 * LANES), 1)
    small_pieces = s_ref[...]
    aug = c_ref[...]

    def shifted(x, prow):
        prow = jnp.where(seq_start, jnp.zeros_like(prow), prow)
        return jnp.where(rid == 0, prow, pltpu.roll(x, 1, 0))

    def head_rms(x, gain):
        ms = _head_sums(x * x, head_ones) * (1.0 / HEAD_DIM)
        return x * lax.rsqrt(ms + RMS_EPS) * gain

    for p in range(npairs):
        pair = pl.program_id(1) * npairs + p
        ln = slice(p * LANES, (p + 1) * LANES)
        src = (((col & (LANES - 1)) >> HEAD_SHIFT) + HEADS_PER_VREG * pair
               + N_HEADS * (1 + (col >> LANE_SHIFT)))
        sel = (r == src).astype(bf16)
        logits = jnp.dot(small_pieces, jnp.concatenate([sel] * LOGIT_SPLIT, axis=0), preferred_element_type=f32)
        ak = jax.nn.sigmoid(logits[:, :LANES])
        av = jax.nn.sigmoid(logits[:, LANES:])
        k = k_ref[:, ln]
        v = v_ref[:, ln]
        k = k + ak * (shifted(k, kp_ref[SUBLANES - 1:SUBLANES, ln]) - k)
        v = v + av * (shifted(v, vp_ref[SUBLANES - 1:SUBLANES, ln]) - v)
        qo_ref[:, ln] = head_rms(q_ref[:, ln], qg_ref[...] * (HEAD_DIM ** -0.5 * LOG2_E)).astype(qo_ref.dtype)
        ko_ref[:, 2 * p * LANES:2 * (p + 1) * LANES] = jnp.concatenate(
            [head_rms(k, kg_ref[...]).astype(ko_ref.dtype), aug], axis=1)
        vo_ref[ln, :] = v.T.astype(vo_ref.dtype)


def _fox_prep(proj, small_pieces, c_pieces, qn_g, kn_g, tm=1024):
    m = proj.shape[0]
    rb = tm // SUBLANES
    nb = D_MODEL // PREP_LANES
    tile = lambda cb: pl.BlockSpec((tm, PREP_LANES), lambda i, j, cb=cb: (i, cb * nb + j))
    prev = lambda cb: pl.BlockSpec(
        (SUBLANES, PREP_LANES), lambda i, j, cb=cb: (jnp.maximum(i * rb - 1, 0), cb * nb + j))
    gain = pl.BlockSpec((1, LANES), lambda i, j: (0, 0))
    small_tile = lambda a: pl.BlockSpec((tm, a.shape[1]), lambda i, j: (i, 0))
    tile_gain = lambda x: jnp.tile(x, HEADS_PER_VREG).reshape(1, LANES)
    return pl.pallas_call(
        functools.partial(_fox_prep_kernel, tm=tm),
        grid=(m // tm, nb),
        in_specs=[tile(0), tile(1), tile(2), prev(1), prev(2), small_tile(small_pieces), small_tile(c_pieces),
                  gain, gain],
        out_specs=[pl.BlockSpec((tm, PREP_LANES), lambda i, j: (i, j)),
                   pl.BlockSpec((tm, 2 * PREP_LANES), lambda i, j: (i, j)),
                   pl.BlockSpec((PREP_LANES, tm), lambda i, j: (j, i))],
        out_shape=[jax.ShapeDtypeStruct((m, D_MODEL), bf16),
                   jax.ShapeDtypeStruct((m, 2 * D_MODEL), bf16),
                   jax.ShapeDtypeStruct((D_MODEL, m), bf16)],
        compiler_params=pltpu.CompilerParams(dimension_semantics=("parallel", "parallel"),
                                             vmem_limit_bytes=VMEM_LIMIT),
        name="fox_prep",
    )(proj, proj, proj, proj, proj, small_pieces, c_pieces, tile_gain(qn_g), tile_gain(kn_g))


def _fox_attn_kernel(q_ref, k_ref, vt_ref, gate_ref, og_ref, o_ref):
    tq, tk = ATT_TQ, ATT_TK
    qi = pl.program_id(2)
    npairs = ATT_LANES // LANES
    chains = [(p, h) for p in range(npairs) for h in range(HEADS_PER_VREG)]
    lane = lax.broadcasted_iota(jnp.int32, (tq, LANES), 1)
    q_aug = []
    for p, h in chains:
        q = q_ref[:, p * LANES:(p + 1) * LANES]
        own = (lane >> HEAD_SHIFT) == h
        head = (pl.program_id(1) * npairs + p) * HEADS_PER_VREG + h
        minus_one = (lane >= N_SPLIT * head) & (lane < N_SPLIT * (head + 1))
        q_aug.append(jnp.concatenate([jnp.where(own, q, jnp.zeros_like(q)),
                                      jnp.where(minus_one, -1.0, 0.0).astype(bf16)], axis=1))

    def step(key0, width, carry, diag):
        m_run, l_run, acc = carry
        keys = pl.ds(pl.multiple_of(key0, tq), width)
        s = [_dot_nt(k_ref[keys, 2 * p * LANES:2 * (p + 1) * LANES], qa)
             for (p, h), qa in zip(chains, q_aug)]
        if diag:
            kidx = key0 + lax.broadcasted_iota(jnp.int32, (width, tq), 0)
            qidx = qi * tq + lax.broadcasted_iota(jnp.int32, (width, tq), 1)
            s = [jnp.where(qidx >= kidx, x, NEG_BIG) for x in s]
        m_new = [jnp.maximum(mr, jnp.max(x, axis=0, keepdims=True)) for mr, x in zip(m_run, s)]
        alpha = [jnp.exp2(mr - mn) for mr, mn in zip(m_run, m_new)]
        pr = [jnp.exp2(x - mn) for x, mn in zip(s, m_new)]
        l_new = [a * lr + jnp.sum(x, axis=0, keepdims=True) for a, lr, x in zip(alpha, l_run, pr)]
        pv = [jnp.dot(vt_ref[pl.ds((p * HEADS_PER_VREG + h) * HEAD_DIM, HEAD_DIM), keys], x.astype(bf16),
                      preferred_element_type=f32) for (p, h), x in zip(chains, pr)]
        acc = [ac * a + x for ac, a, x in zip(acc, alpha, pv)]
        return m_new, l_new, acc

    n = len(chains)
    init = ([jnp.full((1, tq), NEG_BIG, f32)] * n, [jnp.zeros((1, tq), f32)] * n,
            [jnp.zeros((HEAD_DIM, tq), f32)] * n)
    n_full = (qi * tq) // tk
    carry = lax.fori_loop(0, n_full, lambda j, cy: step(j * tk, tk, cy, False), init)
    _, l_run, acc = lax.cond((qi * tq) % tk == 0,
                             lambda cy: step(n_full * tk, tq, cy, True),
                             lambda cy: step(n_full * tk, tk, cy, True), carry)
    o_t = []
    for ac, lr in zip(acc, l_run):
        o = ac / lr
        o_t.append(o * lax.rsqrt(jnp.mean(o * o, axis=0, keepdims=True) + RMS_EPS))
    for p in range(npairs):
        ln = slice(p * LANES, (p + 1) * LANES)
        o = jnp.concatenate(o_t[HEADS_PER_VREG * p:HEADS_PER_VREG * (p + 1)], axis=0).T
        o_ref[:, ln] = (o * og_ref[:, ln] * jax.nn.sigmoid(gate_ref[:, ln])).astype(o_ref.dtype)


def _fox_attn(q, k_aug, v_t, proj, on_g):
    m, d = q.shape
    nq = SEQ // ATT_TQ
    gate_col0 = 3 * D_MODEL // ATT_LANES
    return pl.pallas_call(
        _fox_attn_kernel,
        grid=(BATCH, d // ATT_LANES, nq),
        in_specs=[pl.BlockSpec((ATT_TQ, ATT_LANES), lambda b, p, i: (b * nq + i, p)),
                  pl.BlockSpec((SEQ, 2 * ATT_LANES), lambda b, p, i: (b, p)),
                  pl.BlockSpec((ATT_LANES, SEQ), lambda b, p, i: (p, b)),
                  pl.BlockSpec((ATT_TQ, ATT_LANES), lambda b, p, i: (b * nq + i, gate_col0 + p)),
                  pl.BlockSpec((1, ATT_LANES), lambda b, p, i: (0, p))],
        out_specs=pl.BlockSpec((ATT_TQ, ATT_LANES), lambda b, p, i: (b * nq + i, p)),
        out_shape=jax.ShapeDtypeStruct((m, d), bf16),
        compiler_params=pltpu.CompilerParams(
            dimension_semantics=("parallel", "parallel", "arbitrary"),
            vmem_limit_bytes=VMEM_LIMIT),
        name="fox_attention",
    )(q, k_aug, v_t, proj, on_g.reshape(1, d))


def _swiglu_block(x, hn, w_gu, w_d, layer, next_norm_g, last):
    act = _matmul(hn, w_gu, layer, (0, 1), D_FF, tm=2048, tn=512, tk=D_MODEL,
                  epilogue=_epi_swiglu, out_dtype=bf16, name="swiglu_gate_up")
    return _matmul(act, w_d, layer, (0,), D_MODEL, tm=256, tn=D_MODEL, tk=D_FF, resident_w=True,
                   epilogue=_epi_residual_norm_only if last else _epi_residual_norm,
                   extras=((x, "tile"), (next_norm_g.reshape(1, D_MODEL), "row")),
                   out_dtype=f32 if last else (f32, bf16), name="swiglu_down")


def _proj(x, w, layer=0, *, name):
    if w.ndim == 2:
        w = w[None]
    n = w.shape[2]
    resident = n <= D_MODEL
    return _matmul(x, w, layer, (0,), n, tm=1024 if resident else 2048, tn=n if resident else 1024,
                   tk=x.shape[1], epilogue=_epi_plain, out_dtype=f32, name=name, resident_w=resident)


def _out_proj_norm(y, w_o, x, next_norm_g, *, name):
    return _matmul(y, w_o.astype(bf16)[None], 0, (0,), D_MODEL, tm=512, tn=D_MODEL, tk=D_MODEL,
                   epilogue=_epi_residual_norm, extras=((x, "tile"), (next_norm_g.reshape(1, D_MODEL), "row")),
                   out_dtype=(f32, bf16), name=name)


def _rwkv7_block(x, norm_g, mix, w_rkv, w0, w1, w2, a0, a1, a2, g1, g2, k_k, k_a, r_k, lnx_g, lnx_b, w_o,
                 next_norm_g):
    rank = LANES * pl.cdiv(w1.shape[1], LANES)
    downs = (_pad_cols(w1, rank).astype(bf16), _pad_cols(a1, rank).astype(bf16), g1.astype(bf16))
    ups = (_pad_rows(w2, rank).astype(bf16), _pad_rows(a2, rank).astype(bf16), g2.astype(bf16))
    xr, xk, xv, *zs = _norm_mix(x, norm_g, mix, *downs)
    w_rkv = w_rkv.astype(bf16)
    r = _proj(xr, w_rkv, 0, name="rwkv_r")
    k = _proj(xk, w_rkv, 1, name="rwkv_k")
    v = _proj(xv, w_rkv, 2, name="rwkv_v")
    yg = _wkv(r, k, v, zs, ups, w0, a0, k_k, k_a, r_k.reshape(-1), lnx_g, lnx_b)
    return _out_proj_norm(yg, w_o, x, next_norm_g, name="rwkv_out")


def _fox_block(x, hn, w_in, b_f, qn_g, kn_g, on_g, w_o, next_norm_g):
    n_main = 4 * D_MODEL
    proj = _proj(hn, w_in[:, :n_main].astype(bf16), name="fox_in")
    w_small = _pad_cols(w_in[:, n_main:], LANES).astype(bf16)
    small = _proj(hn, w_small, name="fox_in_gates")
    c_pieces, small_pieces = _fox_gate(small, b_f)
    q, k_aug, v_t = _fox_prep(proj, small_pieces, c_pieces, qn_g, kn_g)
    og = _fox_attn(q, k_aug, v_t, proj, on_g)
    return _out_proj_norm(og, w_o, x, next_norm_g, name="fox_out")


def kernel(x, a_norm_g, a_mix, a_w_rkv, a_w0, a_w1, a_w2, a_a0, a_a1, a_a2, a_g1, a_g2, a_k_k, a_k_a, a_r_k, a_lnx_g, a_lnx_b, a_w_o, b_norm_g, b_w_in, b_b_f, b_qn_g, b_kn_g, b_on_g, b_w_o, f_norm_g, f_w_gu, f_w_d, final_g):
    b, t, d = x.shape
    h = x.reshape(b * t, d)
    w_d = f_w_d.astype(bf16)
    h = _rwkv7_block(h, a_norm_g[0], a_mix[0], a_w_rkv[0], a_w0[0], a_w1[0], a_w2[0], a_a0[0], a_a1[0],
                     a_a2[0], a_g1[0], a_g2[0], a_k_k[0], a_k_a[0], a_r_k[0], a_lnx_g[0], a_lnx_b[0],
                     a_w_o[0], f_norm_g[0])
    h = _swiglu_block(*h, f_w_gu, w_d, 0, b_norm_g[0], False)
    h = _fox_block(*h, b_w_in[0], b_b_f[0], b_qn_g[0], b_kn_g[0], b_on_g[0], b_w_o[0],
                   f_norm_g[1])
    return _swiglu_block(*h, f_w_gu, w_d, 1, final_g, True).reshape(b, t, d)
```

```python
import functools

import jax
import jax.numpy as jnp
from jax import lax
from jax.experimental import pallas as pl
from jax.experimental.pallas import tpu as pltpu

D_MODEL = 2048
BATCH = 8
SEQ = 2048
N_TOK = BATCH * SEQ
HEAD_DIM = 64
HEAD_SHIFT = HEAD_DIM.bit_length() - 1
N_HEADS = D_MODEL // HEAD_DIM
D_FF = 5632
RMS_EPS = 1e-6
GN_EPS = 64e-5

LANES = 128
LANE_SHIFT = LANES.bit_length() - 1
SUBLANES = 8
HEADS_PER_VREG = LANES // HEAD_DIM
V7X_VMEM_BYTES = 64 * 1024 * 1024
VMEM_LIMIT = V7X_VMEM_BYTES * 7 // 8

NORM_MIX_DOT_ROWS = 128
WKV_CHUNK = 64
WKV_TBLK = 512
WKV_GROUPS = 2
WKV_LANES = 512
ATT_TQ = 256
ATT_TK = 512
ATT_LANES = 1024
PREP_LANES = 512
N_SPLIT = 3
LOGIT_SPLIT = 2
NEG_BIG = -1e30
LOG2_E = 1.4426950408889634
EXP_NEG_HALF = 0.6065306597126334

f32 = jnp.float32
bf16 = jnp.bfloat16


def _dot(a, b):
    return jnp.dot(a.astype(bf16), b.astype(bf16), preferred_element_type=f32)


def _dot_nt(a, b):
    return lax.dot_general(a.astype(bf16), b.astype(bf16), (((1,), (1,)), ((), ())),
                           preferred_element_type=f32)


def _split3(x):
    hi = x.astype(bf16)
    r1 = x - hi.astype(f32)
    mid = r1.astype(bf16)
    lo = (r1 - mid.astype(f32)).astype(bf16)
    return hi, mid, lo


def _dot_sel_lhs(sel, x):
    hi, mid, lo = _split3(x)
    d = lambda p: jnp.dot(sel, p, preferred_element_type=f32)
    return d(hi) + d(mid) + d(lo)


def _head_sums(x, head_ones):
    hi = x.astype(bf16)
    lo = (x - hi.astype(f32)).astype(bf16)
    return jnp.dot(jnp.concatenate([hi, lo], axis=1), jnp.concatenate([head_ones, head_ones], axis=0),
                   preferred_element_type=f32)


def _head_sums_lanes(x, m0):
    zero = jnp.zeros_like(x)
    s0 = jnp.sum(jnp.where(m0, x, zero), axis=-1, keepdims=True)
    s1 = jnp.sum(jnp.where(m0, zero, x), axis=-1, keepdims=True)
    return jnp.where(m0, s0, s1)


def _head_ones():
    r = lax.broadcasted_iota(jnp.int32, (LANES, LANES), 0) >> HEAD_SHIFT
    c = lax.broadcasted_iota(jnp.int32, (LANES, LANES), 1) >> HEAD_SHIFT
    return (r == c).astype(bf16)


def _first_head_mask():
    return lax.broadcasted_iota(jnp.int32, (1, LANES), 1) < HEAD_DIM


def _stack_heads(x, m0):
    z = jnp.zeros_like(x)
    return jnp.concatenate([jnp.where(m0, x, z), jnp.where(m0, z, x)], axis=0)


def _softplus(z):
    return jnp.maximum(z, 0.0) + jnp.log(1.0 + jnp.exp(-jnp.abs(z)))


def _rms(x, g):
    return x * lax.rsqrt(jnp.mean(x * x, axis=-1, keepdims=True) + RMS_EPS) * g


def _norm_mix_kernel(x_ref, xp_ref, g_ref, mix_ref, w1_ref, a1_ref, g1_ref,
                     xr_ref, xk_ref, xv_ref, zw_ref, za_ref, zg_ref, xw_s, xa_s, xg_s, *, tm):
    i = pl.program_id(0)
    d = x_ref.shape[1]
    rows_per_chunk = 2 * SUBLANES
    lane_blk = 4 * LANES
    seq_start = (i * tm) % SEQ == 0
    hp_row = _rms(xp_ref[...], g_ref[...])[SUBLANES - 1:SUBLANES, :]
    hp_row = jnp.where(seq_start, jnp.zeros_like(hp_row), hp_row)
    rid = lax.broadcasted_iota(jnp.int32, (rows_per_chunk, 1), 0)
    mix_dsts = (xr_ref, xk_ref, xv_ref, xw_s, xa_s, xg_s)

    def chunk(c, last_row):
        rows = slice(c * rows_per_chunk, (c + 1) * rows_per_chunk)
        x = x_ref[rows, :]
        inv = lax.rsqrt(jnp.mean(x * x, axis=-1, keepdims=True) + RMS_EPS)
        new_last = []
        for b in range(d // lane_blk):
            ln = slice(b * lane_blk, (b + 1) * lane_blk)
            h = x[:, ln] * inv * g_ref[:, ln]
            hprev = jnp.where(rid == 0, last_row[:, ln], pltpu.roll(h, 1, 0))
            xx = hprev - h
            for p, dst in enumerate(mix_dsts):
                dst[rows, ln] = (h + xx * mix_ref[p:p + 1, ln]).astype(dst.dtype)
            new_last.append(h[rows_per_chunk - 1:rows_per_chunk, :])
        return jnp.concatenate(new_last, axis=1)

    chunks_per_blk = NORM_MIX_DOT_ROWS // rows_per_chunk
    last_row = hp_row
    for blk in range(tm // NORM_MIX_DOT_ROWS):
        for c in range(blk * chunks_per_blk, (blk + 1) * chunks_per_blk):
            last_row = chunk(c, last_row)
        rows = slice(blk * NORM_MIX_DOT_ROWS, (blk + 1) * NORM_MIX_DOT_ROWS)
        down = lambda xs, w: jnp.dot(xs[rows, :], w[...], preferred_element_type=f32)
        zw_ref[rows, :] = jnp.tanh(down(xw_s, w1_ref)).astype(zw_ref.dtype)
        za_ref[rows, :] = down(xa_s, a1_ref).astype(za_ref.dtype)
        zg_ref[rows, :] = jax.nn.sigmoid(down(xg_s, g1_ref)).astype(zg_ref.dtype)


def _norm_mix(x, g, mix, w1, a1, g1, tm=512):
    m, d = x.shape
    rb = tm // SUBLANES
    row_tile = lambda n: pl.BlockSpec((tm, n), lambda i: (i, 0))
    whole = lambda a: pl.BlockSpec(a.shape, lambda i: (0, 0))
    lows = (w1, a1, g1)
    return pl.pallas_call(
        functools.partial(_norm_mix_kernel, tm=tm),
        grid=(m // tm,),
        in_specs=[row_tile(d),
                  pl.BlockSpec((SUBLANES, d), lambda i: (jnp.maximum(i * rb - 1, 0), 0)),
                  pl.BlockSpec((1, d), lambda i: (0, 0)),
                  pl.BlockSpec((6, d), lambda i: (0, 0))] + [whole(w) for w in lows],
        out_specs=[row_tile(d)] * 3 + [row_tile(w.shape[1]) for w in lows],
        out_shape=[jax.ShapeDtypeStruct((m, d), bf16)] * 3
        + [jax.ShapeDtypeStruct((m, w.shape[1]), bf16) for w in lows],
        scratch_shapes=[pltpu.VMEM((tm, d), bf16)] * 3,
        compiler_params=pltpu.CompilerParams(dimension_semantics=("parallel",),
                                             vmem_limit_bytes=VMEM_LIMIT),
        name="norm_mix",
    )(x, x, g.reshape(1, d), mix, *lows)


def _mm_kernel(*refs, n_w, n_e, n_o, nk, epilogue):
    x_ref = refs[0]
    w_refs = refs[1:1 + n_w]
    e_refs = refs[1 + n_w:1 + n_w + n_e]
    o_refs = refs[1 + n_w + n_e:1 + n_w + n_e + n_o]
    acc_refs = refs[1 + n_w + n_e + n_o:]
    x = x_ref[...]

    def finish(accs):
        outs = epilogue(accs, [e[...] for e in e_refs])
        for o_ref, out in zip(o_refs, outs if isinstance(outs, tuple) else (outs,)):
            o_ref[...] = out.astype(o_ref.dtype)

    if nk == 1:
        finish([jnp.dot(x, w[...].astype(x.dtype), preferred_element_type=f32) for w in w_refs])
        return
    k = pl.program_id(2)

    @pl.when(k == 0)
    def _():
        for a in acc_refs:
            a[...] = jnp.zeros_like(a)

    for a, w in zip(acc_refs, w_refs):
        a[...] += jnp.dot(x, w[...], preferred_element_type=f32)

    @pl.when(k == nk - 1)
    def _():
        finish([a[...] for a in acc_refs])


def _matmul(x, w, layer, w_col_blocks, n_out, *, tm, tn, tk, epilogue, extras=(), out_dtype, name,
            resident_w=False):
    m, kdim = x.shape
    nk = kdim // tk
    n_w = len(w_col_blocks)
    nb = n_out // tn
    in_specs = [pl.BlockSpec((tm, tk), lambda i, j, k: (i, k))]
    args = [x]
    w_mode = pl.Buffered(1) if resident_w else None
    for cb in w_col_blocks:
        in_specs.append(pl.BlockSpec((None, tk, tn), lambda i, j, k, cb=cb: (layer, k, cb * nb + j),
                                     pipeline_mode=w_mode))
        args.append(w)
    for arr, kind in extras:
        if kind == "row":
            in_specs.append(pl.BlockSpec((1, tn), lambda i, j, k: (0, j)))
        else:
            in_specs.append(pl.BlockSpec((tm, tn), lambda i, j, k: (i, j)))
        args.append(arr)
    scratch = [pltpu.VMEM((tm, tn), f32) for _ in range(n_w)] if nk > 1 else []
    multi = isinstance(out_dtype, tuple)
    dtypes = out_dtype if multi else (out_dtype,)
    out_spec = pl.BlockSpec((tm, tn), lambda i, j, k: (i, j))
    outs = pl.pallas_call(
        functools.partial(_mm_kernel, n_w=n_w, n_e=len(extras), n_o=len(dtypes), nk=nk, epilogue=epilogue),
        grid=(m // tm, nb, nk),
        in_specs=in_specs,
        out_specs=[out_spec] * len(dtypes),
        out_shape=[jax.ShapeDtypeStruct((m, n_out), dt) for dt in dtypes],
        scratch_shapes=scratch,
        compiler_params=pltpu.CompilerParams(
            dimension_semantics=("parallel", "parallel", "arbitrary"),
            vmem_limit_bytes=VMEM_LIMIT),
        name=name,
    )(*args)
    return tuple(outs) if multi else outs[0]


def _epi_plain(accs, extras):
    return accs[0]


def _epi_residual_norm(accs, extras):
    y = extras[0] + accs[0]
    return y, _rms(y, extras[1])


def _epi_residual_norm_only(accs, extras):
    return _rms(extras[0] + accs[0], extras[1])


def _epi_swiglu(accs, extras):
    gate, up = accs
    return gate * jax.nn.sigmoid(gate) * up


def _log_decay(y, w0):
    return -EXP_NEG_HALF * jax.nn.sigmoid(w0 + y)


def _pad_cols(w, n):
    return jnp.pad(w, ((0, 0), (0, n - w.shape[1])))


def _pad_rows(w, n):
    return jnp.pad(w, ((0, n - w.shape[0]), (0, 0)))


def _each(fn, *lists):
    return [fn(*xs) for xs in zip(*lists)]


def _wkv_chunk_maps(tiles, consts):
    m0, strict, incl, eye = consts
    c = WKV_CHUNK
    n2 = 2 * c
    r, lw, lg, k2, v, av, bv = (list(x) for x in zip(*tiles))
    stack = lambda x: _stack_heads(x, m0)
    lg_end = _each(lambda x: x[c - 1:c, :], lg)
    g_inv = _each(lambda x: jnp.exp(-x), lg)
    g_rem = _each(lambda e, x: jnp.exp(e - x), lg_end, lg)
    rs = _each(lambda x, l: stack(x * jnp.exp(l)), r, lg)
    as_ = _each(lambda x, l, w: stack(x * jnp.exp(l - w)).astype(bf16), av, lg, lw)
    bs = _each(lambda x, g: stack(x * g).astype(bf16), bv, g_inv)
    ks = _each(lambda x, g: stack(x * g).astype(bf16), k2, g_inv)
    bhs = _each(lambda x, g: stack(x * g).astype(bf16), bv, g_rem)
    khs = _each(lambda x, g: stack(x * g).astype(bf16), k2, g_rem)
    vs = _each(stack, v)
    sc = _each(lambda a, rr, b, k: _dot_nt(jnp.concatenate([a, rr.astype(bf16)], axis=0),
                                           jnp.concatenate([b, k], axis=0)), as_, rs, bs, ks)
    zero = jnp.zeros((n2, n2), f32)
    yield
    a_ab = _each(lambda s: jnp.where(strict, s[:n2, :n2], zero), sc)
    a_ak = _each(lambda s: jnp.where(strict, s[:n2, n2:], zero), sc)
    a_r = _each(lambda s: jnp.where(jnp.concatenate([incl, incl], axis=1), s[n2:, :],
                                    jnp.zeros((n2, 2 * n2), f32)).astype(bf16), sc)
    side = lambda a, b: jnp.concatenate([a, b], axis=1)
    t = _each(lambda x: eye + x, a_ab)
    p = _each(lambda x: _dot(x, x), a_ab)
    yield
    for _ in range(c.bit_length() - 3):
        pt = _each(lambda pp, tt: _dot(pp, side(pp, tt)), p, t)
        yield
        p = _each(lambda x: x[:, :n2], pt)
        t = _each(lambda tt, x: tt + x[:, n2:], t, pt)
    t = _each(lambda tt, pp: (tt + _dot(pp, tt)).astype(bf16), t, p)
    yield
    akv = _each(_dot, a_ak, vs)
    yield
    hw = _each(lambda tt, a, x: _dot(tt, side(a, x)), t, as_, akv)
    yield
    ah = _each(lambda x: x[:, :LANES], hw)
    ws = _each(lambda x: x[:, LANES:], hw)
    pq = _each(lambda ar, x, vv: _dot(ar, jnp.concatenate([x, side(jnp.zeros_like(vv), vv)], axis=0)),
               a_r, hw, vs)
    yield
    pc = _each(lambda x, y: x + y[:, :LANES], rs, pq)
    qc = _each(lambda y: y[:, LANES:], pq)
    gm = _each(lambda h, b: _dot(h.T, b), ah, bhs)
    yield
    nc = _each(lambda w, x, b, k: _dot(jnp.concatenate([w, x], axis=0).T, jnp.concatenate([b, k], axis=0)),
               ws, vs, bhs, khs)
    decay = _each(jnp.exp, lg_end)
    return pc, qc, gm, nc, decay


def _drive(staged, between=()):
    pending = list(between)
    while True:
        try:
            next(staged)
        except StopIteration as done:
            for fn in pending:
                if fn is not None:
                    fn()
            return done.value
        if pending:
            fn = pending.pop(0)
            if fn is not None:
                fn()


def _wkv_kernel(r_ref, k_ref, v_ref, zw_ref, za_ref, zg_ref, w2_ref, a2_ref, g2_ref, w0_ref, a0_ref,
                kk_ref, ka_ref, rk_ref, lng_ref, lnb_ref, o_ref, s_ref):
    c = WKV_CHUNK
    n2 = 2 * c
    up = lambda z_ref, w_ref: jnp.dot(z_ref[...], w_ref[...], preferred_element_type=f32)
    lw_all = _log_decay(up(zw_ref, w2_ref), w0_ref[...])
    a_all = jax.nn.sigmoid(a0_ref[...] + up(za_ref, a2_ref))
    g_all = up(zg_ref, g2_ref)

    @pl.when(pl.program_id(2) == 0)
    def _():
        s_ref[...] = jnp.zeros_like(s_ref)

    ri = lax.broadcasted_iota(jnp.int32, (n2, n2), 0)
    ci = lax.broadcasted_iota(jnp.int32, (n2, n2), 1)
    tr = lax.broadcasted_iota(jnp.int32, (c, c), 0)
    tc = lax.broadcasted_iota(jnp.int32, (c, c), 1)
    consts = (_first_head_mask(), (ri & (c - 1)) > (ci & (c - 1)), (ri & (c - 1)) >= (ci & (c - 1)),
              (ri == ci).astype(f32))
    tri = (tr >= tc).astype(bf16)
    m0 = consts[0]
    inv_n = 1.0 / HEAD_DIM

    npairs = WKV_LANES // LANES
    nchunk = WKV_TBLK // c
    lanes = [slice(p * LANES, (p + 1) * LANES) for p in range(npairs)]
    r = [r_ref[:, ln] for ln in lanes]
    v = [v_ref[:, ln] for ln in lanes]
    k2, av, bv, bonus = [], [], [], []
    for p, ln in enumerate(lanes):
        k, a = k_ref[:, ln], a_all[:, ln]
        kk = k * kk_ref[:, ln]
        ss = _head_sums_lanes(kk * kk, m0)
        kk = kk / jnp.maximum(jnp.sqrt(ss), 1e-12)
        k2.append(k * (1.0 + (a - 1.0) * ka_ref[:, ln]))
        av.append(-kk)
        bv.append(kk * a)
        bonus.append(_head_sums_lanes(r[p] * k2[p] * rk_ref[:, ln], m0))
    tiles = []
    for ic in range(nchunk):
        rows = slice(ic * c, (ic + 1) * c)
        lw = lw_all[rows, :]
        lg = _dot_sel_lhs(tri, lw)
        for p, ln in enumerate(lanes):
            tiles.append((r[p][rows], lw[:, ln], lg[:, ln], k2[p][rows], v[p][rows], av[p][rows], bv[p][rows]))
    s = [s_ref[p] for p in range(npairs)]
    ys = [[] for _ in range(npairs)]

    def advance(maps, j):
        pc, qc, gm, nc, decay = maps
        for p in range(npairs):
            i = j * npairs + p
            y = _dot_nt(pc[i], s[p]) + qc[i]
            ys[p].append(y[:c] + y[c:])
            s[p] = s[p] * decay[i] + _dot(s[p], gm[i]) + nc[i]

    per_group = nchunk // WKV_GROUPS
    slots = []
    for gi in range(WKV_GROUPS):
        maps = _drive(_wkv_chunk_maps(tiles[gi * per_group * npairs:(gi + 1) * per_group * npairs], consts), slots)
        slots = []
        for j in range(per_group):
            slots += [functools.partial(advance, maps, j), None, None]
    _drive(iter(()), slots)
    for p, ln in enumerate(lanes):
        s_ref[p] = s[p]
        y = jnp.concatenate(ys[p], axis=0)
        mu = _head_sums_lanes(y, m0) * inv_n
        yc = y - mu
        var = _head_sums_lanes(yc * yc, m0) * inv_n
        yn = yc * lax.rsqrt(var + GN_EPS) * lng_ref[:, ln] + lnb_ref[:, ln]
        o_ref[:, ln] = ((yn + bonus[p] * v[p]) * g_all[:, ln]).astype(o_ref.dtype)


def _wkv(r, k, v, zs, ups, w0, a0, k_k, k_a, r_k, lnx_g, lnx_b):
    m, d = r.shape
    tb = SEQ // WKV_TBLK
    tile = pl.BlockSpec((WKV_TBLK, WKV_LANES), lambda b, j, t: (b * tb + t, j))
    row = pl.BlockSpec((1, WKV_LANES), lambda b, j, t: (0, j))
    z_tile = lambda z: pl.BlockSpec((WKV_TBLK, z.shape[1]), lambda b, j, t: (b * tb + t, 0))
    up_tile = lambda w: pl.BlockSpec((w.shape[0], WKV_LANES), lambda b, j, t: (0, j))
    rows = [x.reshape(1, d) for x in (w0, a0, k_k, k_a, r_k, lnx_g, lnx_b)]
    return pl.pallas_call(
        _wkv_kernel,
        grid=(BATCH, d // WKV_LANES, tb),
        in_specs=[tile] * 3 + [z_tile(z) for z in zs] + [up_tile(w) for w in ups] + [row] * len(rows),
        out_specs=tile,
        out_shape=jax.ShapeDtypeStruct((m, d), bf16),
        scratch_shapes=[pltpu.VMEM((WKV_LANES // LANES, LANES, LANES), f32)],
        compiler_params=pltpu.CompilerParams(
            dimension_semantics=("parallel", "parallel", "arbitrary"),
            vmem_limit_bytes=VMEM_LIMIT),
        name="wkv7",
    )(r, k, v, *zs, *ups, *rows)


def _fox_gate_kernel(s_ref, bf_ref, cp_ref, sp_ref, run_ref, *, tc):
    @pl.when(pl.program_id(1) == 0)
    def _():
        run_ref[...] = jnp.zeros_like(run_ref)

    small = s_ref[...]
    ls = -_softplus(-(small + bf_ref[...]))
    tr = lax.broadcasted_iota(jnp.int32, (tc, tc), 0)
    tcc = lax.broadcasted_iota(jnp.int32, (tc, tc), 1)
    cs = _dot_sel_lhs((tr >= tcc).astype(bf16), ls) + run_ref[...]
    run_ref[...] = cs[tc - 1:tc, :]
    w = cs.shape[1]
    src = lax.broadcasted_iota(jnp.int32, (w, w), 0)
    dst = lax.broadcasted_iota(jnp.int32, (w, w), 1)
    sel = [((dst == N_SPLIT * src + n) & (src < N_HEADS)).astype(bf16) for n in range(N_SPLIT)]
    cp_ref[...] = jnp.dot(jnp.concatenate(_split3(cs * LOG2_E), axis=1), jnp.concatenate(sel, axis=0),
                          preferred_element_type=f32).astype(cp_ref.dtype)
    sp_ref[...] = jnp.concatenate(_split3(small)[:LOGIT_SPLIT], axis=1)


def _fox_gate(small, b_f, tc=512):
    m, w = small.shape
    nt = SEQ // tc
    bias = jnp.pad(b_f, (0, w - b_f.shape[0])).reshape(1, w)
    pieces = lambda n: pl.BlockSpec((tc, n * w), lambda b, t: (b * nt + t, 0))
    return pl.pallas_call(
        functools.partial(_fox_gate_kernel, tc=tc),
        grid=(BATCH, nt),
        in_specs=[pl.BlockSpec((tc, w), lambda b, t: (b * nt + t, 0)),
                  pl.BlockSpec((1, w), lambda b, t: (0, 0))],
        out_specs=[pieces(1), pieces(LOGIT_SPLIT)],
        out_shape=[jax.ShapeDtypeStruct((m, n * w), bf16) for n in (1, LOGIT_SPLIT)],
        scratch_shapes=[pltpu.VMEM((1, w), f32)],
        compiler_params=pltpu.CompilerParams(dimension_semantics=("parallel", "arbitrary"),
                                             vmem_limit_bytes=VMEM_LIMIT),
        name="fox_gate_cumsum",
    )(small, bias)


def _fox_prep_kernel(q_ref, k_ref, v_ref, kp_ref, vp_ref, s_ref, c_ref, qg_ref, kg_ref,
                     qo_ref, ko_ref, vo_ref, *, tm):
    i = pl.program_id(0)
    npairs = PREP_LANES // LANES
    head_ones = _head_ones()
    seq_start = (i * tm) % SEQ == 0
    rid = lax.broadcasted_iota(jnp.int32, (tm, 1), 0)
    r = lax.broadcasted_iota(jnp.int32, (LANES, 2 * LANES), 0)
    col = lax.broadcasted_iota(jnp.int32, (LANES, 2 * LANES), 1)
    small_pieces = s_ref[...]
    aug = c_ref[...]

    def shifted(x, prow):
        prow = jnp.where(seq_start, jnp.zeros_like(prow), prow)
        return jnp.where(rid == 0, prow, pltpu.roll(x, 1, 0))

    def head_rms(x, gain):
        ms = _head_sums(x * x, head_ones) * (1.0 / HEAD_DIM)
        return x * lax.rsqrt(ms + RMS_EPS) * gain

    for p in range(npairs):
        pair = pl.program_id(1) * npairs + p
        ln = slice(p * LANES, (p + 1) * LANES)
        src = (((col & (LANES - 1)) >> HEAD_SHIFT) + HEADS_PER_VREG * pair
               + N_HEADS * (1 + (col >> LANE_SHIFT)))
        sel = (r == src).astype(bf16)
        logits = jnp.dot(small_pieces, jnp.concatenate([sel] * LOGIT_SPLIT, axis=0), preferred_element_type=f32)
        ak = jax.nn.sigmoid(logits[:, :LANES])
        av = jax.nn.sigmoid(logits[:, LANES:])
        k = k_ref[:, ln]
        v = v_ref[:, ln]
        k = k + ak * (shifted(k, kp_ref[SUBLANES - 1:SUBLANES, ln]) - k)
        v = v + av * (shifted(v, vp_ref[SUBLANES - 1:SUBLANES, ln]) - v)
        qo_ref[:, ln] = head_rms(q_ref[:, ln], qg_ref[...] * (HEAD_DIM ** -0.5 * LOG2_E)).astype(qo_ref.dtype)
        ko_ref[:, 2 * p * LANES:2 * (p + 1) * LANES] = jnp.concatenate(
            [head_rms(k, kg_ref[...]).astype(ko_ref.dtype), aug], axis=1)
        vo_ref[ln, :] = v.T.astype(vo_ref.dtype)


def _fox_prep(proj, small_pieces, c_pieces, qn_g, kn_g, tm=1024):
    m = proj.shape[0]
    rb = tm // SUBLANES
    nb = D_MODEL // PREP_LANES
    tile = lambda cb: pl.BlockSpec((tm, PREP_LANES), lambda i, j, cb=cb: (i, cb * nb + j))
    prev = lambda cb: pl.BlockSpec(
        (SUBLANES, PREP_LANES), lambda i, j, cb=cb: (jnp.maximum(i * rb - 1, 0), cb * nb + j))
    gain = pl.BlockSpec((1, LANES), lambda i, j: (0, 0))
    small_tile = lambda a: pl.BlockSpec((tm, a.shape[1]), lambda i, j: (i, 0))
    tile_gain = lambda x: jnp.tile(x, HEADS_PER_VREG).reshape(1, LANES)
    return pl.pallas_call(
        functools.partial(_fox_prep_kernel, tm=tm),
        grid=(m // tm, nb),
        in_specs=[tile(0), tile(1), tile(2), prev(1), prev(2), small_tile(small_pieces), small_tile(c_pieces),
                  gain, gain],
        out_specs=[pl.BlockSpec((tm, PREP_LANES), lambda i, j: (i, j)),
                   pl.BlockSpec((tm, 2 * PREP_LANES), lambda i, j: (i, j)),
                   pl.BlockSpec((PREP_LANES, tm), lambda i, j: (j, i))],
        out_shape=[jax.ShapeDtypeStruct((m, D_MODEL), bf16),
                   jax.ShapeDtypeStruct((m, 2 * D_MODEL), bf16),
                   jax.ShapeDtypeStruct((D_MODEL, m), bf16)],
        compiler_params=pltpu.CompilerParams(dimension_semantics=("parallel", "parallel"),
                                             vmem_limit_bytes=VMEM_LIMIT),
        name="fox_prep",
    )(proj, proj, proj, proj, proj, small_pieces, c_pieces, tile_gain(qn_g), tile_gain(kn_g))


def _fox_attn_kernel(q_ref, k_ref, vt_ref, gate_ref, og_ref, o_ref):
    tq, tk = ATT_TQ, ATT_TK
    qi = pl.program_id(2)
    npairs = ATT_LANES // LANES
    chains = [(p, h) for p in range(npairs) for h in range(HEADS_PER_VREG)]
    lane = lax.broadcasted_iota(jnp.int32, (tq, LANES), 1)
    q_aug = []
    for p, h in chains:
        q = q_ref[:, p * LANES:(p + 1) * LANES]
        own = (lane >> HEAD_SHIFT) == h
        head = (pl.program_id(1) * npairs + p) * HEADS_PER_VREG + h
        minus_one = (lane >= N_SPLIT * head) & (lane < N_SPLIT * (head + 1))
        q_aug.append(jnp.concatenate([jnp.where(own, q, jnp.zeros_like(q)),
                                      jnp.where(minus_one, -1.0, 0.0).astype(bf16)], axis=1))

    def step(key0, width, carry, diag):
        m_run, l_run, acc = carry
        keys = pl.ds(pl.multiple_of(key0, tq), width)
        s = [_dot_nt(k_ref[keys, 2 * p * LANES:2 * (p + 1) * LANES], qa)
             for (p, h), qa in zip(chains, q_aug)]
        if diag:
            kidx = key0 + lax.broadcasted_iota(jnp.int32, (width, tq), 0)
            qidx = qi * tq + lax.broadcasted_iota(jnp.int32, (width, tq), 1)
            s = [jnp.where(qidx >= kidx, x, NEG_BIG) for x in s]
        m_new = [jnp.maximum(mr, jnp.max(x, axis=0, keepdims=True)) for mr, x in zip(m_run, s)]
        alpha = [jnp.exp2(mr - mn) for mr, mn in zip(m_run, m_new)]
        pr = [jnp.exp2(x - mn) for x, mn in zip(s, m_new)]
        l_new = [a * lr + jnp.sum(x, axis=0, keepdims=True) for a, lr, x in zip(alpha, l_run, pr)]
        pv = [jnp.dot(vt_ref[pl.ds((p * HEADS_PER_VREG + h) * HEAD_DIM, HEAD_DIM), keys], x.astype(bf16),
                      preferred_element_type=f32) for (p, h), x in zip(chains, pr)]
        acc = [ac * a + x for ac, a, x in zip(acc, alpha, pv)]
        return m_new, l_new, acc

    n = len(chains)
    init = ([jnp.full((1, tq), NEG_BIG, f32)] * n, [jnp.zeros((1, tq), f32)] * n,
            [jnp.zeros((HEAD_DIM, tq), f32)] * n)
    n_full = (qi * tq) // tk
    carry = lax.fori_loop(0, n_full, lambda j, cy: step(j * tk, tk, cy, False), init)
    _, l_run, acc = lax.cond((qi * tq) % tk == 0,
                             lambda cy: step(n_full * tk, tq, cy, True),
                             lambda cy: step(n_full * tk, tk, cy, True), carry)
    o_t = []
    for ac, lr in zip(acc, l_run):
        o = ac / lr
        o_t.append(o * lax.rsqrt(jnp.mean(o * o, axis=0, keepdims=True) + RMS_EPS))
    for p in range(npairs):
        ln = slice(p * LANES, (p + 1) * LANES)
        o = jnp.concatenate(o_t[HEADS_PER_VREG * p:HEADS_PER_VREG * (p + 1)], axis=0).T
        o_ref[:, ln] = (o * og_ref[:, ln] * jax.nn.sigmoid(gate_ref[:, ln])).astype(o_ref.dtype)


def _fox_attn(q, k_aug, v_t, proj, on_g):
    m, d = q.shape
    nq = SEQ // ATT_TQ
    gate_col0 = 3 * D_MODEL // ATT_LANES
    return pl.pallas_call(
        _fox_attn_kernel,
        grid=(BATCH, d // ATT_LANES, nq),
        in_specs=[pl.BlockSpec((ATT_TQ, ATT_LANES), lambda b, p, i: (b * nq + i, p)),
                  pl.BlockSpec((SEQ, 2 * ATT_LANES), lambda b, p, i: (b, p)),
                  pl.BlockSpec((ATT_LANES, SEQ), lambda b, p, i: (p, b)),
                  pl.BlockSpec((ATT_TQ, ATT_LANES), lambda b, p, i: (b * nq + i, gate_col0 + p)),
                  pl.BlockSpec((1, ATT_LANES), lambda b, p, i: (0, p))],
        out_specs=pl.BlockSpec((ATT_TQ, ATT_LANES), lambda b, p, i: (b * nq + i, p)),
        out_shape=jax.ShapeDtypeStruct((m, d), bf16),
        compiler_params=pltpu.CompilerParams(
            dimension_semantics=("parallel", "parallel", "arbitrary"),
            vmem_limit_bytes=VMEM_LIMIT),
        name="fox_attention",
    )(q, k_aug, v_t, proj, on_g.reshape(1, d))


def _swiglu_block(x, hn, w_gu, w_d, layer, next_norm_g, last):
    act = _matmul(hn, w_gu, layer, (0, 1), D_FF, tm=2048, tn=512, tk=D_MODEL,
                  epilogue=_epi_swiglu, out_dtype=bf16, name="swiglu_gate_up")
    return _matmul(act, w_d, layer, (0,), D_MODEL, tm=256, tn=D_MODEL, tk=D_FF, resident_w=True,
                   epilogue=_epi_residual_norm_only if last else _epi_residual_norm,
                   extras=((x, "tile"), (next_norm_g.reshape(1, D_MODEL), "row")),
                   out_dtype=f32 if last else (f32, bf16), name="swiglu_down")


def _proj(x, w, layer=0, *, name):
    if w.ndim == 2:
        w = w[None]
    n = w.shape[2]
    resident = n <= D_MODEL
    return _matmul(x, w, layer, (0,), n, tm=1024 if resident else 2048, tn=n if resident else 1024,
                   tk=x.shape[1], epilogue=_epi_plain, out_dtype=f32, name=name, resident_w=resident)


def _out_proj_norm(y, w_o, x, next_norm_g, *, name):
    return _matmul(y, w_o.astype(bf16)[None], 0, (0,), D_MODEL, tm=512, tn=D_MODEL, tk=D_MODEL,
                   epilogue=_epi_residual_norm, extras=((x, "tile"), (next_norm_g.reshape(1, D_MODEL), "row")),
                   out_dtype=(f32, bf16), name=name)


def _rwkv7_block(x, norm_g, mix, w_rkv, w0, w1, w2, a0, a1, a2, g1, g2, k_k, k_a, r_k, lnx_g, lnx_b, w_o,
                 next_norm_g):
    rank = LANES * pl.cdiv(w1.shape[1], LANES)
    downs = (_pad_cols(w1, rank).astype(bf16), _pad_cols(a1, rank).astype(bf16), g1.astype(bf16))
    ups = (_pad_rows(w2, rank).astype(bf16), _pad_rows(a2, rank).astype(bf16), g2.astype(bf16))
    xr, xk, xv, *zs = _norm_mix(x, norm_g, mix, *downs)
    w_rkv = w_rkv.astype(bf16)
    r = _proj(xr, w_rkv, 0, name="rwkv_r")
    k = _proj(xk, w_rkv, 1, name="rwkv_k")
    v = _proj(xv, w_rkv, 2, name="rwkv_v")
    yg = _wkv(r, k, v, zs, ups, w0, a0, k_k, k_a, r_k.reshape(-1), lnx_g, lnx_b)
    return _out_proj_norm(yg, w_o, x, next_norm_g, name="rwkv_out")


def _fox_block(x, hn, w_in, b_f, qn_g, kn_g, on_g, w_o, next_norm_g):
    n_main = 4 * D_MODEL
    proj = _proj(hn, w_in[:, :n_main].astype(bf16), name="fox_in")
    w_small = _pad_cols(w_in[:, n_main:], LANES).astype(bf16)
    small = _proj(hn, w_small, name="fox_in_gates")
    c_pieces, small_pieces = _fox_gate(small, b_f)
    q, k_aug, v_t = _fox_prep(proj, small_pieces, c_pieces, qn_g, kn_g)
    og = _fox_attn(q, k_aug, v_t, proj, on_g)
    return _out_proj_norm(og, w_o, x, next_norm_g, name="fox_out")


def kernel(x, a_norm_g, a_mix, a_w_rkv, a_w0, a_w1, a_w2, a_a0, a_a1, a_a2, a_g1, a_g2, a_k_k, a_k_a, a_r_k, a_lnx_g, a_lnx_b, a_w_o, b_norm_g, b_w_in, b_b_f, b_qn_g, b_kn_g, b_on_g, b_w_o, f_norm_g, f_w_gu, f_w_d, final_g):
    b, t, d = x.shape
    h = x.reshape(b * t, d)
    w_d = f_w_d.astype(bf16)
    h = _rwkv7_block(h, a_norm_g[0], a_mix[0], a_w_rkv[0], a_w0[0], a_w1[0], a_w2[0], a_a0[0], a_a1[0],
                     a_a2[0], a_g1[0], a_g2[0], a_k_k[0], a_k_a[0], a_r_k[0], a_lnx_g[0], a_lnx_b[0],
                     a_w_o[0], f_norm_g[0])
    h = _swiglu_block(*h, f_w_gu, w_d, 0, b_norm_g[0], False)
    h = _fox_block(*h, b_w_in[0], b_b_f[0], b_qn_g[0], b_kn_g[0], b_on_g[0], b_w_o[0],
                   f_norm_g[1])
    return _swiglu_block(*h, f_w_gu, w_d, 1, final_g, True).reshape(b, t, d)
```

```python
import functools

import jax
import jax.numpy as jnp
from jax import lax
from jax.experimental import pallas as pl
from jax.experimental.pallas import tpu as pltpu

D_MODEL = 2048
BATCH = 8
SEQ = 2048
N_TOK = BATCH * SEQ
HEAD_DIM = 64
HEAD_SHIFT = HEAD_DIM.bit_length() - 1
N_HEADS = D_MODEL // HEAD_DIM
D_FF = 5632
RMS_EPS = 1e-6
GN_EPS = 64e-5

LANES = 128
LANE_SHIFT = LANES.bit_length() - 1
SUBLANES = 8
HEADS_PER_VREG = LANES // HEAD_DIM
V7X_VMEM_BYTES = 64 * 1024 * 1024
VMEM_LIMIT = V7X_VMEM_BYTES * 7 // 8

NORM_MIX_DOT_ROWS = 128
WKV_CHUNK = 64
WKV_TBLK = 512
WKV_GROUPS = 2
WKV_LANES = 512
ATT_TQ = 256
ATT_TK = 512
ATT_LANES = 1024
PREP_LANES = 512
N_SPLIT = 3
GATE_SPLIT = 2
NEG_BIG = -1e30
LOG2_E = 1.4426950408889634
EXP_NEG_HALF = 0.6065306597126334

f32 = jnp.float32
bf16 = jnp.bfloat16


def _dot(a, b):
    return jnp.dot(a.astype(bf16), b.astype(bf16), preferred_element_type=f32)


def _dot_nt(a, b):
    return lax.dot_general(a.astype(bf16), b.astype(bf16), (((1,), (1,)), ((), ())),
                           preferred_element_type=f32)


def _split3(x):
    hi = x.astype(bf16)
    r1 = x - hi.astype(f32)
    mid = r1.astype(bf16)
    lo = (r1 - mid.astype(f32)).astype(bf16)
    return hi, mid, lo


def _dot_sel_lhs(sel, x):
    hi, mid, lo = _split3(x)
    d = lambda p: jnp.dot(sel, p, preferred_element_type=f32)
    return d(hi) + d(mid) + d(lo)


def _head_sums(x, head_ones):
    hi = x.astype(bf16)
    lo = (x - hi.astype(f32)).astype(bf16)
    return jnp.dot(jnp.concatenate([hi, lo], axis=1), jnp.concatenate([head_ones, head_ones], axis=0),
                   preferred_element_type=f32)


def _head_sums_lanes(x, m0):
    zero = jnp.zeros_like(x)
    s0 = jnp.sum(jnp.where(m0, x, zero), axis=-1, keepdims=True)
    s1 = jnp.sum(jnp.where(m0, zero, x), axis=-1, keepdims=True)
    return jnp.where(m0, s0, s1)


def _head_ones():
    r = lax.broadcasted_iota(jnp.int32, (LANES, LANES), 0) >> HEAD_SHIFT
    c = lax.broadcasted_iota(jnp.int32, (LANES, LANES), 1) >> HEAD_SHIFT
    return (r == c).astype(bf16)


def _first_head_mask():
    return lax.broadcasted_iota(jnp.int32, (1, LANES), 1) < HEAD_DIM


def _stack_heads(x, m0):
    z = jnp.zeros_like(x)
    return jnp.concatenate([jnp.where(m0, x, z), jnp.where(m0, z, x)], axis=0)


def _softplus(z):
    return jnp.maximum(z, 0.0) + jnp.log(1.0 + jnp.exp(-jnp.abs(z)))


def _rms(x, g):
    return x * lax.rsqrt(jnp.mean(x * x, axis=-1, keepdims=True) + RMS_EPS) * g


def _norm_mix_kernel(x_ref, xp_ref, g_ref, mix_ref, w1_ref, a1_ref, g1_ref,
                     xr_ref, xk_ref, xv_ref, zw_ref, za_ref, zg_ref, xw_s, xa_s, xg_s, *, tm):
    i = pl.program_id(0)
    d = x_ref.shape[1]
    rows_per_chunk = 2 * SUBLANES
    lane_blk = 4 * LANES
    seq_start = (i * tm) % SEQ == 0
    hp_row = _rms(xp_ref[...], g_ref[...])[SUBLANES - 1:SUBLANES, :]
    hp_row = jnp.where(seq_start, jnp.zeros_like(hp_row), hp_row)
    rid = lax.broadcasted_iota(jnp.int32, (rows_per_chunk, 1), 0)
    mix_dsts = (xr_ref, xk_ref, xv_ref, xw_s, xa_s, xg_s)

    def chunk(c, last_row):
        rows = slice(c * rows_per_chunk, (c + 1) * rows_per_chunk)
        x = x_ref[rows, :]
        inv = lax.rsqrt(jnp.mean(x * x, axis=-1, keepdims=True) + RMS_EPS)
        new_last = []
        for b in range(d // lane_blk):
            ln = slice(b * lane_blk, (b + 1) * lane_blk)
            h = x[:, ln] * inv * g_ref[:, ln]
            hprev = jnp.where(rid == 0, last_row[:, ln], pltpu.roll(h, 1, 0))
            xx = hprev - h
            for p, dst in enumerate(mix_dsts):
                dst[rows, ln] = (h + xx * mix_ref[p:p + 1, ln]).astype(dst.dtype)
            new_last.append(h[rows_per_chunk - 1:rows_per_chunk, :])
        return jnp.concatenate(new_last, axis=1)

    chunks_per_blk = NORM_MIX_DOT_ROWS // rows_per_chunk
    last_row = hp_row
    for blk in range(tm // NORM_MIX_DOT_ROWS):
        for c in range(blk * chunks_per_blk, (blk + 1) * chunks_per_blk):
            last_row = chunk(c, last_row)
        rows = slice(blk * NORM_MIX_DOT_ROWS, (blk + 1) * NORM_MIX_DOT_ROWS)
        down = lambda xs, w: jnp.dot(xs[rows, :], w[...], preferred_element_type=f32)
        zw_ref[rows, :] = jnp.tanh(down(xw_s, w1_ref)).astype(zw_ref.dtype)
        za_ref[rows, :] = down(xa_s, a1_ref).astype(za_ref.dtype)
        zg_ref[rows, :] = jax.nn.sigmoid(down(xg_s, g1_ref)).astype(zg_ref.dtype)


def _norm_mix(x, g, mix, w1, a1, g1, tm=512):
    m, d = x.shape
    rb = tm // SUBLANES
    row_tile = lambda n: pl.BlockSpec((tm, n), lambda i: (i, 0))
    whole = lambda a: pl.BlockSpec(a.shape, lambda i: (0, 0))
    lows = (w1, a1, g1)
    return pl.pallas_call(
        functools.partial(_norm_mix_kernel, tm=tm),
        grid=(m // tm,),
        in_specs=[row_tile(d),
                  pl.BlockSpec((SUBLANES, d), lambda i: (jnp.maximum(i * rb - 1, 0), 0)),
                  pl.BlockSpec((1, d), lambda i: (0, 0)),
                  pl.BlockSpec((6, d), lambda i: (0, 0))] + [whole(w) for w in lows],
        out_specs=[row_tile(d)] * 3 + [row_tile(w.shape[1]) for w in lows],
        out_shape=[jax.ShapeDtypeStruct((m, d), bf16)] * 3
        + [jax.ShapeDtypeStruct((m, w.shape[1]), bf16) for w in lows],
        scratch_shapes=[pltpu.VMEM((tm, d), bf16)] * 3,
        compiler_params=pltpu.CompilerParams(dimension_semantics=("parallel",),
                                             vmem_limit_bytes=VMEM_LIMIT),
        name="norm_mix",
    )(x, x, g.reshape(1, d), mix, *lows)


def _mm_kernel(*refs, n_w, n_e, n_o, nk, epilogue):
    x_ref = refs[0]
    w_refs = refs[1:1 + n_w]
    e_refs = refs[1 + n_w:1 + n_w + n_e]
    o_refs = refs[1 + n_w + n_e:1 + n_w + n_e + n_o]
    acc_refs = refs[1 + n_w + n_e + n_o:]
    x = x_ref[...]

    def finish(accs):
        outs = epilogue(accs, [e[...] for e in e_refs])
        for o_ref, out in zip(o_refs, outs if isinstance(outs, tuple) else (outs,)):
            o_ref[...] = out.astype(o_ref.dtype)

    if nk == 1:
        finish([jnp.dot(x, w[...].astype(x.dtype), preferred_element_type=f32) for w in w_refs])
        return
    k = pl.program_id(2)

    @pl.when(k == 0)
    def _():
        for a in acc_refs:
            a[...] = jnp.zeros_like(a)

    for a, w in zip(acc_refs, w_refs):
        a[...] += jnp.dot(x, w[...], preferred_element_type=f32)

    @pl.when(k == nk - 1)
    def _():
        finish([a[...] for a in acc_refs])


def _matmul(x, w, layer, w_col_blocks, n_out, *, tm, tn, tk, epilogue, extras=(), out_dtype, name,
            resident_w=False):
    m, kdim = x.shape
    nk = kdim // tk
    n_w = len(w_col_blocks)
    nb = n_out // tn
    in_specs = [pl.BlockSpec((tm, tk), lambda i, j, k: (i, k))]
    args = [x]
    w_mode = pl.Buffered(1) if resident_w else None
    for cb in w_col_blocks:
        in_specs.append(pl.BlockSpec((None, tk, tn), lambda i, j, k, cb=cb: (layer, k, cb * nb + j),
                                     pipeline_mode=w_mode))
        args.append(w)
    for arr, kind in extras:
        if kind == "row":
            in_specs.append(pl.BlockSpec((1, tn), lambda i, j, k: (0, j)))
        else:
            in_specs.append(pl.BlockSpec((tm, tn), lambda i, j, k: (i, j)))
        args.append(arr)
    scratch = [pltpu.VMEM((tm, tn), f32) for _ in range(n_w)] if nk > 1 else []
    multi = isinstance(out_dtype, tuple)
    dtypes = out_dtype if multi else (out_dtype,)
    out_spec = pl.BlockSpec((tm, tn), lambda i, j, k: (i, j))
    outs = pl.pallas_call(
        functools.partial(_mm_kernel, n_w=n_w, n_e=len(extras), n_o=len(dtypes), nk=nk, epilogue=epilogue),
        grid=(m // tm, nb, nk),
        in_specs=in_specs,
        out_specs=[out_spec] * len(dtypes),
        out_shape=[jax.ShapeDtypeStruct((m, n_out), dt) for dt in dtypes],
        scratch_shapes=scratch,
        compiler_params=pltpu.CompilerParams(
            dimension_semantics=("parallel", "parallel", "arbitrary"),
            vmem_limit_bytes=VMEM_LIMIT),
        name=name,
    )(*args)
    return tuple(outs) if multi else outs[0]


def _epi_plain(accs, extras):
    return accs[0]


def _epi_residual_norm(accs, extras):
    y = extras[0] + accs[0]
    return y, _rms(y, extras[1])


def _epi_residual_norm_only(accs, extras):
    return _rms(extras[0] + accs[0], extras[1])


def _epi_swiglu(accs, extras):
    gate, up = accs
    return gate * jax.nn.sigmoid(gate) * up


def _log_decay(y, w0):
    return -EXP_NEG_HALF * jax.nn.sigmoid(w0 + y)


def _pad_cols(w, n):
    return jnp.pad(w, ((0, 0), (0, n - w.shape[1])))


def _pad_rows(w, n):
    return jnp.pad(w, ((0, n - w.shape[0]), (0, 0)))


def _each(fn, *lists):
    return [fn(*xs) for xs in zip(*lists)]


def _wkv_chunk_maps(tiles, consts):
    m0, strict, incl, eye = consts
    c = WKV_CHUNK
    n2 = 2 * c
    r, lw, lg, k2, v, av, bv = (list(x) for x in zip(*tiles))
    stack = lambda x: _stack_heads(x, m0)
    lg_end = _each(lambda x: x[c - 1:c, :], lg)
    g_inv = _each(lambda x: jnp.exp(-x), lg)
    g_rem = _each(lambda e, x: jnp.exp(e - x), lg_end, lg)
    rs = _each(lambda x, l: stack(x * jnp.exp(l)), r, lg)
    as_ = _each(lambda x, l, w: stack(x * jnp.exp(l - w)).astype(bf16), av, lg, lw)
    bs = _each(lambda x, g: stack(x * g).astype(bf16), bv, g_inv)
    ks = _each(lambda x, g: stack(x * g).astype(bf16), k2, g_inv)
    bhs = _each(lambda x, g: stack(x * g).astype(bf16), bv, g_rem)
    khs = _each(lambda x, g: stack(x * g).astype(bf16), k2, g_rem)
    vs = _each(stack, v)
    sc = _each(lambda a, rr, b, k: _dot_nt(jnp.concatenate([a, rr.astype(bf16)], axis=0),
                                           jnp.concatenate([b, k], axis=0)), as_, rs, bs, ks)
    zero = jnp.zeros((n2, n2), f32)
    yield
    a_ab = _each(lambda s: jnp.where(strict, s[:n2, :n2], zero), sc)
    a_ak = _each(lambda s: jnp.where(strict, s[:n2, n2:], zero), sc)
    a_r = _each(lambda s: jnp.where(jnp.concatenate([incl, incl], axis=1), s[n2:, :],
                                    jnp.zeros((n2, 2 * n2), f32)).astype(bf16), sc)
    side = lambda a, b: jnp.concatenate([a, b], axis=1)
    t = _each(lambda x: eye + x, a_ab)
    p = _each(lambda x: _dot(x, x), a_ab)
    yield
    for _ in range(c.bit_length() - 3):
        pt = _each(lambda pp, tt: _dot(pp, side(pp, tt)), p, t)
        yield
        p = _each(lambda x: x[:, :n2], pt)
        t = _each(lambda tt, x: tt + x[:, n2:], t, pt)
    t = _each(lambda tt, pp: (tt + _dot(pp, tt)).astype(bf16), t, p)
    yield
    akv = _each(_dot, a_ak, vs)
    yield
    hw = _each(lambda tt, a, x: _dot(tt, side(a, x)), t, as_, akv)
    yield
    ah = _each(lambda x: x[:, :LANES], hw)
    ws = _each(lambda x: x[:, LANES:], hw)
    pq = _each(lambda ar, x, vv: _dot(ar, jnp.concatenate([x, side(jnp.zeros_like(vv), vv)], axis=0)),
               a_r, hw, vs)
    yield
    pc = _each(lambda x, y: x + y[:, :LANES], rs, pq)
    qc = _each(lambda y: y[:, LANES:], pq)
    gm = _each(lambda h, b: _dot(h.T, b), ah, bhs)
    yield
    nc = _each(lambda w, x, b, k: _dot(jnp.concatenate([w, x], axis=0).T, jnp.concatenate([b, k], axis=0)),
               ws, vs, bhs, khs)
    decay = _each(jnp.exp, lg_end)
    return pc, qc, gm, nc, decay


def _drive(staged, between=()):
    pending = list(between)
    while True:
        try:
            next(staged)
        except StopIteration as done:
            for fn in pending:
                if fn is not None:
                    fn()
            return done.value
        if pending:
            fn = pending.pop(0)
            if fn is not None:
                fn()


def _wkv_kernel(r_ref, k_ref, v_ref, zw_ref, za_ref, zg_ref, w2_ref, a2_ref, g2_ref, w0_ref, a0_ref,
                kk_ref, ka_ref, rk_ref, lng_ref, lnb_ref, o_ref, s_ref):
    c = WKV_CHUNK
    n2 = 2 * c
    up = lambda z_ref, w_ref: jnp.dot(z_ref[...], w_ref[...], preferred_element_type=f32)
    lw_all = _log_decay(up(zw_ref, w2_ref), w0_ref[...])
    a_all = jax.nn.sigmoid(a0_ref[...] + up(za_ref, a2_ref))
    g_all = up(zg_ref, g2_ref)

    @pl.when(pl.program_id(2) == 0)
    def _():
        s_ref[...] = jnp.zeros_like(s_ref)

    ri = lax.broadcasted_iota(jnp.int32, (n2, n2), 0)
    ci = lax.broadcasted_iota(jnp.int32, (n2, n2), 1)
    tr = lax.broadcasted_iota(jnp.int32, (c, c), 0)
    tc = lax.broadcasted_iota(jnp.int32, (c, c), 1)
    consts = (_first_head_mask(), (ri & (c - 1)) > (ci & (c - 1)), (ri & (c - 1)) >= (ci & (c - 1)),
              (ri == ci).astype(f32))
    tri = (tr >= tc).astype(bf16)
    m0 = consts[0]
    inv_n = 1.0 / HEAD_DIM

    npairs = WKV_LANES // LANES
    nchunk = WKV_TBLK // c
    lanes = [slice(p * LANES, (p + 1) * LANES) for p in range(npairs)]
    r = [r_ref[:, ln] for ln in lanes]
    v = [v_ref[:, ln] for ln in lanes]
    k2, av, bv, bonus = [], [], [], []
    for p, ln in enumerate(lanes):
        k, a = k_ref[:, ln], a_all[:, ln]
        kk = k * kk_ref[:, ln]
        ss = _head_sums_lanes(kk * kk, m0)
        kk = kk / jnp.maximum(jnp.sqrt(ss), 1e-12)
        k2.append(k * (1.0 + (a - 1.0) * ka_ref[:, ln]))
        av.append(-kk)
        bv.append(kk * a)
        bonus.append(_head_sums_lanes(r[p] * k2[p] * rk_ref[:, ln], m0))
    tiles = []
    for ic in range(nchunk):
        rows = slice(ic * c, (ic + 1) * c)
        lw = lw_all[rows, :]
        lg = _dot_sel_lhs(tri, lw)
        for p, ln in enumerate(lanes):
            tiles.append((r[p][rows], lw[:, ln], lg[:, ln], k2[p][rows], v[p][rows], av[p][rows], bv[p][rows]))
    s = [s_ref[p] for p in range(npairs)]
    ys = [[] for _ in range(npairs)]

    def advance(maps, j):
        pc, qc, gm, nc, decay = maps
        for p in range(npairs):
            i = j * npairs + p
            y = _dot_nt(pc[i], s[p]) + qc[i]
            ys[p].append(y[:c] + y[c:])
            s[p] = s[p] * decay[i] + _dot(s[p], gm[i]) + nc[i]

    per_group = nchunk // WKV_GROUPS
    slots = []
    for gi in range(WKV_GROUPS):
        maps = _drive(_wkv_chunk_maps(tiles[gi * per_group * npairs:(gi + 1) * per_group * npairs], consts), slots)
        slots = []
        for j in range(per_group):
            slots += [functools.partial(advance, maps, j), None, None]
    _drive(iter(()), slots)
    for p, ln in enumerate(lanes):
        s_ref[p] = s[p]
        y = jnp.concatenate(ys[p], axis=0)
        mu = _head_sums_lanes(y, m0) * inv_n
        yc = y - mu
        var = _head_sums_lanes(yc * yc, m0) * inv_n
        yn = yc * lax.rsqrt(var + GN_EPS) * lng_ref[:, ln] + lnb_ref[:, ln]
        o_ref[:, ln] = ((yn + bonus[p] * v[p]) * g_all[:, ln]).astype(o_ref.dtype)


def _wkv(r, k, v, zs, ups, w0, a0, k_k, k_a, r_k, lnx_g, lnx_b):
    m, d = r.shape
    tb = SEQ // WKV_TBLK
    tile = pl.BlockSpec((WKV_TBLK, WKV_LANES), lambda b, j, t: (b * tb + t, j))
    row = pl.BlockSpec((1, WKV_LANES), lambda b, j, t: (0, j))
    z_tile = lambda z: pl.BlockSpec((WKV_TBLK, z.shape[1]), lambda b, j, t: (b * tb + t, 0))
    up_tile = lambda w: pl.BlockSpec((w.shape[0], WKV_LANES), lambda b, j, t: (0, j))
    rows = [x.reshape(1, d) for x in (w0, a0, k_k, k_a, r_k, lnx_g, lnx_b)]
    return pl.pallas_call(
        _wkv_kernel,
        grid=(BATCH, d // WKV_LANES, tb),
        in_specs=[tile] * 3 + [z_tile(z) for z in zs] + [up_tile(w) for w in ups] + [row] * len(rows),
        out_specs=tile,
        out_shape=jax.ShapeDtypeStruct((m, d), bf16),
        scratch_shapes=[pltpu.VMEM((WKV_LANES // LANES, LANES, LANES), f32)],
        compiler_params=pltpu.CompilerParams(
            dimension_semantics=("parallel", "parallel", "arbitrary"),
            vmem_limit_bytes=VMEM_LIMIT),
        name="wkv7",
    )(r, k, v, *zs, *ups, *rows)


def _fox_gate_kernel(s_ref, bf_ref, cp_ref, sp_ref, run_ref, *, tc):
    @pl.when(pl.program_id(1) == 0)
    def _():
        run_ref[...] = jnp.zeros_like(run_ref)

    small = s_ref[...]
    ls = -_softplus(-(small + bf_ref[...]))
    tr = lax.broadcasted_iota(jnp.int32, (tc, tc), 0)
    tcc = lax.broadcasted_iota(jnp.int32, (tc, tc), 1)
    cs = _dot_sel_lhs((tr >= tcc).astype(bf16), ls) + run_ref[...]
    run_ref[...] = cs[tc - 1:tc, :]
    w = cs.shape[1]
    src = lax.broadcasted_iota(jnp.int32, (w, w), 0)
    dst = lax.broadcasted_iota(jnp.int32, (w, w), 1)
    sel = [((dst == N_SPLIT * src + n) & (src < N_HEADS)).astype(bf16) for n in range(N_SPLIT)]
    cp_ref[...] = jnp.dot(jnp.concatenate(_split3(cs * LOG2_E), axis=1), jnp.concatenate(sel, axis=0),
                          preferred_element_type=f32).astype(cp_ref.dtype)
    sp_ref[...] = jnp.concatenate(_split3(jax.nn.sigmoid(small))[:GATE_SPLIT], axis=1)


def _fox_gate(small, b_f, tc=512):
    m, w = small.shape
    nt = SEQ // tc
    bias = jnp.pad(b_f, (0, w - b_f.shape[0])).reshape(1, w)
    pieces = lambda n: pl.BlockSpec((tc, n * w), lambda b, t: (b * nt + t, 0))
    return pl.pallas_call(
        functools.partial(_fox_gate_kernel, tc=tc),
        grid=(BATCH, nt),
        in_specs=[pl.BlockSpec((tc, w), lambda b, t: (b * nt + t, 0)),
                  pl.BlockSpec((1, w), lambda b, t: (0, 0))],
        out_specs=[pieces(1), pieces(GATE_SPLIT)],
        out_shape=[jax.ShapeDtypeStruct((m, n * w), bf16) for n in (1, GATE_SPLIT)],
        scratch_shapes=[pltpu.VMEM((1, w), f32)],
        compiler_params=pltpu.CompilerParams(dimension_semantics=("parallel", "arbitrary"),
                                             vmem_limit_bytes=VMEM_LIMIT),
        name="fox_gate_cumsum",
    )(small, bias)


def _fox_prep_kernel(q_ref, k_ref, v_ref, kp_ref, vp_ref, s_ref, c_ref, qg_ref, kg_ref,
                     qo_ref, ko_ref, vo_ref, *, tm):
    i = pl.program_id(0)
    npairs = PREP_LANES // LANES
    head_ones = _head_ones()
    seq_start = (i * tm) % SEQ == 0
    rid = lax.broadcasted_iota(jnp.int32, (tm, 1), 0)
    r = lax.broadcasted_iota(jnp.int32, (LANES, 2 * LANES), 0)
    col = lax.broadcasted_iota(jnp.int32, (LANES, 2 * LANES), 1)
    small_pieces = s_ref[...]
    aug = c_ref[...]

    def shifted(x, prow):
        prow = jnp.where(seq_start, jnp.zeros_like(prow), prow)
        return jnp.where(rid == 0, prow, pltpu.roll(x, 1, 0))

    def head_rms(x, gain):
        ms = _head_sums(x * x, head_ones) * (1.0 / HEAD_DIM)
        return x * lax.rsqrt(ms + RMS_EPS) * gain

    for p in range(npairs):
        pair = pl.program_id(1) * npairs + p
        ln = slice(p * LANES, (p + 1) * LANES)
        src = (((col & (LANES - 1)) >> HEAD_SHIFT) + HEADS_PER_VREG * pair
               + N_HEADS * (1 + (col >> LANE_SHIFT)))
        sel = (r == src).astype(bf16)
        gates = jnp.dot(small_pieces, jnp.concatenate([sel] * GATE_SPLIT, axis=0), preferred_element_type=f32)
        ak = gates[:, :LANES]
        av = gates[:, LANES:]
        k = k_ref[:, ln]
        v = v_ref[:, ln]
        k = k + ak * (shifted(k, kp_ref[SUBLANES - 1:SUBLANES, ln]) - k)
        v = v + av * (shifted(v, vp_ref[SUBLANES - 1:SUBLANES, ln]) - v)
        qo_ref[:, ln] = head_rms(q_ref[:, ln], qg_ref[...] * (HEAD_DIM ** -0.5 * LOG2_E)).astype(qo_ref.dtype)
        ko_ref[:, 2 * p * LANES:2 * (p + 1) * LANES] = jnp.concatenate(
            [head_rms(k, kg_ref[...]).astype(ko_ref.dtype), aug], axis=1)
        vo_ref[ln, :] = v.T.astype(vo_ref.dtype)


def _fox_prep(proj, small_pieces, c_pieces, qn_g, kn_g, tm=1024):
    m = proj.shape[0]
    rb = tm // SUBLANES
    nb = D_MODEL // PREP_LANES
    tile = lambda cb: pl.BlockSpec((tm, PREP_LANES), lambda i, j, cb=cb: (i, cb * nb + j))
    prev = lambda cb: pl.BlockSpec(
        (SUBLANES, PREP_LANES), lambda i, j, cb=cb: (jnp.maximum(i * rb - 1, 0), cb * nb + j))
    gain = pl.BlockSpec((1, LANES), lambda i, j: (0, 0))
    small_tile = lambda a: pl.BlockSpec((tm, a.shape[1]), lambda i, j: (i, 0))
    tile_gain = lambda x: jnp.tile(x, HEADS_PER_VREG).reshape(1, LANES)
    return pl.pallas_call(
        functools.partial(_fox_prep_kernel, tm=tm),
        grid=(m // tm, nb),
        in_specs=[tile(0), tile(1), tile(2), prev(1), prev(2), small_tile(small_pieces), small_tile(c_pieces),
                  gain, gain],
        out_specs=[pl.BlockSpec((tm, PREP_LANES), lambda i, j: (i, j)),
                   pl.BlockSpec((tm, 2 * PREP_LANES), lambda i, j: (i, j)),
                   pl.BlockSpec((PREP_LANES, tm), lambda i, j: (j, i))],
        out_shape=[jax.ShapeDtypeStruct((m, D_MODEL), bf16),
                   jax.ShapeDtypeStruct((m, 2 * D_MODEL), bf16),
                   jax.ShapeDtypeStruct((D_MODEL, m), bf16)],
        compiler_params=pltpu.CompilerParams(dimension_semantics=("parallel", "parallel"),
                                             vmem_limit_bytes=VMEM_LIMIT),
        name="fox_prep",
    )(proj, proj, proj, proj, proj, small_pieces, c_pieces, tile_gain(qn_g), tile_gain(kn_g))


def _fox_attn_kernel(q_ref, k_ref, vt_ref, gate_ref, og_ref, o_ref):
    tq, tk = ATT_TQ, ATT_TK
    qi = pl.program_id(2)
    npairs = ATT_LANES // LANES
    chains = [(p, h) for p in range(npairs) for h in range(HEADS_PER_VREG)]
    lane = lax.broadcasted_iota(jnp.int32, (tq, LANES), 1)
    q_aug = []
    for p, h in chains:
        q = q_ref[:, p * LANES:(p + 1) * LANES]
        own = (lane >> HEAD_SHIFT) == h
        head = (pl.program_id(1) * npairs + p) * HEADS_PER_VREG + h
        minus_one = (lane >= N_SPLIT * head) & (lane < N_SPLIT * (head + 1))
        q_aug.append(jnp.concatenate([jnp.where(own, q, jnp.zeros_like(q)),
                                      jnp.where(minus_one, -1.0, 0.0).astype(bf16)], axis=1))

    def step(key0, width, carry, diag):
        m_run, l_run, acc = carry
        keys = pl.ds(pl.multiple_of(key0, tq), width)
        s = [_dot_nt(k_ref[keys, 2 * p * LANES:2 * (p + 1) * LANES], qa)
             for (p, h), qa in zip(chains, q_aug)]
        if diag:
            kidx = key0 + lax.broadcasted_iota(jnp.int32, (width, tq), 0)
            qidx = qi * tq + lax.broadcasted_iota(jnp.int32, (width, tq), 1)
            s = [jnp.where(qidx >= kidx, x, NEG_BIG) for x in s]
        m_new = [jnp.maximum(mr, jnp.max(x, axis=0, keepdims=True)) for mr, x in zip(m_run, s)]
        alpha = [jnp.exp2(mr - mn) for mr, mn in zip(m_run, m_new)]
        pr = [jnp.exp2(x - mn) for x, mn in zip(s, m_new)]
        l_new = [a * lr + jnp.sum(x, axis=0, keepdims=True) for a, lr, x in zip(alpha, l_run, pr)]
        pv = [jnp.dot(vt_ref[pl.ds((p * HEADS_PER_VREG + h) * HEAD_DIM, HEAD_DIM), keys], x.astype(bf16),
                      preferred_element_type=f32) for (p, h), x in zip(chains, pr)]
        acc = [ac * a + x for ac, a, x in zip(acc, alpha, pv)]
        return m_new, l_new, acc

    n = len(chains)
    init = ([jnp.full((1, tq), NEG_BIG, f32)] * n, [jnp.zeros((1, tq), f32)] * n,
            [jnp.zeros((HEAD_DIM, tq), f32)] * n)
    n_full = (qi * tq) // tk
    carry = lax.fori_loop(0, n_full, lambda j, cy: step(j * tk, tk, cy, False), init)
    _, l_run, acc = lax.cond((qi * tq) % tk == 0,
                             lambda cy: step(n_full * tk, tq, cy, True),
                             lambda cy: step(n_full * tk, tk, cy, True), carry)
    o_t = []
    for ac, lr in zip(acc, l_run):
        o = ac / lr
        o_t.append(o * lax.rsqrt(jnp.mean(o * o, axis=0, keepdims=True) + RMS_EPS))
    for p in range(npairs):
        ln = slice(p * LANES, (p + 1) * LANES)
        o = jnp.concatenate(o_t[HEADS_PER_VREG * p:HEADS_PER_VREG * (p + 1)], axis=0).T
        o_ref[:, ln] = (o * og_ref[:, ln] * jax.nn.sigmoid(gate_ref[:, ln])).astype(o_ref.dtype)


def _fox_attn(q, k_aug, v_t, proj, on_g):
    m, d = q.shape
    nq = SEQ // ATT_TQ
    gate_col0 = 3 * D_MODEL // ATT_LANES
    return pl.pallas_call(
        _fox_attn_kernel,
        grid=(BATCH, d // ATT_LANES, nq),
        in_specs=[pl.BlockSpec((ATT_TQ, ATT_LANES), lambda b, p, i: (b * nq + i, p)),
                  pl.BlockSpec((SEQ, 2 * ATT_LANES), lambda b, p, i: (b, p)),
                  pl.BlockSpec((ATT_LANES, SEQ), lambda b, p, i: (p, b)),
                  pl.BlockSpec((ATT_TQ, ATT_LANES), lambda b, p, i: (b * nq + i, gate_col0 + p)),
                  pl.BlockSpec((1, ATT_LANES), lambda b, p, i: (0, p))],
        out_specs=pl.BlockSpec((ATT_TQ, ATT_LANES), lambda b, p, i: (b * nq + i, p)),
        out_shape=jax.ShapeDtypeStruct((m, d), bf16),
        compiler_params=pltpu.CompilerParams(
            dimension_semantics=("parallel", "parallel", "arbitrary"),
            vmem_limit_bytes=VMEM_LIMIT),
        name="fox_attention",
    )(q, k_aug, v_t, proj, on_g.reshape(1, d))


def _swiglu_block(x, hn, w_gu, w_d, layer, next_norm_g, last):
    act = _matmul(hn, w_gu, layer, (0, 1), D_FF, tm=2048, tn=512, tk=D_MODEL,
                  epilogue=_epi_swiglu, out_dtype=bf16, name="swiglu_gate_up")
    return _matmul(act, w_d, layer, (0,), D_MODEL, tm=256, tn=D_MODEL, tk=D_FF, resident_w=True,
                   epilogue=_epi_residual_norm_only if last else _epi_residual_norm,
                   extras=((x, "tile"), (next_norm_g.reshape(1, D_MODEL), "row")),
                   out_dtype=f32 if last else (f32, bf16), name="swiglu_down")


def _proj(x, w, layer=0, *, name):
    if w.ndim == 2:
        w = w[None]
    n = w.shape[2]
    resident = n <= D_MODEL
    return _matmul(x, w, layer, (0,), n, tm=1024 if resident else 2048, tn=n if resident else 1024,
                   tk=x.shape[1], epilogue=_epi_plain, out_dtype=f32, name=name, resident_w=resident)


def _out_proj_norm(y, w_o, x, next_norm_g, *, name):
    return _matmul(y, w_o.astype(bf16)[None], 0, (0,), D_MODEL, tm=512, tn=D_MODEL, tk=D_MODEL,
                   epilogue=_epi_residual_norm, extras=((x, "tile"), (next_norm_g.reshape(1, D_MODEL), "row")),
                   out_dtype=(f32, bf16), name=name)


def _rwkv7_block(x, norm_g, mix, w_rkv, w0, w1, w2, a0, a1, a2, g1, g2, k_k, k_a, r_k, lnx_g, lnx_b, w_o,
                 next_norm_g):
    rank = LANES * pl.cdiv(w1.shape[1], LANES)
    downs = (_pad_cols(w1, rank).astype(bf16), _pad_cols(a1, rank).astype(bf16), g1.astype(bf16))
    ups = (_pad_rows(w2, rank).astype(bf16), _pad_rows(a2, rank).astype(bf16), g2.astype(bf16))
    xr, xk, xv, *zs = _norm_mix(x, norm_g, mix, *downs)
    w_rkv = w_rkv.astype(bf16)
    r = _proj(xr, w_rkv, 0, name="rwkv_r")
    k = _proj(xk, w_rkv, 1, name="rwkv_k")
    v = _proj(xv, w_rkv, 2, name="rwkv_v")
    yg = _wkv(r, k, v, zs, ups, w0, a0, k_k, k_a, r_k.reshape(-1), lnx_g, lnx_b)
    return _out_proj_norm(yg, w_o, x, next_norm_g, name="rwkv_out")


def _fox_block(x, hn, w_in, b_f, qn_g, kn_g, on_g, w_o, next_norm_g):
    n_main = 4 * D_MODEL
    proj = _proj(hn, w_in[:, :n_main].astype(bf16), name="fox_in")
    w_small = _pad_cols(w_in[:, n_main:], LANES).astype(bf16)
    small = _proj(hn, w_small, name="fox_in_gates")
    c_pieces, small_pieces = _fox_gate(small, b_f)
    q, k_aug, v_t = _fox_prep(proj, small_pieces, c_pieces, qn_g, kn_g)
    og = _fox_attn(q, k_aug, v_t, proj, on_g)
    return _out_proj_norm(og, w_o, x, next_norm_g, name="fox_out")


def kernel(x, a_norm_g, a_mix, a_w_rkv, a_w0, a_w1, a_w2, a_a0, a_a1, a_a2, a_g1, a_g2, a_k_k, a_k_a, a_r_k, a_lnx_g, a_lnx_b, a_w_o, b_norm_g, b_w_in, b_b_f, b_qn_g, b_kn_g, b_on_g, b_w_o, f_norm_g, f_w_gu, f_w_d, final_g):
    b, t, d = x.shape
    h = x.reshape(b * t, d)
    w_d = f_w_d.astype(bf16)
    h = _rwkv7_block(h, a_norm_g[0], a_mix[0], a_w_rkv[0], a_w0[0], a_w1[0], a_w2[0], a_a0[0], a_a1[0],
                     a_a2[0], a_g1[0], a_g2[0], a_k_k[0], a_k_a[0], a_r_k[0], a_lnx_g[0], a_lnx_b[0],
                     a_w_o[0], f_norm_g[0])
    h = _swiglu_block(*h, f_w_gu, w_d, 0, b_norm_g[0], False)
    h = _fox_block(*h, b_w_in[0], b_b_f[0], b_qn_g[0], b_kn_g[0], b_on_g[0], b_w_o[0],
                   f_norm_g[1])
    return _swiglu_block(*h, f_w_gu, w_d, 1, final_g, True).reshape(b, t, d)
```

```python
import functools

import jax
import jax.numpy as jnp
from jax import lax
from jax.experimental import pallas as pl
from jax.experimental.pallas import tpu as pltpu

D_MODEL = 2048
BATCH = 8
SEQ = 2048
N_TOK = BATCH * SEQ
HEAD_DIM = 64
HEAD_SHIFT = HEAD_DIM.bit_length() - 1
N_HEADS = D_MODEL // HEAD_DIM
D_FF = 5632
RMS_EPS = 1e-6
GN_EPS = 64e-5

LANES = 128
LANE_SHIFT = LANES.bit_length() - 1
SUBLANES = 8
PREV_ROWS = 2 * SUBLANES
HEADS_PER_VREG = LANES // HEAD_DIM
V7X_VMEM_BYTES = 64 * 1024 * 1024
VMEM_LIMIT = V7X_VMEM_BYTES * 7 // 8

NORM_MIX_DOT_ROWS = 128
WKV_CHUNK = 64
WKV_TBLK = 512
WKV_GROUPS = 2
WKV_LANES = 512
ATT_TQ = 256
ATT_TK = 512
ATT_LANES = 1024
PREP_LANES = 512
N_SPLIT = 3
GATE_SPLIT = 2
NEG_BIG = -1e30
LOG2_E = 1.4426950408889634
EXP_NEG_HALF = 0.6065306597126334

f32 = jnp.float32
bf16 = jnp.bfloat16


def _dot(a, b):
    return jnp.dot(a.astype(bf16), b.astype(bf16), preferred_element_type=f32)


def _dot_nt(a, b):
    return lax.dot_general(a.astype(bf16), b.astype(bf16), (((1,), (1,)), ((), ())),
                           preferred_element_type=f32)


def _split3(x):
    hi = x.astype(bf16)
    r1 = x - hi.astype(f32)
    mid = r1.astype(bf16)
    lo = (r1 - mid.astype(f32)).astype(bf16)
    return hi, mid, lo


def _dot_sel_lhs(sel, x):
    hi, mid, lo = _split3(x)
    d = lambda p: jnp.dot(sel, p, preferred_element_type=f32)
    return d(hi) + d(mid) + d(lo)


def _head_sums(x, head_ones):
    hi = x.astype(bf16)
    lo = (x - hi.astype(f32)).astype(bf16)
    return jnp.dot(jnp.concatenate([hi, lo], axis=1), jnp.concatenate([head_ones, head_ones], axis=0),
                   preferred_element_type=f32)


def _head_sums_lanes(x, m0):
    zero = jnp.zeros_like(x)
    s0 = jnp.sum(jnp.where(m0, x, zero), axis=-1, keepdims=True)
    s1 = jnp.sum(jnp.where(m0, zero, x), axis=-1, keepdims=True)
    return jnp.where(m0, s0, s1)


def _head_ones():
    r = lax.broadcasted_iota(jnp.int32, (LANES, LANES), 0) >> HEAD_SHIFT
    c = lax.broadcasted_iota(jnp.int32, (LANES, LANES), 1) >> HEAD_SHIFT
    return (r == c).astype(bf16)


def _first_head_mask():
    return lax.broadcasted_iota(jnp.int32, (1, LANES), 1) < HEAD_DIM


def _stack_heads(x, m0):
    z = jnp.zeros_like(x)
    return jnp.concatenate([jnp.where(m0, x, z), jnp.where(m0, z, x)], axis=0)


def _softplus(z):
    return jnp.maximum(z, 0.0) + jnp.log(1.0 + jnp.exp(-jnp.abs(z)))


def _rms(x, g):
    return x * lax.rsqrt(jnp.mean(x * x, axis=-1, keepdims=True) + RMS_EPS) * g


def _norm_mix_kernel(x_ref, xp_ref, g_ref, mix_ref, w1_ref, a1_ref, g1_ref,
                     xr_ref, xk_ref, xv_ref, zw_ref, za_ref, zg_ref, xw_s, xa_s, xg_s, *, tm):
    i = pl.program_id(0)
    d = x_ref.shape[1]
    rows_per_chunk = 2 * SUBLANES
    lane_blk = 4 * LANES
    seq_start = (i * tm) % SEQ == 0
    hp_row = _rms(xp_ref[...], g_ref[...])[SUBLANES - 1:SUBLANES, :]
    hp_row = jnp.where(seq_start, jnp.zeros_like(hp_row), hp_row)
    rid = lax.broadcasted_iota(jnp.int32, (rows_per_chunk, 1), 0)
    mix_dsts = (xr_ref, xk_ref, xv_ref, xw_s, xa_s, xg_s)

    def chunk(c, last_row):
        rows = slice(c * rows_per_chunk, (c + 1) * rows_per_chunk)
        x = x_ref[rows, :]
        inv = lax.rsqrt(jnp.mean(x * x, axis=-1, keepdims=True) + RMS_EPS)
        new_last = []
        for b in range(d // lane_blk):
            ln = slice(b * lane_blk, (b + 1) * lane_blk)
            h = x[:, ln] * inv * g_ref[:, ln]
            hprev = jnp.where(rid == 0, last_row[:, ln], pltpu.roll(h, 1, 0))
            xx = hprev - h
            for p, dst in enumerate(mix_dsts):
                dst[rows, ln] = (h + xx * mix_ref[p:p + 1, ln]).astype(dst.dtype)
            new_last.append(h[rows_per_chunk - 1:rows_per_chunk, :])
        return jnp.concatenate(new_last, axis=1)

    chunks_per_blk = NORM_MIX_DOT_ROWS // rows_per_chunk
    last_row = hp_row
    for blk in range(tm // NORM_MIX_DOT_ROWS):
        for c in range(blk * chunks_per_blk, (blk + 1) * chunks_per_blk):
            last_row = chunk(c, last_row)
        rows = slice(blk * NORM_MIX_DOT_ROWS, (blk + 1) * NORM_MIX_DOT_ROWS)
        down = lambda xs, w: jnp.dot(xs[rows, :], w[...], preferred_element_type=f32)
        zw_ref[rows, :] = jnp.tanh(down(xw_s, w1_ref)).astype(zw_ref.dtype)
        za_ref[rows, :] = down(xa_s, a1_ref).astype(za_ref.dtype)
        zg_ref[rows, :] = jax.nn.sigmoid(down(xg_s, g1_ref)).astype(zg_ref.dtype)


def _norm_mix(x, g, mix, w1, a1, g1, tm=512):
    m, d = x.shape
    rb = tm // SUBLANES
    row_tile = lambda n: pl.BlockSpec((tm, n), lambda i: (i, 0))
    whole = lambda a: pl.BlockSpec(a.shape, lambda i: (0, 0))
    lows = (w1, a1, g1)
    return pl.pallas_call(
        functools.partial(_norm_mix_kernel, tm=tm),
        grid=(m // tm,),
        in_specs=[row_tile(d),
                  pl.BlockSpec((SUBLANES, d), lambda i: (jnp.maximum(i * rb - 1, 0), 0)),
                  pl.BlockSpec((1, d), lambda i: (0, 0)),
                  pl.BlockSpec((6, d), lambda i: (0, 0))] + [whole(w) for w in lows],
        out_specs=[row_tile(d)] * 3 + [row_tile(w.shape[1]) for w in lows],
        out_shape=[jax.ShapeDtypeStruct((m, d), bf16)] * 3
        + [jax.ShapeDtypeStruct((m, w.shape[1]), bf16) for w in lows],
        scratch_shapes=[pltpu.VMEM((tm, d), bf16)] * 3,
        compiler_params=pltpu.CompilerParams(dimension_semantics=("parallel",),
                                             vmem_limit_bytes=VMEM_LIMIT),
        name="norm_mix",
    )(x, x, g.reshape(1, d), mix, *lows)


def _mm_kernel(*refs, n_w, n_e, n_o, nk, epilogue):
    x_ref = refs[0]
    w_refs = refs[1:1 + n_w]
    e_refs = refs[1 + n_w:1 + n_w + n_e]
    o_refs = refs[1 + n_w + n_e:1 + n_w + n_e + n_o]
    acc_refs = refs[1 + n_w + n_e + n_o:]
    x = x_ref[...]

    def finish(accs):
        outs = epilogue(accs, [e[...] for e in e_refs])
        for o_ref, out in zip(o_refs, outs if isinstance(outs, tuple) else (outs,)):
            o_ref[...] = out.astype(o_ref.dtype)

    if nk == 1:
        finish([jnp.dot(x, w[...].astype(x.dtype), preferred_element_type=f32) for w in w_refs])
        return
    k = pl.program_id(2)

    @pl.when(k == 0)
    def _():
        for a in acc_refs:
            a[...] = jnp.zeros_like(a)

    for a, w in zip(acc_refs, w_refs):
        a[...] += jnp.dot(x, w[...], preferred_element_type=f32)

    @pl.when(k == nk - 1)
    def _():
        finish([a[...] for a in acc_refs])


def _matmul(x, w, layer, w_col_blocks, n_out, *, tm, tn, tk, epilogue, extras=(), out_dtype, name,
            resident_w=False):
    m, kdim = x.shape
    nk = kdim // tk
    n_w = len(w_col_blocks)
    nb = n_out // tn
    in_specs = [pl.BlockSpec((tm, tk), lambda i, j, k: (i, k))]
    args = [x]
    w_mode = pl.Buffered(1) if resident_w else None
    for cb in w_col_blocks:
        in_specs.append(pl.BlockSpec((None, tk, tn), lambda i, j, k, cb=cb: (layer, k, cb * nb + j),
                                     pipeline_mode=w_mode))
        args.append(w)
    for arr, kind in extras:
        if kind == "row":
            in_specs.append(pl.BlockSpec((1, tn), lambda i, j, k: (0, j)))
        else:
            in_specs.append(pl.BlockSpec((tm, tn), lambda i, j, k: (i, j)))
        args.append(arr)
    scratch = [pltpu.VMEM((tm, tn), f32) for _ in range(n_w)] if nk > 1 else []
    multi = isinstance(out_dtype, tuple)
    dtypes = out_dtype if multi else (out_dtype,)
    out_spec = pl.BlockSpec((tm, tn), lambda i, j, k: (i, j))
    outs = pl.pallas_call(
        functools.partial(_mm_kernel, n_w=n_w, n_e=len(extras), n_o=len(dtypes), nk=nk, epilogue=epilogue),
        grid=(m // tm, nb, nk),
        in_specs=in_specs,
        out_specs=[out_spec] * len(dtypes),
        out_shape=[jax.ShapeDtypeStruct((m, n_out), dt) for dt in dtypes],
        scratch_shapes=scratch,
        compiler_params=pltpu.CompilerParams(
            dimension_semantics=("parallel", "parallel", "arbitrary"),
            vmem_limit_bytes=VMEM_LIMIT),
        name=name,
    )(*args)
    return tuple(outs) if multi else outs[0]


def _epi_plain(accs, extras):
    return accs[0]


def _epi_residual_norm(accs, extras):
    y = extras[0] + accs[0]
    return y, _rms(y, extras[1])


def _epi_residual_norm_only(accs, extras):
    return _rms(extras[0] + accs[0], extras[1])


def _epi_swiglu(accs, extras):
    gate, up = accs
    return gate * jax.nn.sigmoid(gate) * up


def _log_decay(y, w0):
    return -EXP_NEG_HALF * jax.nn.sigmoid(w0 + y)


def _pad_cols(w, n):
    return jnp.pad(w, ((0, 0), (0, n - w.shape[1])))


def _pad_rows(w, n):
    return jnp.pad(w, ((0, n - w.shape[0]), (0, 0)))


def _each(fn, *lists):
    return [fn(*xs) for xs in zip(*lists)]


def _wkv_chunk_maps(tiles, consts):
    m0, strict, incl, eye = consts
    c = WKV_CHUNK
    n2 = 2 * c
    r, lw, lg, k2, v, av, bv = (list(x) for x in zip(*tiles))
    stack = lambda x: _stack_heads(x, m0)
    lg_end = _each(lambda x: x[c - 1:c, :], lg)
    g_inv = _each(lambda x: jnp.exp(-x), lg)
    g_rem = _each(lambda e, x: jnp.exp(e - x), lg_end, lg)
    rs = _each(lambda x, l: stack(x * jnp.exp(l)), r, lg)
    as_ = _each(lambda x, l, w: stack(x * jnp.exp(l - w)).astype(bf16), av, lg, lw)
    bs = _each(lambda x, g: stack(x * g).astype(bf16), bv, g_inv)
    ks = _each(lambda x, g: stack(x * g).astype(bf16), k2, g_inv)
    bhs = _each(lambda x, g: stack(x * g).astype(bf16), bv, g_rem)
    khs = _each(lambda x, g: stack(x * g).astype(bf16), k2, g_rem)
    vs = _each(stack, v)
    sc = _each(lambda a, rr, b, k: _dot_nt(jnp.concatenate([a, rr.astype(bf16)], axis=0),
                                           jnp.concatenate([b, k], axis=0)), as_, rs, bs, ks)
    zero = jnp.zeros((n2, n2), f32)
    yield
    a_ab = _each(lambda s: jnp.where(strict, s[:n2, :n2], zero), sc)
    a_ak = _each(lambda s: jnp.where(strict, s[:n2, n2:], zero), sc)
    a_r = _each(lambda s: jnp.where(jnp.concatenate([incl, incl], axis=1), s[n2:, :],
                                    jnp.zeros((n2, 2 * n2), f32)).astype(bf16), sc)
    side = lambda a, b: jnp.concatenate([a, b], axis=1)
    t = _each(lambda x: eye + x, a_ab)
    p = _each(lambda x: _dot(x, x), a_ab)
    yield
    for _ in range(c.bit_length() - 3):
        pt = _each(lambda pp, tt: _dot(pp, side(pp, tt)), p, t)
        yield
        p = _each(lambda x: x[:, :n2], pt)
        t = _each(lambda tt, x: tt + x[:, n2:], t, pt)
    t = _each(lambda tt, pp: (tt + _dot(pp, tt)).astype(bf16), t, p)
    yield
    akv = _each(_dot, a_ak, vs)
    yield
    hw = _each(lambda tt, a, x: _dot(tt, side(a, x)), t, as_, akv)
    yield
    ah = _each(lambda x: x[:, :LANES], hw)
    ws = _each(lambda x: x[:, LANES:], hw)
    pq = _each(lambda ar, x, vv: _dot(ar, jnp.concatenate([x, side(jnp.zeros_like(vv), vv)], axis=0)),
               a_r, hw, vs)
    yield
    pc = _each(lambda x, y: x + y[:, :LANES], rs, pq)
    qc = _each(lambda y: y[:, LANES:], pq)
    gm = _each(lambda h, b: _dot(h.T, b), ah, bhs)
    yield
    nc = _each(lambda w, x, b, k: _dot(jnp.concatenate([w, x], axis=0).T, jnp.concatenate([b, k], axis=0)),
               ws, vs, bhs, khs)
    decay = _each(jnp.exp, lg_end)
    return pc, qc, gm, nc, decay


def _drive(staged, between=()):
    pending = list(between)
    while True:
        try:
            next(staged)
        except StopIteration as done:
            for fn in pending:
                if fn is not None:
                    fn()
            return done.value
        if pending:
            fn = pending.pop(0)
            if fn is not None:
                fn()


def _wkv_kernel(r_ref, k_ref, v_ref, zw_ref, za_ref, zg_ref, w2_ref, a2_ref, g2_ref, w0_ref, a0_ref,
                kk_ref, ka_ref, rk_ref, lng_ref, lnb_ref, o_ref, s_ref):
    c = WKV_CHUNK
    n2 = 2 * c
    up = lambda z_ref, w_ref: jnp.dot(z_ref[...], w_ref[...], preferred_element_type=f32)
    lw_all = _log_decay(up(zw_ref, w2_ref), w0_ref[...])
    a_all = jax.nn.sigmoid(a0_ref[...] + up(za_ref, a2_ref))
    g_all = up(zg_ref, g2_ref)

    @pl.when(pl.program_id(2) == 0)
    def _():
        s_ref[...] = jnp.zeros_like(s_ref)

    ri = lax.broadcasted_iota(jnp.int32, (n2, n2), 0)
    ci = lax.broadcasted_iota(jnp.int32, (n2, n2), 1)
    tr = lax.broadcasted_iota(jnp.int32, (c, c), 0)
    tc = lax.broadcasted_iota(jnp.int32, (c, c), 1)
    consts = (_first_head_mask(), (ri & (c - 1)) > (ci & (c - 1)), (ri & (c - 1)) >= (ci & (c - 1)),
              (ri == ci).astype(f32))
    tri = (tr >= tc).astype(bf16)
    m0 = consts[0]
    inv_n = 1.0 / HEAD_DIM

    npairs = WKV_LANES // LANES
    nchunk = WKV_TBLK // c
    lanes = [slice(p * LANES, (p + 1) * LANES) for p in range(npairs)]
    r = [r_ref[:, ln] for ln in lanes]
    v = [v_ref[:, ln] for ln in lanes]
    k2, av, bv, bonus = [], [], [], []
    for p, ln in enumerate(lanes):
        k, a = k_ref[:, ln], a_all[:, ln]
        kk = k * kk_ref[:, ln]
        ss = _head_sums_lanes(kk * kk, m0)
        kk = kk / jnp.maximum(jnp.sqrt(ss), 1e-12)
        k2.append(k * (1.0 + (a - 1.0) * ka_ref[:, ln]))
        av.append(-kk)
        bv.append(kk * a)
        bonus.append(_head_sums_lanes(r[p] * k2[p] * rk_ref[:, ln], m0))
    tiles = []
    for ic in range(nchunk):
        rows = slice(ic * c, (ic + 1) * c)
        lw = lw_all[rows, :]
        lg = _dot_sel_lhs(tri, lw)
        for p, ln in enumerate(lanes):
            tiles.append((r[p][rows], lw[:, ln], lg[:, ln], k2[p][rows], v[p][rows], av[p][rows], bv[p][rows]))
    s = [s_ref[p] for p in range(npairs)]
    ys = [[] for _ in range(npairs)]

    def advance(maps, j):
        pc, qc, gm, nc, decay = maps
        for p in range(npairs):
            i = j * npairs + p
            y = _dot_nt(pc[i], s[p]) + qc[i]
            ys[p].append(y[:c] + y[c:])
            s[p] = s[p] * decay[i] + _dot(s[p], gm[i]) + nc[i]

    per_group = nchunk // WKV_GROUPS
    slots = []
    for gi in range(WKV_GROUPS):
        maps = _drive(_wkv_chunk_maps(tiles[gi * per_group * npairs:(gi + 1) * per_group * npairs], consts), slots)
        slots = []
        for j in range(per_group):
            slots += [functools.partial(advance, maps, j), None, None]
    _drive(iter(()), slots)
    for p, ln in enumerate(lanes):
        s_ref[p] = s[p]
        y = jnp.concatenate(ys[p], axis=0)
        mu = _head_sums_lanes(y, m0) * inv_n
        yc = y - mu
        var = _head_sums_lanes(yc * yc, m0) * inv_n
        yn = yc * lax.rsqrt(var + GN_EPS) * lng_ref[:, ln] + lnb_ref[:, ln]
        o_ref[:, ln] = ((yn + bonus[p] * v[p]) * g_all[:, ln]).astype(o_ref.dtype)


def _wkv(r, k, v, zs, ups, w0, a0, k_k, k_a, r_k, lnx_g, lnx_b):
    m, d = r.shape
    tb = SEQ // WKV_TBLK
    tile = pl.BlockSpec((WKV_TBLK, WKV_LANES), lambda b, j, t: (b * tb + t, j))
    row = pl.BlockSpec((1, WKV_LANES), lambda b, j, t: (0, j))
    z_tile = lambda z: pl.BlockSpec((WKV_TBLK, z.shape[1]), lambda b, j, t: (b * tb + t, 0))
    up_tile = lambda w: pl.BlockSpec((w.shape[0], WKV_LANES), lambda b, j, t: (0, j))
    rows = [x.reshape(1, d) for x in (w0, a0, k_k, k_a, r_k, lnx_g, lnx_b)]
    return pl.pallas_call(
        _wkv_kernel,
        grid=(BATCH, d // WKV_LANES, tb),
        in_specs=[tile] * 3 + [z_tile(z) for z in zs] + [up_tile(w) for w in ups] + [row] * len(rows),
        out_specs=tile,
        out_shape=jax.ShapeDtypeStruct((m, d), bf16),
        scratch_shapes=[pltpu.VMEM((WKV_LANES // LANES, LANES, LANES), f32)],
        compiler_params=pltpu.CompilerParams(
            dimension_semantics=("parallel", "parallel", "arbitrary"),
            vmem_limit_bytes=VMEM_LIMIT),
        name="wkv7",
    )(r, k, v, *zs, *ups, *rows)


def _fox_gate_kernel(s_ref, bf_ref, cp_ref, sp_ref, run_ref, *, tc):
    @pl.when(pl.program_id(1) == 0)
    def _():
        run_ref[...] = jnp.zeros_like(run_ref)

    small = s_ref[...]
    ls = -_softplus(-(small + bf_ref[...]))
    tr = lax.broadcasted_iota(jnp.int32, (tc, tc), 0)
    tcc = lax.broadcasted_iota(jnp.int32, (tc, tc), 1)
    cs = _dot_sel_lhs((tr >= tcc).astype(bf16), ls) + run_ref[...]
    run_ref[...] = cs[tc - 1:tc, :]
    w = cs.shape[1]
    src = lax.broadcasted_iota(jnp.int32, (w, w), 0)
    dst = lax.broadcasted_iota(jnp.int32, (w, w), 1)
    sel = [((dst == N_SPLIT * src + n) & (src < N_HEADS)).astype(bf16) for n in range(N_SPLIT)]
    cp_ref[...] = jnp.dot(jnp.concatenate(_split3(cs * LOG2_E), axis=1), jnp.concatenate(sel, axis=0),
                          preferred_element_type=f32).astype(cp_ref.dtype)
    sp_ref[...] = jnp.concatenate(_split3(jax.nn.sigmoid(small))[:GATE_SPLIT], axis=1)


def _fox_gate(small, b_f, tc=512):
    m, w = small.shape
    nt = SEQ // tc
    bias = jnp.pad(b_f, (0, w - b_f.shape[0])).reshape(1, w)
    pieces = lambda n: pl.BlockSpec((tc, n * w), lambda b, t: (b * nt + t, 0))
    return pl.pallas_call(
        functools.partial(_fox_gate_kernel, tc=tc),
        grid=(BATCH, nt),
        in_specs=[pl.BlockSpec((tc, w), lambda b, t: (b * nt + t, 0)),
                  pl.BlockSpec((1, w), lambda b, t: (0, 0))],
        out_specs=[pieces(1), pieces(GATE_SPLIT)],
        out_shape=[jax.ShapeDtypeStruct((m, n * w), bf16) for n in (1, GATE_SPLIT)],
        scratch_shapes=[pltpu.VMEM((1, w), f32)],
        compiler_params=pltpu.CompilerParams(dimension_semantics=("parallel", "arbitrary"),
                                             vmem_limit_bytes=VMEM_LIMIT),
        name="fox_gate_cumsum",
    )(small, bias)


def _fox_prep_kernel(q_ref, k_ref, v_ref, kp_ref, vp_ref, s_ref, c_ref, qg_ref, kg_ref,
                     qo_ref, ko_ref, vo_ref, *, tm):
    i = pl.program_id(0)
    npairs = PREP_LANES // LANES
    head_ones = _head_ones()
    seq_start = (i * tm) % SEQ == 0
    rid = lax.broadcasted_iota(jnp.int32, (tm, 1), 0)
    r = lax.broadcasted_iota(jnp.int32, (LANES, 2 * LANES), 0)
    col = lax.broadcasted_iota(jnp.int32, (LANES, 2 * LANES), 1)
    small_pieces = s_ref[...]
    aug = c_ref[...]

    def shifted(x, prow):
        prow = jnp.where(seq_start, jnp.zeros_like(prow), prow)
        return jnp.where(rid == 0, prow, pltpu.roll(x, 1, 0))

    def head_rms(x, gain):
        ms = _head_sums(x * x, head_ones) * (1.0 / HEAD_DIM)
        return x * lax.rsqrt(ms + RMS_EPS) * gain

    for p in range(npairs):
        pair = pl.program_id(1) * npairs + p
        ln = slice(p * LANES, (p + 1) * LANES)
        src = (((col & (LANES - 1)) >> HEAD_SHIFT) + HEADS_PER_VREG * pair
               + N_HEADS * (1 + (col >> LANE_SHIFT)))
        sel = (r == src).astype(bf16)
        gates = jnp.dot(small_pieces, jnp.concatenate([sel] * GATE_SPLIT, axis=0), preferred_element_type=f32)
        ak = gates[:, :LANES]
        av = gates[:, LANES:]
        k = k_ref[:, ln].astype(f32)
        v = v_ref[:, ln].astype(f32)
        k = k + ak * (shifted(k, kp_ref[PREV_ROWS - 1:PREV_ROWS, ln].astype(f32)) - k)
        v = v + av * (shifted(v, vp_ref[PREV_ROWS - 1:PREV_ROWS, ln].astype(f32)) - v)
        qo_ref[:, ln] = head_rms(q_ref[:, ln].astype(f32),
                                 qg_ref[...] * (HEAD_DIM ** -0.5 * LOG2_E)).astype(qo_ref.dtype)
        ko_ref[:, 2 * p * LANES:2 * (p + 1) * LANES] = jnp.concatenate(
            [head_rms(k, kg_ref[...]).astype(ko_ref.dtype), aug], axis=1)
        vo_ref[ln, :] = v.T.astype(vo_ref.dtype)


def _fox_prep(proj, small_pieces, c_pieces, qn_g, kn_g, tm=1024):
    m = proj.shape[0]
    rb = tm // PREV_ROWS
    nb = D_MODEL // PREP_LANES
    tile = lambda cb: pl.BlockSpec((tm, PREP_LANES), lambda i, j, cb=cb: (i, cb * nb + j))
    prev = lambda cb: pl.BlockSpec(
        (PREV_ROWS, PREP_LANES), lambda i, j, cb=cb: (jnp.maximum(i * rb - 1, 0), cb * nb + j))
    gain = pl.BlockSpec((1, LANES), lambda i, j: (0, 0))
    small_tile = lambda a: pl.BlockSpec((tm, a.shape[1]), lambda i, j: (i, 0))
    tile_gain = lambda x: jnp.tile(x, HEADS_PER_VREG).reshape(1, LANES)
    return pl.pallas_call(
        functools.partial(_fox_prep_kernel, tm=tm),
        grid=(m // tm, nb),
        in_specs=[tile(0), tile(1), tile(2), prev(1), prev(2), small_tile(small_pieces), small_tile(c_pieces),
                  gain, gain],
        out_specs=[pl.BlockSpec((tm, PREP_LANES), lambda i, j: (i, j)),
                   pl.BlockSpec((tm, 2 * PREP_LANES), lambda i, j: (i, j)),
                   pl.BlockSpec((PREP_LANES, tm), lambda i, j: (j, i))],
        out_shape=[jax.ShapeDtypeStruct((m, D_MODEL), bf16),
                   jax.ShapeDtypeStruct((m, 2 * D_MODEL), bf16),
                   jax.ShapeDtypeStruct((D_MODEL, m), bf16)],
        compiler_params=pltpu.CompilerParams(dimension_semantics=("parallel", "parallel"),
                                             vmem_limit_bytes=VMEM_LIMIT),
        name="fox_prep",
    )(proj, proj, proj, proj, proj, small_pieces, c_pieces, tile_gain(qn_g), tile_gain(kn_g))


def _fox_attn_kernel(q_ref, k_ref, vt_ref, gate_ref, og_ref, o_ref):
    tq, tk = ATT_TQ, ATT_TK
    qi = pl.program_id(2)
    npairs = ATT_LANES // LANES
    chains = [(p, h) for p in range(npairs) for h in range(HEADS_PER_VREG)]
    lane = lax.broadcasted_iota(jnp.int32, (tq, LANES), 1)
    q_aug = []
    for p, h in chains:
        q = q_ref[:, p * LANES:(p + 1) * LANES]
        own = (lane >> HEAD_SHIFT) == h
        head = (pl.program_id(1) * npairs + p) * HEADS_PER_VREG + h
        minus_one = (lane >= N_SPLIT * head) & (lane < N_SPLIT * (head + 1))
        q_aug.append(jnp.concatenate([jnp.where(own, q, jnp.zeros_like(q)),
                                      jnp.where(minus_one, -1.0, 0.0).astype(bf16)], axis=1))

    def step(key0, width, carry, diag):
        m_run, l_run, acc = carry
        keys = pl.ds(pl.multiple_of(key0, tq), width)
        s = [_dot_nt(k_ref[keys, 2 * p * LANES:2 * (p + 1) * LANES], qa)
             for (p, h), qa in zip(chains, q_aug)]
        if diag:
            kidx = key0 + lax.broadcasted_iota(jnp.int32, (width, tq), 0)
            qidx = qi * tq + lax.broadcasted_iota(jnp.int32, (width, tq), 1)
            s = [jnp.where(qidx >= kidx, x, NEG_BIG) for x in s]
        m_new = [jnp.maximum(mr, jnp.max(x, axis=0, keepdims=True)) for mr, x in zip(m_run, s)]
        alpha = [jnp.exp2(mr - mn) for mr, mn in zip(m_run, m_new)]
        pr = [jnp.exp2(x - mn) for x, mn in zip(s, m_new)]
        l_new = [a * lr + jnp.sum(x, axis=0, keepdims=True) for a, lr, x in zip(alpha, l_run, pr)]
        pv = [jnp.dot(vt_ref[pl.ds((p * HEADS_PER_VREG + h) * HEAD_DIM, HEAD_DIM), keys], x.astype(bf16),
                      preferred_element_type=f32) for (p, h), x in zip(chains, pr)]
        acc = [ac * a + x for ac, a, x in zip(acc, alpha, pv)]
        return m_new, l_new, acc

    n = len(chains)
    init = ([jnp.full((1, tq), NEG_BIG, f32)] * n, [jnp.zeros((1, tq), f32)] * n,
            [jnp.zeros((HEAD_DIM, tq), f32)] * n)
    n_full = (qi * tq) // tk
    carry = lax.fori_loop(0, n_full, lambda j, cy: step(j * tk, tk, cy, False), init)
    _, l_run, acc = lax.cond((qi * tq) % tk == 0,
                             lambda cy: step(n_full * tk, tq, cy, True),
                             lambda cy: step(n_full * tk, tk, cy, True), carry)
    o_t = []
    for ac, lr in zip(acc, l_run):
        o = ac / lr
        o_t.append(o * lax.rsqrt(jnp.mean(o * o, axis=0, keepdims=True) + RMS_EPS))
    for p in range(npairs):
        ln = slice(p * LANES, (p + 1) * LANES)
        o = jnp.concatenate(o_t[HEADS_PER_VREG * p:HEADS_PER_VREG * (p + 1)], axis=0).T
        o_ref[:, ln] = (o * og_ref[:, ln] * jax.nn.sigmoid(gate_ref[:, ln].astype(f32))).astype(o_ref.dtype)


def _fox_attn(q, k_aug, v_t, proj, on_g):
    m, d = q.shape
    nq = SEQ // ATT_TQ
    gate_col0 = 3 * D_MODEL // ATT_LANES
    return pl.pallas_call(
        _fox_attn_kernel,
        grid=(BATCH, d // ATT_LANES, nq),
        in_specs=[pl.BlockSpec((ATT_TQ, ATT_LANES), lambda b, p, i: (b * nq + i, p)),
                  pl.BlockSpec((SEQ, 2 * ATT_LANES), lambda b, p, i: (b, p)),
                  pl.BlockSpec((ATT_LANES, SEQ), lambda b, p, i: (p, b)),
                  pl.BlockSpec((ATT_TQ, ATT_LANES), lambda b, p, i: (b * nq + i, gate_col0 + p)),
                  pl.BlockSpec((1, ATT_LANES), lambda b, p, i: (0, p))],
        out_specs=pl.BlockSpec((ATT_TQ, ATT_LANES), lambda b, p, i: (b * nq + i, p)),
        out_shape=jax.ShapeDtypeStruct((m, d), bf16),
        compiler_params=pltpu.CompilerParams(
            dimension_semantics=("parallel", "parallel", "arbitrary"),
            vmem_limit_bytes=VMEM_LIMIT),
        name="fox_attention",
    )(q, k_aug, v_t, proj, on_g.reshape(1, d))


def _swiglu_block(x, hn, w_gu, w_d, layer, next_norm_g, last):
    act = _matmul(hn, w_gu, layer, (0, 1), D_FF, tm=2048, tn=512, tk=D_MODEL,
                  epilogue=_epi_swiglu, out_dtype=bf16, name="swiglu_gate_up")
    return _matmul(act, w_d, layer, (0,), D_MODEL, tm=256, tn=D_MODEL, tk=D_FF, resident_w=True,
                   epilogue=_epi_residual_norm_only if last else _epi_residual_norm,
                   extras=((x, "tile"), (next_norm_g.reshape(1, D_MODEL), "row")),
                   out_dtype=f32 if last else (f32, bf16), name="swiglu_down")


def _proj(x, w, layer=0, *, out_dtype=f32, name):
    if w.ndim == 2:
        w = w[None]
    n = w.shape[2]
    resident = n <= D_MODEL
    return _matmul(x, w, layer, (0,), n, tm=1024 if resident else 2048, tn=n if resident else 1024,
                   tk=x.shape[1], epilogue=_epi_plain, out_dtype=out_dtype, name=name, resident_w=resident)


def _out_proj_norm(y, w_o, x, next_norm_g, *, name):
    return _matmul(y, w_o.astype(bf16)[None], 0, (0,), D_MODEL, tm=512, tn=D_MODEL, tk=D_MODEL,
                   epilogue=_epi_residual_norm, extras=((x, "tile"), (next_norm_g.reshape(1, D_MODEL), "row")),
                   out_dtype=(f32, bf16), name=name)


def _rwkv7_block(x, norm_g, mix, w_rkv, w0, w1, w2, a0, a1, a2, g1, g2, k_k, k_a, r_k, lnx_g, lnx_b, w_o,
                 next_norm_g):
    rank = LANES * pl.cdiv(w1.shape[1], LANES)
    downs = (_pad_cols(w1, rank).astype(bf16), _pad_cols(a1, rank).astype(bf16), g1.astype(bf16))
    ups = (_pad_rows(w2, rank).astype(bf16), _pad_rows(a2, rank).astype(bf16), g2.astype(bf16))
    xr, xk, xv, *zs = _norm_mix(x, norm_g, mix, *downs)
    w_rkv = w_rkv.astype(bf16)
    r = _proj(xr, w_rkv, 0, name="rwkv_r")
    k = _proj(xk, w_rkv, 1, name="rwkv_k")
    v = _proj(xv, w_rkv, 2, name="rwkv_v")
    yg = _wkv(r, k, v, zs, ups, w0, a0, k_k, k_a, r_k.reshape(-1), lnx_g, lnx_b)
    return _out_proj_norm(yg, w_o, x, next_norm_g, name="rwkv_out")


def _fox_block(x, hn, w_in, b_f, qn_g, kn_g, on_g, w_o, next_norm_g):
    n_main = 4 * D_MODEL
    proj = _proj(hn, w_in[:, :n_main].astype(bf16), out_dtype=bf16, name="fox_in")
    w_small = _pad_cols(w_in[:, n_main:], LANES).astype(bf16)
    small = _proj(hn, w_small, name="fox_in_gates")
    c_pieces, small_pieces = _fox_gate(small, b_f)
    q, k_aug, v_t = _fox_prep(proj, small_pieces, c_pieces, qn_g, kn_g)
    og = _fox_attn(q, k_aug, v_t, proj, on_g)
    return _out_proj_norm(og, w_o, x, next_norm_g, name="fox_out")


def kernel(x, a_norm_g, a_mix, a_w_rkv, a_w0, a_w1, a_w2, a_a0, a_a1, a_a2, a_g1, a_g2, a_k_k, a_k_a, a_r_k, a_lnx_g, a_lnx_b, a_w_o, b_norm_g, b_w_in, b_b_f, b_qn_g, b_kn_g, b_on_g, b_w_o, f_norm_g, f_w_gu, f_w_d, final_g):
    b, t, d = x.shape
    h = x.reshape(b * t, d)
    w_d = f_w_d.astype(bf16)
    h = _rwkv7_block(h, a_norm_g[0], a_mix[0], a_w_rkv[0], a_w0[0], a_w1[0], a_w2[0], a_a0[0], a_a1[0],
                     a_a2[0], a_g1[0], a_g2[0], a_k_k[0], a_k_a[0], a_r_k[0], a_lnx_g[0], a_lnx_b[0],
                     a_w_o[0], f_norm_g[0])
    h = _swiglu_block(*h, f_w_gu, w_d, 0, b_norm_g[0], False)
    h = _fox_block(*h, b_w_in[0], b_b_f[0], b_qn_g[0], b_kn_g[0], b_on_g[0], b_w_o[0],
                   f_norm_g[1])
    return _swiglu_block(*h, f_w_gu, w_d, 1, final_g, True).reshape(b, t, d)
```
